```python
import math
import jax, jax.numpy as jnp
from jax import lax
import numpy as np

D_MODEL = 4096
BATCH = 8
SEQ = 2048
DEPTH = 2

GRID_W = 64
CTX_LEN = 256
HEAD_DIM = 128
A_WIDTH = (3 * D_MODEL) // 8
A_HEADS = A_WIDTH // HEAD_DIM
A_QK_DIM = HEAD_DIM // 2
A_V_DIM = HEAD_DIM
B_WIDTH = (3 * D_MODEL) // 8
B_HEADS = B_WIDTH // HEAD_DIM
B_HEAD_DIM = HEAD_DIM
C_WIDTH = D_MODEL - A_WIDTH - B_WIDTH
C_BLOCKS = 16
C_BLOCK_DIM = C_WIDTH // C_BLOCKS
MIX_WIDTH = A_WIDTH + B_WIDTH + C_WIDTH
SPLIT_SIZES = (A_WIDTH, A_WIDTH, A_WIDTH, A_WIDTH,
               B_WIDTH, B_WIDTH, B_WIDTH, B_WIDTH,
               C_WIDTH, C_WIDTH)
IN_WIDTH = 4 * A_WIDTH + 4 * B_WIDTH + 2 * C_WIDTH

NA_KH = 8
NA_KW = 16
ROPE_THETA = 10000.0
RGLRU_C = 8.0
CONV_W = 4
CONV_LEFT = 2
Q_BLOCK = 128
NORM_EPS = 1e-6
SUBLN_EPS = 1e-5
NEG_INF = -1e30

kernel_name = "hybrid_diffattn_natten_rglru_dit"


def rms_norm(x, w, eps=NORM_EPS):
    x32 = x.astype(jnp.float32)
    y = x32 * lax.rsqrt(jnp.mean(x32 * x32, axis=-1, keepdims=True) + eps)
    return (y * w.astype(jnp.float32)).astype(x.dtype)


def split_columns(p):
    points = [int(v) for v in np.cumsum(SPLIT_SIZES)[:-1]]
    return jnp.split(p, points, axis=-1)


def _rope_1d(x, pos):
    n = x.shape[-1] // 2
    freqs = ROPE_THETA ** (-jnp.arange(n, dtype=jnp.float32) / n)
    ang = pos.astype(jnp.float32)[:, None] * freqs[None, :]
    cos = jnp.cos(ang)[None, :, None, None, :]
    sin = jnp.sin(ang)[None, :, None, None, :]
    x32 = x.astype(jnp.float32)
    x1, x2 = x32[..., :n], x32[..., n:]
    return jnp.concatenate([x1 * cos - x2 * sin, x2 * cos + x1 * sin], axis=-1).astype(x.dtype)


def axial_rope(x, rows, cols):
    half = x.shape[-1] // 2
    return jnp.concatenate([_rope_1d(x[..., :half], rows), _rope_1d(x[..., half:], cols)], axis=-1)


def diff_softmax_attn(q, k, v, lam):
    s = jnp.einsum('bqhnd,bkhnd->nbhqk', q, k).astype(jnp.float32) * (A_QK_DIM ** -0.5)
    p = jax.nn.softmax(s, axis=-1)
    attn = (p[0] - lam * p[1]).astype(v.dtype)
    return jnp.einsum('bhqk,bkhe->bqhe', attn, v)


def diff_attention_latent(q, k_all, v_all, lam):
    bn, s = q.shape[:2]
    nqb = s // Q_BLOCK
    qb = q.reshape(bn, nqb, Q_BLOCK, *q.shape[2:]).swapaxes(0, 1)
    o = lax.map(lambda q_blk: diff_softmax_attn(q_blk, k_all, v_all, lam), qb)
    return o.swapaxes(0, 1).reshape(bn, s, *o.shape[3:])


def dense_attn(q, k, v):
    s = jnp.einsum('bqhd,bkhd->bhqk', q, k).astype(jnp.float32) * (q.shape[-1] ** -0.5)
    p = jax.nn.softmax(s, axis=-1).astype(v.dtype)
    return jnp.einsum('bhqk,bkhd->bqhd', p, v)


def neighbourhood_attention(q, k, v, k_ctx, v_ctx, rpb):
    bn, s, h, dh = q.shape
    rows_n = s // GRID_W
    kh = min(NA_KH, rows_n)
    w = GRID_W
    scale = dh ** -0.5
    n_ctx = k_ctx.shape[1]
    q_rows = q.reshape(bn, rows_n, w, h, dh).swapaxes(0, 1)
    k_grid = k.reshape(bn, rows_n, w, h, dh)
    v_grid = v.reshape(bn, rows_n, w, h, dh)
    cq = jnp.arange(w)
    c0 = jnp.clip(cq - NA_KW // 2, 0, w - NA_KW)
    ck = jnp.arange(w)
    col_valid = (ck[None, :] >= c0[:, None]) & (ck[None, :] < c0[:, None] + NA_KW)
    dc_idx = jnp.clip(ck[None, :] - cq[:, None] + NA_KW - 1, 0, 2 * NA_KW - 2)
    rpb_col = rpb[:, :, dc_idx].astype(jnp.float32)

    def row_block(args):
        r, q_row = args
        r0 = jnp.clip(r - kh // 2, 0, rows_n - kh)
        k_slab = lax.dynamic_slice_in_dim(k_grid, r0, kh, axis=1)
        v_slab = lax.dynamic_slice_in_dim(v_grid, r0, kh, axis=1)
        dr_idx = r0 + jnp.arange(kh) - r + NA_KH - 1
        bias = jnp.transpose(rpb_col[:, dr_idx], (0, 2, 1, 3))
        s_nb = jnp.einsum('bqhd,bikhd->bhqik', q_row, k_slab).astype(jnp.float32) * scale + bias[None]
        s_nb = jnp.where(col_valid[:, None, :], s_nb, NEG_INF)
        s_ctx = jnp.einsum('bqhd,bkhd->bhqk', q_row, k_ctx).astype(jnp.float32) * scale
        p = jax.nn.softmax(jnp.concatenate([s_ctx, s_nb.reshape(bn, h, w, kh * w)], axis=-1),
                           axis=-1).astype(v.dtype)
        p_nb = p[..., n_ctx:].reshape(bn, h, w, kh, w)
        return (jnp.einsum('bhqk,bkhd->bqhd', p[..., :n_ctx], v_ctx)
                + jnp.einsum('bhqik,bikhd->bqhd', p_nb, v_slab))

    o = lax.map(row_block, (jnp.arange(rows_n), q_rows))
    return o.swapaxes(0, 1).reshape(bn, s, h * dh)


def dwconv_centred(u, w, b):
    s = u.shape[1]
    up = jnp.pad(u, ((0, 0), (CONV_LEFT, CONV_W - 1 - CONV_LEFT), (0, 0)))
    acc = b
    for j in range(CONV_W):
        acc = acc + w[j] * up[:, j:j + s]
    return acc


def linear_scan(a, b, h0, reverse):
    if reverse:
        a = jnp.flip(a, axis=1)
        b = jnp.flip(b, axis=1)
    b = b.at[:, 0].add(a[:, 0] * h0)

    def combine(e1, e2):
        a1, b1 = e1
        a2, b2 = e2
        return a1 * a2, a2 * b1 + b2

    _, h = lax.associative_scan(combine, (a, b), axis=1)
    final = h[:, -1]
    if reverse:
        h = jnp.flip(h, axis=1)
    return h, final


def rglru_gates(u32, wa, ba, wx, bx, lam):
    bn, s, _ = u32.shape
    ub = u32.reshape(bn, s, C_BLOCKS, C_BLOCK_DIM)
    r = jax.nn.sigmoid(jnp.einsum('bsni,nij->bsnj', ub, wa.astype(jnp.float32)).reshape(bn, s, C_WIDTH)
                       + ba.astype(jnp.float32))
    i = jax.nn.sigmoid(jnp.einsum('bsni,nij->bsnj', ub, wx.astype(jnp.float32)).reshape(bn, s, C_WIDTH)
                       + bx.astype(jnp.float32))
    log_a = -RGLRU_C * r * jax.nn.softplus(-lam.astype(jnp.float32))
    a = jnp.exp(log_a)
    b = jnp.sqrt(-jnp.expm1(2.0 * log_a)) * (i * u32)
    return a, b


def rglru_bidirectional(u_lat, u_ctx, wa, ba, wx, bx, lam, need_ctx):
    bn = u_lat.shape[0]
    ul32 = u_lat.astype(jnp.float32)
    uc32 = u_ctx.astype(jnp.float32)
    y_lat = []
    y_ctx = []
    for d in range(2):
        rev = d == 1
        a_c, b_c = rglru_gates(uc32, wa[d], ba[d], wx[d], bx[d], lam[d])
        h_c, fin_c = linear_scan(a_c, b_c, jnp.zeros((bn, C_WIDTH), jnp.float32), rev)
        a_l, b_l = rglru_gates(ul32, wa[d], ba[d], wx[d], bx[d], lam[d])
        h_l, _ = linear_scan(a_l, b_l, fin_c, rev)
        y_lat.append(h_l)
        y_ctx.append(h_c)
    out_lat = (y_lat[0] + y_lat[1]).astype(u_lat.dtype)
    out_ctx = (y_ctx[0] + y_ctx[1]).astype(u_ctx.dtype) if need_ctx else None
    return out_lat, out_ctx


def hybrid_layer(l, x, ctx, c, c_ctx, ada_w, ada_b, norm_w, w_in, w_out, lambda_qk, subln_w, rpb,
                 conv_w, conv_b, rg_wa, rg_ba, rg_wx, rg_bx, rg_lambda, need_ctx):
    bn, s, d = x.shape
    n_ctx = ctx.shape[1]
    t = jnp.arange(s)
    rows = t // GRID_W
    cols = t % GRID_W

    mod = jax.nn.silu(c) @ ada_w + ada_b
    shift, scale, gate = jnp.split(mod[:, None, :], 3, axis=-1)
    mod_c = jax.nn.silu(c_ctx) @ ada_w + ada_b
    shift_c, scale_c, gate_c = jnp.split(mod_c, 3, axis=-1)
    hx = rms_norm(x, norm_w) * (1.0 + scale) + shift
    hc = rms_norm(ctx, norm_w) * (1.0 + scale_c) + shift_c

    qa, ka, va, ga, qb, kb, vb, gb, xc, gc = split_columns(hx @ w_in)
    qa_c, ka_c, va_c, ga_c, qb_c, kb_c, vb_c, gb_c, xc_c, gc_c = split_columns(hc @ w_in)

    lam_init = 0.8 - 0.6 * math.exp(-0.3 * l)
    lq = lambda_qk.astype(jnp.float32)
    lam = jnp.exp(jnp.sum(lq[0] * lq[1])) - jnp.exp(jnp.sum(lq[2] * lq[3])) + lam_init
    qa_l = axial_rope(qa.reshape(bn, s, A_HEADS, 2, A_QK_DIM), rows, cols)
    ka_l = axial_rope(ka.reshape(bn, s, A_HEADS, 2, A_QK_DIM), rows, cols)
    va_l = va.reshape(bn, s, A_HEADS, A_V_DIM)
    ka_cx = ka_c.reshape(bn, n_ctx, A_HEADS, 2, A_QK_DIM)
    va_cx = va_c.reshape(bn, n_ctx, A_HEADS, A_V_DIM)
    k_all = jnp.concatenate([ka_cx, ka_l], axis=1)
    v_all = jnp.concatenate([va_cx, va_l], axis=1)
    oa = diff_attention_latent(qa_l, k_all, v_all, lam)
    oa = (rms_norm(oa, subln_w, SUBLN_EPS) * (1.0 - lam_init)).reshape(bn, s, A_WIDTH)

    kb_cx = kb_c.reshape(bn, n_ctx, B_HEADS, B_HEAD_DIM)
    vb_cx = vb_c.reshape(bn, n_ctx, B_HEADS, B_HEAD_DIM)
    ob = neighbourhood_attention(qb.reshape(bn, s, B_HEADS, B_HEAD_DIM),
                                 kb.reshape(bn, s, B_HEADS, B_HEAD_DIM),
                                 vb.reshape(bn, s, B_HEADS, B_HEAD_DIM), kb_cx, vb_cx, rpb)

    u_l = dwconv_centred(xc, conv_w, conv_b)
    u_c = dwconv_centred(xc_c, conv_w, conv_b)
    oc, oc_c = rglru_bidirectional(u_l, u_c, rg_wa, rg_ba, rg_wx, rg_bx, rg_lambda, need_ctx)

    merged = jnp.concatenate([oa * jax.nn.silu(ga), ob * jax.nn.silu(gb), oc * jax.nn.silu(gc)], axis=-1)
    x = x + gate * (merged @ w_out)

    if need_ctx:
        oa_c = diff_softmax_attn(qa_c.reshape(bn, n_ctx, A_HEADS, 2, A_QK_DIM), ka_cx, va_cx, lam)
        oa_c = (rms_norm(oa_c, subln_w, SUBLN_EPS) * (1.0 - lam_init)).reshape(bn, n_ctx, A_WIDTH)
        ob_c = dense_attn(qb_c.reshape(bn, n_ctx, B_HEADS, B_HEAD_DIM), kb_cx, vb_cx).reshape(bn, n_ctx, B_WIDTH)
        merged_c = jnp.concatenate([oa_c * jax.nn.silu(ga_c), ob_c * jax.nn.silu(gb_c),
                                    oc_c * jax.nn.silu(gc_c)], axis=-1)
        ctx = ctx + gate_c * (merged_c @ w_out)
    return x, ctx


def setup_inputs(seed: int = 0) -> dict:
    key = jax.random.key(seed)
    ks = jax.random.split(key, 20)
    f32 = jnp.float32
    x = jax.random.normal(ks[0], (BATCH, SEQ, D_MODEL), f32)
    c = jax.random.normal(ks[1], (BATCH, D_MODEL), f32)
    ctx = jax.random.normal(ks[2], (BATCH, CTX_LEN, D_MODEL), f32)
    c_ctx = jax.random.normal(ks[3], (D_MODEL,), f32)
    ada_w = jax.random.normal(ks[4], (DEPTH, D_MODEL, 3 * D_MODEL), f32) * (0.5 * D_MODEL ** -0.5)
    ada_b = jax.random.normal(ks[5], (DEPTH, 3 * D_MODEL), f32) * 0.02
    norm_w = 1.0 + 0.02 * jax.random.normal(ks[6], (DEPTH, D_MODEL), f32)
    w_in = jax.random.normal(ks[7], (DEPTH, D_MODEL, IN_WIDTH), f32) * (D_MODEL ** -0.5)
    w_out = jax.random.normal(ks[8], (DEPTH, MIX_WIDTH, D_MODEL), f32) * (MIX_WIDTH ** -0.5)
    lambda_qk = jax.random.normal(ks[9], (DEPTH, 4, A_QK_DIM), f32) * 0.1
    subln_w = 1.0 + 0.02 * jax.random.normal(ks[10], (DEPTH, A_V_DIM), f32)
    rpb = jax.random.normal(ks[11], (DEPTH, B_HEADS, 2 * NA_KH - 1, 2 * NA_KW - 1), f32) * 0.1
    conv_w = jax.random.normal(ks[12], (DEPTH, CONV_W, C_WIDTH), f32) * (CONV_W ** -0.5)
    conv_b = jax.random.normal(ks[13], (DEPTH, C_WIDTH), f32) * 0.02
    rg_wa = jax.random.normal(ks[14], (DEPTH, 2, C_BLOCKS, C_BLOCK_DIM, C_BLOCK_DIM), f32) * (C_BLOCK_DIM ** -0.5)
    rg_ba = jax.random.normal(ks[15], (DEPTH, 2, C_WIDTH), f32) * 0.02
    rg_wx = jax.random.normal(ks[16], (DEPTH, 2, C_BLOCKS, C_BLOCK_DIM, C_BLOCK_DIM), f32) * (C_BLOCK_DIM ** -0.5)
    rg_bx = jax.random.normal(ks[17], (DEPTH, 2, C_WIDTH), f32) * 0.02
    a_base = jax.random.uniform(ks[18], (DEPTH, 2, C_WIDTH), f32, minval=0.9, maxval=0.999) ** (1.0 / RGLRU_C)
    rg_lambda = jnp.log(a_base) - jnp.log1p(-a_base)
    final_norm_w = 1.0 + 0.02 * jax.random.normal(ks[19], (D_MODEL,), f32)
    return {"x": x, "c": c, "ctx": ctx, "c_ctx": c_ctx, "ada_w": ada_w, "ada_b": ada_b,
            "norm_w": norm_w, "w_in": w_in, "w_out": w_out, "lambda_qk": lambda_qk,
            "subln_w": subln_w, "rpb": rpb, "conv_w": conv_w, "conv_b": conv_b,
            "rg_wa": rg_wa, "rg_ba": rg_ba, "rg_wx": rg_wx, "rg_bx": rg_bx,
            "rg_lambda": rg_lambda, "final_norm_w": final_norm_w}


def reference(x, c, ctx, c_ctx, ada_w, ada_b, norm_w, w_in, w_out, lambda_qk, subln_w, rpb,
              conv_w, conv_b, rg_wa, rg_ba, rg_wx, rg_bx, rg_lambda, final_norm_w):
    for l in range(DEPTH):
        x, ctx = hybrid_layer(l, x, ctx, c, c_ctx, ada_w[l], ada_b[l], norm_w[l], w_in[l], w_out[l],
                              lambda_qk[l], subln_w[l], rpb[l], conv_w[l], conv_b[l],
                              rg_wa[l], rg_ba[l], rg_wx[l], rg_bx[l], rg_lambda[l],
                              need_ctx=(l < DEPTH - 1))
    return rms_norm(x, final_norm_w)
```

```python
import functools
import math

import numpy as np
import jax
import jax.numpy as jnp
from jax import lax
from jax.experimental import pallas as pl
from jax.experimental.pallas import tpu as pltpu

F32 = jnp.float32
BF16 = jnp.bfloat16

D_MODEL = 4096
BATCH = 8
SEQ = 2048
DEPTH = 2
GRID_W = 64
GRID_H = SEQ // GRID_W
CTX_LEN = 256
HEAD_DIM = 128
A_WIDTH = (3 * D_MODEL) // 8
A_HEADS = A_WIDTH // HEAD_DIM
A_QK_DIM = HEAD_DIM // 2
B_WIDTH = (3 * D_MODEL) // 8
B_HEADS = B_WIDTH // HEAD_DIM
C_WIDTH = D_MODEL - A_WIDTH - B_WIDTH
C_BLOCKS = 16
C_BLOCK_DIM = C_WIDTH // C_BLOCKS
MIX_WIDTH = A_WIDTH + B_WIDTH + C_WIDTH
IN_WIDTH = 4 * A_WIDTH + 4 * B_WIDTH + 2 * C_WIDTH
NA_KH = 8
NA_KW = 16
ROPE_THETA = 10000.0
RGLRU_C = 8.0
CONV_W = 4
CONV_LEFT = 2
NORM_EPS = 1e-6
SUBLN_EPS = 1e-5
NEG_INF = -1e30

HB = HEAD_DIM
COL_QA, COL_KA, COL_VA, COL_GA = 0, A_HEADS, 2 * A_HEADS, 3 * A_HEADS
COL_QB = 4 * A_HEADS
COL_KB, COL_VB, COL_GB = COL_QB + B_HEADS, COL_QB + 2 * B_HEADS, COL_QB + 3 * B_HEADS
COL_XC = COL_QB + 4 * B_HEADS
COL_GC = COL_XC + C_WIDTH // HB

MOD_ROWS = 16
CTX_MOD_ROW = BATCH

VMEM_LIMIT = 56 * 1024 * 1024

NB_QROWS = 4
NB_KROWS = 12
NB_GROUPS = GRID_H // NB_QROWS
NB_Q = NB_QROWS * GRID_W
NB_K = NB_KROWS * GRID_W


def _params(sem):
    return pltpu.CompilerParams(dimension_semantics=sem, vmem_limit_bytes=VMEM_LIMIT)


def _sigmoid(x):
    return 1.0 / (1.0 + jnp.exp(-x))


def _silu(x):
    return x * _sigmoid(x)


def _adaln_kernel(c_ref, w_ref, b_ref, o_ref):
    s = _silu(c_ref[...]).astype(BF16)
    w = w_ref[0].astype(BF16)
    o_ref[0] = jnp.dot(s, w, preferred_element_type=F32) + b_ref[0]


def adaln_modulation(cvec, ada_w, ada_b):
    tn = 512
    n3 = 3 * D_MODEL
    return pl.pallas_call(
        _adaln_kernel,
        out_shape=jax.ShapeDtypeStruct((DEPTH, MOD_ROWS, n3), F32),
        grid=(DEPTH, n3 // tn),
        in_specs=[
            pl.BlockSpec((MOD_ROWS, D_MODEL), lambda l, j: (0, 0)),
            pl.BlockSpec((1, D_MODEL, tn), lambda l, j: (l, 0, j)),
            pl.BlockSpec((1, 1, tn), lambda l, j: (l, 0, j)),
        ],
        out_specs=pl.BlockSpec((1, MOD_ROWS, tn), lambda l, j: (l, 0, j)),
        compiler_params=_params(("arbitrary", "arbitrary")),
        name="adaln_modulation",
    )(cvec, ada_w, ada_b.reshape(DEPTH, 1, n3))


_NORM_CHUNK = 32


def _inproj_kernel(x_ref, nw_ref, shift_ref, scale_ref, w_ref, o_ref, hx_ref):
    @pl.when(pl.program_id(1) == 0)
    def _():
        nw = nw_ref[...]
        mul = 1.0 + scale_ref[0]
        add = shift_ref[0]

        def body(r, carry):
            rows = pl.ds(pl.multiple_of(r * _NORM_CHUNK, _NORM_CHUNK), _NORM_CHUNK)
            x = x_ref[rows, :]
            ms = jnp.mean(x * x, axis=-1, keepdims=True)
            y = x * lax.rsqrt(ms + NORM_EPS) * nw
            hx_ref[rows, :] = (y * mul + add).astype(BF16)
            return carry

        lax.fori_loop(0, hx_ref.shape[0] // _NORM_CHUNK, body, 0)

    o_ref[...] = jnp.dot(hx_ref[...], w_ref[...], preferred_element_type=F32).astype(o_ref.dtype)


def input_projection(x2d, norm_w, mod3, w_bf16, rows_per_mod, mod_row0):
    m = x2d.shape[0]
    tm, tn = 512, 1024
    nd = D_MODEL // D_MODEL

    def mod_row(i):
        return mod_row0 + (i * tm) // rows_per_mod

    return pl.pallas_call(
        _inproj_kernel,
        out_shape=jax.ShapeDtypeStruct((m, IN_WIDTH), BF16),
        grid=(m // tm, IN_WIDTH // tn),
        in_specs=[
            pl.BlockSpec((tm, D_MODEL), lambda i, j: (i, 0)),
            pl.BlockSpec((1, D_MODEL), lambda i, j: (0, 0)),
            pl.BlockSpec((1, 1, D_MODEL), lambda i, j: (mod_row(i), 0, 0 * nd)),
            pl.BlockSpec((1, 1, D_MODEL), lambda i, j: (mod_row(i), 0, 1 * nd)),
            pl.BlockSpec((D_MODEL, tn), lambda i, j: (0, j)),
        ],
        out_specs=pl.BlockSpec((tm, tn), lambda i, j: (i, j)),
        scratch_shapes=[pltpu.VMEM((tm, D_MODEL), BF16)],
        compiler_params=_params(("arbitrary", "arbitrary")),
        name="input_projection",
    )(x2d, norm_w.reshape(1, D_MODEL), mod3, mod3, w_bf16)


def _rope(x, cos, sin_lo, sin_hi):
    return (x * cos + pltpu.roll(x, HEAD_DIM - 16, axis=1) * sin_lo
            + pltpu.roll(x, 16, axis=1) * sin_hi)


def _diff_lambda(lq_ref, lam_init):
    lq = lq_ref[...]
    a = jnp.sum(lq[0:1] * lq[1:2], axis=-1, keepdims=True)
    b = jnp.sum(lq[2:3] * lq[3:4], axis=-1, keepdims=True)
    return jnp.exp(a) - jnp.exp(b) + lam_init


def _online_softmax_step(s, v, m, l, acc):
    m_new = jnp.maximum(m, jnp.max(s, axis=-1, keepdims=True))
    alpha = jnp.exp(m - m_new)
    p = jnp.exp(s - m_new)
    l = alpha * l + jnp.sum(p, axis=-1, keepdims=True)
    acc = alpha * acc + jnp.dot(p.astype(BF16), v, preferred_element_type=F32)
    return m_new, l, acc


_NT = (((1,), (1,)), ((), ()))


def _diffattn_kernel(*refs, lam_init, tq, n_lat, rope, kchunk):
    if n_lat:
        (q_ref, kc_ref, vc_ref, kl_ref, vl_ref, g_ref, cos_ref, slo_ref, shi_ref,
         lq_ref, sw_ref, o_ref, kr_ref) = refs
    else:
        q_ref, kc_ref, vc_ref, g_ref, lq_ref, sw_ref, o_ref = refs
    qi = pl.program_id(2)

    if n_lat:
        @pl.when(qi == 0)
        def _():
            def body(c, carry):
                rows = pl.ds(pl.multiple_of(c * kchunk, kchunk), kchunk)
                k = kl_ref[rows, :].astype(F32)
                kr_ref[rows, :] = _rope(k, cos_ref[rows, :], slo_ref[rows, :], shi_ref[rows, :]).astype(BF16)
                return carry
            lax.fori_loop(0, n_lat // kchunk, body, 0)

    q = q_ref[...].astype(F32)
    if rope:
        rows = pl.ds(pl.multiple_of(qi * tq, tq), tq)
        q = _rope(q, cos_ref[rows, :], slo_ref[rows, :], shi_ref[rows, :])
    q = q * (A_QK_DIM ** -0.5)
    lane = lax.broadcasted_iota(jnp.int32, q.shape, 1)
    qs = (jnp.where(lane < A_QK_DIM, q, 0.0).astype(BF16),
          jnp.where(lane >= A_QK_DIM, q, 0.0).astype(BF16))

    state = []
    for n in range(2):
        s = lax.dot_general(qs[n], kc_ref[...], _NT, preferred_element_type=F32)
        m = jnp.max(s, axis=-1, keepdims=True)
        p = jnp.exp(s - m)
        l = jnp.sum(p, axis=-1, keepdims=True)
        acc = jnp.dot(p.astype(BF16), vc_ref[...], preferred_element_type=F32)
        state += [m, l, acc]

    if n_lat:
        def kbody(c, st):
            rows = pl.ds(pl.multiple_of(c * kchunk, kchunk), kchunk)
            k = kr_ref[rows, :]
            v = vl_ref[rows, :]
            out = []
            for n in range(2):
                s = lax.dot_general(qs[n], k, _NT, preferred_element_type=F32)
                out += list(_online_softmax_step(s, v, *st[3 * n:3 * n + 3]))
            return tuple(out)
        state = lax.fori_loop(0, n_lat // kchunk, kbody, tuple(state))

    lam = _diff_lambda(lq_ref, lam_init)
    o = state[2] * (1.0 / state[1]) - (lam / state[4]) * state[5]
    o = o * lax.rsqrt(jnp.mean(o * o, axis=-1, keepdims=True) + SUBLN_EPS) * sw_ref[...]
    o = o * (1.0 - lam_init)
    o_ref[...] = (o * _silu(g_ref[...].astype(F32))).astype(o_ref.dtype)


def diff_attention(proj_q, proj_c, proj_l, rope_tabs, lambda_qk, subln_w, lam_init, nq_rows):
    lat = proj_l is not None
    tq = 256
    nq = nq_rows // tq
    kern = functools.partial(_diffattn_kernel, lam_init=lam_init, tq=tq,
                             n_lat=SEQ if lat else 0, rope=lat, kchunk=512)
    head = lambda col: (lambda b, h, i: (b, col + h))
    qmap = lambda col: (lambda b, h, i: (b * nq + i, col + h))
    const = lambda b, h, i: (0, 0)
    in_specs = [pl.BlockSpec((tq, HB), qmap(COL_QA)),
                pl.BlockSpec((CTX_LEN, HB), head(COL_KA)),
                pl.BlockSpec((CTX_LEN, HB), head(COL_VA))]
    args = [proj_q, proj_c, proj_c]
    if lat:
        in_specs += [pl.BlockSpec((SEQ, HB), head(COL_KA)), pl.BlockSpec((SEQ, HB), head(COL_VA))]
        args += [proj_l, proj_l]
    in_specs += [pl.BlockSpec((tq, HB), qmap(COL_GA))]
    args += [proj_q]
    if lat:
        in_specs += [pl.BlockSpec((SEQ, HB), const)] * 3
        args += list(rope_tabs)
    in_specs += [pl.BlockSpec((4, A_QK_DIM), const), pl.BlockSpec((1, HB), const)]
    args += [lambda_qk, subln_w.reshape(1, HB)]
    return pl.pallas_call(
        kern,
        out_shape=jax.ShapeDtypeStruct((BATCH * nq_rows, A_WIDTH), BF16),
        grid=(BATCH, A_HEADS, nq),
        in_specs=in_specs,
        out_specs=pl.BlockSpec((tq, HB), lambda b, h, i: (b * nq + i, h)),
        scratch_shapes=[pltpu.VMEM((SEQ, HB), BF16)] if lat else [],
        compiler_params=_params(("arbitrary", "arbitrary", "arbitrary")),
        name="diff_attention_latent" if lat else "diff_attention_context",
    )(*args)


def _nbr_group_start(g):
    return jnp.clip(g * NB_QROWS - NA_KH // 2, 0, GRID_H - NB_KROWS)


def _nbr_kernel(q_ref, kc_ref, vc_ref, kl_ref, vl_ref, g_ref, bias_ref, o_ref):
    g = pl.program_id(2)
    pat = jnp.where(g == 0, 0, jnp.where(g == NB_GROUPS - 1, 2, 1))
    r0 = pl.multiple_of(_nbr_group_start(g) * GRID_W, GRID_W)
    scale = HEAD_DIM ** -0.5
    q = q_ref[...]
    k = kl_ref[pl.ds(r0, NB_K), :]
    v = vl_ref[pl.ds(r0, NB_K), :]
    s_ctx = lax.dot_general(q, kc_ref[...], _NT, preferred_element_type=F32) * scale
    s_nb = lax.dot_general(q, k, _NT, preferred_element_type=F32) * scale + bias_ref[0, pat]
    m = jnp.maximum(jnp.max(s_ctx, axis=-1, keepdims=True), jnp.max(s_nb, axis=-1, keepdims=True))
    p_ctx = jnp.exp(s_ctx - m)
    p_nb = jnp.exp(s_nb - m)
    inv = 1.0 / (jnp.sum(p_ctx, axis=-1, keepdims=True) + jnp.sum(p_nb, axis=-1, keepdims=True))
    o = (jnp.dot((p_ctx * inv).astype(BF16), vc_ref[...], preferred_element_type=F32)
         + jnp.dot((p_nb * inv).astype(BF16), v, preferred_element_type=F32))
    o_ref[...] = (o * _silu(g_ref[...].astype(F32))).astype(o_ref.dtype)


def neighbourhood_bias_table(rpb):
    w = GRID_W
    cq = np.arange(w)
    c0 = np.clip(cq - NA_KW // 2, 0, w - NA_KW)
    ck = np.arange(w)
    col_valid = (ck[None, :] >= c0[:, None]) & (ck[None, :] < c0[:, None] + NA_KW)
    dc_idx = np.clip(ck[None, :] - cq[:, None] + NA_KW - 1, 0, 2 * NA_KW - 2)
    tabs = []
    for g in (0, 1, NB_GROUPS - 1):
        slab0 = int(np.clip(g * NB_QROWS - NA_KH // 2, 0, GRID_H - NB_KROWS))
        qr = g * NB_QROWS + np.arange(NB_QROWS)
        kr = slab0 + np.arange(NB_KROWS)
        win0 = np.clip(qr - NA_KH // 2, 0, GRID_H - NA_KH)
        row_valid = (kr[None, :] >= win0[:, None]) & (kr[None, :] < win0[:, None] + NA_KH)
        dr_idx = np.clip(kr[None, :] - qr[:, None] + NA_KH - 1, 0, 2 * NA_KH - 2)
        valid = row_valid[:, None, :, None] & col_valid[None, :, None, :]
        bias = rpb[:, dr_idx[:, None, :, None], dc_idx[None, :, None, :]].astype(F32)
        bias = jnp.where(valid[None], bias, NEG_INF)
        tabs.append(bias.reshape(B_HEADS, NB_Q, NB_K))
    return jnp.stack(tabs, axis=1)


def neighbourhood_attention(proj_l, proj_c, bias_tab):
    head = lambda col: (lambda h, b, g: (b, col + h))
    qmap = lambda col: (lambda h, b, g: (b * NB_GROUPS + g, col + h))
    return pl.pallas_call(
        _nbr_kernel,
        out_shape=jax.ShapeDtypeStruct((BATCH * SEQ, B_WIDTH), BF16),
        grid=(B_HEADS, BATCH, NB_GROUPS),
        in_specs=[
            pl.BlockSpec((NB_Q, HB), qmap(COL_QB)),
            pl.BlockSpec((CTX_LEN, HB), head(COL_KB)),
            pl.BlockSpec((CTX_LEN, HB), head(COL_VB)),
            pl.BlockSpec((SEQ, HB), head(COL_KB)),
            pl.BlockSpec((SEQ, HB), head(COL_VB)),
            pl.BlockSpec((NB_Q, HB), qmap(COL_GB)),
            pl.BlockSpec((1, 3, NB_Q, NB_K), lambda h, b, g: (h, 0, 0, 0)),
        ],
        out_specs=pl.BlockSpec((NB_Q, HB), lambda h, b, g: (b * NB_GROUPS + g, h)),
        compiler_params=_params(("arbitrary", "arbitrary", "arbitrary")),
        name="neighbourhood_attention",
    )(proj_l, proj_c, proj_c, proj_l, proj_l, proj_l, bias_tab)


def _dense_ctx_kernel(q_ref, k_ref, v_ref, g_ref, o_ref):
    scale = HEAD_DIM ** -0.5
    s = lax.dot_general(q_ref[...], k_ref[...], _NT, preferred_element_type=F32) * scale
    p = jnp.exp(s - jnp.max(s, axis=-1, keepdims=True))
    p = p * (1.0 / jnp.sum(p, axis=-1, keepdims=True))
    o = jnp.dot(p.astype(BF16), v_ref[...], preferred_element_type=F32)
    o_ref[...] = (o * _silu(g_ref[...].astype(F32))).astype(o_ref.dtype)


def dense_context_attention(proj_c):
    head = lambda col: (lambda b, h: (b, col + h))
    return pl.pallas_call(
        _dense_ctx_kernel,
        out_shape=jax.ShapeDtypeStruct((BATCH * CTX_LEN, B_WIDTH), BF16),
        grid=(BATCH, B_HEADS),
        in_specs=[pl.BlockSpec((CTX_LEN, HB), head(COL_QB)),
                  pl.BlockSpec((CTX_LEN, HB), head(COL_KB)),
                  pl.BlockSpec((CTX_LEN, HB), head(COL_VB)),
                  pl.BlockSpec((CTX_LEN, HB), head(COL_GB))],
        out_specs=pl.BlockSpec((CTX_LEN, HB), lambda b, h: (b, h)),
        compiler_params=_params(("arbitrary", "arbitrary")),
        name="dense_context_attention",
    )(proj_c, proj_c, proj_c, proj_c)


def _shift_rows(x, d, fill):
    n = x.shape[0]
    rolled = pltpu.roll(x, d % n, axis=0)
    row = lax.broadcasted_iota(jnp.int32, x.shape, 0)
    keep = (row >= d) if d > 0 else (row < n + d)
    return jnp.where(keep, rolled, fill)


def _dwconv(x, w_ref, b_ref):
    acc = b_ref[...] + w_ref[CONV_LEFT:CONV_LEFT + 1, :] * x
    for j in range(CONV_W):
        if j != CONV_LEFT:
            acc = acc + w_ref[j:j + 1, :] * _shift_rows(x, CONV_LEFT - j, 0.0)
    return acc


def _linear_scan(a, b, h0, reverse):
    n = a.shape[0]
    row = lax.broadcasted_iota(jnp.int32, a.shape, 0)
    first = (row == n - 1) if reverse else (row == 0)
    b = b + jnp.where(first, a * h0, 0.0)
    d = 1
    while d < n:
        sd = -d if reverse else d
        b = b + a * _shift_rows(b, sd, 0.0)
        if 2 * d < n:
            a = a * _shift_rows(a, sd, 1.0)
        d *= 2
    return b


def _expm1_from_exp(x, ex):
    em1 = ex - 1.0
    tiny = em1 == 0.0
    return jnp.where(tiny, x, em1 * x / jnp.where(tiny, 1.0, jnp.log(ex)))


def _rglru_gates(u, wa_ref, ba_ref, wx_ref, bx_ref, lam_ref, d):
    ub = u.astype(BF16)
    r = _sigmoid(jnp.dot(ub, wa_ref[0, d], preferred_element_type=F32) + ba_ref[0, d])
    i = _sigmoid(jnp.dot(ub, wx_ref[0, d], preferred_element_type=F32) + bx_ref[0, d])
    neg_lam = -lam_ref[0, d]
    softplus = jnp.maximum(neg_lam, 0.0) + jnp.log1p(jnp.exp(-jnp.abs(neg_lam)))
    log_a = (-RGLRU_C) * r * softplus
    a = jnp.exp(log_a)
    b = jnp.sqrt(-_expm1_from_exp(2.0 * log_a, a * a)) * (i * u)
    return a, b


def _rglru_kernel(xl_ref, xc_ref, gl_ref, gc_ref, cw_ref, cb_ref, wa_ref, ba_ref, wx_ref, bx_ref,
                  lam_ref, ol_ref, *maybe_oc_ref):
    u_l = _dwconv(xl_ref[...].astype(F32), cw_ref, cb_ref)
    u_c = _dwconv(xc_ref[...].astype(F32), cw_ref, cb_ref)
    y_l = None
    y_c = None
    for d in range(2):
        rev = d == 1
        a_c, b_c = _rglru_gates(u_c, wa_ref, ba_ref, wx_ref, bx_ref, lam_ref, d)
        h_c = _linear_scan(a_c, b_c, jnp.zeros((1, a_c.shape[1]), F32), rev)
        fin = h_c[0:1, :] if rev else h_c[CTX_LEN - 1:CTX_LEN, :]
        a_l, b_l = _rglru_gates(u_l, wa_ref, ba_ref, wx_ref, bx_ref, lam_ref, d)
        h_l = _linear_scan(a_l, b_l, fin, rev)
        y_l = h_l if y_l is None else y_l + h_l
        y_c = h_c if y_c is None else y_c + h_c
    ol_ref[...] = (y_l * _silu(gl_ref[...].astype(F32))).astype(ol_ref.dtype)
    if maybe_oc_ref:
        maybe_oc_ref[0][...] = (y_c * _silu(gc_ref[...].astype(F32))).astype(ol_ref.dtype)


def _block_diag(w, per_tile):
    n = w.shape[-1]
    tiles = C_BLOCKS // per_tile
    w = w.reshape(2, tiles, per_tile, n, n)
    eye = jnp.eye(per_tile, dtype=w.dtype)
    bd = jnp.einsum('dtpij,pq->dtpiqj', w, eye).reshape(2, tiles, per_tile * n, per_tile * n)
    return jnp.transpose(bd, (1, 0, 2, 3)).astype(BF16)


def rglru_mixer(proj_l, proj_c, conv_w, conv_b, rg_wa, rg_ba, rg_wx, rg_bx, rg_lambda, need_ctx):
    ct = 128
    nt = C_WIDTH // ct
    per_tile = ct // C_BLOCK_DIM
    vec = lambda p: jnp.transpose(p.reshape(2, nt, 1, ct), (1, 0, 2, 3))
    wa = _block_diag(rg_wa, per_tile)
    wx = _block_diag(rg_wx, per_tile)
    xcol = lambda col: (lambda b, t: (b, col + t))
    par = lambda b, t: (t, 0, 0, 0)
    out_shape = [jax.ShapeDtypeStruct((BATCH * SEQ, C_WIDTH), BF16)]
    out_specs = [pl.BlockSpec((SEQ, ct), lambda b, t: (b, t))]
    if need_ctx:
        out_shape.append(jax.ShapeDtypeStruct((BATCH * CTX_LEN, C_WIDTH), BF16))
        out_specs.append(pl.BlockSpec((CTX_LEN, ct), lambda b, t: (b, t)))
    outs = pl.pallas_call(
        _rglru_kernel,
        out_shape=out_shape,
        grid=(BATCH, nt),
        in_specs=[
            pl.BlockSpec((SEQ, ct), xcol(COL_XC)),
            pl.BlockSpec((CTX_LEN, ct), xcol(COL_XC)),
            pl.BlockSpec((SEQ, ct), xcol(COL_GC)),
            pl.BlockSpec((CTX_LEN, ct), xcol(COL_GC)),
            pl.BlockSpec((CONV_W, ct), lambda b, t: (0, t)),
            pl.BlockSpec((1, ct), lambda b, t: (0, t)),
            pl.BlockSpec((1, 2, ct, ct), par),
            pl.BlockSpec((1, 2, 1, ct), par),
            pl.BlockSpec((1, 2, ct, ct), par),
            pl.BlockSpec((1, 2, 1, ct), par),
            pl.BlockSpec((1, 2, 1, ct), par),
        ],
        out_specs=out_specs,
        compiler_params=_params(("arbitrary", "arbitrary")),
        name="rglru_mixer",
    )(proj_l, proj_c, proj_l, proj_c, conv_w, conv_b.reshape(1, C_WIDTH),
      wa, vec(rg_ba), wx, vec(rg_bx), vec(rg_lambda))
    return outs if need_ctx else (outs[0], None)


def _outproj_kernel(x_ref, gate_ref, ma_ref, mb_ref, mc_ref, wa_ref, wb_ref, wc_ref, o_ref):
    y = jnp.dot(ma_ref[...], wa_ref[...], preferred_element_type=F32)
    y = y + jnp.dot(mb_ref[...], wb_ref[...], preferred_element_type=F32)
    y = y + jnp.dot(mc_ref[...], wc_ref[...], preferred_element_type=F32)
    o_ref[...] = x_ref[...] + gate_ref[0] * y


def output_projection(x2d, mod3, ma, mb, mc, w_bf16, rows_per_mod, mod_row0):
    m = x2d.shape[0]
    tm, tn = 512, 512

    def mod_row(i):
        return mod_row0 + (i * tm) // rows_per_mod

    return pl.pallas_call(
        _outproj_kernel,
        out_shape=jax.ShapeDtypeStruct((m, D_MODEL), F32),
        grid=(m // tm, D_MODEL // tn),
        in_specs=[
            pl.BlockSpec((tm, tn), lambda i, j: (i, j)),
            pl.BlockSpec((1, 1, tn), lambda i, j: (mod_row(i), 0, 2 * (D_MODEL // tn) + j)),
            pl.BlockSpec((tm, A_WIDTH), lambda i, j: (i, 0)),
            pl.BlockSpec((tm, B_WIDTH), lambda i, j: (i, 0)),
            pl.BlockSpec((tm, C_WIDTH), lambda i, j: (i, 0)),
            pl.BlockSpec((A_WIDTH, tn), lambda i, j: (0, j)),
            pl.BlockSpec((B_WIDTH, tn), lambda i, j: (1, j)),
            pl.BlockSpec((C_WIDTH, tn), lambda i, j: ((A_WIDTH + B_WIDTH) // C_WIDTH, j)),
        ],
        out_specs=pl.BlockSpec((tm, tn), lambda i, j: (i, j)),
        compiler_params=_params(("arbitrary", "arbitrary")),
        name="output_projection",
    )(x2d, mod3, ma, mb, mc, w_bf16, w_bf16, w_bf16)


def _final_norm_kernel(x_ref, w_ref, o_ref):
    x = x_ref[...]
    o_ref[...] = x * lax.rsqrt(jnp.mean(x * x, axis=-1, keepdims=True) + NORM_EPS) * w_ref[...]


def final_norm(x2d, w):
    tm = 128
    return pl.pallas_call(
        _final_norm_kernel,
        out_shape=jax.ShapeDtypeStruct(x2d.shape, F32),
        grid=(x2d.shape[0] // tm,),
        in_specs=[pl.BlockSpec((tm, D_MODEL), lambda i: (i, 0)),
                  pl.BlockSpec((1, D_MODEL), lambda i: (0, 0))],
        out_specs=pl.BlockSpec((tm, D_MODEL), lambda i: (i, 0)),
        compiler_params=_params(("arbitrary",)),
        name="final_norm",
    )(x2d, w.reshape(1, D_MODEL))


def _rope_tables():
    t = np.arange(SEQ)
    pos = np.stack([t // GRID_W, t % GRID_W], axis=0).astype(np.float32)
    lane = np.arange(HEAD_DIM)
    within = lane % A_QK_DIM
    axis = within // (A_QK_DIM // 2)
    k = within % (A_QK_DIM // 2)
    nfreq = A_QK_DIM // 4
    upper = k >= nfreq
    freqs = jnp.asarray(ROPE_THETA, F32) ** (-jnp.arange(nfreq, dtype=F32) / nfreq)
    ang = jnp.asarray(pos)[axis, :].T * freqs[k % nfreq][None, :]
    cos, sin = jnp.cos(ang), jnp.sin(ang)
    sin_lo = jnp.where(upper[None, :], 0.0, -sin)
    sin_hi = jnp.where(upper[None, :], sin, 0.0)
    return cos, sin_lo, sin_hi


def kernel(x, c, ctx, c_ctx, ada_w, ada_b, norm_w, w_in, w_out, lambda_qk, subln_w, rpb,
           conv_w, conv_b, rg_wa, rg_ba, rg_wx, rg_bx, rg_lambda, final_norm_w):
    cvec = jnp.zeros((MOD_ROWS, D_MODEL), F32).at[:BATCH].set(c).at[CTX_MOD_ROW].set(c_ctx)
    mod = adaln_modulation(cvec, ada_w, ada_b)
    rope_tabs = _rope_tables()
    xl = x.reshape(BATCH * SEQ, D_MODEL)
    xc = ctx.reshape(BATCH * CTX_LEN, D_MODEL)
    for l in range(DEPTH):
        need_ctx = l < DEPTH - 1
        lam_init = 0.8 - 0.6 * math.exp(-0.3 * l)
        mod3 = mod[l].reshape(MOD_ROWS, 1, 3 * D_MODEL)
        w_in_l = w_in[l].astype(BF16)
        w_out_l = w_out[l].astype(BF16)
        proj_l = input_projection(xl, norm_w[l], mod3, w_in_l, SEQ, 0)
        proj_c = input_projection(xc, norm_w[l], mod3, w_in_l, BATCH * CTX_LEN, CTX_MOD_ROW)

        ma = diff_attention(proj_l, proj_c, proj_l, rope_tabs, lambda_qk[l], subln_w[l], lam_init, SEQ)
        mb = neighbourhood_attention(proj_l, proj_c, neighbourhood_bias_table(rpb[l]))
        mc, mc_c = rglru_mixer(proj_l, proj_c, conv_w[l], conv_b[l], rg_wa[l], rg_ba[l],
                               rg_wx[l], rg_bx[l], rg_lambda[l], need_ctx)
        xl = output_projection(xl, mod3, ma, mb, mc, w_out_l, SEQ, 0)
        if need_ctx:
            ma_c = diff_attention(proj_c, proj_c, None, None, lambda_qk[l], subln_w[l], lam_init, CTX_LEN)
            mb_c = dense_context_attention(proj_c)
            xc = output_projection(xc, mod3, ma_c, mb_c, mc_c, w_out_l, BATCH * CTX_LEN, CTX_MOD_ROW)
    return final_norm(xl, final_norm_w).reshape(BATCH, SEQ, D_MODEL)
```

```python
import functools
import math

import numpy as np
import jax
import jax.numpy as jnp
from jax import lax
from jax.experimental import pallas as pl
from jax.experimental.pallas import tpu as pltpu

F32 = jnp.float32
BF16 = jnp.bfloat16

D_MODEL = 4096
BATCH = 8
SEQ = 2048
DEPTH = 2
GRID_W = 64
GRID_H = SEQ // GRID_W
CTX_LEN = 256
HEAD_DIM = 128
A_WIDTH = (3 * D_MODEL) // 8
A_HEADS = A_WIDTH // HEAD_DIM
A_QK_DIM = HEAD_DIM // 2
B_WIDTH = (3 * D_MODEL) // 8
B_HEADS = B_WIDTH // HEAD_DIM
C_WIDTH = D_MODEL - A_WIDTH - B_WIDTH
C_BLOCKS = 16
C_BLOCK_DIM = C_WIDTH // C_BLOCKS
MIX_WIDTH = A_WIDTH + B_WIDTH + C_WIDTH
IN_WIDTH = 4 * A_WIDTH + 4 * B_WIDTH + 2 * C_WIDTH
NA_KH = 8
NA_KW = 16
ROPE_THETA = 10000.0
RGLRU_C = 8.0
CONV_W = 4
CONV_LEFT = 2
NORM_EPS = 1e-6
SUBLN_EPS = 1e-5
NEG_INF = -1e30

HB = HEAD_DIM
COL_QA, COL_KA, COL_VA, COL_GA = 0, A_HEADS, 2 * A_HEADS, 3 * A_HEADS
COL_QB = 4 * A_HEADS
COL_KB, COL_VB, COL_GB = COL_QB + B_HEADS, COL_QB + 2 * B_HEADS, COL_QB + 3 * B_HEADS
COL_XC = COL_QB + 4 * B_HEADS
COL_GC = COL_XC + C_WIDTH // HB

MOD_ROWS = 16
CTX_MOD_ROW = BATCH

VMEM_LIMIT = 56 * 1024 * 1024

NB_QROWS = 4
NB_KROWS = 12
NB_GROUPS = GRID_H // NB_QROWS
NB_Q = NB_QROWS * GRID_W
NB_K = NB_KROWS * GRID_W


def _params(sem):
    return pltpu.CompilerParams(dimension_semantics=sem, vmem_limit_bytes=VMEM_LIMIT)


def _sigmoid(x):
    return 1.0 / (1.0 + jnp.exp(-x))


def _silu(x):
    return x * _sigmoid(x)


def _adaln_kernel(c_ref, w_ref, b_ref, o_ref):
    s = _silu(c_ref[...]).astype(BF16)
    w = w_ref[0].astype(BF16)
    o_ref[0] = jnp.dot(s, w, preferred_element_type=F32) + b_ref[0]


def adaln_modulation(cvec, ada_w, ada_b):
    tn = 512
    n3 = 3 * D_MODEL
    return pl.pallas_call(
        _adaln_kernel,
        out_shape=jax.ShapeDtypeStruct((DEPTH, MOD_ROWS, n3), F32),
        grid=(DEPTH, n3 // tn),
        in_specs=[
            pl.BlockSpec((MOD_ROWS, D_MODEL), lambda l, j: (0, 0)),
            pl.BlockSpec((1, D_MODEL, tn), lambda l, j: (l, 0, j)),
            pl.BlockSpec((1, 1, tn), lambda l, j: (l, 0, j)),
        ],
        out_specs=pl.BlockSpec((1, MOD_ROWS, tn), lambda l, j: (l, 0, j)),
        compiler_params=_params(("arbitrary", "arbitrary")),
        name="adaln_modulation",
    )(cvec, ada_w, ada_b.reshape(DEPTH, 1, n3))


_NORM_CHUNK = 32


def _inproj_kernel(x_ref, nw_ref, shift_ref, scale_ref, w_ref, o_ref, hx_ref):
    @pl.when(pl.program_id(1) == 0)
    def _():
        nw = nw_ref[...]
        mul = 1.0 + scale_ref[0]
        add = shift_ref[0]

        def body(r, carry):
            rows = pl.ds(pl.multiple_of(r * _NORM_CHUNK, _NORM_CHUNK), _NORM_CHUNK)
            x = x_ref[rows, :]
            ms = jnp.mean(x * x, axis=-1, keepdims=True)
            y = x * lax.rsqrt(ms + NORM_EPS) * nw
            hx_ref[rows, :] = (y * mul + add).astype(BF16)
            return carry

        lax.fori_loop(0, hx_ref.shape[0] // _NORM_CHUNK, body, 0)

    o_ref[...] = jnp.dot(hx_ref[...], w_ref[...], preferred_element_type=F32).astype(o_ref.dtype)


def input_projection(x2d, norm_w3, mod3, w_bf16, layer, rows_per_mod, mod_row0):
    m = x2d.shape[0]
    tm, tn = 512, 1024

    def mod_row(i):
        return mod_row0 + (i * tm) // rows_per_mod

    return pl.pallas_call(
        _inproj_kernel,
        out_shape=jax.ShapeDtypeStruct((m, IN_WIDTH), BF16),
        grid=(m // tm, IN_WIDTH // tn),
        in_specs=[
            pl.BlockSpec((tm, D_MODEL), lambda i, j: (i, 0)),
            pl.BlockSpec((None, 1, D_MODEL), lambda i, j: (layer, 0, 0)),
            pl.BlockSpec((1, 1, D_MODEL), lambda i, j: (mod_row(i), 0, 0)),
            pl.BlockSpec((1, 1, D_MODEL), lambda i, j: (mod_row(i), 0, 1)),
            pl.BlockSpec((None, D_MODEL, tn), lambda i, j: (layer, 0, j)),
        ],
        out_specs=pl.BlockSpec((tm, tn), lambda i, j: (i, j)),
        scratch_shapes=[pltpu.VMEM((tm, D_MODEL), BF16)],
        compiler_params=_params(("arbitrary", "arbitrary")),
        name="input_projection",
    )(x2d, norm_w3, mod3, mod3, w_bf16)


def _rope(x, cos, sin_lo, sin_hi):
    return (x * cos + pltpu.roll(x, HEAD_DIM - 16, axis=1) * sin_lo
            + pltpu.roll(x, 16, axis=1) * sin_hi)


def _diff_lambda(lq_ref, lam_init):
    lq = lq_ref[...]
    a = jnp.sum(lq[0:1] * lq[1:2], axis=-1, keepdims=True)
    b = jnp.sum(lq[2:3] * lq[3:4], axis=-1, keepdims=True)
    return jnp.exp(a) - jnp.exp(b) + lam_init


LOG2E = 1.4426950408889634
NKEYS = CTX_LEN + SEQ
DA_TQ = 256
DA_W = 2 * DA_TQ
DA_ROWS = 16


def _transpose_bf16(x):
    return x.astype(F32).T.astype(BF16)


def _diffattn_lat_kernel(q_ref, kc_ref, vc_ref, kl_ref, vl_ref, g_ref, cos_ref, slo_ref, shi_ref,
                         lq_ref, sw_ref, o_ref, k_scr, vt_scr, qt_scr, s_scr, p_scr, *, lam_init):
    qi = pl.program_id(2)

    @pl.when(qi == 0)
    def _():
        k_scr[0:CTX_LEN, :] = kc_ref[...]
        vt_scr[:, 0:CTX_LEN] = _transpose_bf16(vc_ref[...])
        dim = lax.broadcasted_iota(jnp.int32, (HB, DA_TQ), 0)
        for c in range(SEQ // DA_TQ):
            rows = slice(c * DA_TQ, (c + 1) * DA_TQ)
            dst = slice(CTX_LEN + c * DA_TQ, CTX_LEN + (c + 1) * DA_TQ)
            cos, slo, shi = cos_ref[rows, :], slo_ref[rows, :], shi_ref[rows, :]
            k_scr[dst, :] = _rope(kl_ref[rows, :].astype(F32), cos, slo, shi).astype(BF16)
            vt_scr[:, dst] = _transpose_bf16(vl_ref[rows, :])
            q = _rope(q_ref[rows, :].astype(F32), cos, slo, shi)
            qt = (q * (A_QK_DIM ** -0.5 * LOG2E)).T
            qt_scr[:, c * DA_W:c * DA_W + DA_TQ] = jnp.where(dim < A_QK_DIM, qt, 0.0).astype(BF16)
            qt_scr[:, c * DA_W + DA_TQ:(c + 1) * DA_W] = jnp.where(dim >= A_QK_DIM, qt, 0.0).astype(BF16)

    qt2 = qt_scr[:, pl.ds(pl.multiple_of(qi * DA_W, DA_W), DA_W)]
    s_scr[...] = jnp.dot(k_scr[...], qt2, preferred_element_type=F32)

    n_groups = NKEYS // DA_ROWS
    m_acc = [s_scr[0:DA_ROWS, :], s_scr[DA_ROWS:2 * DA_ROWS, :]]
    for r in range(2, n_groups):
        m_acc[r % 2] = jnp.maximum(m_acc[r % 2], s_scr[r * DA_ROWS:(r + 1) * DA_ROWS, :])
    m = jnp.max(jnp.maximum(m_acc[0], m_acc[1]), axis=0, keepdims=True)
    mb = jnp.broadcast_to(m, (DA_ROWS, DA_W))
    l_acc = [None, None]
    for r in range(n_groups):
        grp = slice(r * DA_ROWS, (r + 1) * DA_ROWS)
        p = jnp.exp2(s_scr[grp, :] - mb)
        p_scr[grp, :] = p.astype(BF16)
        l_acc[r % 2] = p if l_acc[r % 2] is None else l_acc[r % 2] + p
    l = jnp.sum(l_acc[0] + l_acc[1], axis=0, keepdims=True)
    acc = jnp.dot(vt_scr[...], p_scr[...], preferred_element_type=F32)

    lam = _diff_lambda(lq_ref, lam_init)
    ot = acc[:, :DA_TQ] * (1.0 / l[:, :DA_TQ]) - acc[:, DA_TQ:] * (lam / l[:, DA_TQ:])
    ot = ot * lax.rsqrt(jnp.mean(ot * ot, axis=0, keepdims=True) + SUBLN_EPS)
    o = ot.T * sw_ref[...] * (1.0 - lam_init)
    o_ref[...] = (o * _silu(g_ref[...].astype(F32))).astype(o_ref.dtype)


_NT = (((1,), (1,)), ((), ()))


def _diffattn_ctx_kernel(q_ref, k_ref, v_ref, g_ref, lq_ref, sw_ref, o_ref, *, lam_init):
    q = q_ref[...].astype(F32) * (A_QK_DIM ** -0.5)
    lane = lax.broadcasted_iota(jnp.int32, q.shape, 1)
    outs = []
    for qn in (jnp.where(lane < A_QK_DIM, q, 0.0), jnp.where(lane >= A_QK_DIM, q, 0.0)):
        s = lax.dot_general(qn.astype(BF16), k_ref[...], _NT, preferred_element_type=F32)
        p = jnp.exp(s - jnp.max(s, axis=-1, keepdims=True))
        acc = jnp.dot(p.astype(BF16), v_ref[...], preferred_element_type=F32)
        outs.append((acc, jnp.sum(p, axis=-1, keepdims=True)))
    lam = _diff_lambda(lq_ref, lam_init)
    o = outs[0][0] * (1.0 / outs[0][1]) - (lam / outs[1][1]) * outs[1][0]
    o = o * lax.rsqrt(jnp.mean(o * o, axis=-1, keepdims=True) + SUBLN_EPS) * sw_ref[...]
    o = o * (1.0 - lam_init)
    o_ref[...] = (o * _silu(g_ref[...].astype(F32))).astype(o_ref.dtype)


def diff_attention_latent(proj_l, proj_c, rope_tabs, lambda_qk, subln_w, lam_init):
    nq = SEQ // DA_TQ
    head = lambda col: (lambda b, h, i: (b, col + h))
    qmap = lambda col: (lambda b, h, i: (b * nq + i, col + h))
    const = lambda b, h, i: (0, 0)
    return pl.pallas_call(
        functools.partial(_diffattn_lat_kernel, lam_init=lam_init),
        out_shape=jax.ShapeDtypeStruct((BATCH * SEQ, A_WIDTH), BF16),
        grid=(BATCH, A_HEADS, nq),
        in_specs=[
            pl.BlockSpec((SEQ, HB), head(COL_QA)),
            pl.BlockSpec((CTX_LEN, HB), head(COL_KA)),
            pl.BlockSpec((CTX_LEN, HB), head(COL_VA)),
            pl.BlockSpec((SEQ, HB), head(COL_KA)),
            pl.BlockSpec((SEQ, HB), head(COL_VA)),
            pl.BlockSpec((DA_TQ, HB), qmap(COL_GA)),
            pl.BlockSpec((SEQ, HB), const),
            pl.BlockSpec((SEQ, HB), const),
            pl.BlockSpec((SEQ, HB), const),
            pl.BlockSpec((4, A_QK_DIM), const),
            pl.BlockSpec((1, HB), const),
        ],
        out_specs=pl.BlockSpec((DA_TQ, HB), lambda b, h, i: (b * nq + i, h)),
        scratch_shapes=[pltpu.VMEM((NKEYS, HB), BF16), pltpu.VMEM((HB, NKEYS), BF16),
                        pltpu.VMEM((HB, 2 * SEQ), BF16),
                        pltpu.VMEM((NKEYS, DA_W), F32), pltpu.VMEM((NKEYS, DA_W), BF16)],
        compiler_params=_params(("arbitrary", "arbitrary", "arbitrary")),
        name="diff_attention_latent",
    )(proj_l, proj_c, proj_c, proj_l, proj_l, proj_l, *rope_tabs, lambda_qk, subln_w.reshape(1, HB))


def diff_attention_context(proj_c, lambda_qk, subln_w, lam_init):
    head = lambda col: (lambda b, h: (b, col + h))
    const = lambda b, h: (0, 0)
    return pl.pallas_call(
        functools.partial(_diffattn_ctx_kernel, lam_init=lam_init),
        out_shape=jax.ShapeDtypeStruct((BATCH * CTX_LEN, A_WIDTH), BF16),
        grid=(BATCH, A_HEADS),
        in_specs=[pl.BlockSpec((CTX_LEN, HB), head(COL_QA)),
                  pl.BlockSpec((CTX_LEN, HB), head(COL_KA)),
                  pl.BlockSpec((CTX_LEN, HB), head(COL_VA)),
                  pl.BlockSpec((CTX_LEN, HB), head(COL_GA)),
                  pl.BlockSpec((4, A_QK_DIM), const),
                  pl.BlockSpec((1, HB), const)],
        out_specs=pl.BlockSpec((CTX_LEN, HB), lambda b, h: (b, h)),
        compiler_params=_params(("arbitrary", "arbitrary")),
        name="diff_attention_context",
    )(proj_c, proj_c, proj_c, proj_c, lambda_qk, subln_w.reshape(1, HB))


def _nbr_group_start(g):
    return jnp.clip(g * NB_QROWS - NA_KH // 2, 0, GRID_H - NB_KROWS)


def _nbr_kernel(q_ref, kc_ref, vc_ref, kl_ref, vl_ref, g_ref, bias_ref, o_ref):
    g = pl.program_id(2)
    pat = jnp.where(g == 0, 0, jnp.where(g == NB_GROUPS - 1, 2, 1))
    r0 = pl.multiple_of(_nbr_group_start(g) * GRID_W, GRID_W)
    scale = HEAD_DIM ** -0.5
    q = q_ref[...]
    k = kl_ref[pl.ds(r0, NB_K), :]
    v = vl_ref[pl.ds(r0, NB_K), :]
    s_ctx = lax.dot_general(q, kc_ref[...], _NT, preferred_element_type=F32) * scale
    s_nb = lax.dot_general(q, k, _NT, preferred_element_type=F32) * scale + bias_ref[0, pat]
    m = jnp.maximum(jnp.max(s_ctx, axis=-1, keepdims=True), jnp.max(s_nb, axis=-1, keepdims=True))
    p_ctx = jnp.exp(s_ctx - m)
    p_nb = jnp.exp(s_nb - m)
    inv = 1.0 / (jnp.sum(p_ctx, axis=-1, keepdims=True) + jnp.sum(p_nb, axis=-1, keepdims=True))
    o = (jnp.dot((p_ctx * inv).astype(BF16), vc_ref[...], preferred_element_type=F32)
         + jnp.dot((p_nb * inv).astype(BF16), v, preferred_element_type=F32))
    o_ref[...] = (o * _silu(g_ref[...].astype(F32))).astype(o_ref.dtype)


def neighbourhood_bias_table(rpb):
    w = GRID_W
    cq = np.arange(w)
    c0 = np.clip(cq - NA_KW // 2, 0, w - NA_KW)
    ck = np.arange(w)
    col_valid = (ck[None, :] >= c0[:, None]) & (ck[None, :] < c0[:, None] + NA_KW)
    pad = w - NA_KW
    padded = jnp.pad(rpb.astype(F32), ((0, 0), (0, 0), (pad, pad)))
    toep = jnp.stack([padded[:, :, w - 1 - q:2 * w - 1 - q] for q in range(w)], axis=2)
    toep = jnp.where(col_valid[None, None], toep, NEG_INF)
    masked = jnp.full((B_HEADS, w, w), NEG_INF, F32)
    tabs = []
    for g in (0, 1, NB_GROUPS - 1):
        slab0 = int(np.clip(g * NB_QROWS - NA_KH // 2, 0, GRID_H - NB_KROWS))
        qr = g * NB_QROWS + np.arange(NB_QROWS)
        kr = slab0 + np.arange(NB_KROWS)
        win0 = np.clip(qr - NA_KH // 2, 0, GRID_H - NA_KH)
        row_valid = (kr[None, :] >= win0[:, None]) & (kr[None, :] < win0[:, None] + NA_KH)
        dr_idx = kr[None, :] - qr[:, None] + NA_KH - 1
        rows = [jnp.concatenate([toep[:, int(dr_idx[a, k])] if row_valid[a, k] else masked
                                 for k in range(NB_KROWS)], axis=-1) for a in range(NB_QROWS)]
        tabs.append(jnp.concatenate(rows, axis=1))
    return jnp.stack(tabs, axis=1)


def neighbourhood_attention(proj_l, proj_c, bias_tab):
    head = lambda col: (lambda h, b, g: (b, col + h))
    qmap = lambda col: (lambda h, b, g: (b * NB_GROUPS + g, col + h))
    return pl.pallas_call(
        _nbr_kernel,
        out_shape=jax.ShapeDtypeStruct((BATCH * SEQ, B_WIDTH), BF16),
        grid=(B_HEADS, BATCH, NB_GROUPS),
        in_specs=[
            pl.BlockSpec((NB_Q, HB), qmap(COL_QB)),
            pl.BlockSpec((CTX_LEN, HB), head(COL_KB)),
            pl.BlockSpec((CTX_LEN, HB), head(COL_VB)),
            pl.BlockSpec((SEQ, HB), head(COL_KB)),
            pl.BlockSpec((SEQ, HB), head(COL_VB)),
            pl.BlockSpec((NB_Q, HB), qmap(COL_GB)),
            pl.BlockSpec((1, 3, NB_Q, NB_K), lambda h, b, g: (h, 0, 0, 0)),
        ],
        out_specs=pl.BlockSpec((NB_Q, HB), lambda h, b, g: (b * NB_GROUPS + g, h)),
        compiler_params=_params(("arbitrary", "arbitrary", "arbitrary")),
        name="neighbourhood_attention",
    )(proj_l, proj_c, proj_c, proj_l, proj_l, proj_l, bias_tab)


def _dense_ctx_kernel(q_ref, k_ref, v_ref, g_ref, o_ref):
    scale = HEAD_DIM ** -0.5
    s = lax.dot_general(q_ref[...], k_ref[...], _NT, preferred_element_type=F32) * scale
    p = jnp.exp(s - jnp.max(s, axis=-1, keepdims=True))
    p = p * (1.0 / jnp.sum(p, axis=-1, keepdims=True))
    o = jnp.dot(p.astype(BF16), v_ref[...], preferred_element_type=F32)
    o_ref[...] = (o * _silu(g_ref[...].astype(F32))).astype(o_ref.dtype)


def dense_context_attention(proj_c):
    head = lambda col: (lambda b, h: (b, col + h))
    return pl.pallas_call(
        _dense_ctx_kernel,
        out_shape=jax.ShapeDtypeStruct((BATCH * CTX_LEN, B_WIDTH), BF16),
        grid=(BATCH, B_HEADS),
        in_specs=[pl.BlockSpec((CTX_LEN, HB), head(COL_QB)),
                  pl.BlockSpec((CTX_LEN, HB), head(COL_KB)),
                  pl.BlockSpec((CTX_LEN, HB), head(COL_VB)),
                  pl.BlockSpec((CTX_LEN, HB), head(COL_GB))],
        out_specs=pl.BlockSpec((CTX_LEN, HB), lambda b, h: (b, h)),
        compiler_params=_params(("arbitrary", "arbitrary")),
        name="dense_context_attention",
    )(proj_c, proj_c, proj_c, proj_c)


def _shift_rows(x, d, fill):
    n = x.shape[0]
    rolled = pltpu.roll(x, d % n, axis=0)
    row = lax.broadcasted_iota(jnp.int32, x.shape, 0)
    keep = (row >= d) if d > 0 else (row < n + d)
    return jnp.where(keep, rolled, fill)


def _dwconv(x, w_ref, b_ref):
    acc = b_ref[...] + w_ref[CONV_LEFT:CONV_LEFT + 1, :] * x
    for j in range(CONV_W):
        if j != CONV_LEFT:
            acc = acc + w_ref[j:j + 1, :] * _shift_rows(x, CONV_LEFT - j, 0.0)
    return acc


def _linear_scan(a, b, h0, reverse):
    n = a.shape[0]
    row = lax.broadcasted_iota(jnp.int32, a.shape, 0)
    first = (row == n - 1) if reverse else (row == 0)
    b = b + jnp.where(first, a * h0, 0.0)
    d = 1
    while d < n:
        sd = -d if reverse else d
        b = b + a * _shift_rows(b, sd, 0.0)
        if 2 * d < n:
            a = a * _shift_rows(a, sd, 1.0)
        d *= 2
    return b


def _expm1_from_exp(x, ex):
    em1 = ex - 1.0
    tiny = em1 == 0.0
    return jnp.where(tiny, x, em1 * x / jnp.where(tiny, 1.0, jnp.log(ex)))


def _rglru_gates(u, wa_ref, ba_ref, wx_ref, bx_ref, lam_ref, d):
    ub = u.astype(BF16)
    r = _sigmoid(jnp.dot(ub, wa_ref[0, d], preferred_element_type=F32) + ba_ref[0, d])
    i = _sigmoid(jnp.dot(ub, wx_ref[0, d], preferred_element_type=F32) + bx_ref[0, d])
    neg_lam = -lam_ref[0, d]
    softplus = jnp.maximum(neg_lam, 0.0) + jnp.log1p(jnp.exp(-jnp.abs(neg_lam)))
    log_a = (-RGLRU_C) * r * softplus
    a = jnp.exp(log_a)
    b = jnp.sqrt(-_expm1_from_exp(2.0 * log_a, a * a)) * (i * u)
    return a, b


def _rglru_kernel(xl_ref, xc_ref, gl_ref, gc_ref, cw_ref, cb_ref, wa_ref, ba_ref, wx_ref, bx_ref,
                  lam_ref, ol_ref, *maybe_oc_ref):
    u_l = _dwconv(xl_ref[...].astype(F32), cw_ref, cb_ref)
    u_c = _dwconv(xc_ref[...].astype(F32), cw_ref, cb_ref)
    y_l = None
    y_c = None
    for d in range(2):
        rev = d == 1
        a_c, b_c = _rglru_gates(u_c, wa_ref, ba_ref, wx_ref, bx_ref, lam_ref, d)
        h_c = _linear_scan(a_c, b_c, jnp.zeros((1, a_c.shape[1]), F32), rev)
        fin = h_c[0:1, :] if rev else h_c[CTX_LEN - 1:CTX_LEN, :]
        a_l, b_l = _rglru_gates(u_l, wa_ref, ba_ref, wx_ref, bx_ref, lam_ref, d)
        h_l = _linear_scan(a_l, b_l, fin, rev)
        y_l = h_l if y_l is None else y_l + h_l
        y_c = h_c if y_c is None else y_c + h_c
    ol_ref[...] = (y_l * _silu(gl_ref[...].astype(F32))).astype(ol_ref.dtype)
    if maybe_oc_ref:
        maybe_oc_ref[0][...] = (y_c * _silu(gc_ref[...].astype(F32))).astype(ol_ref.dtype)


def _block_diag(w, per_tile):
    n = w.shape[-1]
    tiles = C_BLOCKS // per_tile
    w = w.reshape(2, tiles, per_tile, n, n)
    eye = jnp.eye(per_tile, dtype=w.dtype)
    bd = jnp.einsum('dtpij,pq->dtpiqj', w, eye).reshape(2, tiles, per_tile * n, per_tile * n)
    return jnp.transpose(bd, (1, 0, 2, 3)).astype(BF16)


def rglru_mixer(proj_l, proj_c, conv_w, conv_b, rg_wa, rg_ba, rg_wx, rg_bx, rg_lambda, need_ctx):
    ct = 128
    nt = C_WIDTH // ct
    per_tile = ct // C_BLOCK_DIM
    vec = lambda p: jnp.transpose(p.reshape(2, nt, 1, ct), (1, 0, 2, 3))
    wa = _block_diag(rg_wa, per_tile)
    wx = _block_diag(rg_wx, per_tile)
    xcol = lambda col: (lambda b, t: (b, col + t))
    par = lambda b, t: (t, 0, 0, 0)
    out_shape = [jax.ShapeDtypeStruct((BATCH * SEQ, C_WIDTH), BF16)]
    out_specs = [pl.BlockSpec((SEQ, ct), lambda b, t: (b, t))]
    if need_ctx:
        out_shape.append(jax.ShapeDtypeStruct((BATCH * CTX_LEN, C_WIDTH), BF16))
        out_specs.append(pl.BlockSpec((CTX_LEN, ct), lambda b, t: (b, t)))
    outs = pl.pallas_call(
        _rglru_kernel,
        out_shape=out_shape,
        grid=(BATCH, nt),
        in_specs=[
            pl.BlockSpec((SEQ, ct), xcol(COL_XC)),
            pl.BlockSpec((CTX_LEN, ct), xcol(COL_XC)),
            pl.BlockSpec((SEQ, ct), xcol(COL_GC)),
            pl.BlockSpec((CTX_LEN, ct), xcol(COL_GC)),
            pl.BlockSpec((CONV_W, ct), lambda b, t: (0, t)),
            pl.BlockSpec((1, ct), lambda b, t: (0, t)),
            pl.BlockSpec((1, 2, ct, ct), par),
            pl.BlockSpec((1, 2, 1, ct), par),
            pl.BlockSpec((1, 2, ct, ct), par),
            pl.BlockSpec((1, 2, 1, ct), par),
            pl.BlockSpec((1, 2, 1, ct), par),
        ],
        out_specs=out_specs,
        compiler_params=_params(("arbitrary", "arbitrary")),
        name="rglru_mixer",
    )(proj_l, proj_c, proj_l, proj_c, conv_w, conv_b.reshape(1, C_WIDTH),
      wa, vec(rg_ba), wx, vec(rg_bx), vec(rg_lambda))
    return outs if need_ctx else (outs[0], None)


def _outproj_kernel(x_ref, gate_ref, ma_ref, mb_ref, mc_ref, wa_ref, wb_ref, wc_ref, o_ref):
    y = jnp.dot(ma_ref[...], wa_ref[...], preferred_element_type=F32)
    y = y + jnp.dot(mb_ref[...], wb_ref[...], preferred_element_type=F32)
    y = y + jnp.dot(mc_ref[...], wc_ref[...], preferred_element_type=F32)
    o_ref[...] = x_ref[...] + gate_ref[0] * y


def output_projection(x2d, mod3, ma, mb, mc, w_bf16, layer, rows_per_mod, mod_row0):
    m = x2d.shape[0]
    tm, tn = 512, 512

    def mod_row(i):
        return mod_row0 + (i * tm) // rows_per_mod

    return pl.pallas_call(
        _outproj_kernel,
        out_shape=jax.ShapeDtypeStruct((m, D_MODEL), F32),
        grid=(m // tm, D_MODEL // tn),
        in_specs=[
            pl.BlockSpec((tm, tn), lambda i, j: (i, j)),
            pl.BlockSpec((1, 1, tn), lambda i, j: (mod_row(i), 0, 2 * (D_MODEL // tn) + j)),
            pl.BlockSpec((tm, A_WIDTH), lambda i, j: (i, 0)),
            pl.BlockSpec((tm, B_WIDTH), lambda i, j: (i, 0)),
            pl.BlockSpec((tm, C_WIDTH), lambda i, j: (i, 0)),
            pl.BlockSpec((None, A_WIDTH, tn), lambda i, j: (layer, 0, j)),
            pl.BlockSpec((None, B_WIDTH, tn), lambda i, j: (layer, 1, j)),
            pl.BlockSpec((None, C_WIDTH, tn), lambda i, j: (layer, (A_WIDTH + B_WIDTH) // C_WIDTH, j)),
        ],
        out_specs=pl.BlockSpec((tm, tn), lambda i, j: (i, j)),
        compiler_params=_params(("arbitrary", "arbitrary")),
        name="output_projection",
    )(x2d, mod3, ma, mb, mc, w_bf16, w_bf16, w_bf16)


def _final_norm_kernel(x_ref, w_ref, o_ref):
    x = x_ref[...]
    o_ref[...] = x * lax.rsqrt(jnp.mean(x * x, axis=-1, keepdims=True) + NORM_EPS) * w_ref[...]


def final_norm(x2d, w):
    tm = 128
    return pl.pallas_call(
        _final_norm_kernel,
        out_shape=jax.ShapeDtypeStruct(x2d.shape, F32),
        grid=(x2d.shape[0] // tm,),
        in_specs=[pl.BlockSpec((tm, D_MODEL), lambda i: (i, 0)),
                  pl.BlockSpec((1, D_MODEL), lambda i: (0, 0))],
        out_specs=pl.BlockSpec((tm, D_MODEL), lambda i: (i, 0)),
        compiler_params=_params(("arbitrary",)),
        name="final_norm",
    )(x2d, w.reshape(1, D_MODEL))


def _rope_tables():
    t = np.arange(SEQ)
    pos = np.stack([t // GRID_W, t % GRID_W], axis=0).astype(np.float32)
    lane = np.arange(HEAD_DIM)
    within = lane % A_QK_DIM
    axis = within // (A_QK_DIM // 2)
    k = within % (A_QK_DIM // 2)
    nfreq = A_QK_DIM // 4
    upper = k >= nfreq
    freqs = jnp.asarray(ROPE_THETA, F32) ** (-jnp.arange(nfreq, dtype=F32) / nfreq)
    ang = jnp.asarray(pos)[axis, :].T * freqs[k % nfreq][None, :]
    cos, sin = jnp.cos(ang), jnp.sin(ang)
    sin_lo = jnp.where(upper[None, :], 0.0, -sin)
    sin_hi = jnp.where(upper[None, :], sin, 0.0)
    return cos, sin_lo, sin_hi


def kernel(x, c, ctx, c_ctx, ada_w, ada_b, norm_w, w_in, w_out, lambda_qk, subln_w, rpb,
           conv_w, conv_b, rg_wa, rg_ba, rg_wx, rg_bx, rg_lambda, final_norm_w):
    cvec = jnp.zeros((MOD_ROWS, D_MODEL), F32).at[:BATCH].set(c).at[CTX_MOD_ROW].set(c_ctx)
    mod = adaln_modulation(cvec, ada_w, ada_b)
    mod3 = mod.reshape(DEPTH * MOD_ROWS, 1, 3 * D_MODEL)
    norm_w3 = norm_w.reshape(DEPTH, 1, D_MODEL)
    w_in_bf = w_in.astype(BF16)
    w_out_bf = w_out.astype(BF16)
    rope_tabs = _rope_tables()
    xl = x.reshape(BATCH * SEQ, D_MODEL)
    xc = ctx.reshape(BATCH * CTX_LEN, D_MODEL)
    for l in range(DEPTH):
        need_ctx = l < DEPTH - 1
        lam_init = 0.8 - 0.6 * math.exp(-0.3 * l)
        row_l, row_c = l * MOD_ROWS, l * MOD_ROWS + CTX_MOD_ROW
        proj_l = input_projection(xl, norm_w3, mod3, w_in_bf, l, SEQ, row_l)
        proj_c = input_projection(xc, norm_w3, mod3, w_in_bf, l, BATCH * CTX_LEN, row_c)

        ma = diff_attention_latent(proj_l, proj_c, rope_tabs, lambda_qk[l], subln_w[l], lam_init)
        mb = neighbourhood_attention(proj_l, proj_c, neighbourhood_bias_table(rpb[l]))
        mc, mc_c = rglru_mixer(proj_l, proj_c, conv_w[l], conv_b[l], rg_wa[l], rg_ba[l],
                               rg_wx[l], rg_bx[l], rg_lambda[l], need_ctx)
        xl = output_projection(xl, mod3, ma, mb, mc, w_out_bf, l, SEQ, row_l)
        if need_ctx:
            ma_c = diff_attention_context(proj_c, lambda_qk[l], subln_w[l], lam_init)
            mb_c = dense_context_attention(proj_c)
            xc = output_projection(xc, mod3, ma_c, mb_c, mc_c, w_out_bf, l, BATCH * CTX_LEN, row_c)
    return final_norm(xl, final_norm_w).reshape(BATCH, SEQ, D_MODEL)
```

```python
import functools
import math

import numpy as np
import jax
import jax.numpy as jnp
from jax import lax
from jax.experimental import pallas as pl
from jax.experimental.pallas import tpu as pltpu

F32 = jnp.float32
BF16 = jnp.bfloat16

D_MODEL = 4096
BATCH = 8
SEQ = 2048
DEPTH = 2
GRID_W = 64
GRID_H = SEQ // GRID_W
CTX_LEN = 256
HEAD_DIM = 128
A_WIDTH = (3 * D_MODEL) // 8
A_HEADS = A_WIDTH // HEAD_DIM
A_QK_DIM = HEAD_DIM // 2
B_WIDTH = (3 * D_MODEL) // 8
B_HEADS = B_WIDTH // HEAD_DIM
C_WIDTH = D_MODEL - A_WIDTH - B_WIDTH
C_BLOCKS = 16
C_BLOCK_DIM = C_WIDTH // C_BLOCKS
MIX_WIDTH = A_WIDTH + B_WIDTH + C_WIDTH
IN_WIDTH = 4 * A_WIDTH + 4 * B_WIDTH + 2 * C_WIDTH
NA_KH = 8
NA_KW = 16
ROPE_THETA = 10000.0
RGLRU_C = 8.0
CONV_W = 4
CONV_LEFT = 2
NORM_EPS = 1e-6
SUBLN_EPS = 1e-5
NEG_INF = -1e30

HB = HEAD_DIM
COL_QA, COL_KA, COL_VA, COL_GA = 0, A_HEADS, 2 * A_HEADS, 3 * A_HEADS
COL_QB = 4 * A_HEADS
COL_KB, COL_VB, COL_GB = COL_QB + B_HEADS, COL_QB + 2 * B_HEADS, COL_QB + 3 * B_HEADS
COL_XC = COL_QB + 4 * B_HEADS
COL_GC = COL_XC + C_WIDTH // HB

MOD_ROWS = 16
CTX_MOD_ROW = BATCH

VMEM_LIMIT = 56 * 1024 * 1024

NB_QROWS = 4
NB_KROWS = 12
NB_GROUPS = GRID_H // NB_QROWS
NB_Q = NB_QROWS * GRID_W
NB_K = NB_KROWS * GRID_W


def _params(sem):
    return pltpu.CompilerParams(dimension_semantics=sem, vmem_limit_bytes=VMEM_LIMIT)


def _sigmoid(x):
    return 1.0 / (1.0 + jnp.exp(-x))


def _silu(x):
    return x * _sigmoid(x)


def _adaln_kernel(c_ref, w_ref, b_ref, o_ref):
    s = _silu(c_ref[...]).astype(BF16)
    w = w_ref[0].astype(BF16)
    o_ref[0] = jnp.dot(s, w, preferred_element_type=F32) + b_ref[0]


def adaln_modulation(cvec, ada_w, ada_b):
    tn = 512
    n3 = 3 * D_MODEL
    return pl.pallas_call(
        _adaln_kernel,
        out_shape=jax.ShapeDtypeStruct((DEPTH, MOD_ROWS, n3), F32),
        grid=(DEPTH, n3 // tn),
        in_specs=[
            pl.BlockSpec((MOD_ROWS, D_MODEL), lambda l, j: (0, 0)),
            pl.BlockSpec((1, D_MODEL, tn), lambda l, j: (l, 0, j)),
            pl.BlockSpec((1, 1, tn), lambda l, j: (l, 0, j)),
        ],
        out_specs=pl.BlockSpec((1, MOD_ROWS, tn), lambda l, j: (l, 0, j)),
        compiler_params=_params(("arbitrary", "arbitrary")),
        name="adaln_modulation",
    )(cvec, ada_w, ada_b.reshape(DEPTH, 1, n3))


_NORM_CHUNK = 32


def _inproj_kernel(x_ref, nw_ref, shift_ref, scale_ref, w_ref, o_ref, hx_ref):
    @pl.when(pl.program_id(1) == 0)
    def _():
        nw = nw_ref[...]
        mul = 1.0 + scale_ref[0]
        add = shift_ref[0]

        def body(r, carry):
            rows = pl.ds(pl.multiple_of(r * _NORM_CHUNK, _NORM_CHUNK), _NORM_CHUNK)
            x = x_ref[rows, :]
            ms = jnp.mean(x * x, axis=-1, keepdims=True)
            y = x * lax.rsqrt(ms + NORM_EPS) * nw
            hx_ref[rows, :] = (y * mul + add).astype(BF16)
            return carry

        lax.fori_loop(0, hx_ref.shape[0] // _NORM_CHUNK, body, 0)

    o_ref[...] = jnp.dot(hx_ref[...], w_ref[...], preferred_element_type=F32).astype(o_ref.dtype)


def input_projection(x2d, norm_w3, mod3, w_bf16, layer, rows_per_mod, mod_row0):
    m = x2d.shape[0]
    tm, tn = 512, 1024

    def mod_row(i):
        return mod_row0 + (i * tm) // rows_per_mod

    return pl.pallas_call(
        _inproj_kernel,
        out_shape=jax.ShapeDtypeStruct((m, IN_WIDTH), BF16),
        grid=(m // tm, IN_WIDTH // tn),
        in_specs=[
            pl.BlockSpec((tm, D_MODEL), lambda i, j: (i, 0)),
            pl.BlockSpec((None, 1, D_MODEL), lambda i, j: (layer, 0, 0)),
            pl.BlockSpec((1, 1, D_MODEL), lambda i, j: (mod_row(i), 0, 0)),
            pl.BlockSpec((1, 1, D_MODEL), lambda i, j: (mod_row(i), 0, 1)),
            pl.BlockSpec((None, D_MODEL, tn), lambda i, j: (layer, 0, j)),
        ],
        out_specs=pl.BlockSpec((tm, tn), lambda i, j: (i, j)),
        scratch_shapes=[pltpu.VMEM((tm, D_MODEL), BF16)],
        compiler_params=_params(("arbitrary", "arbitrary")),
        name="input_projection",
    )(x2d, norm_w3, mod3, mod3, w_bf16)


def _rope(x, cos, sin_lo, sin_hi):
    return (x * cos + pltpu.roll(x, HEAD_DIM - 16, axis=1) * sin_lo
            + pltpu.roll(x, 16, axis=1) * sin_hi)


def _diff_lambda(lq_ref, lam_init):
    lq = lq_ref[...]
    a = jnp.sum(lq[0:1] * lq[1:2], axis=-1, keepdims=True)
    b = jnp.sum(lq[2:3] * lq[3:4], axis=-1, keepdims=True)
    return jnp.exp(a) - jnp.exp(b) + lam_init


LOG2E = 1.4426950408889634
NKEYS = CTX_LEN + SEQ
DA_TQ = 256
DA_W = 2 * DA_TQ
DA_ROWS = 16
DA_KCHUNK = 256


def _transpose_bf16(x):
    return x.astype(F32).T.astype(BF16)


def _diffattn_lat_kernel(q_ref, kc_ref, vc_ref, kl_ref, vl_ref, g_ref, cos_ref, slo_ref, shi_ref,
                         lq_ref, sw_ref, o_ref, k_scr, vt_scr, qt_scr, s0_scr, s1_scr, m_scr, p_scr,
                         *, lam_init):
    qi = pl.program_id(2)
    n_tiles = SEQ // DA_TQ
    n_chunks = NKEYS // DA_KCHUNK
    groups_per_chunk = DA_KCHUNK // DA_ROWS

    def score_chunk(s_scr, qt2, c, m_run):
        rows = slice(c * DA_KCHUNK, (c + 1) * DA_KCHUNK)
        s = jnp.dot(k_scr[rows, :], qt2, preferred_element_type=F32)
        s_scr[rows, :] = s
        parts = [s[i * DA_ROWS:(i + 1) * DA_ROWS, :] for i in range(groups_per_chunk)]
        if m_run is not None:
            parts.append(m_run)
        while len(parts) > 1:
            parts = [jnp.maximum(parts[i], parts[i + 1]) for i in range(0, len(parts) - 1, 2)] + (
                [parts[-1]] if len(parts) % 2 else [])
        return parts[0]

    def store_colmax(m_run):
        m_scr[...] = jnp.broadcast_to(jnp.max(m_run, axis=0, keepdims=True), (DA_ROWS, DA_W))

    @pl.when(qi == 0)
    def _():
        k_scr[0:CTX_LEN, :] = kc_ref[...]
        vt_scr[:, 0:CTX_LEN] = _transpose_bf16(vc_ref[...])
        dim = lax.broadcasted_iota(jnp.int32, (HB, DA_TQ), 0)
        for c in range(SEQ // DA_TQ):
            rows = slice(c * DA_TQ, (c + 1) * DA_TQ)
            dst = slice(CTX_LEN + c * DA_TQ, CTX_LEN + (c + 1) * DA_TQ)
            cos, slo, shi = cos_ref[rows, :], slo_ref[rows, :], shi_ref[rows, :]
            k_scr[dst, :] = _rope(kl_ref[rows, :].astype(F32), cos, slo, shi).astype(BF16)
            vt_scr[:, dst] = _transpose_bf16(vl_ref[rows, :])
            q = _rope(q_ref[rows, :].astype(F32), cos, slo, shi)
            qt = (q * (A_QK_DIM ** -0.5 * LOG2E)).T
            qt_scr[:, c * DA_W:c * DA_W + DA_TQ] = jnp.where(dim < A_QK_DIM, qt, 0.0).astype(BF16)
            qt_scr[:, c * DA_W + DA_TQ:(c + 1) * DA_W] = jnp.where(dim >= A_QK_DIM, qt, 0.0).astype(BF16)
        m_run = None
        for c in range(n_chunks):
            m_run = score_chunk(s0_scr, qt_scr[:, 0:DA_W], c, m_run)
        store_colmax(m_run)

    def tile(s_scr, s_next_scr):
        nxt = jnp.where(qi == n_tiles - 1, 0, qi + 1)
        qt_next = qt_scr[:, pl.ds(pl.multiple_of(nxt * DA_W, DA_W), DA_W)]
        mb = m_scr[...]
        m_run = None
        l_acc = [None, None]
        acc = None
        for c in range(n_chunks):
            rows = slice(c * DA_KCHUNK, (c + 1) * DA_KCHUNK)
            m_run = score_chunk(s_next_scr, qt_next, c, m_run)
            for r in range(c * groups_per_chunk, (c + 1) * groups_per_chunk):
                grp = slice(r * DA_ROWS, (r + 1) * DA_ROWS)
                p = jnp.exp2(s_scr[grp, :] - mb)
                p_scr[grp, :] = p.astype(BF16)
                l_acc[r % 2] = p if l_acc[r % 2] is None else l_acc[r % 2] + p
            part = jnp.dot(vt_scr[:, rows], p_scr[rows, :], preferred_element_type=F32)
            acc = part if acc is None else acc + part
        store_colmax(m_run)
        l = jnp.sum(l_acc[0] + l_acc[1], axis=0, keepdims=True)

        lam = _diff_lambda(lq_ref, lam_init)
        ot = acc[:, :DA_TQ] * (1.0 / l[:, :DA_TQ]) - acc[:, DA_TQ:] * (lam / l[:, DA_TQ:])
        ot = ot * lax.rsqrt(jnp.mean(ot * ot, axis=0, keepdims=True) + SUBLN_EPS)
        o = ot.T * sw_ref[...] * (1.0 - lam_init)
        o_ref[...] = (o * _silu(g_ref[...].astype(F32))).astype(o_ref.dtype)

    parity = qi % 2
    pl.when(parity == 0)(functools.partial(tile, s0_scr, s1_scr))
    pl.when(parity == 1)(functools.partial(tile, s1_scr, s0_scr))


_NT = (((1,), (1,)), ((), ()))


def _diffattn_ctx_kernel(q_ref, k_ref, v_ref, g_ref, lq_ref, sw_ref, o_ref, *, lam_init):
    q = q_ref[...].astype(F32) * (A_QK_DIM ** -0.5)
    lane = lax.broadcasted_iota(jnp.int32, q.shape, 1)
    outs = []
    for qn in (jnp.where(lane < A_QK_DIM, q, 0.0), jnp.where(lane >= A_QK_DIM, q, 0.0)):
        s = lax.dot_general(qn.astype(BF16), k_ref[...], _NT, preferred_element_type=F32)
        p = jnp.exp(s - jnp.max(s, axis=-1, keepdims=True))
        acc = jnp.dot(p.astype(BF16), v_ref[...], preferred_element_type=F32)
        outs.append((acc, jnp.sum(p, axis=-1, keepdims=True)))
    lam = _diff_lambda(lq_ref, lam_init)
    o = outs[0][0] * (1.0 / outs[0][1]) - (lam / outs[1][1]) * outs[1][0]
    o = o * lax.rsqrt(jnp.mean(o * o, axis=-1, keepdims=True) + SUBLN_EPS) * sw_ref[...]
    o = o * (1.0 - lam_init)
    o_ref[...] = (o * _silu(g_ref[...].astype(F32))).astype(o_ref.dtype)


def diff_attention_latent(proj_l, proj_c, rope_tabs, lambda_qk, subln_w, lam_init):
    nq = SEQ // DA_TQ
    head = lambda col: (lambda b, h, i: (b, col + h))
    qmap = lambda col: (lambda b, h, i: (b * nq + i, col + h))
    const = lambda b, h, i: (0, 0)
    return pl.pallas_call(
        functools.partial(_diffattn_lat_kernel, lam_init=lam_init),
        out_shape=jax.ShapeDtypeStruct((BATCH * SEQ, A_WIDTH), BF16),
        grid=(BATCH, A_HEADS, nq),
        in_specs=[
            pl.BlockSpec((SEQ, HB), head(COL_QA)),
            pl.BlockSpec((CTX_LEN, HB), head(COL_KA)),
            pl.BlockSpec((CTX_LEN, HB), head(COL_VA)),
            pl.BlockSpec((SEQ, HB), head(COL_KA)),
            pl.BlockSpec((SEQ, HB), head(COL_VA)),
            pl.BlockSpec((DA_TQ, HB), qmap(COL_GA)),
            pl.BlockSpec((SEQ, HB), const),
            pl.BlockSpec((SEQ, HB), const),
            pl.BlockSpec((SEQ, HB), const),
            pl.BlockSpec((4, A_QK_DIM), const),
            pl.BlockSpec((1, HB), const),
        ],
        out_specs=pl.BlockSpec((DA_TQ, HB), lambda b, h, i: (b * nq + i, h)),
        scratch_shapes=[pltpu.VMEM((NKEYS, HB), BF16), pltpu.VMEM((HB, NKEYS), BF16),
                        pltpu.VMEM((HB, 2 * SEQ), BF16),
                        pltpu.VMEM((NKEYS, DA_W), F32), pltpu.VMEM((NKEYS, DA_W), F32),
                        pltpu.VMEM((DA_ROWS, DA_W), F32), pltpu.VMEM((NKEYS, DA_W), BF16)],
        compiler_params=_params(("arbitrary", "arbitrary", "arbitrary")),
        name="diff_attention_latent",
    )(proj_l, proj_c, proj_c, proj_l, proj_l, proj_l, *rope_tabs, lambda_qk, subln_w.reshape(1, HB))


def diff_attention_context(proj_c, lambda_qk, subln_w, lam_init):
    head = lambda col: (lambda b, h: (b, col + h))
    const = lambda b, h: (0, 0)
    return pl.pallas_call(
        functools.partial(_diffattn_ctx_kernel, lam_init=lam_init),
        out_shape=jax.ShapeDtypeStruct((BATCH * CTX_LEN, A_WIDTH), BF16),
        grid=(BATCH, A_HEADS),
        in_specs=[pl.BlockSpec((CTX_LEN, HB), head(COL_QA)),
                  pl.BlockSpec((CTX_LEN, HB), head(COL_KA)),
                  pl.BlockSpec((CTX_LEN, HB), head(COL_VA)),
                  pl.BlockSpec((CTX_LEN, HB), head(COL_GA)),
                  pl.BlockSpec((4, A_QK_DIM), const),
                  pl.BlockSpec((1, HB), const)],
        out_specs=pl.BlockSpec((CTX_LEN, HB), lambda b, h: (b, h)),
        compiler_params=_params(("arbitrary", "arbitrary")),
        name="diff_attention_context",
    )(proj_c, proj_c, proj_c, proj_c, lambda_qk, subln_w.reshape(1, HB))


def _nbr_slab_start(g):
    return min(max(g * NB_QROWS - NA_KH // 2, 0), GRID_H - NB_KROWS)


def _nbr_pattern(g):
    return 0 if g == 0 else (2 if g == NB_GROUPS - 1 else 1)


def _nbr_kernel(q_ref, kc_ref, vc_ref, kl_ref, vl_ref, g_ref, bias_ref, o_ref, vt_scr):
    vt_scr[:, 0:CTX_LEN] = _transpose_bf16(vc_ref[...])
    for c in range(SEQ // NB_Q):
        vt_scr[:, CTX_LEN + c * NB_Q:CTX_LEN + (c + 1) * NB_Q] = _transpose_bf16(vl_ref[c * NB_Q:(c + 1) * NB_Q, :])
    for g in range(NB_GROUPS):
        rows = slice(g * NB_Q, (g + 1) * NB_Q)
        k0 = _nbr_slab_start(g) * GRID_W
        qt = (q_ref[rows, :].astype(F32) * (HEAD_DIM ** -0.5 * LOG2E)).T.astype(BF16)
        s_c = jnp.dot(kc_ref[...], qt, preferred_element_type=F32)
        s_n = (jnp.dot(kl_ref[k0:k0 + NB_K, :], qt, preferred_element_type=F32)
               + bias_ref[0, _nbr_pattern(g)])
        m = jnp.maximum(jnp.max(s_c, axis=0, keepdims=True), jnp.max(s_n, axis=0, keepdims=True))
        p_c = jnp.exp2(s_c - m)
        p_n = jnp.exp2(s_n - m)
        l = jnp.sum(p_c, axis=0, keepdims=True) + jnp.sum(p_n, axis=0, keepdims=True)
        acc = (jnp.dot(vt_scr[:, 0:CTX_LEN], p_c.astype(BF16), preferred_element_type=F32)
               + jnp.dot(vt_scr[:, CTX_LEN + k0:CTX_LEN + k0 + NB_K], p_n.astype(BF16),
                         preferred_element_type=F32))
        o = (acc * (1.0 / l)).T
        o_ref[rows, :] = (o * _silu(g_ref[rows, :].astype(F32))).astype(o_ref.dtype)


def neighbourhood_bias_table(rpb):
    w = GRID_W
    cq = np.arange(w)
    c0 = np.clip(cq - NA_KW // 2, 0, w - NA_KW)
    ck = np.arange(w)
    col_valid = (ck[None, :] >= c0[:, None]) & (ck[None, :] < c0[:, None] + NA_KW)
    pad = w - NA_KW
    padded = jnp.pad(rpb.astype(F32), ((0, 0), (0, 0), (pad, pad)))
    toep = jnp.stack([padded[:, :, w - 1 - q:2 * w - 1 - q] for q in range(w)], axis=2)
    toep = jnp.where(col_valid[None, None], toep, NEG_INF)
    toep_t = jnp.swapaxes(toep, 2, 3) * LOG2E
    masked = jnp.full((B_HEADS, w, w), NEG_INF, F32)
    tabs = []
    for g in (0, 1, NB_GROUPS - 1):
        slab0 = _nbr_slab_start(g)
        qr = g * NB_QROWS + np.arange(NB_QROWS)
        kr = slab0 + np.arange(NB_KROWS)
        win0 = np.clip(qr - NA_KH // 2, 0, GRID_H - NA_KH)
        row_valid = (kr[None, :] >= win0[:, None]) & (kr[None, :] < win0[:, None] + NA_KH)
        dr_idx = kr[None, :] - qr[:, None] + NA_KH - 1
        rows = [jnp.concatenate([toep_t[:, int(dr_idx[a, k])] if row_valid[a, k] else masked
                                 for a in range(NB_QROWS)], axis=-1) for k in range(NB_KROWS)]
        tabs.append(jnp.concatenate(rows, axis=1))
    return jnp.stack(tabs, axis=1)


def neighbourhood_attention(proj_l, proj_c, bias_tab):
    head = lambda col: (lambda h, b: (b, col + h))
    return pl.pallas_call(
        _nbr_kernel,
        out_shape=jax.ShapeDtypeStruct((BATCH * SEQ, B_WIDTH), BF16),
        grid=(B_HEADS, BATCH),
        in_specs=[
            pl.BlockSpec((SEQ, HB), head(COL_QB)),
            pl.BlockSpec((CTX_LEN, HB), head(COL_KB)),
            pl.BlockSpec((CTX_LEN, HB), head(COL_VB)),
            pl.BlockSpec((SEQ, HB), head(COL_KB)),
            pl.BlockSpec((SEQ, HB), head(COL_VB)),
            pl.BlockSpec((SEQ, HB), head(COL_GB)),
            pl.BlockSpec((1, 3, NB_K, NB_Q), lambda h, b: (h, 0, 0, 0)),
        ],
        out_specs=pl.BlockSpec((SEQ, HB), lambda h, b: (b, h)),
        scratch_shapes=[pltpu.VMEM((HB, NKEYS), BF16)],
        compiler_params=_params(("arbitrary", "arbitrary")),
        name="neighbourhood_attention",
    )(proj_l, proj_c, proj_c, proj_l, proj_l, proj_l, bias_tab)


def _dense_ctx_kernel(q_ref, k_ref, v_ref, g_ref, o_ref):
    scale = HEAD_DIM ** -0.5
    s = lax.dot_general(q_ref[...], k_ref[...], _NT, preferred_element_type=F32) * scale
    p = jnp.exp(s - jnp.max(s, axis=-1, keepdims=True))
    p = p * (1.0 / jnp.sum(p, axis=-1, keepdims=True))
    o = jnp.dot(p.astype(BF16), v_ref[...], preferred_element_type=F32)
    o_ref[...] = (o * _silu(g_ref[...].astype(F32))).astype(o_ref.dtype)


def dense_context_attention(proj_c):
    head = lambda col: (lambda b, h: (b, col + h))
    return pl.pallas_call(
        _dense_ctx_kernel,
        out_shape=jax.ShapeDtypeStruct((BATCH * CTX_LEN, B_WIDTH), BF16),
        grid=(BATCH, B_HEADS),
        in_specs=[pl.BlockSpec((CTX_LEN, HB), head(COL_QB)),
                  pl.BlockSpec((CTX_LEN, HB), head(COL_KB)),
                  pl.BlockSpec((CTX_LEN, HB), head(COL_VB)),
                  pl.BlockSpec((CTX_LEN, HB), head(COL_GB))],
        out_specs=pl.BlockSpec((CTX_LEN, HB), lambda b, h: (b, h)),
        compiler_params=_params(("arbitrary", "arbitrary")),
        name="dense_context_attention",
    )(proj_c, proj_c, proj_c, proj_c)


def _shift_rows(x, d, fill):
    n = x.shape[0]
    rolled = pltpu.roll(x, d % n, axis=0)
    row = lax.broadcasted_iota(jnp.int32, x.shape, 0)
    keep = (row >= d) if d > 0 else (row < n + d)
    return jnp.where(keep, rolled, fill)


def _dwconv(x, w_ref, b_ref):
    acc = b_ref[...] + w_ref[CONV_LEFT:CONV_LEFT + 1, :] * x
    for j in range(CONV_W):
        if j != CONV_LEFT:
            acc = acc + w_ref[j:j + 1, :] * _shift_rows(x, CONV_LEFT - j, 0.0)
    return acc


def _linear_scan(a, b, h0, reverse):
    n = a.shape[0]
    row = lax.broadcasted_iota(jnp.int32, a.shape, 0)
    first = (row == n - 1) if reverse else (row == 0)
    b = b + jnp.where(first, a * h0, 0.0)
    d = 1
    while d < n:
        sd = -d if reverse else d
        b = b + a * _shift_rows(b, sd, 0.0)
        if 2 * d < n:
            a = a * _shift_rows(a, sd, 1.0)
        d *= 2
    return b


def _sigmoid_tanh(x):
    return 0.5 * jnp.tanh(0.5 * x) + 0.5


def _rglru_gates(u, wa_ref, ba_ref, wx_ref, bx_ref, lam_ref, d):
    ub = u.astype(BF16)
    r = _sigmoid_tanh(jnp.dot(ub, wa_ref[0, d], preferred_element_type=F32) + ba_ref[0, d])
    i = _sigmoid_tanh(jnp.dot(ub, wx_ref[0, d], preferred_element_type=F32) + bx_ref[0, d])
    neg_lam = -lam_ref[0, d]
    softplus = jnp.maximum(neg_lam, 0.0) + jnp.log1p(jnp.exp(-jnp.abs(neg_lam)))
    log_a = (-RGLRU_C) * r * softplus
    a = jnp.exp(log_a)
    t = jnp.tanh(log_a)
    b = jnp.sqrt(-2.0 * t / (1.0 - t)) * (i * u)
    return a, b


def _rglru_kernel(xl_ref, xc_ref, gl_ref, gc_ref, cw_ref, cb_ref, wa_ref, ba_ref, wx_ref, bx_ref,
                  lam_ref, ol_ref, *maybe_oc_ref):
    u_l = _dwconv(xl_ref[...].astype(F32), cw_ref, cb_ref)
    u_c = _dwconv(xc_ref[...].astype(F32), cw_ref, cb_ref)
    y_l = None
    y_c = None
    for d in range(2):
        rev = d == 1
        a_c, b_c = _rglru_gates(u_c, wa_ref, ba_ref, wx_ref, bx_ref, lam_ref, d)
        h_c = _linear_scan(a_c, b_c, jnp.zeros((1, a_c.shape[1]), F32), rev)
        fin = h_c[0:1, :] if rev else h_c[CTX_LEN - 1:CTX_LEN, :]
        a_l, b_l = _rglru_gates(u_l, wa_ref, ba_ref, wx_ref, bx_ref, lam_ref, d)
        h_l = _linear_scan(a_l, b_l, fin, rev)
        y_l = h_l if y_l is None else y_l + h_l
        y_c = h_c if y_c is None else y_c + h_c
    ol_ref[...] = (y_l * _silu(gl_ref[...].astype(F32))).astype(ol_ref.dtype)
    if maybe_oc_ref:
        maybe_oc_ref[0][...] = (y_c * _silu(gc_ref[...].astype(F32))).astype(ol_ref.dtype)


def _block_diag(w, per_tile):
    n = w.shape[-1]
    tiles = C_BLOCKS // per_tile
    w = w.reshape(2, tiles, per_tile, n, n)
    eye = jnp.eye(per_tile, dtype=w.dtype)
    bd = jnp.einsum('dtpij,pq->dtpiqj', w, eye).reshape(2, tiles, per_tile * n, per_tile * n)
    return jnp.transpose(bd, (1, 0, 2, 3)).astype(BF16)


def rglru_mixer(proj_l, proj_c, conv_w, conv_b, rg_wa, rg_ba, rg_wx, rg_bx, rg_lambda, need_ctx):
    ct = 128
    nt = C_WIDTH // ct
    per_tile = ct // C_BLOCK_DIM
    vec = lambda p: jnp.transpose(p.reshape(2, nt, 1, ct), (1, 0, 2, 3))
    wa = _block_diag(rg_wa, per_tile)
    wx = _block_diag(rg_wx, per_tile)
    xcol = lambda col: (lambda b, t: (b, col + t))
    par = lambda b, t: (t, 0, 0, 0)
    out_shape = [jax.ShapeDtypeStruct((BATCH * SEQ, C_WIDTH), BF16)]
    out_specs = [pl.BlockSpec((SEQ, ct), lambda b, t: (b, t))]
    if need_ctx:
        out_shape.append(jax.ShapeDtypeStruct((BATCH * CTX_LEN, C_WIDTH), BF16))
        out_specs.append(pl.BlockSpec((CTX_LEN, ct), lambda b, t: (b, t)))
    outs = pl.pallas_call(
        _rglru_kernel,
        out_shape=out_shape,
        grid=(BATCH, nt),
        in_specs=[
            pl.BlockSpec((SEQ, ct), xcol(COL_XC)),
            pl.BlockSpec((CTX_LEN, ct), xcol(COL_XC)),
            pl.BlockSpec((SEQ, ct), xcol(COL_GC)),
            pl.BlockSpec((CTX_LEN, ct), xcol(COL_GC)),
            pl.BlockSpec((CONV_W, ct), lambda b, t: (0, t)),
            pl.BlockSpec((1, ct), lambda b, t: (0, t)),
            pl.BlockSpec((1, 2, ct, ct), par),
            pl.BlockSpec((1, 2, 1, ct), par),
            pl.BlockSpec((1, 2, ct, ct), par),
            pl.BlockSpec((1, 2, 1, ct), par),
            pl.BlockSpec((1, 2, 1, ct), par),
        ],
        out_specs=out_specs,
        compiler_params=_params(("arbitrary", "arbitrary")),
        name="rglru_mixer",
    )(proj_l, proj_c, proj_l, proj_c, conv_w, conv_b.reshape(1, C_WIDTH),
      wa, vec(rg_ba), wx, vec(rg_bx), vec(rg_lambda))
    return outs if need_ctx else (outs[0], None)


def _outproj_kernel(x_ref, gate_ref, ma_ref, mb_ref, mc_ref, wa_ref, wb_ref, wc_ref, o_ref):
    y = jnp.dot(ma_ref[...], wa_ref[...], preferred_element_type=F32)
    y = y + jnp.dot(mb_ref[...], wb_ref[...], preferred_element_type=F32)
    y = y + jnp.dot(mc_ref[...], wc_ref[...], preferred_element_type=F32)
    o_ref[...] = x_ref[...] + gate_ref[0] * y


def output_projection(x2d, mod3, ma, mb, mc, w_bf16, layer, rows_per_mod, mod_row0):
    m = x2d.shape[0]
    tm, tn = 1024, 512

    def mod_row(i):
        return mod_row0 + (i * tm) // rows_per_mod

    return pl.pallas_call(
        _outproj_kernel,
        out_shape=jax.ShapeDtypeStruct((m, D_MODEL), F32),
        grid=(m // tm, D_MODEL // tn),
        in_specs=[
            pl.BlockSpec((tm, tn), lambda i, j: (i, j)),
            pl.BlockSpec((1, 1, tn), lambda i, j: (mod_row(i), 0, 2 * (D_MODEL // tn) + j)),
            pl.BlockSpec((tm, A_WIDTH), lambda i, j: (i, 0)),
            pl.BlockSpec((tm, B_WIDTH), lambda i, j: (i, 0)),
            pl.BlockSpec((tm, C_WIDTH), lambda i, j: (i, 0)),
            pl.BlockSpec((None, A_WIDTH, tn), lambda i, j: (layer, 0, j)),
            pl.BlockSpec((None, B_WIDTH, tn), lambda i, j: (layer, 1, j)),
            pl.BlockSpec((None, C_WIDTH, tn), lambda i, j: (layer, (A_WIDTH + B_WIDTH) // C_WIDTH, j)),
        ],
        out_specs=pl.BlockSpec((tm, tn), lambda i, j: (i, j)),
        compiler_params=_params(("arbitrary", "arbitrary")),
        name="output_projection",
    )(x2d, mod3, ma, mb, mc, w_bf16, w_bf16, w_bf16)


def _final_norm_kernel(x_ref, w_ref, o_ref):
    x = x_ref[...]
    o_ref[...] = x * lax.rsqrt(jnp.mean(x * x, axis=-1, keepdims=True) + NORM_EPS) * w_ref[...]


def final_norm(x2d, w):
    tm = 128
    return pl.pallas_call(
        _final_norm_kernel,
        out_shape=jax.ShapeDtypeStruct(x2d.shape, F32),
        grid=(x2d.shape[0] // tm,),
        in_specs=[pl.BlockSpec((tm, D_MODEL), lambda i: (i, 0)),
                  pl.BlockSpec((1, D_MODEL), lambda i: (0, 0))],
        out_specs=pl.BlockSpec((tm, D_MODEL), lambda i: (i, 0)),
        compiler_params=_params(("arbitrary",)),
        name="final_norm",
    )(x2d, w.reshape(1, D_MODEL))


def _rope_tables():
    t = np.arange(SEQ)
    pos = np.stack([t // GRID_W, t % GRID_W], axis=0).astype(np.float32)
    lane = np.arange(HEAD_DIM)
    within = lane % A_QK_DIM
    axis = within // (A_QK_DIM // 2)
    k = within % (A_QK_DIM // 2)
    nfreq = A_QK_DIM // 4
    upper = k >= nfreq
    freqs = jnp.asarray(ROPE_THETA, F32) ** (-jnp.arange(nfreq, dtype=F32) / nfreq)
    ang = jnp.asarray(pos)[axis, :].T * freqs[k % nfreq][None, :]
    cos, sin = jnp.cos(ang), jnp.sin(ang)
    sin_lo = jnp.where(upper[None, :], 0.0, -sin)
    sin_hi = jnp.where(upper[None, :], sin, 0.0)
    return cos, sin_lo, sin_hi


def kernel(x, c, ctx, c_ctx, ada_w, ada_b, norm_w, w_in, w_out, lambda_qk, subln_w, rpb,
           conv_w, conv_b, rg_wa, rg_ba, rg_wx, rg_bx, rg_lambda, final_norm_w):
    cvec = jnp.zeros((MOD_ROWS, D_MODEL), F32).at[:BATCH].set(c).at[CTX_MOD_ROW].set(c_ctx)
    mod = adaln_modulation(cvec, ada_w, ada_b)
    mod3 = mod.reshape(DEPTH * MOD_ROWS, 1, 3 * D_MODEL)
    norm_w3 = norm_w.reshape(DEPTH, 1, D_MODEL)
    w_in_bf = w_in.astype(BF16)
    w_out_bf = w_out.astype(BF16)
    rope_tabs = _rope_tables()
    xl = x.reshape(BATCH * SEQ, D_MODEL)
    xc = ctx.reshape(BATCH * CTX_LEN, D_MODEL)
    for l in range(DEPTH):
        need_ctx = l < DEPTH - 1
        lam_init = 0.8 - 0.6 * math.exp(-0.3 * l)
        row_l, row_c = l * MOD_ROWS, l * MOD_ROWS + CTX_MOD_ROW
        proj_l = input_projection(xl, norm_w3, mod3, w_in_bf, l, SEQ, row_l)
        proj_c = input_projection(xc, norm_w3, mod3, w_in_bf, l, BATCH * CTX_LEN, row_c)

        ma = diff_attention_latent(proj_l, proj_c, rope_tabs, lambda_qk[l], subln_w[l], lam_init)
        mb = neighbourhood_attention(proj_l, proj_c, neighbourhood_bias_table(rpb[l]))
        mc, mc_c = rglru_mixer(proj_l, proj_c, conv_w[l], conv_b[l], rg_wa[l], rg_ba[l],
                               rg_wx[l], rg_bx[l], rg_lambda[l], need_ctx)
        xl = output_projection(xl, mod3, ma, mb, mc, w_out_bf, l, SEQ, row_l)
        if need_ctx:
            ma_c = diff_attention_context(proj_c, lambda_qk[l], subln_w[l], lam_init)
            mb_c = dense_context_attention(proj_c)
            xc = output_projection(xc, mod3, ma_c, mb_c, mc_c, w_out_bf, l, BATCH * CTX_LEN, row_c)
    return final_norm(xl, final_norm_w).reshape(BATCH, SEQ, D_MODEL)
```

```python
import functools
import math

import numpy as np
import jax
import jax.numpy as jnp
from jax import lax
from jax.experimental import pallas as pl
from jax.experimental.pallas import tpu as pltpu

F32 = jnp.float32
BF16 = jnp.bfloat16

D_MODEL = 4096
BATCH = 8
SEQ = 2048
DEPTH = 2
GRID_W = 64
GRID_H = SEQ // GRID_W
CTX_LEN = 256
HEAD_DIM = 128
A_WIDTH = (3 * D_MODEL) // 8
A_HEADS = A_WIDTH // HEAD_DIM
A_QK_DIM = HEAD_DIM // 2
B_WIDTH = (3 * D_MODEL) // 8
B_HEADS = B_WIDTH // HEAD_DIM
C_WIDTH = D_MODEL - A_WIDTH - B_WIDTH
C_BLOCKS = 16
C_BLOCK_DIM = C_WIDTH // C_BLOCKS
MIX_WIDTH = A_WIDTH + B_WIDTH + C_WIDTH
IN_WIDTH = 4 * A_WIDTH + 4 * B_WIDTH + 2 * C_WIDTH
NA_KH = 8
NA_KW = 16
ROPE_THETA = 10000.0
RGLRU_C = 8.0
CONV_W = 4
CONV_LEFT = 2
NORM_EPS = 1e-6
SUBLN_EPS = 1e-5
NEG_INF = -1e30

HB = HEAD_DIM
COL_QA, COL_KA, COL_VA, COL_GA = 0, A_HEADS, 2 * A_HEADS, 3 * A_HEADS
COL_QB = 4 * A_HEADS
COL_KB, COL_VB, COL_GB = COL_QB + B_HEADS, COL_QB + 2 * B_HEADS, COL_QB + 3 * B_HEADS
COL_XC = COL_QB + 4 * B_HEADS
COL_GC = COL_XC + C_WIDTH // HB

MOD_ROWS = 16
CTX_MOD_ROW = BATCH

VMEM_LIMIT = 56 * 1024 * 1024
SUBLANES = 8

NB_QROWS = 4
NB_KROWS = 12
NB_GROUPS = GRID_H // NB_QROWS
NB_Q = NB_QROWS * GRID_W
NB_K = NB_KROWS * GRID_W


def _params(sem):
    return pltpu.CompilerParams(dimension_semantics=sem, vmem_limit_bytes=VMEM_LIMIT)


def _sigmoid(x):
    return 1.0 / (1.0 + jnp.exp(-x))


def _silu(x):
    return x * _sigmoid(x)


def _adaln_kernel(c_ref, w_ref, b_ref, o_ref):
    s = _silu(c_ref[...]).astype(BF16)
    w = w_ref[0].astype(BF16)
    o_ref[0] = jnp.dot(s, w, preferred_element_type=F32) + b_ref[0]


def adaln_modulation(cvec, ada_w, ada_b):
    tn = 512
    n3 = 3 * D_MODEL
    return pl.pallas_call(
        _adaln_kernel,
        out_shape=jax.ShapeDtypeStruct((DEPTH, MOD_ROWS, n3), F32),
        grid=(DEPTH, n3 // tn),
        in_specs=[
            pl.BlockSpec((MOD_ROWS, D_MODEL), lambda l, j: (0, 0)),
            pl.BlockSpec((1, D_MODEL, tn), lambda l, j: (l, 0, j)),
            pl.BlockSpec((1, 1, tn), lambda l, j: (l, 0, j)),
        ],
        out_specs=pl.BlockSpec((1, MOD_ROWS, tn), lambda l, j: (l, 0, j)),
        compiler_params=_params(("arbitrary", "arbitrary")),
        name="adaln_modulation",
    )(cvec, ada_w, ada_b.reshape(DEPTH, 1, n3))


_NORM_CHUNK = 32


def _inproj_kernel(x_ref, nw_ref, shift_ref, scale_ref, w_ref, o_ref, hx_ref):
    @pl.when(pl.program_id(1) == 0)
    def _():
        gain = nw_ref[...] * (1.0 + scale_ref[0])
        add = shift_ref[0]

        def body(r, carry):
            rows = pl.ds(pl.multiple_of(r * _NORM_CHUNK, _NORM_CHUNK), _NORM_CHUNK)
            x = x_ref[rows, :]
            inv = lax.rsqrt(jnp.mean(x * x, axis=-1, keepdims=True) + NORM_EPS)
            hx_ref[rows, :] = (x_ref[rows, :] * inv * gain + add).astype(BF16)
            return carry

        lax.fori_loop(0, hx_ref.shape[0] // _NORM_CHUNK, body, 0)

    o_ref[...] = jnp.dot(hx_ref[...], w_ref[...], preferred_element_type=F32).astype(o_ref.dtype)


class ProjCols:
    def __init__(self, groups):
        self.start, pos = {}, 0
        for name, _, width in groups:
            self.start[name] = pos
            pos += width
        self.width = pos * HB
        self.src_blocks = [src + k for _, src, width in groups for k in range(width)]

    def __getitem__(self, name):
        return self.start[name]


_ALL_GROUPS = (("qa", COL_QA, A_HEADS), ("ka", COL_KA, A_HEADS), ("va", COL_VA, A_HEADS),
               ("ga", COL_GA, A_HEADS), ("qb", COL_QB, B_HEADS), ("kb", COL_KB, B_HEADS),
               ("vb", COL_VB, B_HEADS), ("gb", COL_GB, B_HEADS), ("xc", COL_XC, C_WIDTH // HB),
               ("gc", COL_GC, C_WIDTH // HB))
FULL_COLS = ProjCols(_ALL_GROUPS)
KV_COLS = ProjCols(tuple(g for g in _ALL_GROUPS if g[0] in ("ka", "va", "kb", "vb", "xc")))


def input_projection(x2d, norm_w3, mod3, w_bf16, layer, rows_per_mod, mod_row0, cols=FULL_COLS):
    m = x2d.shape[0]
    tm = 512
    tn = 1024 if cols is FULL_COLS else 512
    per_tile = tn // HB
    n_tiles = cols.width // tn
    src_tiles = [cols.src_blocks[t * per_tile] // per_tile for t in range(n_tiles)]
    assert all(cols.src_blocks[t * per_tile + k] == src_tiles[t] * per_tile + k
               for t in range(n_tiles) for k in range(per_tile))

    def src_tile(j):
        t = j + src_tiles[0]
        for k in range(1, n_tiles):
            jump = (src_tiles[k] - k) - (src_tiles[k - 1] - (k - 1))
            if jump:
                t = t + jnp.where(j >= k, jump, 0)
        return t

    def mod_row(i):
        return mod_row0 + (i * tm) // rows_per_mod

    return pl.pallas_call(
        _inproj_kernel,
        out_shape=jax.ShapeDtypeStruct((m, cols.width), BF16),
        grid=(m // tm, n_tiles),
        in_specs=[
            pl.BlockSpec((tm, D_MODEL), lambda i, j: (i, 0)),
            pl.BlockSpec((None, 1, D_MODEL), lambda i, j: (layer, 0, 0)),
            pl.BlockSpec((1, 1, D_MODEL), lambda i, j: (mod_row(i), 0, 0)),
            pl.BlockSpec((1, 1, D_MODEL), lambda i, j: (mod_row(i), 0, 1)),
            pl.BlockSpec((None, D_MODEL, tn), lambda i, j: (layer, 0, src_tile(j))),
        ],
        out_specs=pl.BlockSpec((tm, tn), lambda i, j: (i, j)),
        scratch_shapes=[pltpu.VMEM((tm, D_MODEL), BF16)],
        compiler_params=_params(("arbitrary", "arbitrary")),
        name="input_projection",
    )(x2d, norm_w3, mod3, mod3, w_bf16)


def _rope(x, cos, sin_lo, sin_hi):
    return (x * cos + pltpu.roll(x, HEAD_DIM - 16, axis=1) * sin_lo
            + pltpu.roll(x, 16, axis=1) * sin_hi)


def _diff_lambda(lq_ref, lam_init):
    lq = lq_ref[...]
    a = jnp.sum(lq[0:1] * lq[1:2], axis=-1, keepdims=True)
    b = jnp.sum(lq[2:3] * lq[3:4], axis=-1, keepdims=True)
    return jnp.exp(a) - jnp.exp(b) + lam_init


LOG2E = 1.4426950408889634
NKEYS = CTX_LEN + SEQ
DA_TQ = 256
DA_W = 2 * DA_TQ
DA_ROWS = 16
DA_KCHUNK = 256


def _transpose_bf16(x):
    return x.astype(F32).T.astype(BF16)


def _diffattn_lat_kernel(q_ref, kc_ref, vc_ref, kl_ref, vl_ref, g_ref, cos_ref, slo_ref, shi_ref,
                         lq_ref, sw_ref, o_ref, k_scr, vt_scr, qt_scr, s0_scr, s1_scr, m_scr, p_scr,
                         *, lam_init):
    qi = pl.program_id(2)
    n_tiles = SEQ // DA_TQ
    n_chunks = NKEYS // DA_KCHUNK
    groups_per_chunk = DA_KCHUNK // DA_ROWS

    def score_chunk(s_scr, qt2, c, m_run):
        rows = slice(c * DA_KCHUNK, (c + 1) * DA_KCHUNK)
        s = jnp.dot(k_scr[rows, :], qt2, preferred_element_type=F32)
        s_scr[rows, :] = s
        parts = [s[i * DA_ROWS:(i + 1) * DA_ROWS, :] for i in range(groups_per_chunk)]
        if m_run is not None:
            parts.append(m_run)
        while len(parts) > 1:
            parts = [jnp.maximum(parts[i], parts[i + 1]) for i in range(0, len(parts) - 1, 2)] + (
                [parts[-1]] if len(parts) % 2 else [])
        return parts[0]

    def store_colmax(m_run):
        m_scr[...] = jnp.broadcast_to(jnp.max(m_run, axis=0, keepdims=True), (DA_ROWS, DA_W))

    @pl.when(qi == 0)
    def _():
        k_scr[0:CTX_LEN, :] = kc_ref[...]
        vt_scr[:, 0:CTX_LEN] = _transpose_bf16(vc_ref[...])
        dim = lax.broadcasted_iota(jnp.int32, (HB, DA_TQ), 0)
        for c in range(SEQ // DA_TQ):
            rows = slice(c * DA_TQ, (c + 1) * DA_TQ)
            dst = slice(CTX_LEN + c * DA_TQ, CTX_LEN + (c + 1) * DA_TQ)
            cos, slo, shi = cos_ref[rows, :], slo_ref[rows, :], shi_ref[rows, :]
            k_scr[dst, :] = _rope(kl_ref[rows, :].astype(F32), cos, slo, shi).astype(BF16)
            vt_scr[:, dst] = _transpose_bf16(vl_ref[rows, :])
            q = _rope(q_ref[rows, :].astype(F32), cos, slo, shi)
            qt = (q * (A_QK_DIM ** -0.5 * LOG2E)).T
            qt_scr[:, c * DA_W:c * DA_W + DA_TQ] = jnp.where(dim < A_QK_DIM, qt, 0.0).astype(BF16)
            qt_scr[:, c * DA_W + DA_TQ:(c + 1) * DA_W] = jnp.where(dim >= A_QK_DIM, qt, 0.0).astype(BF16)
        m_run = None
        for c in range(n_chunks):
            m_run = score_chunk(s0_scr, qt_scr[:, 0:DA_W], c, m_run)
        store_colmax(m_run)

    def tile(s_scr, s_next_scr):
        nxt = jnp.where(qi == n_tiles - 1, 0, qi + 1)
        qt_next = qt_scr[:, pl.ds(pl.multiple_of(nxt * DA_W, DA_W), DA_W)]
        mb = m_scr[...]
        m_run = None
        l_acc = [None, None]
        acc = None
        for c in range(n_chunks):
            rows = slice(c * DA_KCHUNK, (c + 1) * DA_KCHUNK)
            m_run = score_chunk(s_next_scr, qt_next, c, m_run)
            for r in range(c * groups_per_chunk, (c + 1) * groups_per_chunk):
                grp = slice(r * DA_ROWS, (r + 1) * DA_ROWS)
                p = jnp.exp2(s_scr[grp, :] - mb)
                p_scr[grp, :] = p.astype(BF16)
                l_acc[r % 2] = p if l_acc[r % 2] is None else l_acc[r % 2] + p
            part = jnp.dot(vt_scr[:, rows], p_scr[rows, :], preferred_element_type=F32)
            acc = part if acc is None else acc + part
        store_colmax(m_run)
        l = jnp.sum(l_acc[0] + l_acc[1], axis=0, keepdims=True)

        lam = _diff_lambda(lq_ref, lam_init)
        ot = acc[:, :DA_TQ] * (1.0 / l[:, :DA_TQ]) - acc[:, DA_TQ:] * (lam / l[:, DA_TQ:])
        ot = ot * lax.rsqrt(jnp.mean(ot * ot, axis=0, keepdims=True) + SUBLN_EPS)
        o = ot.T * sw_ref[...] * (1.0 - lam_init)
        o_ref[...] = (o * _silu(g_ref[...].astype(F32))).astype(o_ref.dtype)

    parity = qi % 2
    pl.when(parity == 0)(functools.partial(tile, s0_scr, s1_scr))
    pl.when(parity == 1)(functools.partial(tile, s1_scr, s0_scr))


_NT = (((1,), (1,)), ((), ()))


def _diffattn_ctx_kernel(q_ref, k_ref, v_ref, g_ref, lq_ref, sw_ref, o_ref, *, lam_init):
    q = q_ref[...].astype(F32) * (A_QK_DIM ** -0.5)
    lane = lax.broadcasted_iota(jnp.int32, q.shape, 1)
    outs = []
    for qn in (jnp.where(lane < A_QK_DIM, q, 0.0), jnp.where(lane >= A_QK_DIM, q, 0.0)):
        s = lax.dot_general(qn.astype(BF16), k_ref[...], _NT, preferred_element_type=F32)
        p = jnp.exp(s - jnp.max(s, axis=-1, keepdims=True))
        acc = jnp.dot(p.astype(BF16), v_ref[...], preferred_element_type=F32)
        outs.append((acc, jnp.sum(p, axis=-1, keepdims=True)))
    lam = _diff_lambda(lq_ref, lam_init)
    o = outs[0][0] * (1.0 / outs[0][1]) - (lam / outs[1][1]) * outs[1][0]
    o = o * lax.rsqrt(jnp.mean(o * o, axis=-1, keepdims=True) + SUBLN_EPS) * sw_ref[...]
    o = o * (1.0 - lam_init)
    o_ref[...] = (o * _silu(g_ref[...].astype(F32))).astype(o_ref.dtype)


def diff_attention_latent(proj_l, proj_c, ccols, rope_tabs, lambda_qk, subln_w, lam_init):
    nq = SEQ // DA_TQ
    head = lambda col: (lambda b, h, i: (b, col + h))
    qmap = lambda col: (lambda b, h, i: (b * nq + i, col + h))
    const = lambda b, h, i: (0, 0)
    return pl.pallas_call(
        functools.partial(_diffattn_lat_kernel, lam_init=lam_init),
        out_shape=jax.ShapeDtypeStruct((BATCH * SEQ, A_WIDTH), BF16),
        grid=(BATCH, A_HEADS, nq),
        in_specs=[
            pl.BlockSpec((SEQ, HB), head(COL_QA)),
            pl.BlockSpec((CTX_LEN, HB), head(ccols["ka"])),
            pl.BlockSpec((CTX_LEN, HB), head(ccols["va"])),
            pl.BlockSpec((SEQ, HB), head(COL_KA)),
            pl.BlockSpec((SEQ, HB), head(COL_VA)),
            pl.BlockSpec((DA_TQ, HB), qmap(COL_GA)),
            pl.BlockSpec((SEQ, HB), const),
            pl.BlockSpec((SEQ, HB), const),
            pl.BlockSpec((SEQ, HB), const),
            pl.BlockSpec((4, A_QK_DIM), const),
            pl.BlockSpec((1, HB), const),
        ],
        out_specs=pl.BlockSpec((DA_TQ, HB), lambda b, h, i: (b * nq + i, h)),
        scratch_shapes=[pltpu.VMEM((NKEYS, HB), BF16), pltpu.VMEM((HB, NKEYS), BF16),
                        pltpu.VMEM((HB, 2 * SEQ), BF16),
                        pltpu.VMEM((NKEYS, DA_W), F32), pltpu.VMEM((NKEYS, DA_W), F32),
                        pltpu.VMEM((DA_ROWS, DA_W), F32), pltpu.VMEM((NKEYS, DA_W), BF16)],
        compiler_params=_params(("arbitrary", "arbitrary", "arbitrary")),
        name="diff_attention_latent",
    )(proj_l, proj_c, proj_c, proj_l, proj_l, proj_l, *rope_tabs, lambda_qk, subln_w.reshape(1, HB))


def diff_attention_context(proj_c, lambda_qk, subln_w, lam_init):
    head = lambda col: (lambda b, h: (b, col + h))
    const = lambda b, h: (0, 0)
    return pl.pallas_call(
        functools.partial(_diffattn_ctx_kernel, lam_init=lam_init),
        out_shape=jax.ShapeDtypeStruct((BATCH * CTX_LEN, A_WIDTH), BF16),
        grid=(BATCH, A_HEADS),
        in_specs=[pl.BlockSpec((CTX_LEN, HB), head(COL_QA)),
                  pl.BlockSpec((CTX_LEN, HB), head(COL_KA)),
                  pl.BlockSpec((CTX_LEN, HB), head(COL_VA)),
                  pl.BlockSpec((CTX_LEN, HB), head(COL_GA)),
                  pl.BlockSpec((4, A_QK_DIM), const),
                  pl.BlockSpec((1, HB), const)],
        out_specs=pl.BlockSpec((CTX_LEN, HB), lambda b, h: (b, h)),
        compiler_params=_params(("arbitrary", "arbitrary")),
        name="diff_attention_context",
    )(proj_c, proj_c, proj_c, proj_c, lambda_qk, subln_w.reshape(1, HB))


def _nbr_slab_start(g):
    return min(max(g * NB_QROWS - NA_KH // 2, 0), GRID_H - NB_KROWS)


def _nbr_pattern(g):
    return 0 if g == 0 else (2 if g == NB_GROUPS - 1 else 1)


def _nbr_kernel(q_ref, kc_ref, vc_ref, kl_ref, vl_ref, g_ref, bias_ref, o_ref, vt_scr):
    vt_scr[:, 0:CTX_LEN] = _transpose_bf16(vc_ref[...])
    for c in range(SEQ // NB_Q):
        vt_scr[:, CTX_LEN + c * NB_Q:CTX_LEN + (c + 1) * NB_Q] = _transpose_bf16(vl_ref[c * NB_Q:(c + 1) * NB_Q, :])
    for g in range(NB_GROUPS):
        rows = slice(g * NB_Q, (g + 1) * NB_Q)
        k0 = _nbr_slab_start(g) * GRID_W
        qt = (q_ref[rows, :].astype(F32) * (HEAD_DIM ** -0.5 * LOG2E)).T.astype(BF16)
        s_c = jnp.dot(kc_ref[...], qt, preferred_element_type=F32)
        s_n = (jnp.dot(kl_ref[k0:k0 + NB_K, :], qt, preferred_element_type=F32)
               + bias_ref[0, _nbr_pattern(g)])
        m = jnp.maximum(jnp.max(s_c, axis=0, keepdims=True), jnp.max(s_n, axis=0, keepdims=True))
        p_c = jnp.exp2(s_c - m)
        p_n = jnp.exp2(s_n - m)
        l = jnp.sum(p_c, axis=0, keepdims=True) + jnp.sum(p_n, axis=0, keepdims=True)
        acc = (jnp.dot(vt_scr[:, 0:CTX_LEN], p_c.astype(BF16), preferred_element_type=F32)
               + jnp.dot(vt_scr[:, CTX_LEN + k0:CTX_LEN + k0 + NB_K], p_n.astype(BF16),
                         preferred_element_type=F32))
        o = (acc * (1.0 / l)).T
        o_ref[rows, :] = (o * _silu(g_ref[rows, :].astype(F32))).astype(o_ref.dtype)


def neighbourhood_bias_table(rpb):
    w = GRID_W
    cq = np.arange(w)
    c0 = np.clip(cq - NA_KW // 2, 0, w - NA_KW)
    ck = np.arange(w)
    col_valid = (ck[None, :] >= c0[:, None]) & (ck[None, :] < c0[:, None] + NA_KW)
    pad = w - NA_KW
    padded = jnp.pad(rpb.astype(F32), ((0, 0), (0, 0), (pad, pad)))
    toep = jnp.stack([padded[:, :, w - 1 - q:2 * w - 1 - q] for q in range(w)], axis=2)
    toep = jnp.where(col_valid[None, None], toep, NEG_INF)
    toep_t = jnp.swapaxes(toep, 2, 3) * LOG2E
    masked = jnp.full((B_HEADS, w, w), NEG_INF, F32)
    tabs = []
    for g in (0, 1, NB_GROUPS - 1):
        slab0 = _nbr_slab_start(g)
        qr = g * NB_QROWS + np.arange(NB_QROWS)
        kr = slab0 + np.arange(NB_KROWS)
        win0 = np.clip(qr - NA_KH // 2, 0, GRID_H - NA_KH)
        row_valid = (kr[None, :] >= win0[:, None]) & (kr[None, :] < win0[:, None] + NA_KH)
        dr_idx = kr[None, :] - qr[:, None] + NA_KH - 1
        rows = [jnp.concatenate([toep_t[:, int(dr_idx[a, k])] if row_valid[a, k] else masked
                                 for a in range(NB_QROWS)], axis=-1) for k in range(NB_KROWS)]
        tabs.append(jnp.concatenate(rows, axis=1))
    return jnp.stack(tabs, axis=1)


def neighbourhood_attention(proj_l, proj_c, ccols, bias_tab):
    head = lambda col: (lambda h, b: (b, col + h))
    return pl.pallas_call(
        _nbr_kernel,
        out_shape=jax.ShapeDtypeStruct((BATCH * SEQ, B_WIDTH), BF16),
        grid=(B_HEADS, BATCH),
        in_specs=[
            pl.BlockSpec((SEQ, HB), head(COL_QB)),
            pl.BlockSpec((CTX_LEN, HB), head(ccols["kb"])),
            pl.BlockSpec((CTX_LEN, HB), head(ccols["vb"])),
            pl.BlockSpec((SEQ, HB), head(COL_KB)),
            pl.BlockSpec((SEQ, HB), head(COL_VB)),
            pl.BlockSpec((SEQ, HB), head(COL_GB)),
            pl.BlockSpec((1, 3, NB_K, NB_Q), lambda h, b: (h, 0, 0, 0)),
        ],
        out_specs=pl.BlockSpec((SEQ, HB), lambda h, b: (b, h)),
        scratch_shapes=[pltpu.VMEM((HB, NKEYS), BF16)],
        compiler_params=_params(("arbitrary", "arbitrary")),
        name="neighbourhood_attention",
    )(proj_l, proj_c, proj_c, proj_l, proj_l, proj_l, bias_tab)


def _dense_ctx_kernel(q_ref, k_ref, v_ref, g_ref, o_ref):
    scale = HEAD_DIM ** -0.5
    s = lax.dot_general(q_ref[...], k_ref[...], _NT, preferred_element_type=F32) * scale
    p = jnp.exp(s - jnp.max(s, axis=-1, keepdims=True))
    p = p * (1.0 / jnp.sum(p, axis=-1, keepdims=True))
    o = jnp.dot(p.astype(BF16), v_ref[...], preferred_element_type=F32)
    o_ref[...] = (o * _silu(g_ref[...].astype(F32))).astype(o_ref.dtype)


def dense_context_attention(proj_c):
    head = lambda col: (lambda b, h: (b, col + h))
    return pl.pallas_call(
        _dense_ctx_kernel,
        out_shape=jax.ShapeDtypeStruct((BATCH * CTX_LEN, B_WIDTH), BF16),
        grid=(BATCH, B_HEADS),
        in_specs=[pl.BlockSpec((CTX_LEN, HB), head(COL_QB)),
                  pl.BlockSpec((CTX_LEN, HB), head(COL_KB)),
                  pl.BlockSpec((CTX_LEN, HB), head(COL_VB)),
                  pl.BlockSpec((CTX_LEN, HB), head(COL_GB))],
        out_specs=pl.BlockSpec((CTX_LEN, HB), lambda b, h: (b, h)),
        compiler_params=_params(("arbitrary", "arbitrary")),
        name="dense_context_attention",
    )(proj_c, proj_c, proj_c, proj_c)


def _shift_rows(x, d, fill):
    n = x.shape[0]
    if d % SUBLANES == 0:
        pad = jnp.full((abs(d), x.shape[1]), fill, x.dtype)
        return jnp.concatenate([pad, x[:n - d]] if d > 0 else [x[-d:], pad], axis=0)
    rolled = pltpu.roll(x, d % n, axis=0)
    row = lax.broadcasted_iota(jnp.int32, x.shape, 0)
    keep = (row >= d) if d > 0 else (row < n + d)
    return jnp.where(keep, rolled, fill)


def _dwconv(x, w_ref, b_ref):
    acc = b_ref[...] + w_ref[CONV_LEFT:CONV_LEFT + 1, :] * x
    for j in range(CONV_W):
        if j != CONV_LEFT:
            acc = acc + w_ref[j:j + 1, :] * _shift_rows(x, CONV_LEFT - j, 0.0)
    return acc


def _linear_scan(a, b, h0, reverse):
    n = a.shape[0]
    row = lax.broadcasted_iota(jnp.int32, a.shape, 0)
    first = (row == n - 1) if reverse else (row == 0)
    b = b + jnp.where(first, a * h0, 0.0)
    d = 1
    while d < n:
        sd = -d if reverse else d
        b = b + a * _shift_rows(b, sd, 0.0)
        if 2 * d < n:
            a = a * _shift_rows(a, sd, 1.0)
        d *= 2
    return b


def _sigmoid_tanh(x):
    return 0.5 * jnp.tanh(0.5 * x) + 0.5


def _rglru_gates(u, wa_ref, ba_ref, wx_ref, bx_ref, lam_ref, d):
    ub = u.astype(BF16)
    r = _sigmoid_tanh(jnp.dot(ub, wa_ref[0, d], preferred_element_type=F32) + ba_ref[0, d])
    i = _sigmoid_tanh(jnp.dot(ub, wx_ref[0, d], preferred_element_type=F32) + bx_ref[0, d])
    neg_lam = -lam_ref[0, d]
    softplus = jnp.maximum(neg_lam, 0.0) + jnp.log1p(jnp.exp(-jnp.abs(neg_lam)))
    log_a = (-RGLRU_C) * r * softplus
    a = jnp.exp(log_a)
    t = jnp.tanh(log_a)
    b = jnp.sqrt(-2.0 * t / (1.0 - t)) * (i * u)
    return a, b


SCAN_CHUNKS = SUBLANES


def _scan_chunk_len(n):
    return n // SCAN_CHUNKS + 1


def _blocked_scan(a_scr, b_scr, n, h0, reverse, need_states):
    length = _scan_chunk_len(n)
    order = range(length - 1, -1, -1) if reverse else range(length)
    h = jnp.zeros((SCAN_CHUNKS, a_scr.shape[1]), F32)
    p = jnp.ones_like(h)
    hs, ps = [None] * length, [None] * length
    for i in order:
        va = a_scr[pl.ds(i, SCAN_CHUNKS, stride=length), :]
        vb = b_scr[pl.ds(i, SCAN_CHUNKS, stride=length), :]
        h = va * h + vb
        p = va * p
        hs[i], ps[i] = h, p
    rows = [None] * SCAN_CHUNKS
    carry = h0
    for j in (range(SCAN_CHUNKS - 1, -1, -1) if reverse else range(SCAN_CHUNKS)):
        rows[j] = carry
        carry = h[j:j + 1, :] + p[j:j + 1, :] * carry
    if not need_states:
        return None, carry
    start = jnp.concatenate(rows, axis=0)
    return [hs[i] + ps[i] * start for i in range(length)], carry


def _rglru_kernel(*refs, need_ctx):
    if need_ctx:
        (xl_ref, xc_ref, gl_ref, gc_ref, cw_ref, cb_ref, wa_ref, ba_ref, wx_ref, bx_ref, lam_ref,
         ol_ref, oc_ref, al_scr, bl_scr, ac_scr, bc_scr, yl_scr, yc_scr) = refs
    else:
        (xl_ref, xc_ref, gl_ref, cw_ref, cb_ref, wa_ref, ba_ref, wx_ref, bx_ref, lam_ref,
         ol_ref, al_scr, bl_scr, ac_scr, bc_scr, yl_scr, yc_scr) = refs
    ct = al_scr.shape[1]
    u_l = _dwconv(xl_ref[...].astype(F32), cw_ref, cb_ref)
    u_c = _dwconv(xc_ref[...].astype(F32), cw_ref, cb_ref)
    for a_scr, b_scr, n in ((al_scr, bl_scr, SEQ), (ac_scr, bc_scr, CTX_LEN)):
        a_scr[n:n + SUBLANES, :] = jnp.ones((SUBLANES, ct), F32)
        b_scr[n:n + SUBLANES, :] = jnp.zeros((SUBLANES, ct), F32)
    y_l = None
    y_c = None
    for d in range(2):
        rev = d == 1
        a_c, b_c = _rglru_gates(u_c, wa_ref, ba_ref, wx_ref, bx_ref, lam_ref, d)
        ac_scr[0:CTX_LEN, :] = a_c
        bc_scr[0:CTX_LEN, :] = b_c
        h_c, fin = _blocked_scan(ac_scr, bc_scr, CTX_LEN, jnp.zeros((1, ct), F32), rev, need_ctx)
        a_l, b_l = _rglru_gates(u_l, wa_ref, ba_ref, wx_ref, bx_ref, lam_ref, d)
        al_scr[0:SEQ, :] = a_l
        bl_scr[0:SEQ, :] = b_l
        h_l, _ = _blocked_scan(al_scr, bl_scr, SEQ, fin, rev, True)
        y_l = h_l if y_l is None else [f + r for f, r in zip(y_l, h_l)]
        if need_ctx:
            y_c = h_c if y_c is None else [f + r for f, r in zip(y_c, h_c)]
    length = _scan_chunk_len(SEQ)
    for i in range(length):
        yl_scr[pl.ds(i, SCAN_CHUNKS, stride=length), :] = y_l[i]
    ol_ref[...] = (yl_scr[0:SEQ, :] * _silu(gl_ref[...].astype(F32))).astype(ol_ref.dtype)
    if need_ctx:
        length = _scan_chunk_len(CTX_LEN)
        for i in range(length):
            yc_scr[pl.ds(i, SCAN_CHUNKS, stride=length), :] = y_c[i]
        oc_ref[...] = (yc_scr[0:CTX_LEN, :] * _silu(gc_ref[...].astype(F32))).astype(oc_ref.dtype)


def _block_diag(w, per_tile):
    n = w.shape[-1]
    tiles = C_BLOCKS // per_tile
    w = w.reshape(2, tiles, per_tile, n, n)
    eye = jnp.eye(per_tile, dtype=w.dtype)
    bd = jnp.einsum('dtpij,pq->dtpiqj', w, eye).reshape(2, tiles, per_tile * n, per_tile * n)
    return jnp.transpose(bd, (1, 0, 2, 3)).astype(BF16)


def rglru_mixer(proj_l, proj_c, ccols, conv_w, conv_b, rg_wa, rg_ba, rg_wx, rg_bx, rg_lambda, need_ctx):
    ct = 128
    nt = C_WIDTH // ct
    per_tile = ct // C_BLOCK_DIM
    vec = lambda p: jnp.transpose(p.reshape(2, nt, 1, ct), (1, 0, 2, 3))
    wa = _block_diag(rg_wa, per_tile)
    wx = _block_diag(rg_wx, per_tile)
    xcol = lambda col: (lambda b, t: (b, col + t))
    par = lambda b, t: (t, 0, 0, 0)
    out_shape = [jax.ShapeDtypeStruct((BATCH * SEQ, C_WIDTH), BF16)]
    out_specs = [pl.BlockSpec((SEQ, ct), lambda b, t: (b, t))]
    if need_ctx:
        out_shape.append(jax.ShapeDtypeStruct((BATCH * CTX_LEN, C_WIDTH), BF16))
        out_specs.append(pl.BlockSpec((CTX_LEN, ct), lambda b, t: (b, t)))
    in_specs = [pl.BlockSpec((SEQ, ct), xcol(COL_XC)),
                pl.BlockSpec((CTX_LEN, ct), xcol(ccols["xc"])),
                pl.BlockSpec((SEQ, ct), xcol(COL_GC))]
    args = [proj_l, proj_c, proj_l]
    if need_ctx:
        in_specs.append(pl.BlockSpec((CTX_LEN, ct), xcol(ccols["gc"])))
        args.append(proj_c)
    in_specs += [pl.BlockSpec((CONV_W, ct), lambda b, t: (0, t)),
                 pl.BlockSpec((1, ct), lambda b, t: (0, t)),
                 pl.BlockSpec((1, 2, ct, ct), par),
                 pl.BlockSpec((1, 2, 1, ct), par),
                 pl.BlockSpec((1, 2, ct, ct), par),
                 pl.BlockSpec((1, 2, 1, ct), par),
                 pl.BlockSpec((1, 2, 1, ct), par)]
    args += [conv_w, conv_b.reshape(1, C_WIDTH), wa, vec(rg_ba), wx, vec(rg_bx), vec(rg_lambda)]
    outs = pl.pallas_call(
        functools.partial(_rglru_kernel, need_ctx=need_ctx),
        out_shape=out_shape,
        grid=(BATCH, nt),
        in_specs=in_specs,
        out_specs=out_specs,
        scratch_shapes=([pltpu.VMEM((SEQ + SUBLANES, ct), F32)] * 2
                        + [pltpu.VMEM((CTX_LEN + SUBLANES, ct), F32)] * 2
                        + [pltpu.VMEM((SEQ + SUBLANES, ct), F32), pltpu.VMEM((CTX_LEN + SUBLANES, ct), F32)]),
        compiler_params=_params(("arbitrary", "arbitrary")),
        name="rglru_mixer",
    )(*args)
    return outs if need_ctx else (outs[0], None)


def _outproj_kernel(x_ref, gate_ref, ma_ref, mb_ref, mc_ref, wa_ref, wb_ref, wc_ref, o_ref):
    y = jnp.dot(ma_ref[...], wa_ref[...], preferred_element_type=F32)
    y = y + jnp.dot(mb_ref[...], wb_ref[...], preferred_element_type=F32)
    y = y + jnp.dot(mc_ref[...], wc_ref[...], preferred_element_type=F32)
    o_ref[...] = x_ref[...] + gate_ref[0] * y


def output_projection(x2d, mod3, ma, mb, mc, w_bf16, layer, rows_per_mod, mod_row0):
    m = x2d.shape[0]
    tm, tn = 1024, 512

    def mod_row(i):
        return mod_row0 + (i * tm) // rows_per_mod

    return pl.pallas_call(
        _outproj_kernel,
        out_shape=jax.ShapeDtypeStruct((m, D_MODEL), F32),
        grid=(m // tm, D_MODEL // tn),
        in_specs=[
            pl.BlockSpec((tm, tn), lambda i, j: (i, j)),
            pl.BlockSpec((1, 1, tn), lambda i, j: (mod_row(i), 0, 2 * (D_MODEL // tn) + j)),
            pl.BlockSpec((tm, A_WIDTH), lambda i, j: (i, 0)),
            pl.BlockSpec((tm, B_WIDTH), lambda i, j: (i, 0)),
            pl.BlockSpec((tm, C_WIDTH), lambda i, j: (i, 0)),
            pl.BlockSpec((None, A_WIDTH, tn), lambda i, j: (layer, 0, j)),
            pl.BlockSpec((None, B_WIDTH, tn), lambda i, j: (layer, 1, j)),
            pl.BlockSpec((None, C_WIDTH, tn), lambda i, j: (layer, (A_WIDTH + B_WIDTH) // C_WIDTH, j)),
        ],
        out_specs=pl.BlockSpec((tm, tn), lambda i, j: (i, j)),
        compiler_params=_params(("arbitrary", "arbitrary")),
        name="output_projection",
    )(x2d, mod3, ma, mb, mc, w_bf16, w_bf16, w_bf16)


def _final_norm_kernel(x_ref, w_ref, o_ref):
    x = x_ref[...]
    o_ref[...] = x * lax.rsqrt(jnp.mean(x * x, axis=-1, keepdims=True) + NORM_EPS) * w_ref[...]


def final_norm(x2d, w):
    tm = 128
    return pl.pallas_call(
        _final_norm_kernel,
        out_shape=jax.ShapeDtypeStruct(x2d.shape, F32),
        grid=(x2d.shape[0] // tm,),
        in_specs=[pl.BlockSpec((tm, D_MODEL), lambda i: (i, 0)),
                  pl.BlockSpec((1, D_MODEL), lambda i: (0, 0))],
        out_specs=pl.BlockSpec((tm, D_MODEL), lambda i: (i, 0)),
        compiler_params=_params(("arbitrary",)),
        name="final_norm",
    )(x2d, w.reshape(1, D_MODEL))


def _rope_tables():
    t = np.arange(SEQ)
    pos = np.stack([t // GRID_W, t % GRID_W], axis=0).astype(np.float32)
    lane = np.arange(HEAD_DIM)
    within = lane % A_QK_DIM
    axis = within // (A_QK_DIM // 2)
    k = within % (A_QK_DIM // 2)
    nfreq = A_QK_DIM // 4
    upper = k >= nfreq
    freqs = jnp.asarray(ROPE_THETA, F32) ** (-jnp.arange(nfreq, dtype=F32) / nfreq)
    ang = jnp.asarray(pos)[axis, :].T * freqs[k % nfreq][None, :]
    cos, sin = jnp.cos(ang), jnp.sin(ang)
    sin_lo = jnp.where(upper[None, :], 0.0, -sin)
    sin_hi = jnp.where(upper[None, :], sin, 0.0)
    return cos, sin_lo, sin_hi


def kernel(x, c, ctx, c_ctx, ada_w, ada_b, norm_w, w_in, w_out, lambda_qk, subln_w, rpb,
           conv_w, conv_b, rg_wa, rg_ba, rg_wx, rg_bx, rg_lambda, final_norm_w):
    cvec = jnp.zeros((MOD_ROWS, D_MODEL), F32).at[:BATCH].set(c).at[CTX_MOD_ROW].set(c_ctx)
    mod = adaln_modulation(cvec, ada_w, ada_b)
    mod3 = mod.reshape(DEPTH * MOD_ROWS, 1, 3 * D_MODEL)
    norm_w3 = norm_w.reshape(DEPTH, 1, D_MODEL)
    w_in_bf = w_in.astype(BF16)
    w_out_bf = w_out.astype(BF16)
    rope_tabs = _rope_tables()
    xl = x.reshape(BATCH * SEQ, D_MODEL)
    xc = ctx.reshape(BATCH * CTX_LEN, D_MODEL)
    for l in range(DEPTH):
        need_ctx = l < DEPTH - 1
        lam_init = 0.8 - 0.6 * math.exp(-0.3 * l)
        row_l, row_c = l * MOD_ROWS, l * MOD_ROWS + CTX_MOD_ROW
        proj_l = input_projection(xl, norm_w3, mod3, w_in_bf, l, SEQ, row_l)
        ccols = FULL_COLS if need_ctx else KV_COLS
        proj_c = input_projection(xc, norm_w3, mod3, w_in_bf, l, BATCH * CTX_LEN, row_c, ccols)

        ma = diff_attention_latent(proj_l, proj_c, ccols, rope_tabs, lambda_qk[l], subln_w[l], lam_init)
        mb = neighbourhood_attention(proj_l, proj_c, ccols, neighbourhood_bias_table(rpb[l]))
        mc, mc_c = rglru_mixer(proj_l, proj_c, ccols, conv_w[l], conv_b[l], rg_wa[l], rg_ba[l],
                               rg_wx[l], rg_bx[l], rg_lambda[l], need_ctx)
        xl = output_projection(xl, mod3, ma, mb, mc, w_out_bf, l, SEQ, row_l)
        if need_ctx:
            ma_c = diff_attention_context(proj_c, lambda_qk[l], subln_w[l], lam_init)
            mb_c = dense_context_attention(proj_c)
            xc = output_projection(xc, mod3, ma_c, mb_c, mc_c, w_out_bf, l, BATCH * CTX_LEN, row_c)
    return final_norm(xl, final_norm_w).reshape(BATCH, SEQ, D_MODEL)
```

```python
import functools
import math

import numpy as np
import jax
import jax.numpy as jnp
from jax import lax
from jax.experimental import pallas as pl
from jax.experimental.pallas import tpu as pltpu

F32 = jnp.float32
BF16 = jnp.bfloat16

D_MODEL = 4096
BATCH = 8
SEQ = 2048
DEPTH = 2
GRID_W = 64
GRID_H = SEQ // GRID_W
CTX_LEN = 256
HEAD_DIM = 128
A_WIDTH = (3 * D_MODEL) // 8
A_HEADS = A_WIDTH // HEAD_DIM
A_QK_DIM = HEAD_DIM // 2
B_WIDTH = (3 * D_MODEL) // 8
B_HEADS = B_WIDTH // HEAD_DIM
C_WIDTH = D_MODEL - A_WIDTH - B_WIDTH
C_BLOCKS = 16
C_BLOCK_DIM = C_WIDTH // C_BLOCKS
MIX_WIDTH = A_WIDTH + B_WIDTH + C_WIDTH
IN_WIDTH = 4 * A_WIDTH + 4 * B_WIDTH + 2 * C_WIDTH
NA_KH = 8
NA_KW = 16
ROPE_THETA = 10000.0
RGLRU_C = 8.0
CONV_W = 4
CONV_LEFT = 2
NORM_EPS = 1e-6
SUBLN_EPS = 1e-5
NEG_INF = -1e30

HB = HEAD_DIM
COL_QA, COL_KA, COL_VA, COL_GA = 0, A_HEADS, 2 * A_HEADS, 3 * A_HEADS
COL_QB = 4 * A_HEADS
COL_KB, COL_VB, COL_GB = COL_QB + B_HEADS, COL_QB + 2 * B_HEADS, COL_QB + 3 * B_HEADS
COL_XC = COL_QB + 4 * B_HEADS
COL_GC = COL_XC + C_WIDTH // HB

MOD_ROWS = 16
CTX_MOD_ROW = BATCH

VMEM_LIMIT = 56 * 1024 * 1024
SUBLANES = 8

NB_QROWS = 4
NB_KROWS = 12
NB_GROUPS = GRID_H // NB_QROWS
NB_Q = NB_QROWS * GRID_W
NB_K = NB_KROWS * GRID_W


def _params(sem):
    return pltpu.CompilerParams(dimension_semantics=sem, vmem_limit_bytes=VMEM_LIMIT)


def _sigmoid(x):
    return 1.0 / (1.0 + jnp.exp(-x))


def _silu(x):
    return x * _sigmoid(x)


def _adaln_kernel(c_ref, w_ref, b_ref, o_ref):
    s = _silu(c_ref[...]).astype(BF16)
    w = w_ref[0].astype(BF16)
    o_ref[0] = jnp.dot(s, w, preferred_element_type=F32) + b_ref[0]


def adaln_modulation(cvec, ada_w, ada_b):
    tn = 512
    n3 = 3 * D_MODEL
    return pl.pallas_call(
        _adaln_kernel,
        out_shape=jax.ShapeDtypeStruct((DEPTH, MOD_ROWS, n3), F32),
        grid=(DEPTH, n3 // tn),
        in_specs=[
            pl.BlockSpec((MOD_ROWS, D_MODEL), lambda l, j: (0, 0)),
            pl.BlockSpec((1, D_MODEL, tn), lambda l, j: (l, 0, j)),
            pl.BlockSpec((1, 1, tn), lambda l, j: (l, 0, j)),
        ],
        out_specs=pl.BlockSpec((1, MOD_ROWS, tn), lambda l, j: (l, 0, j)),
        compiler_params=_params(("arbitrary", "arbitrary")),
        name="adaln_modulation",
    )(cvec, ada_w, ada_b.reshape(DEPTH, 1, n3))


_NORM_CHUNK = 32


def _inproj_kernel(x_ref, nw_ref, shift_ref, scale_ref, w_ref, o_ref, hx_ref):
    @pl.when(pl.program_id(1) == 0)
    def _():
        gain = nw_ref[...] * (1.0 + scale_ref[0])
        add = shift_ref[0]

        def body(r, carry):
            rows = pl.ds(pl.multiple_of(r * _NORM_CHUNK, _NORM_CHUNK), _NORM_CHUNK)
            x = x_ref[rows, :]
            inv = lax.rsqrt(jnp.mean(x * x, axis=-1, keepdims=True) + NORM_EPS)
            hx_ref[rows, :] = (x_ref[rows, :] * inv * gain + add).astype(BF16)
            return carry

        lax.fori_loop(0, hx_ref.shape[0] // _NORM_CHUNK, body, 0)

    o_ref[...] = jnp.dot(hx_ref[...], w_ref[...], preferred_element_type=F32).astype(o_ref.dtype)


class ProjCols:
    def __init__(self, groups):
        self.start, pos = {}, 0
        for name, _, width in groups:
            self.start[name] = pos
            pos += width
        self.width = pos * HB
        self.src_blocks = [src + k for _, src, width in groups for k in range(width)]

    def __getitem__(self, name):
        return self.start[name]


_ALL_GROUPS = (("qa", COL_QA, A_HEADS), ("ka", COL_KA, A_HEADS), ("va", COL_VA, A_HEADS),
               ("ga", COL_GA, A_HEADS), ("qb", COL_QB, B_HEADS), ("kb", COL_KB, B_HEADS),
               ("vb", COL_VB, B_HEADS), ("gb", COL_GB, B_HEADS), ("xc", COL_XC, C_WIDTH // HB),
               ("gc", COL_GC, C_WIDTH // HB))
FULL_COLS = ProjCols(_ALL_GROUPS)
KV_COLS = ProjCols(tuple(g for g in _ALL_GROUPS if g[0] in ("ka", "va", "kb", "vb", "xc")))


def input_projection(x2d, norm_w3, mod3, w_bf16, layer, rows_per_mod, mod_row0, cols=FULL_COLS):
    m = x2d.shape[0]
    tm = 512
    tn = 1024 if cols is FULL_COLS else 512
    per_tile = tn // HB
    n_tiles = cols.width // tn
    src_tiles = [cols.src_blocks[t * per_tile] // per_tile for t in range(n_tiles)]
    assert all(cols.src_blocks[t * per_tile + k] == src_tiles[t] * per_tile + k
               for t in range(n_tiles) for k in range(per_tile))

    def src_tile(j):
        t = j + src_tiles[0]
        for k in range(1, n_tiles):
            jump = (src_tiles[k] - k) - (src_tiles[k - 1] - (k - 1))
            if jump:
                t = t + jnp.where(j >= k, jump, 0)
        return t

    def mod_row(i):
        return mod_row0 + (i * tm) // rows_per_mod

    return pl.pallas_call(
        _inproj_kernel,
        out_shape=jax.ShapeDtypeStruct((m, cols.width), BF16),
        grid=(m // tm, n_tiles),
        in_specs=[
            pl.BlockSpec((tm, D_MODEL), lambda i, j: (i, 0)),
            pl.BlockSpec((None, 1, D_MODEL), lambda i, j: (layer, 0, 0)),
            pl.BlockSpec((1, 1, D_MODEL), lambda i, j: (mod_row(i), 0, 0)),
            pl.BlockSpec((1, 1, D_MODEL), lambda i, j: (mod_row(i), 0, 1)),
            pl.BlockSpec((None, D_MODEL, tn), lambda i, j: (layer, 0, src_tile(j))),
        ],
        out_specs=pl.BlockSpec((tm, tn), lambda i, j: (i, j)),
        scratch_shapes=[pltpu.VMEM((tm, D_MODEL), BF16)],
        compiler_params=_params(("arbitrary", "arbitrary")),
        name="input_projection",
    )(x2d, norm_w3, mod3, mod3, w_bf16)


def _rope(x, cos, sin_lo, sin_hi):
    return (x * cos + pltpu.roll(x, HEAD_DIM - 16, axis=1) * sin_lo
            + pltpu.roll(x, 16, axis=1) * sin_hi)


def _diff_lambda(lq_ref, lam_init):
    lq = lq_ref[...]
    a = jnp.sum(lq[0:1] * lq[1:2], axis=-1, keepdims=True)
    b = jnp.sum(lq[2:3] * lq[3:4], axis=-1, keepdims=True)
    return jnp.exp(a) - jnp.exp(b) + lam_init


LOG2E = 1.4426950408889634
NKEYS = CTX_LEN + SEQ
DA_TQ = 256
DA_W = 2 * DA_TQ
DA_ROWS = 16
DA_KCHUNK = 256


def _transpose_bf16(x):
    return x.astype(F32).T.astype(BF16)


def _diffattn_lat_kernel(q_ref, kc_ref, vc_ref, kl_ref, vl_ref, g_ref, cos_ref, slo_ref, shi_ref,
                         lq_ref, sw_ref, o_ref, k_scr, vt_scr, qt_scr, s0_scr, s1_scr, m_scr, p_scr,
                         *, lam_init):
    qi = pl.program_id(2)
    n_tiles = SEQ // DA_TQ
    n_chunks = NKEYS // DA_KCHUNK
    groups_per_chunk = DA_KCHUNK // DA_ROWS

    def score_chunk(s_scr, qt2, c, m_run):
        rows = slice(c * DA_KCHUNK, (c + 1) * DA_KCHUNK)
        s = jnp.dot(k_scr[rows, :], qt2, preferred_element_type=F32)
        s_scr[rows, :] = s
        parts = [s[i * DA_ROWS:(i + 1) * DA_ROWS, :] for i in range(groups_per_chunk)]
        if m_run is not None:
            parts.append(m_run)
        while len(parts) > 1:
            parts = [jnp.maximum(parts[i], parts[i + 1]) for i in range(0, len(parts) - 1, 2)] + (
                [parts[-1]] if len(parts) % 2 else [])
        return parts[0]

    def store_colmax(m_run):
        m_scr[...] = jnp.broadcast_to(jnp.max(m_run, axis=0, keepdims=True), (DA_ROWS, DA_W))

    @pl.when(qi == 0)
    def _():
        k_scr[0:CTX_LEN, :] = kc_ref[...]
        vt_scr[:, 0:CTX_LEN] = _transpose_bf16(vc_ref[...])
        dim = lax.broadcasted_iota(jnp.int32, (HB, DA_TQ), 0)
        for c in range(SEQ // DA_TQ):
            rows = slice(c * DA_TQ, (c + 1) * DA_TQ)
            dst = slice(CTX_LEN + c * DA_TQ, CTX_LEN + (c + 1) * DA_TQ)
            cos, slo, shi = cos_ref[rows, :], slo_ref[rows, :], shi_ref[rows, :]
            k_scr[dst, :] = _rope(kl_ref[rows, :].astype(F32), cos, slo, shi).astype(BF16)
            vt_scr[:, dst] = _transpose_bf16(vl_ref[rows, :])
            q = _rope(q_ref[rows, :].astype(F32), cos, slo, shi)
            qt = (q * (A_QK_DIM ** -0.5 * LOG2E)).T
            qt_scr[:, c * DA_W:c * DA_W + DA_TQ] = jnp.where(dim < A_QK_DIM, qt, 0.0).astype(BF16)
            qt_scr[:, c * DA_W + DA_TQ:(c + 1) * DA_W] = jnp.where(dim >= A_QK_DIM, qt, 0.0).astype(BF16)
        m_run = None
        for c in range(n_chunks):
            m_run = score_chunk(s0_scr, qt_scr[:, 0:DA_W], c, m_run)
        store_colmax(m_run)

    def tile(s_scr, s_next_scr):
        if s_next_scr is not None:
            qt_next = qt_scr[:, pl.ds(pl.multiple_of((qi + 1) * DA_W, DA_W), DA_W)]
        mb = m_scr[...]
        m_run = None
        l_acc = [None, None]
        acc = None

        def value_chunk(c, acc):
            rows = slice(c * DA_KCHUNK, (c + 1) * DA_KCHUNK)
            part = jnp.dot(vt_scr[:, rows], p_scr[rows, :], preferred_element_type=F32)
            return part if acc is None else acc + part

        for c in range(n_chunks):
            if s_next_scr is not None:
                m_run = score_chunk(s_next_scr, qt_next, c, m_run)
            for r in range(c * groups_per_chunk, (c + 1) * groups_per_chunk):
                grp = slice(r * DA_ROWS, (r + 1) * DA_ROWS)
                p = jnp.exp2(s_scr[grp, :] - mb)
                p_scr[grp, :] = p.astype(BF16)
                l_acc[r % 2] = p if l_acc[r % 2] is None else l_acc[r % 2] + p
            acc = value_chunk(c, acc)
        if s_next_scr is not None:
            store_colmax(m_run)
        l = jnp.sum(l_acc[0] + l_acc[1], axis=0, keepdims=True)

        lam = _diff_lambda(lq_ref, lam_init)
        ot = acc[:, :DA_TQ] * (1.0 / l[:, :DA_TQ]) - acc[:, DA_TQ:] * (lam / l[:, DA_TQ:])
        ot = ot * lax.rsqrt(jnp.mean(ot * ot, axis=0, keepdims=True) + SUBLN_EPS)
        o = ot.T * sw_ref[...] * (1.0 - lam_init)
        o_ref[...] = (o * _silu(g_ref[...].astype(F32))).astype(o_ref.dtype)

    parity = qi % 2
    last = qi == n_tiles - 1
    pl.when(parity == 0)(functools.partial(tile, s0_scr, s1_scr))
    pl.when(jnp.logical_and(parity == 1, jnp.logical_not(last)))(functools.partial(tile, s1_scr, s0_scr))
    pl.when(last)(functools.partial(tile, s1_scr, None))


_NT = (((1,), (1,)), ((), ()))


def _diffattn_ctx_kernel(q_ref, k_ref, v_ref, g_ref, lq_ref, sw_ref, o_ref, *, lam_init):
    q = q_ref[...].astype(F32) * (A_QK_DIM ** -0.5)
    lane = lax.broadcasted_iota(jnp.int32, q.shape, 1)
    outs = []
    for qn in (jnp.where(lane < A_QK_DIM, q, 0.0), jnp.where(lane >= A_QK_DIM, q, 0.0)):
        s = lax.dot_general(qn.astype(BF16), k_ref[...], _NT, preferred_element_type=F32)
        p = jnp.exp(s - jnp.max(s, axis=-1, keepdims=True))
        acc = jnp.dot(p.astype(BF16), v_ref[...], preferred_element_type=F32)
        outs.append((acc, jnp.sum(p, axis=-1, keepdims=True)))
    lam = _diff_lambda(lq_ref, lam_init)
    o = outs[0][0] * (1.0 / outs[0][1]) - (lam / outs[1][1]) * outs[1][0]
    o = o * lax.rsqrt(jnp.mean(o * o, axis=-1, keepdims=True) + SUBLN_EPS) * sw_ref[...]
    o = o * (1.0 - lam_init)
    o_ref[...] = (o * _silu(g_ref[...].astype(F32))).astype(o_ref.dtype)


def diff_attention_latent(proj_l, proj_c, ccols, rope_tabs, lambda_qk, subln_w, lam_init):
    nq = SEQ // DA_TQ
    head = lambda col: (lambda b, h, i: (b, col + h))
    qmap = lambda col: (lambda b, h, i: (b * nq + i, col + h))
    const = lambda b, h, i: (0, 0)
    return pl.pallas_call(
        functools.partial(_diffattn_lat_kernel, lam_init=lam_init),
        out_shape=jax.ShapeDtypeStruct((BATCH * SEQ, A_WIDTH), BF16),
        grid=(BATCH, A_HEADS, nq),
        in_specs=[
            pl.BlockSpec((SEQ, HB), head(COL_QA)),
            pl.BlockSpec((CTX_LEN, HB), head(ccols["ka"])),
            pl.BlockSpec((CTX_LEN, HB), head(ccols["va"])),
            pl.BlockSpec((SEQ, HB), head(COL_KA)),
            pl.BlockSpec((SEQ, HB), head(COL_VA)),
            pl.BlockSpec((DA_TQ, HB), qmap(COL_GA)),
            pl.BlockSpec((SEQ, HB), const),
            pl.BlockSpec((SEQ, HB), const),
            pl.BlockSpec((SEQ, HB), const),
            pl.BlockSpec((4, A_QK_DIM), const),
            pl.BlockSpec((1, HB), const),
        ],
        out_specs=pl.BlockSpec((DA_TQ, HB), lambda b, h, i: (b * nq + i, h)),
        scratch_shapes=[pltpu.VMEM((NKEYS, HB), BF16), pltpu.VMEM((HB, NKEYS), BF16),
                        pltpu.VMEM((HB, 2 * SEQ), BF16),
                        pltpu.VMEM((NKEYS, DA_W), F32), pltpu.VMEM((NKEYS, DA_W), F32),
                        pltpu.VMEM((DA_ROWS, DA_W), F32), pltpu.VMEM((NKEYS, DA_W), BF16)],
        compiler_params=_params(("arbitrary", "arbitrary", "arbitrary")),
        name="diff_attention_latent",
    )(proj_l, proj_c, proj_c, proj_l, proj_l, proj_l, *rope_tabs, lambda_qk, subln_w.reshape(1, HB))


def diff_attention_context(proj_c, lambda_qk, subln_w, lam_init):
    head = lambda col: (lambda b, h: (b, col + h))
    const = lambda b, h: (0, 0)
    return pl.pallas_call(
        functools.partial(_diffattn_ctx_kernel, lam_init=lam_init),
        out_shape=jax.ShapeDtypeStruct((BATCH * CTX_LEN, A_WIDTH), BF16),
        grid=(BATCH, A_HEADS),
        in_specs=[pl.BlockSpec((CTX_LEN, HB), head(COL_QA)),
                  pl.BlockSpec((CTX_LEN, HB), head(COL_KA)),
                  pl.BlockSpec((CTX_LEN, HB), head(COL_VA)),
                  pl.BlockSpec((CTX_LEN, HB), head(COL_GA)),
                  pl.BlockSpec((4, A_QK_DIM), const),
                  pl.BlockSpec((1, HB), const)],
        out_specs=pl.BlockSpec((CTX_LEN, HB), lambda b, h: (b, h)),
        compiler_params=_params(("arbitrary", "arbitrary")),
        name="diff_attention_context",
    )(proj_c, proj_c, proj_c, proj_c, lambda_qk, subln_w.reshape(1, HB))


def _nbr_slab_start(g):
    return min(max(g * NB_QROWS - NA_KH // 2, 0), GRID_H - NB_KROWS)


def _nbr_pattern(g):
    return 0 if g == 0 else (2 if g == NB_GROUPS - 1 else 1)


def _nbr_kernel(q_ref, kc_ref, vc_ref, kl_ref, vl_ref, g_ref, bias_ref, o_ref, vt_scr):
    vt_scr[:, 0:CTX_LEN] = _transpose_bf16(vc_ref[...])
    for c in range(SEQ // NB_Q):
        vt_scr[:, CTX_LEN + c * NB_Q:CTX_LEN + (c + 1) * NB_Q] = _transpose_bf16(vl_ref[c * NB_Q:(c + 1) * NB_Q, :])
    def scores(g):
        rows = slice(g * NB_Q, (g + 1) * NB_Q)
        k0 = _nbr_slab_start(g) * GRID_W
        qt = (q_ref[rows, :].astype(F32) * (HEAD_DIM ** -0.5 * LOG2E)).T.astype(BF16)
        s_c = jnp.dot(kc_ref[...], qt, preferred_element_type=F32)
        s_n = (jnp.dot(kl_ref[k0:k0 + NB_K, :], qt, preferred_element_type=F32)
               + bias_ref[0, _nbr_pattern(g)])
        return s_c, s_n

    def softmax(s_c, s_n):
        m = jnp.maximum(jnp.max(s_c, axis=0, keepdims=True), jnp.max(s_n, axis=0, keepdims=True))
        p_c = jnp.exp2(s_c - m)
        p_n = jnp.exp2(s_n - m)
        l = jnp.sum(p_c, axis=0, keepdims=True) + jnp.sum(p_n, axis=0, keepdims=True)
        return p_c.astype(BF16), p_n.astype(BF16), l

    def finish(g, p_c, p_n, l):
        rows = slice(g * NB_Q, (g + 1) * NB_Q)
        k0 = _nbr_slab_start(g) * GRID_W
        acc = (jnp.dot(vt_scr[:, 0:CTX_LEN], p_c, preferred_element_type=F32)
               + jnp.dot(vt_scr[:, CTX_LEN + k0:CTX_LEN + k0 + NB_K], p_n,
                         preferred_element_type=F32))
        o = (acc * (1.0 / l)).T
        o_ref[rows, :] = (o * _silu(g_ref[rows, :].astype(F32))).astype(o_ref.dtype)

    ahead = scores(0)
    probs = None
    for g in range(NB_GROUPS):
        s_c, s_n = ahead
        if g + 1 < NB_GROUPS:
            ahead = scores(g + 1)
        new_probs = softmax(s_c, s_n)
        if probs is not None:
            finish(g - 1, *probs)
        probs = new_probs
    finish(NB_GROUPS - 1, *probs)


def neighbourhood_bias_table(rpb):
    w = GRID_W
    cq = np.arange(w)
    c0 = np.clip(cq - NA_KW // 2, 0, w - NA_KW)
    ck = np.arange(w)
    col_valid = (ck[None, :] >= c0[:, None]) & (ck[None, :] < c0[:, None] + NA_KW)
    pad = w - NA_KW
    padded = jnp.pad(rpb.astype(F32), ((0, 0), (0, 0), (pad, pad)))
    toep = jnp.stack([padded[:, :, w - 1 - q:2 * w - 1 - q] for q in range(w)], axis=2)
    toep = jnp.where(col_valid[None, None], toep, NEG_INF)
    toep_t = jnp.swapaxes(toep, 2, 3) * LOG2E
    masked = jnp.full((B_HEADS, w, w), NEG_INF, F32)
    tabs = []
    for g in (0, 1, NB_GROUPS - 1):
        slab0 = _nbr_slab_start(g)
        qr = g * NB_QROWS + np.arange(NB_QROWS)
        kr = slab0 + np.arange(NB_KROWS)
        win0 = np.clip(qr - NA_KH // 2, 0, GRID_H - NA_KH)
        row_valid = (kr[None, :] >= win0[:, None]) & (kr[None, :] < win0[:, None] + NA_KH)
        dr_idx = kr[None, :] - qr[:, None] + NA_KH - 1
        rows = [jnp.concatenate([toep_t[:, int(dr_idx[a, k])] if row_valid[a, k] else masked
                                 for a in range(NB_QROWS)], axis=-1) for k in range(NB_KROWS)]
        tabs.append(jnp.concatenate(rows, axis=1))
    return jnp.stack(tabs, axis=1)


def neighbourhood_attention(proj_l, proj_c, ccols, bias_tab):
    head = lambda col: (lambda h, b: (b, col + h))
    return pl.pallas_call(
        _nbr_kernel,
        out_shape=jax.ShapeDtypeStruct((BATCH * SEQ, B_WIDTH), BF16),
        grid=(B_HEADS, BATCH),
        in_specs=[
            pl.BlockSpec((SEQ, HB), head(COL_QB)),
            pl.BlockSpec((CTX_LEN, HB), head(ccols["kb"])),
            pl.BlockSpec((CTX_LEN, HB), head(ccols["vb"])),
            pl.BlockSpec((SEQ, HB), head(COL_KB)),
            pl.BlockSpec((SEQ, HB), head(COL_VB)),
            pl.BlockSpec((SEQ, HB), head(COL_GB)),
            pl.BlockSpec((1, 3, NB_K, NB_Q), lambda h, b: (h, 0, 0, 0)),
        ],
        out_specs=pl.BlockSpec((SEQ, HB), lambda h, b: (b, h)),
        scratch_shapes=[pltpu.VMEM((HB, NKEYS), BF16)],
        compiler_params=_params(("arbitrary", "arbitrary")),
        name="neighbourhood_attention",
    )(proj_l, proj_c, proj_c, proj_l, proj_l, proj_l, bias_tab)


def _dense_ctx_kernel(q_ref, k_ref, v_ref, g_ref, o_ref):
    scale = HEAD_DIM ** -0.5
    s = lax.dot_general(q_ref[...], k_ref[...], _NT, preferred_element_type=F32) * scale
    p = jnp.exp(s - jnp.max(s, axis=-1, keepdims=True))
    p = p * (1.0 / jnp.sum(p, axis=-1, keepdims=True))
    o = jnp.dot(p.astype(BF16), v_ref[...], preferred_element_type=F32)
    o_ref[...] = (o * _silu(g_ref[...].astype(F32))).astype(o_ref.dtype)


def dense_context_attention(proj_c):
    head = lambda col: (lambda b, h: (b, col + h))
    return pl.pallas_call(
        _dense_ctx_kernel,
        out_shape=jax.ShapeDtypeStruct((BATCH * CTX_LEN, B_WIDTH), BF16),
        grid=(BATCH, B_HEADS),
        in_specs=[pl.BlockSpec((CTX_LEN, HB), head(COL_QB)),
                  pl.BlockSpec((CTX_LEN, HB), head(COL_KB)),
                  pl.BlockSpec((CTX_LEN, HB), head(COL_VB)),
                  pl.BlockSpec((CTX_LEN, HB), head(COL_GB))],
        out_specs=pl.BlockSpec((CTX_LEN, HB), lambda b, h: (b, h)),
        compiler_params=_params(("arbitrary", "arbitrary")),
        name="dense_context_attention",
    )(proj_c, proj_c, proj_c, proj_c)


def _shift_rows(x, d, fill):
    n = x.shape[0]
    if d % SUBLANES == 0:
        pad = jnp.full((abs(d), x.shape[1]), fill, x.dtype)
        return jnp.concatenate([pad, x[:n - d]] if d > 0 else [x[-d:], pad], axis=0)
    rolled = pltpu.roll(x, d % n, axis=0)
    row = lax.broadcasted_iota(jnp.int32, x.shape, 0)
    keep = (row >= d) if d > 0 else (row < n + d)
    return jnp.where(keep, rolled, fill)


def _dwconv(x, w_ref, b_ref):
    acc = b_ref[...] + w_ref[CONV_LEFT:CONV_LEFT + 1, :] * x
    for j in range(CONV_W):
        if j != CONV_LEFT:
            acc = acc + w_ref[j:j + 1, :] * _shift_rows(x, CONV_LEFT - j, 0.0)
    return acc


def _linear_scan(a, b, h0, reverse):
    n = a.shape[0]
    row = lax.broadcasted_iota(jnp.int32, a.shape, 0)
    first = (row == n - 1) if reverse else (row == 0)
    b = b + jnp.where(first, a * h0, 0.0)
    d = 1
    while d < n:
        sd = -d if reverse else d
        b = b + a * _shift_rows(b, sd, 0.0)
        if 2 * d < n:
            a = a * _shift_rows(a, sd, 1.0)
        d *= 2
    return b


def _sigmoid_tanh(x):
    return 0.5 * jnp.tanh(0.5 * x) + 0.5


def _rglru_gates(u, wa_ref, ba_ref, wx_ref, bx_ref, lam_ref, d):
    ub = u.astype(BF16)
    r = _sigmoid_tanh(jnp.dot(ub, wa_ref[0, d], preferred_element_type=F32) + ba_ref[0, d])
    i = _sigmoid_tanh(jnp.dot(ub, wx_ref[0, d], preferred_element_type=F32) + bx_ref[0, d])
    neg_lam = -lam_ref[0, d]
    softplus = jnp.maximum(neg_lam, 0.0) + jnp.log1p(jnp.exp(-jnp.abs(neg_lam)))
    log_a = (-RGLRU_C) * r * softplus
    a = jnp.exp(log_a)
    t = jnp.tanh(log_a)
    b = jnp.sqrt(-2.0 * t / (1.0 - t)) * (i * u)
    return a, b


SCAN_CHUNKS = SUBLANES


def _scan_chunk_len(n):
    return n // SCAN_CHUNKS + 1


def _blocked_scan(a_scr, b_scr, n, h0, reverse, need_states):
    length = _scan_chunk_len(n)
    order = range(length - 1, -1, -1) if reverse else range(length)
    h = jnp.zeros((SCAN_CHUNKS, a_scr.shape[1]), F32)
    p = jnp.ones_like(h)
    hs, ps = [None] * length, [None] * length
    for i in order:
        va = a_scr[pl.ds(i, SCAN_CHUNKS, stride=length), :]
        vb = b_scr[pl.ds(i, SCAN_CHUNKS, stride=length), :]
        h = va * h + vb
        p = va * p
        hs[i], ps[i] = h, p
    rows = [None] * SCAN_CHUNKS
    carry = h0
    for j in (range(SCAN_CHUNKS - 1, -1, -1) if reverse else range(SCAN_CHUNKS)):
        rows[j] = carry
        carry = h[j:j + 1, :] + p[j:j + 1, :] * carry
    if not need_states:
        return None, carry
    start = jnp.concatenate(rows, axis=0)
    return [hs[i] + ps[i] * start for i in range(length)], carry


def _rglru_kernel(*refs, need_ctx):
    if need_ctx:
        (xl_ref, xc_ref, gl_ref, gc_ref, cw_ref, cb_ref, wa_ref, ba_ref, wx_ref, bx_ref, lam_ref,
         ol_ref, oc_ref, al_scr, bl_scr, ac_scr, bc_scr, yl_scr, yc_scr) = refs
    else:
        (xl_ref, xc_ref, gl_ref, cw_ref, cb_ref, wa_ref, ba_ref, wx_ref, bx_ref, lam_ref,
         ol_ref, al_scr, bl_scr, ac_scr, bc_scr, yl_scr, yc_scr) = refs
    ct = al_scr.shape[1]
    u_l = _dwconv(xl_ref[...].astype(F32), cw_ref, cb_ref)
    u_c = _dwconv(xc_ref[...].astype(F32), cw_ref, cb_ref)
    for a_scr, b_scr, n in ((al_scr, bl_scr, SEQ), (ac_scr, bc_scr, CTX_LEN)):
        a_scr[n:n + SUBLANES, :] = jnp.ones((SUBLANES, ct), F32)
        b_scr[n:n + SUBLANES, :] = jnp.zeros((SUBLANES, ct), F32)
    y_l = None
    y_c = None
    for d in range(2):
        rev = d == 1
        a_c, b_c = _rglru_gates(u_c, wa_ref, ba_ref, wx_ref, bx_ref, lam_ref, d)
        ac_scr[0:CTX_LEN, :] = a_c
        bc_scr[0:CTX_LEN, :] = b_c
        h_c, fin = _blocked_scan(ac_scr, bc_scr, CTX_LEN, jnp.zeros((1, ct), F32), rev, need_ctx)
        a_l, b_l = _rglru_gates(u_l, wa_ref, ba_ref, wx_ref, bx_ref, lam_ref, d)
        al_scr[0:SEQ, :] = a_l
        bl_scr[0:SEQ, :] = b_l
        h_l, _ = _blocked_scan(al_scr, bl_scr, SEQ, fin, rev, True)
        y_l = h_l if y_l is None else [f + r for f, r in zip(y_l, h_l)]
        if need_ctx:
            y_c = h_c if y_c is None else [f + r for f, r in zip(y_c, h_c)]
    length = _scan_chunk_len(SEQ)
    for i in range(length):
        yl_scr[pl.ds(i, SCAN_CHUNKS, stride=length), :] = y_l[i]
    ol_ref[...] = (yl_scr[0:SEQ, :] * _silu(gl_ref[...].astype(F32))).astype(ol_ref.dtype)
    if need_ctx:
        length = _scan_chunk_len(CTX_LEN)
        for i in range(length):
            yc_scr[pl.ds(i, SCAN_CHUNKS, stride=length), :] = y_c[i]
        oc_ref[...] = (yc_scr[0:CTX_LEN, :] * _silu(gc_ref[...].astype(F32))).astype(oc_ref.dtype)


def _block_diag(w, per_tile):
    n = w.shape[-1]
    tiles = C_BLOCKS // per_tile
    w = w.reshape(2, tiles, per_tile, n, n)
    eye = jnp.eye(per_tile, dtype=w.dtype)
    bd = jnp.einsum('dtpij,pq->dtpiqj', w, eye).reshape(2, tiles, per_tile * n, per_tile * n)
    return jnp.transpose(bd, (1, 0, 2, 3)).astype(BF16)


def rglru_mixer(proj_l, proj_c, ccols, conv_w, conv_b, rg_wa, rg_ba, rg_wx, rg_bx, rg_lambda, need_ctx):
    ct = 128
    nt = C_WIDTH // ct
    per_tile = ct // C_BLOCK_DIM
    vec = lambda p: jnp.transpose(p.reshape(2, nt, 1, ct), (1, 0, 2, 3))
    wa = _block_diag(rg_wa, per_tile)
    wx = _block_diag(rg_wx, per_tile)
    xcol = lambda col: (lambda b, t: (b, col + t))
    par = lambda b, t: (t, 0, 0, 0)
    out_shape = [jax.ShapeDtypeStruct((BATCH * SEQ, C_WIDTH), BF16)]
    out_specs = [pl.BlockSpec((SEQ, ct), lambda b, t: (b, t))]
    if need_ctx:
        out_shape.append(jax.ShapeDtypeStruct((BATCH * CTX_LEN, C_WIDTH), BF16))
        out_specs.append(pl.BlockSpec((CTX_LEN, ct), lambda b, t: (b, t)))
    in_specs = [pl.BlockSpec((SEQ, ct), xcol(COL_XC)),
                pl.BlockSpec((CTX_LEN, ct), xcol(ccols["xc"])),
                pl.BlockSpec((SEQ, ct), xcol(COL_GC))]
    args = [proj_l, proj_c, proj_l]
    if need_ctx:
        in_specs.append(pl.BlockSpec((CTX_LEN, ct), xcol(ccols["gc"])))
        args.append(proj_c)
    in_specs += [pl.BlockSpec((CONV_W, ct), lambda b, t: (0, t)),
                 pl.BlockSpec((1, ct), lambda b, t: (0, t)),
                 pl.BlockSpec((1, 2, ct, ct), par),
                 pl.BlockSpec((1, 2, 1, ct), par),
                 pl.BlockSpec((1, 2, ct, ct), par),
                 pl.BlockSpec((1, 2, 1, ct), par),
                 pl.BlockSpec((1, 2, 1, ct), par)]
    args += [conv_w, conv_b.reshape(1, C_WIDTH), wa, vec(rg_ba), wx, vec(rg_bx), vec(rg_lambda)]
    outs = pl.pallas_call(
        functools.partial(_rglru_kernel, need_ctx=need_ctx),
        out_shape=out_shape,
        grid=(BATCH, nt),
        in_specs=in_specs,
        out_specs=out_specs,
        scratch_shapes=([pltpu.VMEM((SEQ + SUBLANES, ct), F32)] * 2
                        + [pltpu.VMEM((CTX_LEN + SUBLANES, ct), F32)] * 2
                        + [pltpu.VMEM((SEQ + SUBLANES, ct), F32), pltpu.VMEM((CTX_LEN + SUBLANES, ct), F32)]),
        compiler_params=_params(("arbitrary", "arbitrary")),
        name="rglru_mixer",
    )(*args)
    return outs if need_ctx else (outs[0], None)


def _outproj_kernel(x_ref, gate_ref, ma_ref, mb_ref, mc_ref, wa_ref, wb_ref, wc_ref, o_ref):
    y = jnp.dot(ma_ref[...], wa_ref[...], preferred_element_type=F32)
    y = y + jnp.dot(mb_ref[...], wb_ref[...], preferred_element_type=F32)
    y = y + jnp.dot(mc_ref[...], wc_ref[...], preferred_element_type=F32)
    o_ref[...] = x_ref[...] + gate_ref[0] * y


def output_projection(x2d, mod3, ma, mb, mc, w_bf16, layer, rows_per_mod, mod_row0):
    m = x2d.shape[0]
    tm, tn = 1024, 512

    def mod_row(i):
        return mod_row0 + (i * tm) // rows_per_mod

    return pl.pallas_call(
        _outproj_kernel,
        out_shape=jax.ShapeDtypeStruct((m, D_MODEL), F32),
        grid=(m // tm, D_MODEL // tn),
        in_specs=[
            pl.BlockSpec((tm, tn), lambda i, j: (i, j)),
            pl.BlockSpec((1, 1, tn), lambda i, j: (mod_row(i), 0, 2 * (D_MODEL // tn) + j)),
            pl.BlockSpec((tm, A_WIDTH), lambda i, j: (i, 0)),
            pl.BlockSpec((tm, B_WIDTH), lambda i, j: (i, 0)),
            pl.BlockSpec((tm, C_WIDTH), lambda i, j: (i, 0)),
            pl.BlockSpec((None, A_WIDTH, tn), lambda i, j: (layer, 0, j)),
            pl.BlockSpec((None, B_WIDTH, tn), lambda i, j: (layer, 1, j)),
            pl.BlockSpec((None, C_WIDTH, tn), lambda i, j: (layer, (A_WIDTH + B_WIDTH) // C_WIDTH, j)),
        ],
        out_specs=pl.BlockSpec((tm, tn), lambda i, j: (i, j)),
        compiler_params=_params(("arbitrary", "arbitrary")),
        name="output_projection",
    )(x2d, mod3, ma, mb, mc, w_bf16, w_bf16, w_bf16)


def _final_norm_kernel(x_ref, w_ref, o_ref):
    x = x_ref[...]
    o_ref[...] = x * lax.rsqrt(jnp.mean(x * x, axis=-1, keepdims=True) + NORM_EPS) * w_ref[...]


def final_norm(x2d, w):
    tm = 128
    return pl.pallas_call(
        _final_norm_kernel,
        out_shape=jax.ShapeDtypeStruct(x2d.shape, F32),
        grid=(x2d.shape[0] // tm,),
        in_specs=[pl.BlockSpec((tm, D_MODEL), lambda i: (i, 0)),
                  pl.BlockSpec((1, D_MODEL), lambda i: (0, 0))],
        out_specs=pl.BlockSpec((tm, D_MODEL), lambda i: (i, 0)),
        compiler_params=_params(("arbitrary",)),
        name="final_norm",
    )(x2d, w.reshape(1, D_MODEL))


def _rope_tables():
    t = np.arange(SEQ)
    pos = np.stack([t // GRID_W, t % GRID_W], axis=0).astype(np.float32)
    lane = np.arange(HEAD_DIM)
    within = lane % A_QK_DIM
    axis = within // (A_QK_DIM // 2)
    k = within % (A_QK_DIM // 2)
    nfreq = A_QK_DIM // 4
    upper = k >= nfreq
    freqs = jnp.asarray(ROPE_THETA, F32) ** (-jnp.arange(nfreq, dtype=F32) / nfreq)
    ang = jnp.asarray(pos)[axis, :].T * freqs[k % nfreq][None, :]
    cos, sin = jnp.cos(ang), jnp.sin(ang)
    sin_lo = jnp.where(upper[None, :], 0.0, -sin)
    sin_hi = jnp.where(upper[None, :], sin, 0.0)
    return cos, sin_lo, sin_hi


def kernel(x, c, ctx, c_ctx, ada_w, ada_b, norm_w, w_in, w_out, lambda_qk, subln_w, rpb,
           conv_w, conv_b, rg_wa, rg_ba, rg_wx, rg_bx, rg_lambda, final_norm_w):
    cvec = jnp.zeros((MOD_ROWS, D_MODEL), F32).at[:BATCH].set(c).at[CTX_MOD_ROW].set(c_ctx)
    mod = adaln_modulation(cvec, ada_w, ada_b)
    mod3 = mod.reshape(DEPTH * MOD_ROWS, 1, 3 * D_MODEL)
    norm_w3 = norm_w.reshape(DEPTH, 1, D_MODEL)
    w_in_bf = w_in.astype(BF16)
    w_out_bf = w_out.astype(BF16)
    rope_tabs = _rope_tables()
    xl = x.reshape(BATCH * SEQ, D_MODEL)
    xc = ctx.reshape(BATCH * CTX_LEN, D_MODEL)
    for l in range(DEPTH):
        need_ctx = l < DEPTH - 1
        lam_init = 0.8 - 0.6 * math.exp(-0.3 * l)
        row_l, row_c = l * MOD_ROWS, l * MOD_ROWS + CTX_MOD_ROW
        proj_l = input_projection(xl, norm_w3, mod3, w_in_bf, l, SEQ, row_l)
        ccols = FULL_COLS if need_ctx else KV_COLS
        proj_c = input_projection(xc, norm_w3, mod3, w_in_bf, l, BATCH * CTX_LEN, row_c, ccols)

        ma = diff_attention_latent(proj_l, proj_c, ccols, rope_tabs, lambda_qk[l], subln_w[l], lam_init)
        mb = neighbourhood_attention(proj_l, proj_c, ccols, neighbourhood_bias_table(rpb[l]))
        mc, mc_c = rglru_mixer(proj_l, proj_c, ccols, conv_w[l], conv_b[l], rg_wa[l], rg_ba[l],
                               rg_wx[l], rg_bx[l], rg_lambda[l], need_ctx)
        xl = output_projection(xl, mod3, ma, mb, mc, w_out_bf, l, SEQ, row_l)
        if need_ctx:
            ma_c = diff_attention_context(proj_c, lambda_qk[l], subln_w[l], lam_init)
            mb_c = dense_context_attention(proj_c)
            xc = output_projection(xc, mod3, ma_c, mb_c, mc_c, w_out_bf, l, BATCH * CTX_LEN, row_c)
    return final_norm(xl, final_norm_w).reshape(BATCH, SEQ, D_MODEL)
```

```python
import functools
import math

import numpy as np
import jax
import jax.numpy as jnp
from jax import lax
from jax.experimental import pallas as pl
from jax.experimental.pallas import tpu as pltpu

F32 = jnp.float32
BF16 = jnp.bfloat16

D_MODEL = 4096
BATCH = 8
SEQ = 2048
DEPTH = 2
GRID_W = 64
GRID_H = SEQ // GRID_W
CTX_LEN = 256
HEAD_DIM = 128
A_WIDTH = (3 * D_MODEL) // 8
A_HEADS = A_WIDTH // HEAD_DIM
A_QK_DIM = HEAD_DIM // 2
B_WIDTH = (3 * D_MODEL) // 8
B_HEADS = B_WIDTH // HEAD_DIM
C_WIDTH = D_MODEL - A_WIDTH - B_WIDTH
C_BLOCKS = 16
C_BLOCK_DIM = C_WIDTH // C_BLOCKS
MIX_WIDTH = A_WIDTH + B_WIDTH + C_WIDTH
IN_WIDTH = 4 * A_WIDTH + 4 * B_WIDTH + 2 * C_WIDTH
NA_KH = 8
NA_KW = 16
ROPE_THETA = 10000.0
RGLRU_C = 8.0
CONV_W = 4
CONV_LEFT = 2
NORM_EPS = 1e-6
SUBLN_EPS = 1e-5
NEG_INF = -1e30

HB = HEAD_DIM
COL_QA, COL_KA, COL_VA, COL_GA = 0, A_HEADS, 2 * A_HEADS, 3 * A_HEADS
COL_QB = 4 * A_HEADS
COL_KB, COL_VB, COL_GB = COL_QB + B_HEADS, COL_QB + 2 * B_HEADS, COL_QB + 3 * B_HEADS
COL_XC = COL_QB + 4 * B_HEADS
COL_GC = COL_XC + C_WIDTH // HB

MOD_ROWS = 16
CTX_MOD_ROW = BATCH

VMEM_LIMIT = 56 * 1024 * 1024
SUBLANES = 8

NB_QROWS = 4
NB_KROWS = 12
NB_GROUPS = GRID_H // NB_QROWS
NB_Q = NB_QROWS * GRID_W
NB_K = NB_KROWS * GRID_W


def _params(sem):
    return pltpu.CompilerParams(dimension_semantics=sem, vmem_limit_bytes=VMEM_LIMIT)


def _sigmoid(x):
    return 1.0 / (1.0 + jnp.exp(-x))


def _silu(x):
    return x * _sigmoid(x)


def _adaln_kernel(c_ref, w_ref, b_ref, o_ref):
    s = _silu(c_ref[...]).astype(BF16)
    w = w_ref[0].astype(BF16)
    o_ref[0] = jnp.dot(s, w, preferred_element_type=F32) + b_ref[0]


def adaln_modulation(cvec, ada_w, ada_b):
    tn = 512
    n3 = 3 * D_MODEL
    return pl.pallas_call(
        _adaln_kernel,
        out_shape=jax.ShapeDtypeStruct((DEPTH, MOD_ROWS, n3), F32),
        grid=(DEPTH, n3 // tn),
        in_specs=[
            pl.BlockSpec((MOD_ROWS, D_MODEL), lambda l, j: (0, 0)),
            pl.BlockSpec((1, D_MODEL, tn), lambda l, j: (l, 0, j)),
            pl.BlockSpec((1, 1, tn), lambda l, j: (l, 0, j)),
        ],
        out_specs=pl.BlockSpec((1, MOD_ROWS, tn), lambda l, j: (l, 0, j)),
        compiler_params=_params(("arbitrary", "arbitrary")),
        name="adaln_modulation",
    )(cvec, ada_w, ada_b.reshape(DEPTH, 1, n3))


_NORM_CHUNK = 32
_NORM_AHEAD_STEPS = 8


def _inproj_kernel(x_ref, nw_ref, shift_ref, scale_ref, w_ref, o_ref, hx0_ref, hx1_ref):
    i, j = pl.program_id(0), pl.program_id(1)
    tm = hx0_ref.shape[0]
    gain = nw_ref[...] * (1.0 + scale_ref[0])
    add = shift_ref[0]

    def normalise(hx_ref, row0):
        rows = pl.ds(pl.multiple_of(row0, _NORM_CHUNK), _NORM_CHUNK)
        x = x_ref[rows, :]
        inv = lax.rsqrt(jnp.mean(x * x, axis=-1, keepdims=True) + NORM_EPS)
        hx_ref[rows, :] = (x * inv * gain + add).astype(BF16)

    @pl.when(jnp.logical_and(i == 0, j == 0))
    def _():
        def body(r, carry):
            normalise(hx0_ref, r * _NORM_CHUNK)
            return carry
        lax.fori_loop(0, tm // _NORM_CHUNK, body, 0)

    def step(hx_cur, hx_next):
        ahead_rows = tm // _NORM_AHEAD_STEPS
        piece = jnp.clip(j - 1, 0, _NORM_AHEAD_STEPS - 1)
        for k in range(ahead_rows // _NORM_CHUNK):
            normalise(hx_next, piece * ahead_rows + k * _NORM_CHUNK)
        o_ref[...] = jnp.dot(hx_cur[...], w_ref[...], preferred_element_type=F32).astype(o_ref.dtype)

    pl.when(i % 2 == 0)(functools.partial(step, hx0_ref, hx1_ref))
    pl.when(i % 2 == 1)(functools.partial(step, hx1_ref, hx0_ref))


class ProjCols:
    def __init__(self, groups):
        self.start, pos = {}, 0
        for name, _, width in groups:
            self.start[name] = pos
            pos += width
        self.width = pos * HB
        self.src_blocks = [src + k for _, src, width in groups for k in range(width)]

    def __getitem__(self, name):
        return self.start[name]


_ALL_GROUPS = (("qa", COL_QA, A_HEADS), ("ka", COL_KA, A_HEADS), ("va", COL_VA, A_HEADS),
               ("ga", COL_GA, A_HEADS), ("qb", COL_QB, B_HEADS), ("kb", COL_KB, B_HEADS),
               ("vb", COL_VB, B_HEADS), ("gb", COL_GB, B_HEADS), ("xc", COL_XC, C_WIDTH // HB),
               ("gc", COL_GC, C_WIDTH // HB))
FULL_COLS = ProjCols(_ALL_GROUPS)
KV_COLS = ProjCols(tuple(g for g in _ALL_GROUPS if g[0] in ("ka", "va", "kb", "vb", "xc")))


def input_projection(x2d, norm_w3, mod3, w_bf16, layer, rows_per_mod, mod_row0, cols=FULL_COLS):
    m = x2d.shape[0]
    tm = 512
    tn = 1024 if cols is FULL_COLS else 512
    per_tile = tn // HB
    n_tiles = cols.width // tn
    src_tiles = [cols.src_blocks[t * per_tile] // per_tile for t in range(n_tiles)]
    assert all(cols.src_blocks[t * per_tile + k] == src_tiles[t] * per_tile + k
               for t in range(n_tiles) for k in range(per_tile))

    def src_tile(j):
        t = j + src_tiles[0]
        for k in range(1, n_tiles):
            jump = (src_tiles[k] - k) - (src_tiles[k - 1] - (k - 1))
            if jump:
                t = t + jnp.where(j >= k, jump, 0)
        return t

    n_rows = m // tm
    assert n_tiles > _NORM_AHEAD_STEPS and tm % (_NORM_AHEAD_STEPS * _NORM_CHUNK) == 0

    def x_tile(i, j):
        return jnp.minimum(i + jnp.where(j > 0, 1, 0), n_rows - 1)

    def mod_row(i, j):
        return mod_row0 + (x_tile(i, j) * tm) // rows_per_mod

    return pl.pallas_call(
        _inproj_kernel,
        out_shape=jax.ShapeDtypeStruct((m, cols.width), BF16),
        grid=(n_rows, n_tiles),
        in_specs=[
            pl.BlockSpec((tm, D_MODEL), lambda i, j: (x_tile(i, j), 0)),
            pl.BlockSpec((None, 1, D_MODEL), lambda i, j: (layer, 0, 0)),
            pl.BlockSpec((1, 1, D_MODEL), lambda i, j: (mod_row(i, j), 0, 0)),
            pl.BlockSpec((1, 1, D_MODEL), lambda i, j: (mod_row(i, j), 0, 1)),
            pl.BlockSpec((None, D_MODEL, tn), lambda i, j: (layer, 0, src_tile(j))),
        ],
        out_specs=pl.BlockSpec((tm, tn), lambda i, j: (i, j)),
        scratch_shapes=[pltpu.VMEM((tm, D_MODEL), BF16), pltpu.VMEM((tm, D_MODEL), BF16)],
        compiler_params=_params(("arbitrary", "arbitrary")),
        name="input_projection",
    )(x2d, norm_w3, mod3, mod3, w_bf16)


def _rope(x, cos, sin_lo, sin_hi):
    return (x * cos + pltpu.roll(x, HEAD_DIM - 16, axis=1) * sin_lo
            + pltpu.roll(x, 16, axis=1) * sin_hi)


def _diff_lambda(lq_ref, lam_init):
    lq = lq_ref[...]
    a = jnp.sum(lq[0:1] * lq[1:2], axis=-1, keepdims=True)
    b = jnp.sum(lq[2:3] * lq[3:4], axis=-1, keepdims=True)
    return jnp.exp(a) - jnp.exp(b) + lam_init


LOG2E = 1.4426950408889634
NKEYS = CTX_LEN + SEQ
DA_TQ = 256
DA_W = 2 * DA_TQ
DA_ROWS = 16
DA_KCHUNK = 256


def _transpose_bf16(x):
    return x.astype(F32).T.astype(BF16)


def _diffattn_lat_kernel(q_ref, kc_ref, vc_ref, kl_ref, vl_ref, g_ref, cos_ref, slo_ref, shi_ref,
                         lq_ref, sw_ref, o_ref, k_scr, vt_scr, qt_scr, s0_scr, s1_scr, m_scr, p_scr,
                         *, lam_init):
    qi = pl.program_id(2)
    n_tiles = SEQ // DA_TQ
    n_chunks = NKEYS // DA_KCHUNK
    groups_per_chunk = DA_KCHUNK // DA_ROWS

    def score_chunk(s_scr, qt2, c, m_run):
        rows = slice(c * DA_KCHUNK, (c + 1) * DA_KCHUNK)
        s = jnp.dot(k_scr[rows, :], qt2, preferred_element_type=F32)
        s_scr[rows, :] = s
        parts = [s[i * DA_ROWS:(i + 1) * DA_ROWS, :] for i in range(groups_per_chunk)]
        if m_run is not None:
            parts.append(m_run)
        while len(parts) > 1:
            parts = [jnp.maximum(parts[i], parts[i + 1]) for i in range(0, len(parts) - 1, 2)] + (
                [parts[-1]] if len(parts) % 2 else [])
        return parts[0]

    def store_colmax(m_run):
        m_scr[...] = jnp.broadcast_to(jnp.max(m_run, axis=0, keepdims=True), (DA_ROWS, DA_W))

    @pl.when(qi == 0)
    def _():
        k_scr[0:CTX_LEN, :] = kc_ref[...]
        vt_scr[:, 0:CTX_LEN] = _transpose_bf16(vc_ref[...])
        dim = lax.broadcasted_iota(jnp.int32, (HB, DA_TQ), 0)
        for c in range(SEQ // DA_TQ):
            rows = slice(c * DA_TQ, (c + 1) * DA_TQ)
            dst = slice(CTX_LEN + c * DA_TQ, CTX_LEN + (c + 1) * DA_TQ)
            cos, slo, shi = cos_ref[rows, :], slo_ref[rows, :], shi_ref[rows, :]
            k_scr[dst, :] = _rope(kl_ref[rows, :].astype(F32), cos, slo, shi).astype(BF16)
            vt_scr[:, dst] = _transpose_bf16(vl_ref[rows, :])
            q = _rope(q_ref[rows, :].astype(F32), cos, slo, shi)
            qt = (q * (A_QK_DIM ** -0.5 * LOG2E)).T
            qt_scr[:, c * DA_W:c * DA_W + DA_TQ] = jnp.where(dim < A_QK_DIM, qt, 0.0).astype(BF16)
            qt_scr[:, c * DA_W + DA_TQ:(c + 1) * DA_W] = jnp.where(dim >= A_QK_DIM, qt, 0.0).astype(BF16)
        m_run = None
        for c in range(n_chunks):
            m_run = score_chunk(s0_scr, qt_scr[:, 0:DA_W], c, m_run)
        store_colmax(m_run)

    def tile(s_scr, s_next_scr):
        if s_next_scr is not None:
            qt_next = qt_scr[:, pl.ds(pl.multiple_of((qi + 1) * DA_W, DA_W), DA_W)]
        mb = m_scr[...]
        m_run = None
        l_acc = [None, None]
        acc = None

        def value_chunk(c, acc):
            rows = slice(c * DA_KCHUNK, (c + 1) * DA_KCHUNK)
            part = jnp.dot(vt_scr[:, rows], p_scr[rows, :], preferred_element_type=F32)
            return part if acc is None else acc + part

        for c in range(n_chunks):
            if s_next_scr is not None:
                m_run = score_chunk(s_next_scr, qt_next, c, m_run)
            for r in range(c * groups_per_chunk, (c + 1) * groups_per_chunk):
                grp = slice(r * DA_ROWS, (r + 1) * DA_ROWS)
                p = jnp.exp2(s_scr[grp, :] - mb)
                p_scr[grp, :] = p.astype(BF16)
                l_acc[r % 2] = p if l_acc[r % 2] is None else l_acc[r % 2] + p
            acc = value_chunk(c, acc)
        if s_next_scr is not None:
            store_colmax(m_run)
        l = jnp.sum(l_acc[0] + l_acc[1], axis=0, keepdims=True)

        lam = _diff_lambda(lq_ref, lam_init)
        ot = acc[:, :DA_TQ] * (1.0 / l[:, :DA_TQ]) - acc[:, DA_TQ:] * (lam / l[:, DA_TQ:])
        ot = ot * lax.rsqrt(jnp.mean(ot * ot, axis=0, keepdims=True) + SUBLN_EPS)
        o = ot.T * sw_ref[...] * (1.0 - lam_init)
        o_ref[...] = (o * _silu(g_ref[...].astype(F32))).astype(o_ref.dtype)

    parity = qi % 2
    last = qi == n_tiles - 1
    pl.when(parity == 0)(functools.partial(tile, s0_scr, s1_scr))
    pl.when(jnp.logical_and(parity == 1, jnp.logical_not(last)))(functools.partial(tile, s1_scr, s0_scr))
    pl.when(last)(functools.partial(tile, s1_scr, None))


_NT = (((1,), (1,)), ((), ()))


def _diffattn_ctx_kernel(q_ref, k_ref, v_ref, g_ref, lq_ref, sw_ref, o_ref, *, lam_init):
    q = q_ref[...].astype(F32) * (A_QK_DIM ** -0.5)
    lane = lax.broadcasted_iota(jnp.int32, q.shape, 1)
    outs = []
    for qn in (jnp.where(lane < A_QK_DIM, q, 0.0), jnp.where(lane >= A_QK_DIM, q, 0.0)):
        s = lax.dot_general(qn.astype(BF16), k_ref[...], _NT, preferred_element_type=F32)
        p = jnp.exp(s - jnp.max(s, axis=-1, keepdims=True))
        acc = jnp.dot(p.astype(BF16), v_ref[...], preferred_element_type=F32)
        outs.append((acc, jnp.sum(p, axis=-1, keepdims=True)))
    lam = _diff_lambda(lq_ref, lam_init)
    o = outs[0][0] * (1.0 / outs[0][1]) - (lam / outs[1][1]) * outs[1][0]
    o = o * lax.rsqrt(jnp.mean(o * o, axis=-1, keepdims=True) + SUBLN_EPS) * sw_ref[...]
    o = o * (1.0 - lam_init)
    o_ref[...] = (o * _silu(g_ref[...].astype(F32))).astype(o_ref.dtype)


def diff_attention_latent(proj_l, proj_c, ccols, rope_tabs, lambda_qk, subln_w, lam_init):
    nq = SEQ // DA_TQ
    head = lambda col: (lambda b, h, i: (b, col + h))
    qmap = lambda col: (lambda b, h, i: (b * nq + i, col + h))
    const = lambda b, h, i: (0, 0)
    return pl.pallas_call(
        functools.partial(_diffattn_lat_kernel, lam_init=lam_init),
        out_shape=jax.ShapeDtypeStruct((BATCH * SEQ, A_WIDTH), BF16),
        grid=(BATCH, A_HEADS, nq),
        in_specs=[
            pl.BlockSpec((SEQ, HB), head(COL_QA)),
            pl.BlockSpec((CTX_LEN, HB), head(ccols["ka"])),
            pl.BlockSpec((CTX_LEN, HB), head(ccols["va"])),
            pl.BlockSpec((SEQ, HB), head(COL_KA)),
            pl.BlockSpec((SEQ, HB), head(COL_VA)),
            pl.BlockSpec((DA_TQ, HB), qmap(COL_GA)),
            pl.BlockSpec((SEQ, HB), const),
            pl.BlockSpec((SEQ, HB), const),
            pl.BlockSpec((SEQ, HB), const),
            pl.BlockSpec((4, A_QK_DIM), const),
            pl.BlockSpec((1, HB), const),
        ],
        out_specs=pl.BlockSpec((DA_TQ, HB), lambda b, h, i: (b * nq + i, h)),
        scratch_shapes=[pltpu.VMEM((NKEYS, HB), BF16), pltpu.VMEM((HB, NKEYS), BF16),
                        pltpu.VMEM((HB, 2 * SEQ), BF16),
                        pltpu.VMEM((NKEYS, DA_W), F32), pltpu.VMEM((NKEYS, DA_W), F32),
                        pltpu.VMEM((DA_ROWS, DA_W), F32), pltpu.VMEM((NKEYS, DA_W), BF16)],
        compiler_params=_params(("arbitrary", "arbitrary", "arbitrary")),
        name="diff_attention_latent",
    )(proj_l, proj_c, proj_c, proj_l, proj_l, proj_l, *rope_tabs, lambda_qk, subln_w.reshape(1, HB))


def diff_attention_context(proj_c, lambda_qk, subln_w, lam_init):
    head = lambda col: (lambda b, h: (b, col + h))
    const = lambda b, h: (0, 0)
    return pl.pallas_call(
        functools.partial(_diffattn_ctx_kernel, lam_init=lam_init),
        out_shape=jax.ShapeDtypeStruct((BATCH * CTX_LEN, A_WIDTH), BF16),
        grid=(BATCH, A_HEADS),
        in_specs=[pl.BlockSpec((CTX_LEN, HB), head(COL_QA)),
                  pl.BlockSpec((CTX_LEN, HB), head(COL_KA)),
                  pl.BlockSpec((CTX_LEN, HB), head(COL_VA)),
                  pl.BlockSpec((CTX_LEN, HB), head(COL_GA)),
                  pl.BlockSpec((4, A_QK_DIM), const),
                  pl.BlockSpec((1, HB), const)],
        out_specs=pl.BlockSpec((CTX_LEN, HB), lambda b, h: (b, h)),
        compiler_params=_params(("arbitrary", "arbitrary")),
        name="diff_attention_context",
    )(proj_c, proj_c, proj_c, proj_c, lambda_qk, subln_w.reshape(1, HB))


def _nbr_slab_start(g):
    return min(max(g * NB_QROWS - NA_KH // 2, 0), GRID_H - NB_KROWS)


def _nbr_pattern(g):
    return 0 if g == 0 else (2 if g == NB_GROUPS - 1 else 1)


def _nbr_kernel(q_ref, kc_ref, vc_ref, kl_ref, vl_ref, g_ref, bias_ref, o_ref, vt_scr):
    vt_scr[:, 0:CTX_LEN] = _transpose_bf16(vc_ref[...])
    for c in range(SEQ // NB_Q):
        vt_scr[:, CTX_LEN + c * NB_Q:CTX_LEN + (c + 1) * NB_Q] = _transpose_bf16(vl_ref[c * NB_Q:(c + 1) * NB_Q, :])
    def scores(g):
        rows = slice(g * NB_Q, (g + 1) * NB_Q)
        k0 = _nbr_slab_start(g) * GRID_W
        qt = (q_ref[rows, :].astype(F32) * (HEAD_DIM ** -0.5 * LOG2E)).T.astype(BF16)
        s_c = jnp.dot(kc_ref[...], qt, preferred_element_type=F32)
        s_n = (jnp.dot(kl_ref[k0:k0 + NB_K, :], qt, preferred_element_type=F32)
               + bias_ref[0, _nbr_pattern(g)])
        return s_c, s_n

    def softmax(s_c, s_n):
        m = jnp.maximum(jnp.max(s_c, axis=0, keepdims=True), jnp.max(s_n, axis=0, keepdims=True))
        p_c = jnp.exp2(s_c - m)
        p_n = jnp.exp2(s_n - m)
        l = jnp.sum(p_c, axis=0, keepdims=True) + jnp.sum(p_n, axis=0, keepdims=True)
        return p_c.astype(BF16), p_n.astype(BF16), l

    def finish(g, p_c, p_n, l):
        rows = slice(g * NB_Q, (g + 1) * NB_Q)
        k0 = _nbr_slab_start(g) * GRID_W
        acc = (jnp.dot(vt_scr[:, 0:CTX_LEN], p_c, preferred_element_type=F32)
               + jnp.dot(vt_scr[:, CTX_LEN + k0:CTX_LEN + k0 + NB_K], p_n,
                         preferred_element_type=F32))
        o = (acc * (1.0 / l)).T
        o_ref[rows, :] = (o * _silu(g_ref[rows, :].astype(F32))).astype(o_ref.dtype)

    ahead = scores(0)
    probs = None
    for g in range(NB_GROUPS):
        s_c, s_n = ahead
        if g + 1 < NB_GROUPS:
            ahead = scores(g + 1)
        new_probs = softmax(s_c, s_n)
        if probs is not None:
            finish(g - 1, *probs)
        probs = new_probs
    finish(NB_GROUPS - 1, *probs)


def neighbourhood_bias_table(rpb):
    w = GRID_W
    cq = np.arange(w)
    c0 = np.clip(cq - NA_KW // 2, 0, w - NA_KW)
    ck = np.arange(w)
    col_valid = (ck[None, :] >= c0[:, None]) & (ck[None, :] < c0[:, None] + NA_KW)
    pad = w - NA_KW
    padded = jnp.pad(rpb.astype(F32), ((0, 0), (0, 0), (pad, pad)))
    toep = jnp.stack([padded[:, :, w - 1 - q:2 * w - 1 - q] for q in range(w)], axis=2)
    toep = jnp.where(col_valid[None, None], toep, NEG_INF)
    toep_t = jnp.swapaxes(toep, 2, 3) * LOG2E
    masked = jnp.full((B_HEADS, w, w), NEG_INF, F32)
    tabs = []
    for g in (0, 1, NB_GROUPS - 1):
        slab0 = _nbr_slab_start(g)
        qr = g * NB_QROWS + np.arange(NB_QROWS)
        kr = slab0 + np.arange(NB_KROWS)
        win0 = np.clip(qr - NA_KH // 2, 0, GRID_H - NA_KH)
        row_valid = (kr[None, :] >= win0[:, None]) & (kr[None, :] < win0[:, None] + NA_KH)
        dr_idx = kr[None, :] - qr[:, None] + NA_KH - 1
        rows = [jnp.concatenate([toep_t[:, int(dr_idx[a, k])] if row_valid[a, k] else masked
                                 for a in range(NB_QROWS)], axis=-1) for k in range(NB_KROWS)]
        tabs.append(jnp.concatenate(rows, axis=1))
    return jnp.stack(tabs, axis=1)


def neighbourhood_attention(proj_l, proj_c, ccols, bias_tab):
    head = lambda col: (lambda h, b: (b, col + h))
    return pl.pallas_call(
        _nbr_kernel,
        out_shape=jax.ShapeDtypeStruct((BATCH * SEQ, B_WIDTH), BF16),
        grid=(B_HEADS, BATCH),
        in_specs=[
            pl.BlockSpec((SEQ, HB), head(COL_QB)),
            pl.BlockSpec((CTX_LEN, HB), head(ccols["kb"])),
            pl.BlockSpec((CTX_LEN, HB), head(ccols["vb"])),
            pl.BlockSpec((SEQ, HB), head(COL_KB)),
            pl.BlockSpec((SEQ, HB), head(COL_VB)),
            pl.BlockSpec((SEQ, HB), head(COL_GB)),
            pl.BlockSpec((1, 3, NB_K, NB_Q), lambda h, b: (h, 0, 0, 0)),
        ],
        out_specs=pl.BlockSpec((SEQ, HB), lambda h, b: (b, h)),
        scratch_shapes=[pltpu.VMEM((HB, NKEYS), BF16)],
        compiler_params=_params(("arbitrary", "arbitrary")),
        name="neighbourhood_attention",
    )(proj_l, proj_c, proj_c, proj_l, proj_l, proj_l, bias_tab)


def _dense_ctx_kernel(q_ref, k_ref, v_ref, g_ref, o_ref):
    scale = HEAD_DIM ** -0.5
    s = lax.dot_general(q_ref[...], k_ref[...], _NT, preferred_element_type=F32) * scale
    p = jnp.exp(s - jnp.max(s, axis=-1, keepdims=True))
    p = p * (1.0 / jnp.sum(p, axis=-1, keepdims=True))
    o = jnp.dot(p.astype(BF16), v_ref[...], preferred_element_type=F32)
    o_ref[...] = (o * _silu(g_ref[...].astype(F32))).astype(o_ref.dtype)


def dense_context_attention(proj_c):
    head = lambda col: (lambda b, h: (b, col + h))
    return pl.pallas_call(
        _dense_ctx_kernel,
        out_shape=jax.ShapeDtypeStruct((BATCH * CTX_LEN, B_WIDTH), BF16),
        grid=(BATCH, B_HEADS),
        in_specs=[pl.BlockSpec((CTX_LEN, HB), head(COL_QB)),
                  pl.BlockSpec((CTX_LEN, HB), head(COL_KB)),
                  pl.BlockSpec((CTX_LEN, HB), head(COL_VB)),
                  pl.BlockSpec((CTX_LEN, HB), head(COL_GB))],
        out_specs=pl.BlockSpec((CTX_LEN, HB), lambda b, h: (b, h)),
        compiler_params=_params(("arbitrary", "arbitrary")),
        name="dense_context_attention",
    )(proj_c, proj_c, proj_c, proj_c)


def _shift_rows(x, d, fill):
    n = x.shape[0]
    if d % SUBLANES == 0:
        pad = jnp.full((abs(d), x.shape[1]), fill, x.dtype)
        return jnp.concatenate([pad, x[:n - d]] if d > 0 else [x[-d:], pad], axis=0)
    rolled = pltpu.roll(x, d % n, axis=0)
    row = lax.broadcasted_iota(jnp.int32, x.shape, 0)
    keep = (row >= d) if d > 0 else (row < n + d)
    return jnp.where(keep, rolled, fill)


def _dwconv(x, w_ref, b_ref):
    acc = b_ref[...] + w_ref[CONV_LEFT:CONV_LEFT + 1, :] * x
    for j in range(CONV_W):
        if j != CONV_LEFT:
            acc = acc + w_ref[j:j + 1, :] * _shift_rows(x, CONV_LEFT - j, 0.0)
    return acc


def _linear_scan(a, b, h0, reverse):
    n = a.shape[0]
    row = lax.broadcasted_iota(jnp.int32, a.shape, 0)
    first = (row == n - 1) if reverse else (row == 0)
    b = b + jnp.where(first, a * h0, 0.0)
    d = 1
    while d < n:
        sd = -d if reverse else d
        b = b + a * _shift_rows(b, sd, 0.0)
        if 2 * d < n:
            a = a * _shift_rows(a, sd, 1.0)
        d *= 2
    return b


def _sigmoid_tanh(x):
    return 0.5 * jnp.tanh(0.5 * x) + 0.5


def _rglru_gates(u, wa_ref, ba_ref, wx_ref, bx_ref, lam_ref, d):
    ub = u.astype(BF16)
    r = _sigmoid_tanh(jnp.dot(ub, wa_ref[0, d], preferred_element_type=F32) + ba_ref[0, d])
    i = _sigmoid_tanh(jnp.dot(ub, wx_ref[0, d], preferred_element_type=F32) + bx_ref[0, d])
    neg_lam = -lam_ref[0, d]
    softplus = jnp.maximum(neg_lam, 0.0) + jnp.log1p(jnp.exp(-jnp.abs(neg_lam)))
    log_a = (-RGLRU_C) * r * softplus
    a = jnp.exp(log_a)
    t = jnp.tanh(log_a)
    b = jnp.sqrt(-2.0 * t / (1.0 - t)) * (i * u)
    return a, b


SCAN_CHUNKS = SUBLANES


def _scan_chunk_len(n):
    return n // SCAN_CHUNKS + 1


def _blocked_scan(a_scr, b_scr, n, h0, reverse, need_states):
    length = _scan_chunk_len(n)
    order = range(length - 1, -1, -1) if reverse else range(length)
    h = jnp.zeros((SCAN_CHUNKS, a_scr.shape[1]), F32)
    p = jnp.ones_like(h)
    hs, ps = [None] * length, [None] * length
    for i in order:
        va = a_scr[pl.ds(i, SCAN_CHUNKS, stride=length), :]
        vb = b_scr[pl.ds(i, SCAN_CHUNKS, stride=length), :]
        h = va * h + vb
        p = va * p
        hs[i], ps[i] = h, p
    rows = [None] * SCAN_CHUNKS
    carry = h0
    for j in (range(SCAN_CHUNKS - 1, -1, -1) if reverse else range(SCAN_CHUNKS)):
        rows[j] = carry
        carry = h[j:j + 1, :] + p[j:j + 1, :] * carry
    if not need_states:
        return None, carry
    start = jnp.concatenate(rows, axis=0)
    return [hs[i] + ps[i] * start for i in range(length)], carry


def _rglru_kernel(*refs, need_ctx):
    if need_ctx:
        (xl_ref, xc_ref, gl_ref, gc_ref, cw_ref, cb_ref, wa_ref, ba_ref, wx_ref, bx_ref, lam_ref,
         ol_ref, oc_ref, al_scr, bl_scr, ac_scr, bc_scr, yl_scr, yc_scr) = refs
    else:
        (xl_ref, xc_ref, gl_ref, cw_ref, cb_ref, wa_ref, ba_ref, wx_ref, bx_ref, lam_ref,
         ol_ref, al_scr, bl_scr, ac_scr, bc_scr, yl_scr, yc_scr) = refs
    ct = al_scr.shape[1]
    u_l = _dwconv(xl_ref[...].astype(F32), cw_ref, cb_ref)
    u_c = _dwconv(xc_ref[...].astype(F32), cw_ref, cb_ref)
    for a_scr, b_scr, n in ((al_scr, bl_scr, SEQ), (ac_scr, bc_scr, CTX_LEN)):
        a_scr[n:n + SUBLANES, :] = jnp.ones((SUBLANES, ct), F32)
        b_scr[n:n + SUBLANES, :] = jnp.zeros((SUBLANES, ct), F32)
    y_l = None
    y_c = None
    for d in range(2):
        rev = d == 1
        a_c, b_c = _rglru_gates(u_c, wa_ref, ba_ref, wx_ref, bx_ref, lam_ref, d)
        ac_scr[0:CTX_LEN, :] = a_c
        bc_scr[0:CTX_LEN, :] = b_c
        h_c, fin = _blocked_scan(ac_scr, bc_scr, CTX_LEN, jnp.zeros((1, ct), F32), rev, need_ctx)
        a_l, b_l = _rglru_gates(u_l, wa_ref, ba_ref, wx_ref, bx_ref, lam_ref, d)
        al_scr[0:SEQ, :] = a_l
        bl_scr[0:SEQ, :] = b_l
        h_l, _ = _blocked_scan(al_scr, bl_scr, SEQ, fin, rev, True)
        y_l = h_l if y_l is None else [f + r for f, r in zip(y_l, h_l)]
        if need_ctx:
            y_c = h_c if y_c is None else [f + r for f, r in zip(y_c, h_c)]
    length = _scan_chunk_len(SEQ)
    for i in range(length):
        yl_scr[pl.ds(i, SCAN_CHUNKS, stride=length), :] = y_l[i]
    ol_ref[...] = (yl_scr[0:SEQ, :] * _silu(gl_ref[...].astype(F32))).astype(ol_ref.dtype)
    if need_ctx:
        length = _scan_chunk_len(CTX_LEN)
        for i in range(length):
            yc_scr[pl.ds(i, SCAN_CHUNKS, stride=length), :] = y_c[i]
        oc_ref[...] = (yc_scr[0:CTX_LEN, :] * _silu(gc_ref[...].astype(F32))).astype(oc_ref.dtype)


def _block_diag(w, per_tile):
    n = w.shape[-1]
    tiles = C_BLOCKS // per_tile
    w = w.reshape(2, tiles, per_tile, n, n)
    eye = jnp.eye(per_tile, dtype=w.dtype)
    bd = jnp.einsum('dtpij,pq->dtpiqj', w, eye).reshape(2, tiles, per_tile * n, per_tile * n)
    return jnp.transpose(bd, (1, 0, 2, 3)).astype(BF16)


def rglru_mixer(proj_l, proj_c, ccols, conv_w, conv_b, rg_wa, rg_ba, rg_wx, rg_bx, rg_lambda, need_ctx):
    ct = 128
    nt = C_WIDTH // ct
    per_tile = ct // C_BLOCK_DIM
    vec = lambda p: jnp.transpose(p.reshape(2, nt, 1, ct), (1, 0, 2, 3))
    wa = _block_diag(rg_wa, per_tile)
    wx = _block_diag(rg_wx, per_tile)
    xcol = lambda col: (lambda b, t: (b, col + t))
    par = lambda b, t: (t, 0, 0, 0)
    out_shape = [jax.ShapeDtypeStruct((BATCH * SEQ, C_WIDTH), BF16)]
    out_specs = [pl.BlockSpec((SEQ, ct), lambda b, t: (b, t))]
    if need_ctx:
        out_shape.append(jax.ShapeDtypeStruct((BATCH * CTX_LEN, C_WIDTH), BF16))
        out_specs.append(pl.BlockSpec((CTX_LEN, ct), lambda b, t: (b, t)))
    in_specs = [pl.BlockSpec((SEQ, ct), xcol(COL_XC)),
                pl.BlockSpec((CTX_LEN, ct), xcol(ccols["xc"])),
                pl.BlockSpec((SEQ, ct), xcol(COL_GC))]
    args = [proj_l, proj_c, proj_l]
    if need_ctx:
        in_specs.append(pl.BlockSpec((CTX_LEN, ct), xcol(ccols["gc"])))
        args.append(proj_c)
    in_specs += [pl.BlockSpec((CONV_W, ct), lambda b, t: (0, t)),
                 pl.BlockSpec((1, ct), lambda b, t: (0, t)),
                 pl.BlockSpec((1, 2, ct, ct), par),
                 pl.BlockSpec((1, 2, 1, ct), par),
                 pl.BlockSpec((1, 2, ct, ct), par),
                 pl.BlockSpec((1, 2, 1, ct), par),
                 pl.BlockSpec((1, 2, 1, ct), par)]
    args += [conv_w, conv_b.reshape(1, C_WIDTH), wa, vec(rg_ba), wx, vec(rg_bx), vec(rg_lambda)]
    outs = pl.pallas_call(
        functools.partial(_rglru_kernel, need_ctx=need_ctx),
        out_shape=out_shape,
        grid=(BATCH, nt),
        in_specs=in_specs,
        out_specs=out_specs,
        scratch_shapes=([pltpu.VMEM((SEQ + SUBLANES, ct), F32)] * 2
                        + [pltpu.VMEM((CTX_LEN + SUBLANES, ct), F32)] * 2
                        + [pltpu.VMEM((SEQ + SUBLANES, ct), F32), pltpu.VMEM((CTX_LEN + SUBLANES, ct), F32)]),
        compiler_params=_params(("arbitrary", "arbitrary")),
        name="rglru_mixer",
    )(*args)
    return outs if need_ctx else (outs[0], None)


def _outproj_kernel(x_ref, gate_ref, ma_ref, mb_ref, mc_ref, wa_ref, wb_ref, wc_ref, o_ref):
    y = jnp.dot(ma_ref[...], wa_ref[...], preferred_element_type=F32)
    y = y + jnp.dot(mb_ref[...], wb_ref[...], preferred_element_type=F32)
    y = y + jnp.dot(mc_ref[...], wc_ref[...], preferred_element_type=F32)
    o_ref[...] = x_ref[...] + gate_ref[0] * y


def output_projection(x2d, mod3, ma, mb, mc, w_bf16, layer, rows_per_mod, mod_row0):
    m = x2d.shape[0]
    tm, tn = 1024, 512

    def mod_row(i):
        return mod_row0 + (i * tm) // rows_per_mod

    return pl.pallas_call(
        _outproj_kernel,
        out_shape=jax.ShapeDtypeStruct((m, D_MODEL), F32),
        grid=(m // tm, D_MODEL // tn),
        in_specs=[
            pl.BlockSpec((tm, tn), lambda i, j: (i, j)),
            pl.BlockSpec((1, 1, tn), lambda i, j: (mod_row(i), 0, 2 * (D_MODEL // tn) + j)),
            pl.BlockSpec((tm, A_WIDTH), lambda i, j: (i, 0)),
            pl.BlockSpec((tm, B_WIDTH), lambda i, j: (i, 0)),
            pl.BlockSpec((tm, C_WIDTH), lambda i, j: (i, 0)),
            pl.BlockSpec((None, A_WIDTH, tn), lambda i, j: (layer, 0, j)),
            pl.BlockSpec((None, B_WIDTH, tn), lambda i, j: (layer, 1, j)),
            pl.BlockSpec((None, C_WIDTH, tn), lambda i, j: (layer, (A_WIDTH + B_WIDTH) // C_WIDTH, j)),
        ],
        out_specs=pl.BlockSpec((tm, tn), lambda i, j: (i, j)),
        compiler_params=_params(("arbitrary", "arbitrary")),
        name="output_projection",
    )(x2d, mod3, ma, mb, mc, w_bf16, w_bf16, w_bf16)


def _final_norm_kernel(x_ref, w_ref, o_ref):
    x = x_ref[...]
    o_ref[...] = x * lax.rsqrt(jnp.mean(x * x, axis=-1, keepdims=True) + NORM_EPS) * w_ref[...]


def final_norm(x2d, w):
    tm = 128
    return pl.pallas_call(
        _final_norm_kernel,
        out_shape=jax.ShapeDtypeStruct(x2d.shape, F32),
        grid=(x2d.shape[0] // tm,),
        in_specs=[pl.BlockSpec((tm, D_MODEL), lambda i: (i, 0)),
                  pl.BlockSpec((1, D_MODEL), lambda i: (0, 0))],
        out_specs=pl.BlockSpec((tm, D_MODEL), lambda i: (i, 0)),
        compiler_params=_params(("arbitrary",)),
        name="final_norm",
    )(x2d, w.reshape(1, D_MODEL))


def _rope_tables():
    t = np.arange(SEQ)
    pos = np.stack([t // GRID_W, t % GRID_W], axis=0).astype(np.float32)
    lane = np.arange(HEAD_DIM)
    within = lane % A_QK_DIM
    axis = within // (A_QK_DIM // 2)
    k = within % (A_QK_DIM // 2)
    nfreq = A_QK_DIM // 4
    upper = k >= nfreq
    freqs = jnp.asarray(ROPE_THETA, F32) ** (-jnp.arange(nfreq, dtype=F32) / nfreq)
    ang = jnp.asarray(pos)[axis, :].T * freqs[k % nfreq][None, :]
    cos, sin = jnp.cos(ang), jnp.sin(ang)
    sin_lo = jnp.where(upper[None, :], 0.0, -sin)
    sin_hi = jnp.where(upper[None, :], sin, 0.0)
    return cos, sin_lo, sin_hi


def kernel(x, c, ctx, c_ctx, ada_w, ada_b, norm_w, w_in, w_out, lambda_qk, subln_w, rpb,
           conv_w, conv_b, rg_wa, rg_ba, rg_wx, rg_bx, rg_lambda, final_norm_w):
    cvec = jnp.zeros((MOD_ROWS, D_MODEL), F32).at[:BATCH].set(c).at[CTX_MOD_ROW].set(c_ctx)
    mod = adaln_modulation(cvec, ada_w, ada_b)
    mod3 = mod.reshape(DEPTH * MOD_ROWS, 1, 3 * D_MODEL)
    norm_w3 = norm_w.reshape(DEPTH, 1, D_MODEL)
    w_in_bf = w_in.astype(BF16)
    w_out_bf = w_out.astype(BF16)
    rope_tabs = _rope_tables()
    xl = x.reshape(BATCH * SEQ, D_MODEL)
    xc = ctx.reshape(BATCH * CTX_LEN, D_MODEL)
    for l in range(DEPTH):
        need_ctx = l < DEPTH - 1
        lam_init = 0.8 - 0.6 * math.exp(-0.3 * l)
        row_l, row_c = l * MOD_ROWS, l * MOD_ROWS + CTX_MOD_ROW
        proj_l = input_projection(xl, norm_w3, mod3, w_in_bf, l, SEQ, row_l)
        ccols = FULL_COLS if need_ctx else KV_COLS
        proj_c = input_projection(xc, norm_w3, mod3, w_in_bf, l, BATCH * CTX_LEN, row_c, ccols)

        ma = diff_attention_latent(proj_l, proj_c, ccols, rope_tabs, lambda_qk[l], subln_w[l], lam_init)
        mb = neighbourhood_attention(proj_l, proj_c, ccols, neighbourhood_bias_table(rpb[l]))
        mc, mc_c = rglru_mixer(proj_l, proj_c, ccols, conv_w[l], conv_b[l], rg_wa[l], rg_ba[l],
                               rg_wx[l], rg_bx[l], rg_lambda[l], need_ctx)
        xl = output_projection(xl, mod3, ma, mb, mc, w_out_bf, l, SEQ, row_l)
        if need_ctx:
            ma_c = diff_attention_context(proj_c, lambda_qk[l], subln_w[l], lam_init)
            mb_c = dense_context_attention(proj_c)
            xc = output_projection(xc, mod3, ma_c, mb_c, mc_c, w_out_bf, l, BATCH * CTX_LEN, row_c)
    return final_norm(xl, final_norm_w).reshape(BATCH, SEQ, D_MODEL)
```

```python
import functools
import math

import numpy as np
import jax
import jax.numpy as jnp
from jax import lax
from jax.experimental import pallas as pl
from jax.experimental.pallas import tpu as pltpu

F32 = jnp.float32
BF16 = jnp.bfloat16

D_MODEL = 4096
BATCH = 8
SEQ = 2048
DEPTH = 2
GRID_W = 64
GRID_H = SEQ // GRID_W
CTX_LEN = 256
HEAD_DIM = 128
A_WIDTH = (3 * D_MODEL) // 8
A_HEADS = A_WIDTH // HEAD_DIM
A_QK_DIM = HEAD_DIM // 2
B_WIDTH = (3 * D_MODEL) // 8
B_HEADS = B_WIDTH // HEAD_DIM
C_WIDTH = D_MODEL - A_WIDTH - B_WIDTH
C_BLOCKS = 16
C_BLOCK_DIM = C_WIDTH // C_BLOCKS
MIX_WIDTH = A_WIDTH + B_WIDTH + C_WIDTH
IN_WIDTH = 4 * A_WIDTH + 4 * B_WIDTH + 2 * C_WIDTH
NA_KH = 8
NA_KW = 16
ROPE_THETA = 10000.0
RGLRU_C = 8.0
CONV_W = 4
CONV_LEFT = 2
NORM_EPS = 1e-6
SUBLN_EPS = 1e-5
NEG_INF = -1e30

HB = HEAD_DIM
COL_QA, COL_KA, COL_VA, COL_GA = 0, A_HEADS, 2 * A_HEADS, 3 * A_HEADS
COL_QB = 4 * A_HEADS
COL_KB, COL_VB, COL_GB = COL_QB + B_HEADS, COL_QB + 2 * B_HEADS, COL_QB + 3 * B_HEADS
COL_XC = COL_QB + 4 * B_HEADS
COL_GC = COL_XC + C_WIDTH // HB

MOD_ROWS = 16
CTX_MOD_ROW = BATCH

VMEM_LIMIT = 56 * 1024 * 1024
SUBLANES = 8

NB_QROWS = 4
NB_KROWS = 12
NB_GROUPS = GRID_H // NB_QROWS
NB_Q = NB_QROWS * GRID_W
NB_K = NB_KROWS * GRID_W


def _params(sem):
    return pltpu.CompilerParams(dimension_semantics=sem, vmem_limit_bytes=VMEM_LIMIT)


def _sigmoid(x):
    return 1.0 / (1.0 + jnp.exp(-x))


def _silu(x):
    return x * _sigmoid(x)


def _adaln_kernel(c_ref, w_ref, b_ref, o_ref):
    s = _silu(c_ref[...]).astype(BF16)
    w = w_ref[0].astype(BF16)
    o_ref[0] = jnp.dot(s, w, preferred_element_type=F32) + b_ref[0]


def adaln_modulation(cvec, ada_w, ada_b):
    tn = 512
    n3 = 3 * D_MODEL
    return pl.pallas_call(
        _adaln_kernel,
        out_shape=jax.ShapeDtypeStruct((DEPTH, MOD_ROWS, n3), F32),
        grid=(DEPTH, n3 // tn),
        in_specs=[
            pl.BlockSpec((MOD_ROWS, D_MODEL), lambda l, j: (0, 0)),
            pl.BlockSpec((1, D_MODEL, tn), lambda l, j: (l, 0, j)),
            pl.BlockSpec((1, 1, tn), lambda l, j: (l, 0, j)),
        ],
        out_specs=pl.BlockSpec((1, MOD_ROWS, tn), lambda l, j: (l, 0, j)),
        compiler_params=_params(("arbitrary", "arbitrary")),
        name="adaln_modulation",
    )(cvec, ada_w, ada_b.reshape(DEPTH, 1, n3))


_NORM_CHUNK = 32
_NORM_AHEAD_STEPS = 8


def _inproj_kernel(x_ref, nw_ref, shift_ref, scale_ref, w_ref, o_ref, hx0_ref, hx1_ref):
    i, j = pl.program_id(0), pl.program_id(1)
    tm = hx0_ref.shape[0]
    gain = nw_ref[...] * (1.0 + scale_ref[0])
    add = shift_ref[0]

    def normalise(hx_ref, row0):
        rows = pl.ds(pl.multiple_of(row0, _NORM_CHUNK), _NORM_CHUNK)
        x = x_ref[rows, :]
        inv = lax.rsqrt(jnp.mean(x * x, axis=-1, keepdims=True) + NORM_EPS)
        hx_ref[rows, :] = (x * inv * gain + add).astype(BF16)

    @pl.when(jnp.logical_and(i == 0, j == 0))
    def _():
        def body(r, carry):
            normalise(hx0_ref, r * _NORM_CHUNK)
            return carry
        lax.fori_loop(0, tm // _NORM_CHUNK, body, 0)

    def step(hx_cur, hx_next):
        ahead_rows = tm // _NORM_AHEAD_STEPS
        piece = jnp.clip(j - 1, 0, _NORM_AHEAD_STEPS - 1)
        for k in range(ahead_rows // _NORM_CHUNK):
            normalise(hx_next, piece * ahead_rows + k * _NORM_CHUNK)
        o_ref[...] = jnp.dot(hx_cur[...], w_ref[...], preferred_element_type=F32).astype(o_ref.dtype)

    pl.when(i % 2 == 0)(functools.partial(step, hx0_ref, hx1_ref))
    pl.when(i % 2 == 1)(functools.partial(step, hx1_ref, hx0_ref))


class ProjCols:
    def __init__(self, groups):
        self.start, pos = {}, 0
        for name, _, width in groups:
            self.start[name] = pos
            pos += width
        self.width = pos * HB
        self.src_blocks = [src + k for _, src, width in groups for k in range(width)]

    def __getitem__(self, name):
        return self.start[name]


_ALL_GROUPS = (("qa", COL_QA, A_HEADS), ("ka", COL_KA, A_HEADS), ("va", COL_VA, A_HEADS),
               ("ga", COL_GA, A_HEADS), ("qb", COL_QB, B_HEADS), ("kb", COL_KB, B_HEADS),
               ("vb", COL_VB, B_HEADS), ("gb", COL_GB, B_HEADS), ("xc", COL_XC, C_WIDTH // HB),
               ("gc", COL_GC, C_WIDTH // HB))
FULL_COLS = ProjCols(_ALL_GROUPS)
KV_COLS = ProjCols(tuple(g for g in _ALL_GROUPS if g[0] in ("ka", "va", "kb", "vb", "xc")))


def input_projection(x2d, norm_w3, mod3, w_bf16, layer, rows_per_mod, mod_row0, cols=FULL_COLS):
    m = x2d.shape[0]
    tm = 512
    tn = 1024 if cols is FULL_COLS else 512
    per_tile = tn // HB
    n_tiles = cols.width // tn
    src_tiles = [cols.src_blocks[t * per_tile] // per_tile for t in range(n_tiles)]
    assert all(cols.src_blocks[t * per_tile + k] == src_tiles[t] * per_tile + k
               for t in range(n_tiles) for k in range(per_tile))

    def src_tile(j):
        t = j + src_tiles[0]
        for k in range(1, n_tiles):
            jump = (src_tiles[k] - k) - (src_tiles[k - 1] - (k - 1))
            if jump:
                t = t + jnp.where(j >= k, jump, 0)
        return t

    n_rows = m // tm
    assert n_tiles > _NORM_AHEAD_STEPS and tm % (_NORM_AHEAD_STEPS * _NORM_CHUNK) == 0

    def x_tile(i, j):
        return jnp.minimum(i + jnp.where(j > 0, 1, 0), n_rows - 1)

    def mod_row(i, j):
        return mod_row0 + (x_tile(i, j) * tm) // rows_per_mod

    return pl.pallas_call(
        _inproj_kernel,
        out_shape=jax.ShapeDtypeStruct((m, cols.width), BF16),
        grid=(n_rows, n_tiles),
        in_specs=[
            pl.BlockSpec((tm, D_MODEL), lambda i, j: (x_tile(i, j), 0)),
            pl.BlockSpec((None, 1, D_MODEL), lambda i, j: (layer, 0, 0)),
            pl.BlockSpec((1, 1, D_MODEL), lambda i, j: (mod_row(i, j), 0, 0)),
            pl.BlockSpec((1, 1, D_MODEL), lambda i, j: (mod_row(i, j), 0, 1)),
            pl.BlockSpec((None, D_MODEL, tn), lambda i, j: (layer, 0, src_tile(j))),
        ],
        out_specs=pl.BlockSpec((tm, tn), lambda i, j: (i, j)),
        scratch_shapes=[pltpu.VMEM((tm, D_MODEL), BF16), pltpu.VMEM((tm, D_MODEL), BF16)],
        compiler_params=_params(("arbitrary", "arbitrary")),
        name="input_projection",
    )(x2d, norm_w3, mod3, mod3, w_bf16)


def _rope(x, cos, sin_lo, sin_hi):
    return (x * cos + pltpu.roll(x, HEAD_DIM - 16, axis=1) * sin_lo
            + pltpu.roll(x, 16, axis=1) * sin_hi)


def _diff_lambda(lq_ref, lam_init):
    lq = lq_ref[...]
    a = jnp.sum(lq[0:1] * lq[1:2], axis=-1, keepdims=True)
    b = jnp.sum(lq[2:3] * lq[3:4], axis=-1, keepdims=True)
    return jnp.exp(a) - jnp.exp(b) + lam_init


LOG2E = 1.4426950408889634
NKEYS = CTX_LEN + SEQ
DA_TQ = 256
DA_W = 2 * DA_TQ
DA_ROWS = 16
DA_KCHUNK = 256


def _transpose_bf16(x):
    return x.astype(F32).T.astype(BF16)


def _diffattn_lat_kernel(q_ref, kc_ref, vc_ref, kl_ref, vl_ref, g_ref, cos_ref, slo_ref, shi_ref,
                         lq_ref, sw_ref, o_ref, k_scr, vt_scr, qt_scr, s0_scr, s1_scr, m_scr, p_scr,
                         *, lam_init):
    qi = pl.program_id(2)
    n_tiles = SEQ // DA_TQ
    n_chunks = NKEYS // DA_KCHUNK
    groups_per_chunk = DA_KCHUNK // DA_ROWS

    def score_chunk(s_scr, qt2, c, m_run):
        rows = slice(c * DA_KCHUNK, (c + 1) * DA_KCHUNK)
        s = jnp.dot(k_scr[rows, :], qt2, preferred_element_type=F32)
        s_scr[rows, :] = s
        parts = [s[i * DA_ROWS:(i + 1) * DA_ROWS, :] for i in range(groups_per_chunk)]
        if m_run is not None:
            parts.append(m_run)
        while len(parts) > 1:
            parts = [jnp.maximum(parts[i], parts[i + 1]) for i in range(0, len(parts) - 1, 2)] + (
                [parts[-1]] if len(parts) % 2 else [])
        return parts[0]

    def store_colmax(m_run):
        m_scr[...] = jnp.broadcast_to(jnp.max(m_run, axis=0, keepdims=True), (DA_ROWS, DA_W))

    @pl.when(qi == 0)
    def _():
        assert DA_KCHUNK == DA_TQ == CTX_LEN
        dim = lax.broadcasted_iota(jnp.int32, (HB, DA_TQ), 0)

        def stage_queries(c):
            rows = slice(c * DA_TQ, (c + 1) * DA_TQ)
            q = _rope(q_ref[rows, :].astype(F32), cos_ref[rows, :], slo_ref[rows, :], shi_ref[rows, :])
            qt = (q * (A_QK_DIM ** -0.5 * LOG2E)).T
            qt_scr[:, c * DA_W:c * DA_W + DA_TQ] = jnp.where(dim < A_QK_DIM, qt, 0.0).astype(BF16)
            qt_scr[:, c * DA_W + DA_TQ:(c + 1) * DA_W] = jnp.where(dim >= A_QK_DIM, qt, 0.0).astype(BF16)

        stage_queries(0)
        qt0 = qt_scr[:, 0:DA_W]
        k_scr[0:CTX_LEN, :] = kc_ref[...]
        vt_scr[:, 0:CTX_LEN] = _transpose_bf16(vc_ref[...])
        m_run = score_chunk(s0_scr, qt0, 0, None)
        for c in range(SEQ // DA_TQ):
            rows = slice(c * DA_TQ, (c + 1) * DA_TQ)
            dst = slice(CTX_LEN + c * DA_TQ, CTX_LEN + (c + 1) * DA_TQ)
            k_scr[dst, :] = _rope(kl_ref[rows, :].astype(F32), cos_ref[rows, :], slo_ref[rows, :],
                                  shi_ref[rows, :]).astype(BF16)
            m_run = score_chunk(s0_scr, qt0, c + 1, m_run)
            vt_scr[:, dst] = _transpose_bf16(vl_ref[rows, :])
            if c > 0:
                stage_queries(c)
        store_colmax(m_run)

    def tile(s_scr, s_next_scr):
        if s_next_scr is not None:
            qt_next = qt_scr[:, pl.ds(pl.multiple_of((qi + 1) * DA_W, DA_W), DA_W)]
        mb = m_scr[...]
        m_run = None
        l_acc = [None, None]
        acc = None

        def value_chunk(c, acc):
            rows = slice(c * DA_KCHUNK, (c + 1) * DA_KCHUNK)
            part = jnp.dot(vt_scr[:, rows], p_scr[rows, :], preferred_element_type=F32)
            return part if acc is None else acc + part

        for c in range(n_chunks):
            if s_next_scr is not None:
                m_run = score_chunk(s_next_scr, qt_next, c, m_run)
            for r in range(c * groups_per_chunk, (c + 1) * groups_per_chunk):
                grp = slice(r * DA_ROWS, (r + 1) * DA_ROWS)
                p = jnp.exp2(s_scr[grp, :] - mb)
                p_scr[grp, :] = p.astype(BF16)
                l_acc[r % 2] = p if l_acc[r % 2] is None else l_acc[r % 2] + p
            acc = value_chunk(c, acc)
        if s_next_scr is not None:
            store_colmax(m_run)
        l = jnp.sum(l_acc[0] + l_acc[1], axis=0, keepdims=True)

        lam = _diff_lambda(lq_ref, lam_init)
        ot = acc[:, :DA_TQ] * (1.0 / l[:, :DA_TQ]) - acc[:, DA_TQ:] * (lam / l[:, DA_TQ:])
        ot = ot * lax.rsqrt(jnp.mean(ot * ot, axis=0, keepdims=True) + SUBLN_EPS)
        o = ot.T * sw_ref[...] * (1.0 - lam_init)
        o_ref[...] = (o * _silu(g_ref[...].astype(F32))).astype(o_ref.dtype)

    parity = qi % 2
    last = qi == n_tiles - 1
    pl.when(parity == 0)(functools.partial(tile, s0_scr, s1_scr))
    pl.when(jnp.logical_and(parity == 1, jnp.logical_not(last)))(functools.partial(tile, s1_scr, s0_scr))
    pl.when(last)(functools.partial(tile, s1_scr, None))


_NT = (((1,), (1,)), ((), ()))


def _diffattn_ctx_kernel(q_ref, k_ref, v_ref, g_ref, lq_ref, sw_ref, o_ref, *, lam_init):
    q = q_ref[...].astype(F32) * (A_QK_DIM ** -0.5)
    lane = lax.broadcasted_iota(jnp.int32, q.shape, 1)
    outs = []
    for qn in (jnp.where(lane < A_QK_DIM, q, 0.0), jnp.where(lane >= A_QK_DIM, q, 0.0)):
        s = lax.dot_general(qn.astype(BF16), k_ref[...], _NT, preferred_element_type=F32)
        p = jnp.exp(s - jnp.max(s, axis=-1, keepdims=True))
        acc = jnp.dot(p.astype(BF16), v_ref[...], preferred_element_type=F32)
        outs.append((acc, jnp.sum(p, axis=-1, keepdims=True)))
    lam = _diff_lambda(lq_ref, lam_init)
    o = outs[0][0] * (1.0 / outs[0][1]) - (lam / outs[1][1]) * outs[1][0]
    o = o * lax.rsqrt(jnp.mean(o * o, axis=-1, keepdims=True) + SUBLN_EPS) * sw_ref[...]
    o = o * (1.0 - lam_init)
    o_ref[...] = (o * _silu(g_ref[...].astype(F32))).astype(o_ref.dtype)


def diff_attention_latent(proj_l, proj_c, ccols, rope_tabs, lambda_qk, subln_w, lam_init):
    nq = SEQ // DA_TQ
    head = lambda col: (lambda b, h, i: (b, col + h))
    qmap = lambda col: (lambda b, h, i: (b * nq + i, col + h))
    const = lambda b, h, i: (0, 0)
    return pl.pallas_call(
        functools.partial(_diffattn_lat_kernel, lam_init=lam_init),
        out_shape=jax.ShapeDtypeStruct((BATCH * SEQ, A_WIDTH), BF16),
        grid=(BATCH, A_HEADS, nq),
        in_specs=[
            pl.BlockSpec((SEQ, HB), head(COL_QA)),
            pl.BlockSpec((CTX_LEN, HB), head(ccols["ka"])),
            pl.BlockSpec((CTX_LEN, HB), head(ccols["va"])),
            pl.BlockSpec((SEQ, HB), head(COL_KA)),
            pl.BlockSpec((SEQ, HB), head(COL_VA)),
            pl.BlockSpec((DA_TQ, HB), qmap(COL_GA)),
            pl.BlockSpec((SEQ, HB), const),
            pl.BlockSpec((SEQ, HB), const),
            pl.BlockSpec((SEQ, HB), const),
            pl.BlockSpec((4, A_QK_DIM), const),
            pl.BlockSpec((1, HB), const),
        ],
        out_specs=pl.BlockSpec((DA_TQ, HB), lambda b, h, i: (b * nq + i, h)),
        scratch_shapes=[pltpu.VMEM((NKEYS, HB), BF16), pltpu.VMEM((HB, NKEYS), BF16),
                        pltpu.VMEM((HB, 2 * SEQ), BF16),
                        pltpu.VMEM((NKEYS, DA_W), F32), pltpu.VMEM((NKEYS, DA_W), F32),
                        pltpu.VMEM((DA_ROWS, DA_W), F32), pltpu.VMEM((NKEYS, DA_W), BF16)],
        compiler_params=_params(("arbitrary", "arbitrary", "arbitrary")),
        name="diff_attention_latent",
    )(proj_l, proj_c, proj_c, proj_l, proj_l, proj_l, *rope_tabs, lambda_qk, subln_w.reshape(1, HB))


def diff_attention_context(proj_c, lambda_qk, subln_w, lam_init):
    head = lambda col: (lambda b, h: (b, col + h))
    const = lambda b, h: (0, 0)
    return pl.pallas_call(
        functools.partial(_diffattn_ctx_kernel, lam_init=lam_init),
        out_shape=jax.ShapeDtypeStruct((BATCH * CTX_LEN, A_WIDTH), BF16),
        grid=(BATCH, A_HEADS),
        in_specs=[pl.BlockSpec((CTX_LEN, HB), head(COL_QA)),
                  pl.BlockSpec((CTX_LEN, HB), head(COL_KA)),
                  pl.BlockSpec((CTX_LEN, HB), head(COL_VA)),
                  pl.BlockSpec((CTX_LEN, HB), head(COL_GA)),
                  pl.BlockSpec((4, A_QK_DIM), const),
                  pl.BlockSpec((1, HB), const)],
        out_specs=pl.BlockSpec((CTX_LEN, HB), lambda b, h: (b, h)),
        compiler_params=_params(("arbitrary", "arbitrary")),
        name="diff_attention_context",
    )(proj_c, proj_c, proj_c, proj_c, lambda_qk, subln_w.reshape(1, HB))


def _nbr_slab_start(g):
    return min(max(g * NB_QROWS - NA_KH // 2, 0), GRID_H - NB_KROWS)


def _nbr_pattern(g):
    return 0 if g == 0 else (2 if g == NB_GROUPS - 1 else 1)


def _nbr_kernel(q_ref, kc_ref, vc_ref, kl_ref, vl_ref, g_ref, bias_ref, o_ref, vt_scr):
    vt_scr[:, 0:CTX_LEN] = _transpose_bf16(vc_ref[...])
    for c in range(SEQ // NB_Q):
        vt_scr[:, CTX_LEN + c * NB_Q:CTX_LEN + (c + 1) * NB_Q] = _transpose_bf16(vl_ref[c * NB_Q:(c + 1) * NB_Q, :])
    def scores(g):
        rows = slice(g * NB_Q, (g + 1) * NB_Q)
        k0 = _nbr_slab_start(g) * GRID_W
        qt = (q_ref[rows, :].astype(F32) * (HEAD_DIM ** -0.5 * LOG2E)).T.astype(BF16)
        s_c = jnp.dot(kc_ref[...], qt, preferred_element_type=F32)
        s_n = (jnp.dot(kl_ref[k0:k0 + NB_K, :], qt, preferred_element_type=F32)
               + bias_ref[0, _nbr_pattern(g)])
        return s_c, s_n

    def softmax(s_c, s_n):
        m = jnp.maximum(jnp.max(s_c, axis=0, keepdims=True), jnp.max(s_n, axis=0, keepdims=True))
        p_c = jnp.exp2(s_c - m)
        p_n = jnp.exp2(s_n - m)
        l = jnp.sum(p_c, axis=0, keepdims=True) + jnp.sum(p_n, axis=0, keepdims=True)
        return p_c.astype(BF16), p_n.astype(BF16), l

    def finish(g, p_c, p_n, l):
        rows = slice(g * NB_Q, (g + 1) * NB_Q)
        k0 = _nbr_slab_start(g) * GRID_W
        acc = (jnp.dot(vt_scr[:, 0:CTX_LEN], p_c, preferred_element_type=F32)
               + jnp.dot(vt_scr[:, CTX_LEN + k0:CTX_LEN + k0 + NB_K], p_n,
                         preferred_element_type=F32))
        o = (acc * (1.0 / l)).T
        o_ref[rows, :] = (o * _silu(g_ref[rows, :].astype(F32))).astype(o_ref.dtype)

    ahead = scores(0)
    probs = None
    for g in range(NB_GROUPS):
        s_c, s_n = ahead
        if g + 1 < NB_GROUPS:
            ahead = scores(g + 1)
        new_probs = softmax(s_c, s_n)
        if probs is not None:
            finish(g - 1, *probs)
        probs = new_probs
    finish(NB_GROUPS - 1, *probs)


def neighbourhood_bias_table(rpb):
    w = GRID_W
    cq = np.arange(w)
    c0 = np.clip(cq - NA_KW // 2, 0, w - NA_KW)
    ck = np.arange(w)
    col_valid = (ck[None, :] >= c0[:, None]) & (ck[None, :] < c0[:, None] + NA_KW)
    pad = w - NA_KW
    padded = jnp.pad(rpb.astype(F32), ((0, 0), (0, 0), (pad, pad)))
    toep = jnp.stack([padded[:, :, w - 1 - q:2 * w - 1 - q] for q in range(w)], axis=2)
    toep = jnp.where(col_valid[None, None], toep, NEG_INF)
    toep_t = jnp.swapaxes(toep, 2, 3) * LOG2E
    masked = jnp.full((B_HEADS, w, w), NEG_INF, F32)
    tabs = []
    for g in (0, 1, NB_GROUPS - 1):
        slab0 = _nbr_slab_start(g)
        qr = g * NB_QROWS + np.arange(NB_QROWS)
        kr = slab0 + np.arange(NB_KROWS)
        win0 = np.clip(qr - NA_KH // 2, 0, GRID_H - NA_KH)
        row_valid = (kr[None, :] >= win0[:, None]) & (kr[None, :] < win0[:, None] + NA_KH)
        dr_idx = kr[None, :] - qr[:, None] + NA_KH - 1
        rows = [jnp.concatenate([toep_t[:, int(dr_idx[a, k])] if row_valid[a, k] else masked
                                 for a in range(NB_QROWS)], axis=-1) for k in range(NB_KROWS)]
        tabs.append(jnp.concatenate(rows, axis=1))
    return jnp.stack(tabs, axis=1)


def neighbourhood_attention(proj_l, proj_c, ccols, bias_tab):
    head = lambda col: (lambda h, b: (b, col + h))
    return pl.pallas_call(
        _nbr_kernel,
        out_shape=jax.ShapeDtypeStruct((BATCH * SEQ, B_WIDTH), BF16),
        grid=(B_HEADS, BATCH),
        in_specs=[
            pl.BlockSpec((SEQ, HB), head(COL_QB)),
            pl.BlockSpec((CTX_LEN, HB), head(ccols["kb"])),
            pl.BlockSpec((CTX_LEN, HB), head(ccols["vb"])),
            pl.BlockSpec((SEQ, HB), head(COL_KB)),
            pl.BlockSpec((SEQ, HB), head(COL_VB)),
            pl.BlockSpec((SEQ, HB), head(COL_GB)),
            pl.BlockSpec((1, 3, NB_K, NB_Q), lambda h, b: (h, 0, 0, 0)),
        ],
        out_specs=pl.BlockSpec((SEQ, HB), lambda h, b: (b, h)),
        scratch_shapes=[pltpu.VMEM((HB, NKEYS), BF16)],
        compiler_params=_params(("arbitrary", "arbitrary")),
        name="neighbourhood_attention",
    )(proj_l, proj_c, proj_c, proj_l, proj_l, proj_l, bias_tab)


def _dense_ctx_kernel(q_ref, k_ref, v_ref, g_ref, o_ref):
    scale = HEAD_DIM ** -0.5
    s = lax.dot_general(q_ref[...], k_ref[...], _NT, preferred_element_type=F32) * scale
    p = jnp.exp(s - jnp.max(s, axis=-1, keepdims=True))
    p = p * (1.0 / jnp.sum(p, axis=-1, keepdims=True))
    o = jnp.dot(p.astype(BF16), v_ref[...], preferred_element_type=F32)
    o_ref[...] = (o * _silu(g_ref[...].astype(F32))).astype(o_ref.dtype)


def dense_context_attention(proj_c):
    head = lambda col: (lambda b, h: (b, col + h))
    return pl.pallas_call(
        _dense_ctx_kernel,
        out_shape=jax.ShapeDtypeStruct((BATCH * CTX_LEN, B_WIDTH), BF16),
        grid=(BATCH, B_HEADS),
        in_specs=[pl.BlockSpec((CTX_LEN, HB), head(COL_QB)),
                  pl.BlockSpec((CTX_LEN, HB), head(COL_KB)),
                  pl.BlockSpec((CTX_LEN, HB), head(COL_VB)),
                  pl.BlockSpec((CTX_LEN, HB), head(COL_GB))],
        out_specs=pl.BlockSpec((CTX_LEN, HB), lambda b, h: (b, h)),
        compiler_params=_params(("arbitrary", "arbitrary")),
        name="dense_context_attention",
    )(proj_c, proj_c, proj_c, proj_c)


def _shift_rows(x, d, fill):
    n = x.shape[0]
    if d % SUBLANES == 0:
        pad = jnp.full((abs(d), x.shape[1]), fill, x.dtype)
        return jnp.concatenate([pad, x[:n - d]] if d > 0 else [x[-d:], pad], axis=0)
    rolled = pltpu.roll(x, d % n, axis=0)
    row = lax.broadcasted_iota(jnp.int32, x.shape, 0)
    keep = (row >= d) if d > 0 else (row < n + d)
    return jnp.where(keep, rolled, fill)


def _dwconv(x, w_ref, b_ref):
    acc = b_ref[...] + w_ref[CONV_LEFT:CONV_LEFT + 1, :] * x
    for j in range(CONV_W):
        if j != CONV_LEFT:
            acc = acc + w_ref[j:j + 1, :] * _shift_rows(x, CONV_LEFT - j, 0.0)
    return acc


def _linear_scan(a, b, h0, reverse):
    n = a.shape[0]
    row = lax.broadcasted_iota(jnp.int32, a.shape, 0)
    first = (row == n - 1) if reverse else (row == 0)
    b = b + jnp.where(first, a * h0, 0.0)
    d = 1
    while d < n:
        sd = -d if reverse else d
        b = b + a * _shift_rows(b, sd, 0.0)
        if 2 * d < n:
            a = a * _shift_rows(a, sd, 1.0)
        d *= 2
    return b


def _sigmoid_tanh(x):
    return 0.5 * jnp.tanh(0.5 * x) + 0.5


def _rglru_gates(u, wa_ref, ba_ref, wx_ref, bx_ref, lam_ref, d):
    ub = u.astype(BF16)
    r = _sigmoid_tanh(jnp.dot(ub, wa_ref[0, d], preferred_element_type=F32) + ba_ref[0, d])
    i = _sigmoid_tanh(jnp.dot(ub, wx_ref[0, d], preferred_element_type=F32) + bx_ref[0, d])
    neg_lam = -lam_ref[0, d]
    softplus = jnp.maximum(neg_lam, 0.0) + jnp.log1p(jnp.exp(-jnp.abs(neg_lam)))
    log_a = (-RGLRU_C) * r * softplus
    a = jnp.exp(log_a)
    t = jnp.tanh(log_a)
    b = jnp.sqrt(-2.0 * t / (1.0 - t)) * (i * u)
    return a, b


SCAN_CHUNKS = SUBLANES


def _scan_chunk_len(n):
    return n // SCAN_CHUNKS + 1


def _blocked_scan(a_scr, b_scr, n, h0, reverse, need_states):
    length = _scan_chunk_len(n)
    order = range(length - 1, -1, -1) if reverse else range(length)
    h = jnp.zeros((SCAN_CHUNKS, a_scr.shape[1]), F32)
    p = jnp.ones_like(h)
    hs, ps = [None] * length, [None] * length
    for i in order:
        va = a_scr[pl.ds(i, SCAN_CHUNKS, stride=length), :]
        vb = b_scr[pl.ds(i, SCAN_CHUNKS, stride=length), :]
        h = va * h + vb
        p = va * p
        hs[i], ps[i] = h, p
    rows = [None] * SCAN_CHUNKS
    carry = h0
    for j in (range(SCAN_CHUNKS - 1, -1, -1) if reverse else range(SCAN_CHUNKS)):
        rows[j] = carry
        carry = h[j:j + 1, :] + p[j:j + 1, :] * carry
    if not need_states:
        return None, carry
    start = jnp.concatenate(rows, axis=0)
    return [hs[i] + ps[i] * start for i in range(length)], carry


def _rglru_kernel(*refs, need_ctx):
    if need_ctx:
        (xl_ref, xc_ref, gl_ref, gc_ref, cw_ref, cb_ref, wa_ref, ba_ref, wx_ref, bx_ref, lam_ref,
         ol_ref, oc_ref, al_scr, bl_scr, ac_scr, bc_scr, yl_scr, yc_scr) = refs
    else:
        (xl_ref, xc_ref, gl_ref, cw_ref, cb_ref, wa_ref, ba_ref, wx_ref, bx_ref, lam_ref,
         ol_ref, al_scr, bl_scr, ac_scr, bc_scr, yl_scr, yc_scr) = refs
    ct = al_scr.shape[1]
    u_l = _dwconv(xl_ref[...].astype(F32), cw_ref, cb_ref)
    u_c = _dwconv(xc_ref[...].astype(F32), cw_ref, cb_ref)
    for a_scr, b_scr, n in ((al_scr, bl_scr, SEQ), (ac_scr, bc_scr, CTX_LEN)):
        a_scr[n:n + SUBLANES, :] = jnp.ones((SUBLANES, ct), F32)
        b_scr[n:n + SUBLANES, :] = jnp.zeros((SUBLANES, ct), F32)
    y_l = None
    y_c = None
    for d in range(2):
        rev = d == 1
        a_c, b_c = _rglru_gates(u_c, wa_ref, ba_ref, wx_ref, bx_ref, lam_ref, d)
        ac_scr[0:CTX_LEN, :] = a_c
        bc_scr[0:CTX_LEN, :] = b_c
        h_c, fin = _blocked_scan(ac_scr, bc_scr, CTX_LEN, jnp.zeros((1, ct), F32), rev, need_ctx)
        a_l, b_l = _rglru_gates(u_l, wa_ref, ba_ref, wx_ref, bx_ref, lam_ref, d)
        al_scr[0:SEQ, :] = a_l
        bl_scr[0:SEQ, :] = b_l
        h_l, _ = _blocked_scan(al_scr, bl_scr, SEQ, fin, rev, True)
        y_l = h_l if y_l is None else [f + r for f, r in zip(y_l, h_l)]
        if need_ctx:
            y_c = h_c if y_c is None else [f + r for f, r in zip(y_c, h_c)]
    length = _scan_chunk_len(SEQ)
    for i in range(length):
        yl_scr[pl.ds(i, SCAN_CHUNKS, stride=length), :] = y_l[i]
    ol_ref[...] = (yl_scr[0:SEQ, :] * _silu(gl_ref[...].astype(F32))).astype(ol_ref.dtype)
    if need_ctx:
        length = _scan_chunk_len(CTX_LEN)
        for i in range(length):
            yc_scr[pl.ds(i, SCAN_CHUNKS, stride=length), :] = y_c[i]
        oc_ref[...] = (yc_scr[0:CTX_LEN, :] * _silu(gc_ref[...].astype(F32))).astype(oc_ref.dtype)


def _block_diag(w, per_tile):
    n = w.shape[-1]
    tiles = C_BLOCKS // per_tile
    w = w.reshape(2, tiles, per_tile, n, n)
    eye = jnp.eye(per_tile, dtype=w.dtype)
    bd = jnp.einsum('dtpij,pq->dtpiqj', w, eye).reshape(2, tiles, per_tile * n, per_tile * n)
    return jnp.transpose(bd, (1, 0, 2, 3)).astype(BF16)


def rglru_mixer(proj_l, proj_c, ccols, conv_w, conv_b, rg_wa, rg_ba, rg_wx, rg_bx, rg_lambda, need_ctx):
    ct = 128
    nt = C_WIDTH // ct
    per_tile = ct // C_BLOCK_DIM
    vec = lambda p: jnp.transpose(p.reshape(2, nt, 1, ct), (1, 0, 2, 3))
    wa = _block_diag(rg_wa, per_tile)
    wx = _block_diag(rg_wx, per_tile)
    xcol = lambda col: (lambda b, t: (b, col + t))
    par = lambda b, t: (t, 0, 0, 0)
    out_shape = [jax.ShapeDtypeStruct((BATCH * SEQ, C_WIDTH), BF16)]
    out_specs = [pl.BlockSpec((SEQ, ct), lambda b, t: (b, t))]
    if need_ctx:
        out_shape.append(jax.ShapeDtypeStruct((BATCH * CTX_LEN, C_WIDTH), BF16))
        out_specs.append(pl.BlockSpec((CTX_LEN, ct), lambda b, t: (b, t)))
    in_specs = [pl.BlockSpec((SEQ, ct), xcol(COL_XC)),
                pl.BlockSpec((CTX_LEN, ct), xcol(ccols["xc"])),
                pl.BlockSpec((SEQ, ct), xcol(COL_GC))]
    args = [proj_l, proj_c, proj_l]
    if need_ctx:
        in_specs.append(pl.BlockSpec((CTX_LEN, ct), xcol(ccols["gc"])))
        args.append(proj_c)
    in_specs += [pl.BlockSpec((CONV_W, ct), lambda b, t: (0, t)),
                 pl.BlockSpec((1, ct), lambda b, t: (0, t)),
                 pl.BlockSpec((1, 2, ct, ct), par),
                 pl.BlockSpec((1, 2, 1, ct), par),
                 pl.BlockSpec((1, 2, ct, ct), par),
                 pl.BlockSpec((1, 2, 1, ct), par),
                 pl.BlockSpec((1, 2, 1, ct), par)]
    args += [conv_w, conv_b.reshape(1, C_WIDTH), wa, vec(rg_ba), wx, vec(rg_bx), vec(rg_lambda)]
    outs = pl.pallas_call(
        functools.partial(_rglru_kernel, need_ctx=need_ctx),
        out_shape=out_shape,
        grid=(BATCH, nt),
        in_specs=in_specs,
        out_specs=out_specs,
        scratch_shapes=([pltpu.VMEM((SEQ + SUBLANES, ct), F32)] * 2
                        + [pltpu.VMEM((CTX_LEN + SUBLANES, ct), F32)] * 2
                        + [pltpu.VMEM((SEQ + SUBLANES, ct), F32), pltpu.VMEM((CTX_LEN + SUBLANES, ct), F32)]),
        compiler_params=_params(("arbitrary", "arbitrary")),
        name="rglru_mixer",
    )(*args)
    return outs if need_ctx else (outs[0], None)


def _outproj_kernel(x_ref, gate_ref, ma_ref, mb_ref, mc_ref, wa_ref, wb_ref, wc_ref, o_ref):
    y = jnp.dot(ma_ref[...], wa_ref[...], preferred_element_type=F32)
    y = y + jnp.dot(mb_ref[...], wb_ref[...], preferred_element_type=F32)
    y = y + jnp.dot(mc_ref[...], wc_ref[...], preferred_element_type=F32)
    o_ref[...] = x_ref[...] + gate_ref[0] * y


def output_projection(x2d, mod3, ma, mb, mc, w_bf16, layer, rows_per_mod, mod_row0):
    m = x2d.shape[0]
    tm, tn = 1024, 1024

    def mod_row(i):
        return mod_row0 + (i * tm) // rows_per_mod

    return pl.pallas_call(
        _outproj_kernel,
        out_shape=jax.ShapeDtypeStruct((m, D_MODEL), F32),
        grid=(m // tm, D_MODEL // tn),
        in_specs=[
            pl.BlockSpec((tm, tn), lambda i, j: (i, j)),
            pl.BlockSpec((1, 1, tn), lambda i, j: (mod_row(i), 0, 2 * (D_MODEL // tn) + j)),
            pl.BlockSpec((tm, A_WIDTH), lambda i, j: (i, 0)),
            pl.BlockSpec((tm, B_WIDTH), lambda i, j: (i, 0)),
            pl.BlockSpec((tm, C_WIDTH), lambda i, j: (i, 0)),
            pl.BlockSpec((None, A_WIDTH, tn), lambda i, j: (layer, 0, j)),
            pl.BlockSpec((None, B_WIDTH, tn), lambda i, j: (layer, 1, j)),
            pl.BlockSpec((None, C_WIDTH, tn), lambda i, j: (layer, (A_WIDTH + B_WIDTH) // C_WIDTH, j)),
        ],
        out_specs=pl.BlockSpec((tm, tn), lambda i, j: (i, j)),
        compiler_params=_params(("arbitrary", "arbitrary")),
        name="output_projection",
    )(x2d, mod3, ma, mb, mc, w_bf16, w_bf16, w_bf16)


def _final_norm_kernel(x_ref, w_ref, o_ref):
    x = x_ref[...]
    o_ref[...] = x * lax.rsqrt(jnp.mean(x * x, axis=-1, keepdims=True) + NORM_EPS) * w_ref[...]


def final_norm(x2d, w):
    tm = 256
    return pl.pallas_call(
        _final_norm_kernel,
        out_shape=jax.ShapeDtypeStruct(x2d.shape, F32),
        grid=(x2d.shape[0] // tm,),
        in_specs=[pl.BlockSpec((tm, D_MODEL), lambda i: (i, 0)),
                  pl.BlockSpec((1, D_MODEL), lambda i: (0, 0))],
        out_specs=pl.BlockSpec((tm, D_MODEL), lambda i: (i, 0)),
        compiler_params=_params(("arbitrary",)),
        name="final_norm",
    )(x2d, w.reshape(1, D_MODEL))


def _rope_tables():
    t = np.arange(SEQ)
    pos = np.stack([t // GRID_W, t % GRID_W], axis=0).astype(np.float32)
    lane = np.arange(HEAD_DIM)
    within = lane % A_QK_DIM
    axis = within // (A_QK_DIM // 2)
    k = within % (A_QK_DIM // 2)
    nfreq = A_QK_DIM // 4
    upper = k >= nfreq
    freqs = jnp.asarray(ROPE_THETA, F32) ** (-jnp.arange(nfreq, dtype=F32) / nfreq)
    ang = jnp.asarray(pos)[axis, :].T * freqs[k % nfreq][None, :]
    cos, sin = jnp.cos(ang), jnp.sin(ang)
    sin_lo = jnp.where(upper[None, :], 0.0, -sin)
    sin_hi = jnp.where(upper[None, :], sin, 0.0)
    return cos, sin_lo, sin_hi


def kernel(x, c, ctx, c_ctx, ada_w, ada_b, norm_w, w_in, w_out, lambda_qk, subln_w, rpb,
           conv_w, conv_b, rg_wa, rg_ba, rg_wx, rg_bx, rg_lambda, final_norm_w):
    cvec = jnp.zeros((MOD_ROWS, D_MODEL), F32).at[:BATCH].set(c).at[CTX_MOD_ROW].set(c_ctx)
    mod = adaln_modulation(cvec, ada_w, ada_b)
    mod3 = mod.reshape(DEPTH * MOD_ROWS, 1, 3 * D_MODEL)
    norm_w3 = norm_w.reshape(DEPTH, 1, D_MODEL)
    w_in_bf = w_in.astype(BF16)
    w_out_bf = w_out.astype(BF16)
    rope_tabs = _rope_tables()
    xl = x.reshape(BATCH * SEQ, D_MODEL)
    xc = ctx.reshape(BATCH * CTX_LEN, D_MODEL)
    for l in range(DEPTH):
        need_ctx = l < DEPTH - 1
        lam_init = 0.8 - 0.6 * math.exp(-0.3 * l)
        row_l, row_c = l * MOD_ROWS, l * MOD_ROWS + CTX_MOD_ROW
        proj_l = input_projection(xl, norm_w3, mod3, w_in_bf, l, SEQ, row_l)
        ccols = FULL_COLS if need_ctx else KV_COLS
        proj_c = input_projection(xc, norm_w3, mod3, w_in_bf, l, BATCH * CTX_LEN, row_c, ccols)

        ma = diff_attention_latent(proj_l, proj_c, ccols, rope_tabs, lambda_qk[l], subln_w[l], lam_init)
        mb = neighbourhood_attention(proj_l, proj_c, ccols, neighbourhood_bias_table(rpb[l]))
        mc, mc_c = rglru_mixer(proj_l, proj_c, ccols, conv_w[l], conv_b[l], rg_wa[l], rg_ba[l],
                               rg_wx[l], rg_bx[l], rg_lambda[l], need_ctx)
        xl = output_projection(xl, mod3, ma, mb, mc, w_out_bf, l, SEQ, row_l)
        if need_ctx:
            ma_c = diff_attention_context(proj_c, lambda_qk[l], subln_w[l], lam_init)
            mb_c = dense_context_attention(proj_c)
            xc = output_projection(xc, mod3, ma_c, mb_c, mc_c, w_out_bf, l, BATCH * CTX_LEN, row_c)
    return final_norm(xl, final_norm_w).reshape(BATCH, SEQ, D_MODEL)
```

```python
import functools
import math

import numpy as np
import jax
import jax.numpy as jnp
from jax import lax
from jax.experimental import pallas as pl
from jax.experimental.pallas import tpu as pltpu

F32 = jnp.float32
BF16 = jnp.bfloat16

D_MODEL = 4096
BATCH = 8
SEQ = 2048
DEPTH = 2
GRID_W = 64
GRID_H = SEQ // GRID_W
CTX_LEN = 256
HEAD_DIM = 128
A_WIDTH = (3 * D_MODEL) // 8
A_HEADS = A_WIDTH // HEAD_DIM
A_QK_DIM = HEAD_DIM // 2
B_WIDTH = (3 * D_MODEL) // 8
B_HEADS = B_WIDTH // HEAD_DIM
C_WIDTH = D_MODEL - A_WIDTH - B_WIDTH
C_BLOCKS = 16
C_BLOCK_DIM = C_WIDTH // C_BLOCKS
MIX_WIDTH = A_WIDTH + B_WIDTH + C_WIDTH
IN_WIDTH = 4 * A_WIDTH + 4 * B_WIDTH + 2 * C_WIDTH
NA_KH = 8
NA_KW = 16
ROPE_THETA = 10000.0
RGLRU_C = 8.0
CONV_W = 4
CONV_LEFT = 2
NORM_EPS = 1e-6
SUBLN_EPS = 1e-5
NEG_INF = -1e30

HB = HEAD_DIM
COL_QA, COL_KA, COL_VA, COL_GA = 0, A_HEADS, 2 * A_HEADS, 3 * A_HEADS
COL_QB = 4 * A_HEADS
COL_KB, COL_VB, COL_GB = COL_QB + B_HEADS, COL_QB + 2 * B_HEADS, COL_QB + 3 * B_HEADS
COL_XC = COL_QB + 4 * B_HEADS
COL_GC = COL_XC + C_WIDTH // HB

MOD_ROWS = 16
CTX_MOD_ROW = BATCH

VMEM_LIMIT = 56 * 1024 * 1024
SUBLANES = 8

NB_QROWS = 4
NB_KROWS = 12
NB_GROUPS = GRID_H // NB_QROWS
NB_Q = NB_QROWS * GRID_W
NB_K = NB_KROWS * GRID_W


def _params(sem):
    return pltpu.CompilerParams(dimension_semantics=sem, vmem_limit_bytes=VMEM_LIMIT)


def _sigmoid(x):
    return 1.0 / (1.0 + jnp.exp(-x))


def _silu(x):
    return x * _sigmoid(x)


def _adaln_kernel(c_ref, w_ref, b_ref, o_ref):
    s = _silu(c_ref[...]).astype(BF16)
    w = w_ref[0].astype(BF16)
    o_ref[0] = jnp.dot(s, w, preferred_element_type=F32) + b_ref[0]


def adaln_modulation(cvec, ada_w, ada_b):
    tn = 512
    n3 = 3 * D_MODEL
    return pl.pallas_call(
        _adaln_kernel,
        out_shape=jax.ShapeDtypeStruct((DEPTH, MOD_ROWS, n3), F32),
        grid=(DEPTH, n3 // tn),
        in_specs=[
            pl.BlockSpec((MOD_ROWS, D_MODEL), lambda l, j: (0, 0)),
            pl.BlockSpec((1, D_MODEL, tn), lambda l, j: (l, 0, j)),
            pl.BlockSpec((1, 1, tn), lambda l, j: (l, 0, j)),
        ],
        out_specs=pl.BlockSpec((1, MOD_ROWS, tn), lambda l, j: (l, 0, j)),
        compiler_params=_params(("arbitrary", "arbitrary")),
        name="adaln_modulation",
    )(cvec, ada_w, ada_b.reshape(DEPTH, 1, n3))


_NORM_CHUNK = 32
_NORM_AHEAD_STEPS = 8


def _inproj_kernel(x_ref, nw_ref, shift_ref, scale_ref, w_ref, o_ref, hx0_ref, hx1_ref):
    i, j = pl.program_id(0), pl.program_id(1)
    tm = hx0_ref.shape[0]
    gain = nw_ref[...] * (1.0 + scale_ref[0])
    add = shift_ref[0]

    def normalise(hx_ref, row0):
        rows = pl.ds(pl.multiple_of(row0, _NORM_CHUNK), _NORM_CHUNK)
        x = x_ref[rows, :]
        inv = lax.rsqrt(jnp.mean(x * x, axis=-1, keepdims=True) + NORM_EPS)
        hx_ref[rows, :] = (x * inv * gain + add).astype(BF16)

    @pl.when(jnp.logical_and(i == 0, j == 0))
    def _():
        def body(r, carry):
            normalise(hx0_ref, r * _NORM_CHUNK)
            return carry
        lax.fori_loop(0, tm // _NORM_CHUNK, body, 0)

    def step(hx_cur, hx_next):
        ahead_rows = tm // _NORM_AHEAD_STEPS
        piece = jnp.clip(j - 1, 0, _NORM_AHEAD_STEPS - 1)
        for k in range(ahead_rows // _NORM_CHUNK):
            normalise(hx_next, piece * ahead_rows + k * _NORM_CHUNK)
        o_ref[...] = jnp.dot(hx_cur[...], w_ref[...], preferred_element_type=F32).astype(o_ref.dtype)

    pl.when(i % 2 == 0)(functools.partial(step, hx0_ref, hx1_ref))
    pl.when(i % 2 == 1)(functools.partial(step, hx1_ref, hx0_ref))


class ProjCols:
    def __init__(self, groups):
        self.start, pos = {}, 0
        for name, _, width in groups:
            self.start[name] = pos
            pos += width
        self.width = pos * HB
        self.src_blocks = [src + k for _, src, width in groups for k in range(width)]

    def __getitem__(self, name):
        return self.start[name]


_ALL_GROUPS = (("qa", COL_QA, A_HEADS), ("ka", COL_KA, A_HEADS), ("va", COL_VA, A_HEADS),
               ("ga", COL_GA, A_HEADS), ("qb", COL_QB, B_HEADS), ("kb", COL_KB, B_HEADS),
               ("vb", COL_VB, B_HEADS), ("gb", COL_GB, B_HEADS), ("xc", COL_XC, C_WIDTH // HB),
               ("gc", COL_GC, C_WIDTH // HB))
FULL_COLS = ProjCols(_ALL_GROUPS)
KV_COLS = ProjCols(tuple(g for g in _ALL_GROUPS if g[0] in ("ka", "va", "kb", "vb", "xc")))


def input_projection(x2d, norm_w3, mod3, w_bf16, layer, rows_per_mod, mod_row0, cols=FULL_COLS):
    m = x2d.shape[0]
    tm = 512
    tn = 1024 if cols is FULL_COLS else 512
    per_tile = tn // HB
    n_tiles = cols.width // tn
    src_tiles = [cols.src_blocks[t * per_tile] // per_tile for t in range(n_tiles)]
    assert all(cols.src_blocks[t * per_tile + k] == src_tiles[t] * per_tile + k
               for t in range(n_tiles) for k in range(per_tile))

    def src_tile(j):
        t = j + src_tiles[0]
        for k in range(1, n_tiles):
            jump = (src_tiles[k] - k) - (src_tiles[k - 1] - (k - 1))
            if jump:
                t = t + jnp.where(j >= k, jump, 0)
        return t

    n_rows = m // tm
    assert n_tiles > _NORM_AHEAD_STEPS and tm % (_NORM_AHEAD_STEPS * _NORM_CHUNK) == 0

    def x_tile(i, j):
        return jnp.minimum(i + jnp.where(j > 0, 1, 0), n_rows - 1)

    def mod_row(i, j):
        return mod_row0 + (x_tile(i, j) * tm) // rows_per_mod

    return pl.pallas_call(
        _inproj_kernel,
        out_shape=jax.ShapeDtypeStruct((m, cols.width), BF16),
        grid=(n_rows, n_tiles),
        in_specs=[
            pl.BlockSpec((tm, D_MODEL), lambda i, j: (x_tile(i, j), 0)),
            pl.BlockSpec((None, 1, D_MODEL), lambda i, j: (layer, 0, 0)),
            pl.BlockSpec((1, 1, D_MODEL), lambda i, j: (mod_row(i, j), 0, 0)),
            pl.BlockSpec((1, 1, D_MODEL), lambda i, j: (mod_row(i, j), 0, 1)),
            pl.BlockSpec((None, D_MODEL, tn), lambda i, j: (layer, 0, src_tile(j))),
        ],
        out_specs=pl.BlockSpec((tm, tn), lambda i, j: (i, j)),
        scratch_shapes=[pltpu.VMEM((tm, D_MODEL), BF16), pltpu.VMEM((tm, D_MODEL), BF16)],
        compiler_params=_params(("arbitrary", "arbitrary")),
        name="input_projection",
    )(x2d, norm_w3, mod3, mod3, w_bf16)


def _rope(x, cos, sin_lo, sin_hi):
    return (x * cos + pltpu.roll(x, HEAD_DIM - 16, axis=1) * sin_lo
            + pltpu.roll(x, 16, axis=1) * sin_hi)


def _diff_lambda(lq_ref, lam_init):
    lq = lq_ref[...]
    a = jnp.sum(lq[0:1] * lq[1:2], axis=-1, keepdims=True)
    b = jnp.sum(lq[2:3] * lq[3:4], axis=-1, keepdims=True)
    return jnp.exp(a) - jnp.exp(b) + lam_init


LOG2E = 1.4426950408889634
NKEYS = CTX_LEN + SEQ
DA_TQ = 256
DA_W = 2 * DA_TQ
DA_ROWS = 16
DA_KCHUNK = 256


def _transpose_bf16(x):
    return x.astype(F32).T.astype(BF16)


def _diffattn_lat_kernel(one_ref, q_ref, kc_ref, vc_ref, kl_ref, vl_ref, g_ref, cos_ref, slo_ref, shi_ref,
                         lq_ref, sw_ref, o_ref, k_scr, vt_scr, qt_scr, s0_scr, s1_scr, m_scr, p_scr,
                         *, lam_init):
    n_tiles = SEQ // DA_TQ
    n_chunks = NKEYS // DA_KCHUNK
    groups_per_chunk = DA_KCHUNK // DA_ROWS

    def score_chunk(s_scr, qt2, c, m_run):
        rows = slice(c * DA_KCHUNK, (c + 1) * DA_KCHUNK)
        s = jnp.dot(k_scr[rows, :], qt2, preferred_element_type=F32)
        s_scr[rows, :] = s
        parts = [s[i * DA_ROWS:(i + 1) * DA_ROWS, :] for i in range(groups_per_chunk)]
        if m_run is not None:
            parts.append(m_run)
        while len(parts) > 1:
            parts = [jnp.maximum(parts[i], parts[i + 1]) for i in range(0, len(parts) - 1, 2)] + (
                [parts[-1]] if len(parts) % 2 else [])
        return parts[0]

    def colmax(m_run):
        return jnp.broadcast_to(jnp.max(m_run, axis=0, keepdims=True), (DA_ROWS, DA_W))

    def always(k):
        return one_ref[0] > -k

    @pl.when(always(0))
    def _():
        assert DA_KCHUNK == DA_TQ == CTX_LEN
        dim = lax.broadcasted_iota(jnp.int32, (HB, DA_TQ), 0)

        def stage_queries(c):
            rows = slice(c * DA_TQ, (c + 1) * DA_TQ)
            q = _rope(q_ref[rows, :].astype(F32), cos_ref[rows, :], slo_ref[rows, :], shi_ref[rows, :])
            qt = (q * (A_QK_DIM ** -0.5 * LOG2E)).T
            qt_scr[:, c * DA_W:c * DA_W + DA_TQ] = jnp.where(dim < A_QK_DIM, qt, 0.0).astype(BF16)
            qt_scr[:, c * DA_W + DA_TQ:(c + 1) * DA_W] = jnp.where(dim >= A_QK_DIM, qt, 0.0).astype(BF16)

        stage_queries(0)
        qt0 = qt_scr[:, 0:DA_W]
        k_scr[0:CTX_LEN, :] = kc_ref[...]
        vt_scr[:, 0:CTX_LEN] = _transpose_bf16(vc_ref[...])
        m_run = score_chunk(s0_scr, qt0, 0, None)
        for c in range(SEQ // DA_TQ):
            rows = slice(c * DA_TQ, (c + 1) * DA_TQ)
            dst = slice(CTX_LEN + c * DA_TQ, CTX_LEN + (c + 1) * DA_TQ)
            k_scr[dst, :] = _rope(kl_ref[rows, :].astype(F32), cos_ref[rows, :], slo_ref[rows, :],
                                  shi_ref[rows, :]).astype(BF16)
            m_run = score_chunk(s0_scr, qt0, c + 1, m_run)
            vt_scr[:, dst] = _transpose_bf16(vl_ref[rows, :])
            if c > 0:
                stage_queries(c)
        m_scr[...] = colmax(m_run)

    def tile(t, s_scr, s_next_scr):
        rows = slice(t * DA_TQ, (t + 1) * DA_TQ)
        mb = m_scr[...]
        if s_next_scr is not None:
            qt_next = qt_scr[:, (t + 1) * DA_W:(t + 2) * DA_W]
        m_run = None
        l_acc = [None, None]
        acc = None

        def value_chunk(c, acc):
            rows = slice(c * DA_KCHUNK, (c + 1) * DA_KCHUNK)
            part = jnp.dot(vt_scr[:, rows], p_scr[rows, :], preferred_element_type=F32)
            return part if acc is None else acc + part

        for c in range(n_chunks):
            if s_next_scr is not None:
                m_run = score_chunk(s_next_scr, qt_next, c, m_run)
            for r in range(c * groups_per_chunk, (c + 1) * groups_per_chunk):
                grp = slice(r * DA_ROWS, (r + 1) * DA_ROWS)
                p = jnp.exp2(s_scr[grp, :] - mb)
                p_scr[grp, :] = p.astype(BF16)
                l_acc[r % 2] = p if l_acc[r % 2] is None else l_acc[r % 2] + p
            acc = value_chunk(c, acc)
        l = jnp.sum(l_acc[0] + l_acc[1], axis=0, keepdims=True)

        lam = _diff_lambda(lq_ref, lam_init)
        ot = acc[:, :DA_TQ] * (1.0 / l[:, :DA_TQ]) - acc[:, DA_TQ:] * (lam / l[:, DA_TQ:])
        ot = ot * lax.rsqrt(jnp.mean(ot * ot, axis=0, keepdims=True) + SUBLN_EPS)
        o = ot.T * sw_ref[...] * (1.0 - lam_init)
        o_ref[rows, :] = (o * _silu(g_ref[rows, :].astype(F32))).astype(o_ref.dtype)
        if s_next_scr is not None:
            m_scr[...] = colmax(m_run)

    bufs = (s0_scr, s1_scr)
    for t in range(n_tiles):
        nxt = bufs[(t + 1) % 2] if t + 1 < n_tiles else None
        pl.when(always(t + 1))(functools.partial(tile, t, bufs[t % 2], nxt))


_NT = (((1,), (1,)), ((), ()))


def _diffattn_ctx_kernel(q_ref, k_ref, v_ref, g_ref, lq_ref, sw_ref, o_ref, *, lam_init):
    q = q_ref[...].astype(F32) * (A_QK_DIM ** -0.5)
    lane = lax.broadcasted_iota(jnp.int32, q.shape, 1)
    outs = []
    for qn in (jnp.where(lane < A_QK_DIM, q, 0.0), jnp.where(lane >= A_QK_DIM, q, 0.0)):
        s = lax.dot_general(qn.astype(BF16), k_ref[...], _NT, preferred_element_type=F32)
        p = jnp.exp(s - jnp.max(s, axis=-1, keepdims=True))
        acc = jnp.dot(p.astype(BF16), v_ref[...], preferred_element_type=F32)
        outs.append((acc, jnp.sum(p, axis=-1, keepdims=True)))
    lam = _diff_lambda(lq_ref, lam_init)
    o = outs[0][0] * (1.0 / outs[0][1]) - (lam / outs[1][1]) * outs[1][0]
    o = o * lax.rsqrt(jnp.mean(o * o, axis=-1, keepdims=True) + SUBLN_EPS) * sw_ref[...]
    o = o * (1.0 - lam_init)
    o_ref[...] = (o * _silu(g_ref[...].astype(F32))).astype(o_ref.dtype)


def diff_attention_latent(proj_l, proj_c, ccols, rope_tabs, lambda_qk, subln_w, lam_init):
    head = lambda col: (lambda b, h, one: (b, col + h))
    const = lambda b, h, one: (0, 0)
    grid_spec = pltpu.PrefetchScalarGridSpec(
        num_scalar_prefetch=1,
        grid=(BATCH, A_HEADS),
        in_specs=[
            pl.BlockSpec((SEQ, HB), head(COL_QA)),
            pl.BlockSpec((CTX_LEN, HB), head(ccols["ka"])),
            pl.BlockSpec((CTX_LEN, HB), head(ccols["va"])),
            pl.BlockSpec((SEQ, HB), head(COL_KA)),
            pl.BlockSpec((SEQ, HB), head(COL_VA)),
            pl.BlockSpec((SEQ, HB), head(COL_GA)),
            pl.BlockSpec((SEQ, HB), const),
            pl.BlockSpec((SEQ, HB), const),
            pl.BlockSpec((SEQ, HB), const),
            pl.BlockSpec((4, A_QK_DIM), const),
            pl.BlockSpec((1, HB), const),
        ],
        out_specs=pl.BlockSpec((SEQ, HB), lambda b, h, one: (b, h)),
        scratch_shapes=[pltpu.VMEM((NKEYS, HB), BF16), pltpu.VMEM((HB, NKEYS), BF16),
                        pltpu.VMEM((HB, 2 * SEQ), BF16),
                        pltpu.VMEM((NKEYS, DA_W), F32), pltpu.VMEM((NKEYS, DA_W), F32),
                        pltpu.VMEM((DA_ROWS, DA_W), F32), pltpu.VMEM((NKEYS, DA_W), BF16)],
    )
    return pl.pallas_call(
        functools.partial(_diffattn_lat_kernel, lam_init=lam_init),
        out_shape=jax.ShapeDtypeStruct((BATCH * SEQ, A_WIDTH), BF16),
        grid_spec=grid_spec,
        compiler_params=_params(("arbitrary", "arbitrary")),
        name="diff_attention_latent",
    )(jnp.ones((1,), jnp.int32), proj_l, proj_c, proj_c, proj_l, proj_l, proj_l, *rope_tabs,
      lambda_qk, subln_w.reshape(1, HB))


def diff_attention_context(proj_c, lambda_qk, subln_w, lam_init):
    head = lambda col: (lambda b, h: (b, col + h))
    const = lambda b, h: (0, 0)
    return pl.pallas_call(
        functools.partial(_diffattn_ctx_kernel, lam_init=lam_init),
        out_shape=jax.ShapeDtypeStruct((BATCH * CTX_LEN, A_WIDTH), BF16),
        grid=(BATCH, A_HEADS),
        in_specs=[pl.BlockSpec((CTX_LEN, HB), head(COL_QA)),
                  pl.BlockSpec((CTX_LEN, HB), head(COL_KA)),
                  pl.BlockSpec((CTX_LEN, HB), head(COL_VA)),
                  pl.BlockSpec((CTX_LEN, HB), head(COL_GA)),
                  pl.BlockSpec((4, A_QK_DIM), const),
                  pl.BlockSpec((1, HB), const)],
        out_specs=pl.BlockSpec((CTX_LEN, HB), lambda b, h: (b, h)),
        compiler_params=_params(("arbitrary", "arbitrary")),
        name="diff_attention_context",
    )(proj_c, proj_c, proj_c, proj_c, lambda_qk, subln_w.reshape(1, HB))


def _nbr_slab_start(g):
    return min(max(g * NB_QROWS - NA_KH // 2, 0), GRID_H - NB_KROWS)


def _nbr_pattern(g):
    return 0 if g == 0 else (2 if g == NB_GROUPS - 1 else 1)


def _nbr_kernel(q_ref, kc_ref, vc_ref, kl_ref, vl_ref, g_ref, bias_ref, o_ref, vt_scr):
    vt_scr[:, 0:CTX_LEN] = _transpose_bf16(vc_ref[...])
    for c in range(SEQ // NB_Q):
        vt_scr[:, CTX_LEN + c * NB_Q:CTX_LEN + (c + 1) * NB_Q] = _transpose_bf16(vl_ref[c * NB_Q:(c + 1) * NB_Q, :])
    def scores(g):
        rows = slice(g * NB_Q, (g + 1) * NB_Q)
        k0 = _nbr_slab_start(g) * GRID_W
        qt = (q_ref[rows, :].astype(F32) * (HEAD_DIM ** -0.5 * LOG2E)).T.astype(BF16)
        s_c = jnp.dot(kc_ref[...], qt, preferred_element_type=F32)
        s_n = (jnp.dot(kl_ref[k0:k0 + NB_K, :], qt, preferred_element_type=F32)
               + bias_ref[0, _nbr_pattern(g)])
        return s_c, s_n

    def softmax(s_c, s_n):
        m = jnp.maximum(jnp.max(s_c, axis=0, keepdims=True), jnp.max(s_n, axis=0, keepdims=True))
        p_c = jnp.exp2(s_c - m)
        p_n = jnp.exp2(s_n - m)
        l = jnp.sum(p_c, axis=0, keepdims=True) + jnp.sum(p_n, axis=0, keepdims=True)
        return p_c.astype(BF16), p_n.astype(BF16), l

    def finish(g, p_c, p_n, l):
        rows = slice(g * NB_Q, (g + 1) * NB_Q)
        k0 = _nbr_slab_start(g) * GRID_W
        acc = (jnp.dot(vt_scr[:, 0:CTX_LEN], p_c, preferred_element_type=F32)
               + jnp.dot(vt_scr[:, CTX_LEN + k0:CTX_LEN + k0 + NB_K], p_n,
                         preferred_element_type=F32))
        o = (acc * (1.0 / l)).T
        o_ref[rows, :] = (o * _silu(g_ref[rows, :].astype(F32))).astype(o_ref.dtype)

    ahead = scores(0)
    probs = None
    for g in range(NB_GROUPS):
        s_c, s_n = ahead
        if g + 1 < NB_GROUPS:
            ahead = scores(g + 1)
        new_probs = softmax(s_c, s_n)
        if probs is not None:
            finish(g - 1, *probs)
        probs = new_probs
    finish(NB_GROUPS - 1, *probs)


def neighbourhood_bias_table(rpb):
    w = GRID_W
    cq = np.arange(w)
    c0 = np.clip(cq - NA_KW // 2, 0, w - NA_KW)
    ck = np.arange(w)
    col_valid = (ck[None, :] >= c0[:, None]) & (ck[None, :] < c0[:, None] + NA_KW)
    pad = w - NA_KW
    padded = jnp.pad(rpb.astype(F32), ((0, 0), (0, 0), (pad, pad)))
    toep = jnp.stack([padded[:, :, w - 1 - q:2 * w - 1 - q] for q in range(w)], axis=2)
    toep = jnp.where(col_valid[None, None], toep, NEG_INF)
    toep_t = jnp.swapaxes(toep, 2, 3) * LOG2E
    masked = jnp.full((B_HEADS, w, w), NEG_INF, F32)
    tabs = []
    for g in (0, 1, NB_GROUPS - 1):
        slab0 = _nbr_slab_start(g)
        qr = g * NB_QROWS + np.arange(NB_QROWS)
        kr = slab0 + np.arange(NB_KROWS)
        win0 = np.clip(qr - NA_KH // 2, 0, GRID_H - NA_KH)
        row_valid = (kr[None, :] >= win0[:, None]) & (kr[None, :] < win0[:, None] + NA_KH)
        dr_idx = kr[None, :] - qr[:, None] + NA_KH - 1
        rows = [jnp.concatenate([toep_t[:, int(dr_idx[a, k])] if row_valid[a, k] else masked
                                 for a in range(NB_QROWS)], axis=-1) for k in range(NB_KROWS)]
        tabs.append(jnp.concatenate(rows, axis=1))
    return jnp.stack(tabs, axis=1)


def neighbourhood_attention(proj_l, proj_c, ccols, bias_tab):
    head = lambda col: (lambda h, b: (b, col + h))
    return pl.pallas_call(
        _nbr_kernel,
        out_shape=jax.ShapeDtypeStruct((BATCH * SEQ, B_WIDTH), BF16),
        grid=(B_HEADS, BATCH),
        in_specs=[
            pl.BlockSpec((SEQ, HB), head(COL_QB)),
            pl.BlockSpec((CTX_LEN, HB), head(ccols["kb"])),
            pl.BlockSpec((CTX_LEN, HB), head(ccols["vb"])),
            pl.BlockSpec((SEQ, HB), head(COL_KB)),
            pl.BlockSpec((SEQ, HB), head(COL_VB)),
            pl.BlockSpec((SEQ, HB), head(COL_GB)),
            pl.BlockSpec((1, 3, NB_K, NB_Q), lambda h, b: (h, 0, 0, 0)),
        ],
        out_specs=pl.BlockSpec((SEQ, HB), lambda h, b: (b, h)),
        scratch_shapes=[pltpu.VMEM((HB, NKEYS), BF16)],
        compiler_params=_params(("arbitrary", "arbitrary")),
        name="neighbourhood_attention",
    )(proj_l, proj_c, proj_c, proj_l, proj_l, proj_l, bias_tab)


def _dense_ctx_kernel(q_ref, k_ref, v_ref, g_ref, o_ref):
    scale = HEAD_DIM ** -0.5
    s = lax.dot_general(q_ref[...], k_ref[...], _NT, preferred_element_type=F32) * scale
    p = jnp.exp(s - jnp.max(s, axis=-1, keepdims=True))
    p = p * (1.0 / jnp.sum(p, axis=-1, keepdims=True))
    o = jnp.dot(p.astype(BF16), v_ref[...], preferred_element_type=F32)
    o_ref[...] = (o * _silu(g_ref[...].astype(F32))).astype(o_ref.dtype)


def dense_context_attention(proj_c):
    head = lambda col: (lambda b, h: (b, col + h))
    return pl.pallas_call(
        _dense_ctx_kernel,
        out_shape=jax.ShapeDtypeStruct((BATCH * CTX_LEN, B_WIDTH), BF16),
        grid=(BATCH, B_HEADS),
        in_specs=[pl.BlockSpec((CTX_LEN, HB), head(COL_QB)),
                  pl.BlockSpec((CTX_LEN, HB), head(COL_KB)),
                  pl.BlockSpec((CTX_LEN, HB), head(COL_VB)),
                  pl.BlockSpec((CTX_LEN, HB), head(COL_GB))],
        out_specs=pl.BlockSpec((CTX_LEN, HB), lambda b, h: (b, h)),
        compiler_params=_params(("arbitrary", "arbitrary")),
        name="dense_context_attention",
    )(proj_c, proj_c, proj_c, proj_c)


def _shift_rows(x, d, fill):
    n = x.shape[0]
    if d % SUBLANES == 0:
        pad = jnp.full((abs(d), x.shape[1]), fill, x.dtype)
        return jnp.concatenate([pad, x[:n - d]] if d > 0 else [x[-d:], pad], axis=0)
    rolled = pltpu.roll(x, d % n, axis=0)
    row = lax.broadcasted_iota(jnp.int32, x.shape, 0)
    keep = (row >= d) if d > 0 else (row < n + d)
    return jnp.where(keep, rolled, fill)


def _dwconv(x, w_ref, b_ref):
    acc = b_ref[...] + w_ref[CONV_LEFT:CONV_LEFT + 1, :] * x
    for j in range(CONV_W):
        if j != CONV_LEFT:
            acc = acc + w_ref[j:j + 1, :] * _shift_rows(x, CONV_LEFT - j, 0.0)
    return acc


def _linear_scan(a, b, h0, reverse):
    n = a.shape[0]
    row = lax.broadcasted_iota(jnp.int32, a.shape, 0)
    first = (row == n - 1) if reverse else (row == 0)
    b = b + jnp.where(first, a * h0, 0.0)
    d = 1
    while d < n:
        sd = -d if reverse else d
        b = b + a * _shift_rows(b, sd, 0.0)
        if 2 * d < n:
            a = a * _shift_rows(a, sd, 1.0)
        d *= 2
    return b


def _sigmoid_tanh(x):
    return 0.5 * jnp.tanh(0.5 * x) + 0.5


def _rglru_gates(u, wa_ref, ba_ref, wx_ref, bx_ref, lam_ref, d):
    ub = u.astype(BF16)
    r = _sigmoid_tanh(jnp.dot(ub, wa_ref[0, d], preferred_element_type=F32) + ba_ref[0, d])
    i = _sigmoid_tanh(jnp.dot(ub, wx_ref[0, d], preferred_element_type=F32) + bx_ref[0, d])
    neg_lam = -lam_ref[0, d]
    softplus = jnp.maximum(neg_lam, 0.0) + jnp.log1p(jnp.exp(-jnp.abs(neg_lam)))
    log_a = (-RGLRU_C) * r * softplus
    a = jnp.exp(log_a)
    t = jnp.tanh(log_a)
    b = jnp.sqrt(-2.0 * t / (1.0 - t)) * (i * u)
    return a, b


SCAN_CHUNKS = SUBLANES


def _scan_chunk_len(n):
    return n // SCAN_CHUNKS + 1


def _blocked_scan(a_scr, b_scr, n, h0, reverse, need_states):
    length = _scan_chunk_len(n)
    order = range(length - 1, -1, -1) if reverse else range(length)
    h = jnp.zeros((SCAN_CHUNKS, a_scr.shape[1]), F32)
    p = jnp.ones_like(h)
    hs, ps = [None] * length, [None] * length
    for i in order:
        va = a_scr[pl.ds(i, SCAN_CHUNKS, stride=length), :]
        vb = b_scr[pl.ds(i, SCAN_CHUNKS, stride=length), :]
        h = va * h + vb
        p = va * p
        hs[i], ps[i] = h, p
    rows = [None] * SCAN_CHUNKS
    carry = h0
    for j in (range(SCAN_CHUNKS - 1, -1, -1) if reverse else range(SCAN_CHUNKS)):
        rows[j] = carry
        carry = h[j:j + 1, :] + p[j:j + 1, :] * carry
    if not need_states:
        return None, carry
    start = jnp.concatenate(rows, axis=0)
    return [hs[i] + ps[i] * start for i in range(length)], carry


def _rglru_kernel(*refs, need_ctx):
    if need_ctx:
        (xl_ref, xc_ref, gl_ref, gc_ref, cw_ref, cb_ref, wa_ref, ba_ref, wx_ref, bx_ref, lam_ref,
         ol_ref, oc_ref, al_scr, bl_scr, ac_scr, bc_scr, yl_scr, yc_scr) = refs
    else:
        (xl_ref, xc_ref, gl_ref, cw_ref, cb_ref, wa_ref, ba_ref, wx_ref, bx_ref, lam_ref,
         ol_ref, al_scr, bl_scr, ac_scr, bc_scr, yl_scr, yc_scr) = refs
    ct = al_scr.shape[1]
    u_l = _dwconv(xl_ref[...].astype(F32), cw_ref, cb_ref)
    u_c = _dwconv(xc_ref[...].astype(F32), cw_ref, cb_ref)
    for a_scr, b_scr, n in ((al_scr, bl_scr, SEQ), (ac_scr, bc_scr, CTX_LEN)):
        a_scr[n:n + SUBLANES, :] = jnp.ones((SUBLANES, ct), F32)
        b_scr[n:n + SUBLANES, :] = jnp.zeros((SUBLANES, ct), F32)
    y_l = None
    y_c = None
    for d in range(2):
        rev = d == 1
        a_c, b_c = _rglru_gates(u_c, wa_ref, ba_ref, wx_ref, bx_ref, lam_ref, d)
        ac_scr[0:CTX_LEN, :] = a_c
        bc_scr[0:CTX_LEN, :] = b_c
        h_c, fin = _blocked_scan(ac_scr, bc_scr, CTX_LEN, jnp.zeros((1, ct), F32), rev, need_ctx)
        a_l, b_l = _rglru_gates(u_l, wa_ref, ba_ref, wx_ref, bx_ref, lam_ref, d)
        al_scr[0:SEQ, :] = a_l
        bl_scr[0:SEQ, :] = b_l
        h_l, _ = _blocked_scan(al_scr, bl_scr, SEQ, fin, rev, True)
        y_l = h_l if y_l is None else [f + r for f, r in zip(y_l, h_l)]
        if need_ctx:
            y_c = h_c if y_c is None else [f + r for f, r in zip(y_c, h_c)]
    length = _scan_chunk_len(SEQ)
    for i in range(length):
        yl_scr[pl.ds(i, SCAN_CHUNKS, stride=length), :] = y_l[i]
    ol_ref[...] = (yl_scr[0:SEQ, :] * _silu(gl_ref[...].astype(F32))).astype(ol_ref.dtype)
    if need_ctx:
        length = _scan_chunk_len(CTX_LEN)
        for i in range(length):
            yc_scr[pl.ds(i, SCAN_CHUNKS, stride=length), :] = y_c[i]
        oc_ref[...] = (yc_scr[0:CTX_LEN, :] * _silu(gc_ref[...].astype(F32))).astype(oc_ref.dtype)


def _block_diag(w, per_tile):
    n = w.shape[-1]
    tiles = C_BLOCKS // per_tile
    w = w.reshape(2, tiles, per_tile, n, n)
    eye = jnp.eye(per_tile, dtype=w.dtype)
    bd = jnp.einsum('dtpij,pq->dtpiqj', w, eye).reshape(2, tiles, per_tile * n, per_tile * n)
    return jnp.transpose(bd, (1, 0, 2, 3)).astype(BF16)


def rglru_mixer(proj_l, proj_c, ccols, conv_w, conv_b, rg_wa, rg_ba, rg_wx, rg_bx, rg_lambda, need_ctx):
    ct = 128
    nt = C_WIDTH // ct
    per_tile = ct // C_BLOCK_DIM
    vec = lambda p: jnp.transpose(p.reshape(2, nt, 1, ct), (1, 0, 2, 3))
    wa = _block_diag(rg_wa, per_tile)
    wx = _block_diag(rg_wx, per_tile)
    xcol = lambda col: (lambda b, t: (b, col + t))
    par = lambda b, t: (t, 0, 0, 0)
    out_shape = [jax.ShapeDtypeStruct((BATCH * SEQ, C_WIDTH), BF16)]
    out_specs = [pl.BlockSpec((SEQ, ct), lambda b, t: (b, t))]
    if need_ctx:
        out_shape.append(jax.ShapeDtypeStruct((BATCH * CTX_LEN, C_WIDTH), BF16))
        out_specs.append(pl.BlockSpec((CTX_LEN, ct), lambda b, t: (b, t)))
    in_specs = [pl.BlockSpec((SEQ, ct), xcol(COL_XC)),
                pl.BlockSpec((CTX_LEN, ct), xcol(ccols["xc"])),
                pl.BlockSpec((SEQ, ct), xcol(COL_GC))]
    args = [proj_l, proj_c, proj_l]
    if need_ctx:
        in_specs.append(pl.BlockSpec((CTX_LEN, ct), xcol(ccols["gc"])))
        args.append(proj_c)
    in_specs += [pl.BlockSpec((CONV_W, ct), lambda b, t: (0, t)),
                 pl.BlockSpec((1, ct), lambda b, t: (0, t)),
                 pl.BlockSpec((1, 2, ct, ct), par),
                 pl.BlockSpec((1, 2, 1, ct), par),
                 pl.BlockSpec((1, 2, ct, ct), par),
                 pl.BlockSpec((1, 2, 1, ct), par),
                 pl.BlockSpec((1, 2, 1, ct), par)]
    args += [conv_w, conv_b.reshape(1, C_WIDTH), wa, vec(rg_ba), wx, vec(rg_bx), vec(rg_lambda)]
    outs = pl.pallas_call(
        functools.partial(_rglru_kernel, need_ctx=need_ctx),
        out_shape=out_shape,
        grid=(BATCH, nt),
        in_specs=in_specs,
        out_specs=out_specs,
        scratch_shapes=([pltpu.VMEM((SEQ + SUBLANES, ct), F32)] * 2
                        + [pltpu.VMEM((CTX_LEN + SUBLANES, ct), F32)] * 2
                        + [pltpu.VMEM((SEQ + SUBLANES, ct), F32), pltpu.VMEM((CTX_LEN + SUBLANES, ct), F32)]),
        compiler_params=_params(("arbitrary", "arbitrary")),
        name="rglru_mixer",
    )(*args)
    return outs if need_ctx else (outs[0], None)


def _outproj_kernel(x_ref, gate_ref, ma_ref, mb_ref, mc_ref, wa_ref, wb_ref, wc_ref, o_ref):
    y = jnp.dot(ma_ref[...], wa_ref[...], preferred_element_type=F32)
    y = y + jnp.dot(mb_ref[...], wb_ref[...], preferred_element_type=F32)
    y = y + jnp.dot(mc_ref[...], wc_ref[...], preferred_element_type=F32)
    o_ref[...] = x_ref[...] + gate_ref[0] * y


def output_projection(x2d, mod3, ma, mb, mc, w_bf16, layer, rows_per_mod, mod_row0):
    m = x2d.shape[0]
    tm, tn = 1024, 1024

    def mod_row(i):
        return mod_row0 + (i * tm) // rows_per_mod

    return pl.pallas_call(
        _outproj_kernel,
        out_shape=jax.ShapeDtypeStruct((m, D_MODEL), F32),
        grid=(m // tm, D_MODEL // tn),
        in_specs=[
            pl.BlockSpec((tm, tn), lambda i, j: (i, j)),
            pl.BlockSpec((1, 1, tn), lambda i, j: (mod_row(i), 0, 2 * (D_MODEL // tn) + j)),
            pl.BlockSpec((tm, A_WIDTH), lambda i, j: (i, 0)),
            pl.BlockSpec((tm, B_WIDTH), lambda i, j: (i, 0)),
            pl.BlockSpec((tm, C_WIDTH), lambda i, j: (i, 0)),
            pl.BlockSpec((None, A_WIDTH, tn), lambda i, j: (layer, 0, j)),
            pl.BlockSpec((None, B_WIDTH, tn), lambda i, j: (layer, 1, j)),
            pl.BlockSpec((None, C_WIDTH, tn), lambda i, j: (layer, (A_WIDTH + B_WIDTH) // C_WIDTH, j)),
        ],
        out_specs=pl.BlockSpec((tm, tn), lambda i, j: (i, j)),
        compiler_params=_params(("arbitrary", "arbitrary")),
        name="output_projection",
    )(x2d, mod3, ma, mb, mc, w_bf16, w_bf16, w_bf16)


def _final_norm_kernel(x_ref, w_ref, o_ref):
    x = x_ref[...]
    o_ref[...] = x * lax.rsqrt(jnp.mean(x * x, axis=-1, keepdims=True) + NORM_EPS) * w_ref[...]


def final_norm(x2d, w):
    tm = 256
    return pl.pallas_call(
        _final_norm_kernel,
        out_shape=jax.ShapeDtypeStruct(x2d.shape, F32),
        grid=(x2d.shape[0] // tm,),
        in_specs=[pl.BlockSpec((tm, D_MODEL), lambda i: (i, 0)),
                  pl.BlockSpec((1, D_MODEL), lambda i: (0, 0))],
        out_specs=pl.BlockSpec((tm, D_MODEL), lambda i: (i, 0)),
        compiler_params=_params(("arbitrary",)),
        name="final_norm",
    )(x2d, w.reshape(1, D_MODEL))


def _rope_tables():
    t = np.arange(SEQ)
    pos = np.stack([t // GRID_W, t % GRID_W], axis=0).astype(np.float32)
    lane = np.arange(HEAD_DIM)
    within = lane % A_QK_DIM
    axis = within // (A_QK_DIM // 2)
    k = within % (A_QK_DIM // 2)
    nfreq = A_QK_DIM // 4
    upper = k >= nfreq
    freqs = jnp.asarray(ROPE_THETA, F32) ** (-jnp.arange(nfreq, dtype=F32) / nfreq)
    ang = jnp.asarray(pos)[axis, :].T * freqs[k % nfreq][None, :]
    cos, sin = jnp.cos(ang), jnp.sin(ang)
    sin_lo = jnp.where(upper[None, :], 0.0, -sin)
    sin_hi = jnp.where(upper[None, :], sin, 0.0)
    return cos, sin_lo, sin_hi


def kernel(x, c, ctx, c_ctx, ada_w, ada_b, norm_w, w_in, w_out, lambda_qk, subln_w, rpb,
           conv_w, conv_b, rg_wa, rg_ba, rg_wx, rg_bx, rg_lambda, final_norm_w):
    cvec = jnp.zeros((MOD_ROWS, D_MODEL), F32).at[:BATCH].set(c).at[CTX_MOD_ROW].set(c_ctx)
    mod = adaln_modulation(cvec, ada_w, ada_b)
    mod3 = mod.reshape(DEPTH * MOD_ROWS, 1, 3 * D_MODEL)
    norm_w3 = norm_w.reshape(DEPTH, 1, D_MODEL)
    w_in_bf = w_in.astype(BF16)
    w_out_bf = w_out.astype(BF16)
    rope_tabs = _rope_tables()
    xl = x.reshape(BATCH * SEQ, D_MODEL)
    xc = ctx.reshape(BATCH * CTX_LEN, D_MODEL)
    for l in range(DEPTH):
        need_ctx = l < DEPTH - 1
        lam_init = 0.8 - 0.6 * math.exp(-0.3 * l)
        row_l, row_c = l * MOD_ROWS, l * MOD_ROWS + CTX_MOD_ROW
        proj_l = input_projection(xl, norm_w3, mod3, w_in_bf, l, SEQ, row_l)
        ccols = FULL_COLS if need_ctx else KV_COLS
        proj_c = input_projection(xc, norm_w3, mod3, w_in_bf, l, BATCH * CTX_LEN, row_c, ccols)

        ma = diff_attention_latent(proj_l, proj_c, ccols, rope_tabs, lambda_qk[l], subln_w[l], lam_init)
        mb = neighbourhood_attention(proj_l, proj_c, ccols, neighbourhood_bias_table(rpb[l]))
        mc, mc_c = rglru_mixer(proj_l, proj_c, ccols, conv_w[l], conv_b[l], rg_wa[l], rg_ba[l],
                               rg_wx[l], rg_bx[l], rg_lambda[l], need_ctx)
        xl = output_projection(xl, mod3, ma, mb, mc, w_out_bf, l, SEQ, row_l)
        if need_ctx:
            ma_c = diff_attention_context(proj_c, lambda_qk[l], subln_w[l], lam_init)
            mb_c = dense_context_attention(proj_c)
            xc = output_projection(xc, mod3, ma_c, mb_c, mc_c, w_out_bf, l, BATCH * CTX_LEN, row_c)
    return final_norm(xl, final_norm_w).reshape(BATCH, SEQ, D_MODEL)
```

```python
import functools
import math

import numpy as np
import jax
import jax.numpy as jnp
from jax import lax
from jax.experimental import pallas as pl
from jax.experimental.pallas import tpu as pltpu

F32 = jnp.float32
BF16 = jnp.bfloat16

D_MODEL = 4096
BATCH = 8
SEQ = 2048
DEPTH = 2
GRID_W = 64
GRID_H = SEQ // GRID_W
CTX_LEN = 256
HEAD_DIM = 128
A_WIDTH = (3 * D_MODEL) // 8
A_HEADS = A_WIDTH // HEAD_DIM
A_QK_DIM = HEAD_DIM // 2
B_WIDTH = (3 * D_MODEL) // 8
B_HEADS = B_WIDTH // HEAD_DIM
C_WIDTH = D_MODEL - A_WIDTH - B_WIDTH
C_BLOCKS = 16
C_BLOCK_DIM = C_WIDTH // C_BLOCKS
MIX_WIDTH = A_WIDTH + B_WIDTH + C_WIDTH
IN_WIDTH = 4 * A_WIDTH + 4 * B_WIDTH + 2 * C_WIDTH
NA_KH = 8
NA_KW = 16
ROPE_THETA = 10000.0
RGLRU_C = 8.0
CONV_W = 4
CONV_LEFT = 2
NORM_EPS = 1e-6
SUBLN_EPS = 1e-5
NEG_INF = -1e30

HB = HEAD_DIM
COL_QA, COL_KA, COL_VA, COL_GA = 0, A_HEADS, 2 * A_HEADS, 3 * A_HEADS
COL_QB = 4 * A_HEADS
COL_KB, COL_VB, COL_GB = COL_QB + B_HEADS, COL_QB + 2 * B_HEADS, COL_QB + 3 * B_HEADS
COL_XC = COL_QB + 4 * B_HEADS
COL_GC = COL_XC + C_WIDTH // HB

MOD_ROWS = 16
CTX_MOD_ROW = BATCH

VMEM_LIMIT = 56 * 1024 * 1024
SUBLANES = 8

NB_QROWS = 4
NB_KROWS = 12
NB_GROUPS = GRID_H // NB_QROWS
NB_Q = NB_QROWS * GRID_W
NB_K = NB_KROWS * GRID_W


def _params(sem):
    return pltpu.CompilerParams(dimension_semantics=sem, vmem_limit_bytes=VMEM_LIMIT)


def _sigmoid(x):
    return 1.0 / (1.0 + jnp.exp(-x))


def _silu(x):
    return x * _sigmoid(x)


def _adaln_kernel(c_ref, w_ref, b_ref, o_ref):
    s = _silu(c_ref[...]).astype(BF16)
    w = w_ref[0].astype(BF16)
    o_ref[0] = jnp.dot(s, w, preferred_element_type=F32) + b_ref[0]


def adaln_modulation(cvec, ada_w, ada_b):
    tn = 512
    n3 = 3 * D_MODEL
    return pl.pallas_call(
        _adaln_kernel,
        out_shape=jax.ShapeDtypeStruct((DEPTH, MOD_ROWS, n3), F32),
        grid=(DEPTH, n3 // tn),
        in_specs=[
            pl.BlockSpec((MOD_ROWS, D_MODEL), lambda l, j: (0, 0)),
            pl.BlockSpec((1, D_MODEL, tn), lambda l, j: (l, 0, j)),
            pl.BlockSpec((1, 1, tn), lambda l, j: (l, 0, j)),
        ],
        out_specs=pl.BlockSpec((1, MOD_ROWS, tn), lambda l, j: (l, 0, j)),
        compiler_params=_params(("arbitrary", "arbitrary")),
        name="adaln_modulation",
    )(cvec, ada_w, ada_b.reshape(DEPTH, 1, n3))


_NORM_CHUNK = 32
_NORM_SLICES = 4


def _inproj_kernel(x_ref, nw_ref, shift_ref, scale_ref, w_ref, o_ref, hx0_ref, hx1_ref):
    r, j = pl.program_id(0), pl.program_id(1)
    xs = x_ref.shape[0]
    gain = nw_ref[...] * (1.0 + scale_ref[0])
    add = shift_ref[0]

    def normalise_slice(hx_ref):
        base = jnp.minimum(j, _NORM_SLICES - 1) * xs
        for k in range(xs // _NORM_CHUNK):
            x = x_ref[k * _NORM_CHUNK:(k + 1) * _NORM_CHUNK, :]
            inv = lax.rsqrt(jnp.mean(x * x, axis=-1, keepdims=True) + NORM_EPS)
            rows = pl.ds(pl.multiple_of(base + k * _NORM_CHUNK, _NORM_CHUNK), _NORM_CHUNK)
            hx_ref[rows, :] = (x * inv * gain + add).astype(BF16)

    def step(hx_cur, hx_next):
        normalise_slice(hx_next)
        o_ref[...] = jnp.dot(hx_cur[...], w_ref[...], preferred_element_type=F32).astype(o_ref.dtype)

    @pl.when(r == 0)
    def _():
        normalise_slice(hx0_ref)
        o_ref[...] = jnp.zeros_like(o_ref)

    pl.when(r % 2 == 1)(functools.partial(step, hx0_ref, hx1_ref))
    pl.when(jnp.logical_and(r % 2 == 0, r > 0))(functools.partial(step, hx1_ref, hx0_ref))


class ProjCols:
    def __init__(self, groups):
        self.start, pos = {}, 0
        for name, _, width in groups:
            self.start[name] = pos
            pos += width
        self.width = pos * HB
        self.src_blocks = [src + k for _, src, width in groups for k in range(width)]

    def __getitem__(self, name):
        return self.start[name]


_ALL_GROUPS = (("qa", COL_QA, A_HEADS), ("ka", COL_KA, A_HEADS), ("va", COL_VA, A_HEADS),
               ("ga", COL_GA, A_HEADS), ("qb", COL_QB, B_HEADS), ("kb", COL_KB, B_HEADS),
               ("vb", COL_VB, B_HEADS), ("gb", COL_GB, B_HEADS), ("xc", COL_XC, C_WIDTH // HB),
               ("gc", COL_GC, C_WIDTH // HB))
FULL_COLS = ProjCols(_ALL_GROUPS)
KV_COLS = ProjCols(tuple(g for g in _ALL_GROUPS if g[0] in ("ka", "va", "kb", "vb", "xc")))


def input_projection(x2d, norm_w3, mod3, w_bf16, layer, rows_per_mod, mod_row0, cols=FULL_COLS):
    m = x2d.shape[0]
    tm = 512
    tn = 2048 if cols is FULL_COLS else 512
    per_tile = tn // HB
    n_tiles = cols.width // tn
    src_tiles = [cols.src_blocks[t * per_tile] // per_tile for t in range(n_tiles)]
    assert all(cols.src_blocks[t * per_tile + k] == src_tiles[t] * per_tile + k
               for t in range(n_tiles) for k in range(per_tile))

    def src_tile(j):
        t = j + src_tiles[0]
        for k in range(1, n_tiles):
            jump = (src_tiles[k] - k) - (src_tiles[k - 1] - (k - 1))
            if jump:
                t = t + jnp.where(j >= k, jump, 0)
        return t

    n_rows = m // tm
    xs = tm // _NORM_SLICES
    assert n_tiles >= _NORM_SLICES and xs % _NORM_CHUNK == 0

    def norm_tile(r):
        return jnp.minimum(r, n_rows - 1)

    def mod_row(r):
        return mod_row0 + (norm_tile(r) * tm) // rows_per_mod

    return pl.pallas_call(
        _inproj_kernel,
        out_shape=jax.ShapeDtypeStruct((m, cols.width), BF16),
        grid=(n_rows + 1, n_tiles),
        in_specs=[
            pl.BlockSpec((xs, D_MODEL),
                         lambda r, j: (norm_tile(r) * _NORM_SLICES + jnp.minimum(j, _NORM_SLICES - 1), 0)),
            pl.BlockSpec((None, 1, D_MODEL), lambda r, j: (layer, 0, 0)),
            pl.BlockSpec((1, 1, D_MODEL), lambda r, j: (mod_row(r), 0, 0)),
            pl.BlockSpec((1, 1, D_MODEL), lambda r, j: (mod_row(r), 0, 1)),
            pl.BlockSpec((None, D_MODEL, tn), lambda r, j: (layer, 0, src_tile(j))),
        ],
        out_specs=pl.BlockSpec((tm, tn), lambda r, j: (jnp.maximum(r - 1, 0), j)),
        scratch_shapes=[pltpu.VMEM((tm, D_MODEL), BF16), pltpu.VMEM((tm, D_MODEL), BF16)],
        compiler_params=_params(("arbitrary", "arbitrary")),
        name="input_projection",
    )(x2d, norm_w3, mod3, mod3, w_bf16)


def _rope(x, cos, sin_lo, sin_hi):
    return (x * cos + pltpu.roll(x, HEAD_DIM - 16, axis=1) * sin_lo
            + pltpu.roll(x, 16, axis=1) * sin_hi)


def _diff_lambda(lq_ref, lam_init):
    lq = lq_ref[...]
    a = jnp.sum(lq[0:1] * lq[1:2], axis=-1, keepdims=True)
    b = jnp.sum(lq[2:3] * lq[3:4], axis=-1, keepdims=True)
    return jnp.exp(a) - jnp.exp(b) + lam_init


LOG2E = 1.4426950408889634
NKEYS = CTX_LEN + SEQ
DA_TQ = 256
DA_W = 2 * DA_TQ
DA_ROWS = 16
DA_KCHUNK = 256


def _transpose_bf16(x):
    return x.astype(F32).T.astype(BF16)


def _diffattn_lat_kernel(one_ref, q_ref, kc_ref, vc_ref, kl_ref, vl_ref, g_ref, cos_ref, slo_ref, shi_ref,
                         lq_ref, sw_ref, o_ref, k_scr, vt_scr, qt_scr, s0_scr, s1_scr, m_scr, p_scr,
                         *, lam_init):
    n_tiles = SEQ // DA_TQ
    n_chunks = NKEYS // DA_KCHUNK
    groups_per_chunk = DA_KCHUNK // DA_ROWS

    def score_chunk(s_scr, qt2, c, m_run):
        rows = slice(c * DA_KCHUNK, (c + 1) * DA_KCHUNK)
        s = jnp.dot(k_scr[rows, :], qt2, preferred_element_type=F32)
        s_scr[rows, :] = s
        parts = [s[i * DA_ROWS:(i + 1) * DA_ROWS, :] for i in range(groups_per_chunk)]
        if m_run is not None:
            parts.append(m_run)
        while len(parts) > 1:
            parts = [jnp.maximum(parts[i], parts[i + 1]) for i in range(0, len(parts) - 1, 2)] + (
                [parts[-1]] if len(parts) % 2 else [])
        return parts[0]

    def colmax(m_run):
        return jnp.broadcast_to(jnp.max(m_run, axis=0, keepdims=True), (DA_ROWS, DA_W))

    def always(k):
        return one_ref[0] > -k

    @pl.when(always(0))
    def _():
        assert DA_KCHUNK == DA_TQ == CTX_LEN
        dim = lax.broadcasted_iota(jnp.int32, (HB, DA_TQ), 0)

        def stage_queries(c):
            rows = slice(c * DA_TQ, (c + 1) * DA_TQ)
            q = _rope(q_ref[rows, :].astype(F32), cos_ref[rows, :], slo_ref[rows, :], shi_ref[rows, :])
            qt = (q * (A_QK_DIM ** -0.5 * LOG2E)).T
            qt_scr[:, c * DA_W:c * DA_W + DA_TQ] = jnp.where(dim < A_QK_DIM, qt, 0.0).astype(BF16)
            qt_scr[:, c * DA_W + DA_TQ:(c + 1) * DA_W] = jnp.where(dim >= A_QK_DIM, qt, 0.0).astype(BF16)

        stage_queries(0)
        qt0 = qt_scr[:, 0:DA_W]
        k_scr[0:CTX_LEN, :] = kc_ref[...]
        vt_scr[:, 0:CTX_LEN] = _transpose_bf16(vc_ref[...])
        m_run = score_chunk(s0_scr, qt0, 0, None)
        for c in range(SEQ // DA_TQ):
            rows = slice(c * DA_TQ, (c + 1) * DA_TQ)
            dst = slice(CTX_LEN + c * DA_TQ, CTX_LEN + (c + 1) * DA_TQ)
            k_scr[dst, :] = _rope(kl_ref[rows, :].astype(F32), cos_ref[rows, :], slo_ref[rows, :],
                                  shi_ref[rows, :]).astype(BF16)
            m_run = score_chunk(s0_scr, qt0, c + 1, m_run)
            vt_scr[:, dst] = _transpose_bf16(vl_ref[rows, :])
            if c > 0:
                stage_queries(c)
        m_scr[...] = colmax(m_run)

    def tile(t, s_scr, s_next_scr):
        rows = slice(t * DA_TQ, (t + 1) * DA_TQ)
        mb = m_scr[...]
        if s_next_scr is not None:
            qt_next = qt_scr[:, (t + 1) * DA_W:(t + 2) * DA_W]
        m_run = None
        l_acc = [None, None]
        acc = None

        def value_chunk(c, acc):
            rows = slice(c * DA_KCHUNK, (c + 1) * DA_KCHUNK)
            part = jnp.dot(vt_scr[:, rows], p_scr[rows, :], preferred_element_type=F32)
            return part if acc is None else acc + part

        for c in range(n_chunks):
            if s_next_scr is not None:
                m_run = score_chunk(s_next_scr, qt_next, c, m_run)
            for r in range(c * groups_per_chunk, (c + 1) * groups_per_chunk):
                grp = slice(r * DA_ROWS, (r + 1) * DA_ROWS)
                p = jnp.exp2(s_scr[grp, :] - mb)
                p_scr[grp, :] = p.astype(BF16)
                l_acc[r % 2] = p if l_acc[r % 2] is None else l_acc[r % 2] + p
            acc = value_chunk(c, acc)
        l = jnp.sum(l_acc[0] + l_acc[1], axis=0, keepdims=True)

        lam = _diff_lambda(lq_ref, lam_init)
        ot = acc[:, :DA_TQ] * (1.0 / l[:, :DA_TQ]) - acc[:, DA_TQ:] * (lam / l[:, DA_TQ:])
        ot = ot * lax.rsqrt(jnp.mean(ot * ot, axis=0, keepdims=True) + SUBLN_EPS)
        o = ot.T * sw_ref[...] * (1.0 - lam_init)
        o_ref[rows, :] = (o * _silu(g_ref[rows, :].astype(F32))).astype(o_ref.dtype)
        if s_next_scr is not None:
            m_scr[...] = colmax(m_run)

    bufs = (s0_scr, s1_scr)
    for t in range(n_tiles):
        nxt = bufs[(t + 1) % 2] if t + 1 < n_tiles else None
        pl.when(always(t + 1))(functools.partial(tile, t, bufs[t % 2], nxt))


_NT = (((1,), (1,)), ((), ()))


def _diffattn_ctx_kernel(q_ref, k_ref, v_ref, g_ref, lq_ref, sw_ref, o_ref, *, lam_init):
    q = q_ref[...].astype(F32) * (A_QK_DIM ** -0.5)
    lane = lax.broadcasted_iota(jnp.int32, q.shape, 1)
    outs = []
    for qn in (jnp.where(lane < A_QK_DIM, q, 0.0), jnp.where(lane >= A_QK_DIM, q, 0.0)):
        s = lax.dot_general(qn.astype(BF16), k_ref[...], _NT, preferred_element_type=F32)
        p = jnp.exp(s - jnp.max(s, axis=-1, keepdims=True))
        acc = jnp.dot(p.astype(BF16), v_ref[...], preferred_element_type=F32)
        outs.append((acc, jnp.sum(p, axis=-1, keepdims=True)))
    lam = _diff_lambda(lq_ref, lam_init)
    o = outs[0][0] * (1.0 / outs[0][1]) - (lam / outs[1][1]) * outs[1][0]
    o = o * lax.rsqrt(jnp.mean(o * o, axis=-1, keepdims=True) + SUBLN_EPS) * sw_ref[...]
    o = o * (1.0 - lam_init)
    o_ref[...] = (o * _silu(g_ref[...].astype(F32))).astype(o_ref.dtype)


def diff_attention_latent(proj_l, proj_c, ccols, rope_tabs, lambda_qk, subln_w, lam_init):
    head = lambda col: (lambda b, h, one: (b, col + h))
    const = lambda b, h, one: (0, 0)
    grid_spec = pltpu.PrefetchScalarGridSpec(
        num_scalar_prefetch=1,
        grid=(BATCH, A_HEADS),
        in_specs=[
            pl.BlockSpec((SEQ, HB), head(COL_QA)),
            pl.BlockSpec((CTX_LEN, HB), head(ccols["ka"])),
            pl.BlockSpec((CTX_LEN, HB), head(ccols["va"])),
            pl.BlockSpec((SEQ, HB), head(COL_KA)),
            pl.BlockSpec((SEQ, HB), head(COL_VA)),
            pl.BlockSpec((SEQ, HB), head(COL_GA)),
            pl.BlockSpec((SEQ, HB), const),
            pl.BlockSpec((SEQ, HB), const),
            pl.BlockSpec((SEQ, HB), const),
            pl.BlockSpec((4, A_QK_DIM), const),
            pl.BlockSpec((1, HB), const),
        ],
        out_specs=pl.BlockSpec((SEQ, HB), lambda b, h, one: (b, h)),
        scratch_shapes=[pltpu.VMEM((NKEYS, HB), BF16), pltpu.VMEM((HB, NKEYS), BF16),
                        pltpu.VMEM((HB, 2 * SEQ), BF16),
                        pltpu.VMEM((NKEYS, DA_W), F32), pltpu.VMEM((NKEYS, DA_W), F32),
                        pltpu.VMEM((DA_ROWS, DA_W), F32), pltpu.VMEM((NKEYS, DA_W), BF16)],
    )
    return pl.pallas_call(
        functools.partial(_diffattn_lat_kernel, lam_init=lam_init),
        out_shape=jax.ShapeDtypeStruct((BATCH * SEQ, A_WIDTH), BF16),
        grid_spec=grid_spec,
        compiler_params=_params(("arbitrary", "arbitrary")),
        name="diff_attention_latent",
    )(jnp.ones((1,), jnp.int32), proj_l, proj_c, proj_c, proj_l, proj_l, proj_l, *rope_tabs,
      lambda_qk, subln_w.reshape(1, HB))


def diff_attention_context(proj_c, lambda_qk, subln_w, lam_init):
    head = lambda col: (lambda b, h: (b, col + h))
    const = lambda b, h: (0, 0)
    return pl.pallas_call(
        functools.partial(_diffattn_ctx_kernel, lam_init=lam_init),
        out_shape=jax.ShapeDtypeStruct((BATCH * CTX_LEN, A_WIDTH), BF16),
        grid=(BATCH, A_HEADS),
        in_specs=[pl.BlockSpec((CTX_LEN, HB), head(COL_QA)),
                  pl.BlockSpec((CTX_LEN, HB), head(COL_KA)),
                  pl.BlockSpec((CTX_LEN, HB), head(COL_VA)),
                  pl.BlockSpec((CTX_LEN, HB), head(COL_GA)),
                  pl.BlockSpec((4, A_QK_DIM), const),
                  pl.BlockSpec((1, HB), const)],
        out_specs=pl.BlockSpec((CTX_LEN, HB), lambda b, h: (b, h)),
        compiler_params=_params(("arbitrary", "arbitrary")),
        name="diff_attention_context",
    )(proj_c, proj_c, proj_c, proj_c, lambda_qk, subln_w.reshape(1, HB))


def _nbr_slab_start(g):
    return min(max(g * NB_QROWS - NA_KH // 2, 0), GRID_H - NB_KROWS)


def _nbr_pattern(g):
    return 0 if g == 0 else (2 if g == NB_GROUPS - 1 else 1)


def _nbr_kernel(q_ref, kc_ref, vc_ref, kl_ref, vl_ref, g_ref, bias_ref, o_ref, vt_scr):
    vt_scr[:, 0:CTX_LEN] = _transpose_bf16(vc_ref[...])
    for c in range(SEQ // NB_Q):
        vt_scr[:, CTX_LEN + c * NB_Q:CTX_LEN + (c + 1) * NB_Q] = _transpose_bf16(vl_ref[c * NB_Q:(c + 1) * NB_Q, :])
    def scores(g):
        rows = slice(g * NB_Q, (g + 1) * NB_Q)
        k0 = _nbr_slab_start(g) * GRID_W
        qt = (q_ref[rows, :].astype(F32) * (HEAD_DIM ** -0.5 * LOG2E)).T.astype(BF16)
        s_c = jnp.dot(kc_ref[...], qt, preferred_element_type=F32)
        s_n = (jnp.dot(kl_ref[k0:k0 + NB_K, :], qt, preferred_element_type=F32)
               + bias_ref[0, _nbr_pattern(g)])
        return s_c, s_n

    def softmax(s_c, s_n):
        m = jnp.maximum(jnp.max(s_c, axis=0, keepdims=True), jnp.max(s_n, axis=0, keepdims=True))
        p_c = jnp.exp2(s_c - m)
        p_n = jnp.exp2(s_n - m)
        l = jnp.sum(p_c, axis=0, keepdims=True) + jnp.sum(p_n, axis=0, keepdims=True)
        return p_c.astype(BF16), p_n.astype(BF16), l

    def finish(g, p_c, p_n, l):
        rows = slice(g * NB_Q, (g + 1) * NB_Q)
        k0 = _nbr_slab_start(g) * GRID_W
        acc = (jnp.dot(vt_scr[:, 0:CTX_LEN], p_c, preferred_element_type=F32)
               + jnp.dot(vt_scr[:, CTX_LEN + k0:CTX_LEN + k0 + NB_K], p_n,
                         preferred_element_type=F32))
        o = (acc * (1.0 / l)).T
        o_ref[rows, :] = (o * _silu(g_ref[rows, :].astype(F32))).astype(o_ref.dtype)

    ahead = scores(0)
    probs = None
    for g in range(NB_GROUPS):
        s_c, s_n = ahead
        if g + 1 < NB_GROUPS:
            ahead = scores(g + 1)
        new_probs = softmax(s_c, s_n)
        if probs is not None:
            finish(g - 1, *probs)
        probs = new_probs
    finish(NB_GROUPS - 1, *probs)


def neighbourhood_bias_table(rpb):
    w = GRID_W
    cq = np.arange(w)
    c0 = np.clip(cq - NA_KW // 2, 0, w - NA_KW)
    ck = np.arange(w)
    col_valid = (ck[None, :] >= c0[:, None]) & (ck[None, :] < c0[:, None] + NA_KW)
    pad = w - NA_KW
    padded = jnp.pad(rpb.astype(F32), ((0, 0), (0, 0), (pad, pad)))
    toep = jnp.stack([padded[:, :, w - 1 - q:2 * w - 1 - q] for q in range(w)], axis=2)
    toep = jnp.where(col_valid[None, None], toep, NEG_INF)
    toep_t = jnp.swapaxes(toep, 2, 3) * LOG2E
    masked = jnp.full((B_HEADS, w, w), NEG_INF, F32)
    tabs = []
    for g in (0, 1, NB_GROUPS - 1):
        slab0 = _nbr_slab_start(g)
        qr = g * NB_QROWS + np.arange(NB_QROWS)
        kr = slab0 + np.arange(NB_KROWS)
        win0 = np.clip(qr - NA_KH // 2, 0, GRID_H - NA_KH)
        row_valid = (kr[None, :] >= win0[:, None]) & (kr[None, :] < win0[:, None] + NA_KH)
        dr_idx = kr[None, :] - qr[:, None] + NA_KH - 1
        rows = [jnp.concatenate([toep_t[:, int(dr_idx[a, k])] if row_valid[a, k] else masked
                                 for a in range(NB_QROWS)], axis=-1) for k in range(NB_KROWS)]
        tabs.append(jnp.concatenate(rows, axis=1))
    return jnp.stack(tabs, axis=1)


def neighbourhood_attention(proj_l, proj_c, ccols, bias_tab):
    head = lambda col: (lambda h, b: (b, col + h))
    return pl.pallas_call(
        _nbr_kernel,
        out_shape=jax.ShapeDtypeStruct((BATCH * SEQ, B_WIDTH), BF16),
        grid=(B_HEADS, BATCH),
        in_specs=[
            pl.BlockSpec((SEQ, HB), head(COL_QB)),
            pl.BlockSpec((CTX_LEN, HB), head(ccols["kb"])),
            pl.BlockSpec((CTX_LEN, HB), head(ccols["vb"])),
            pl.BlockSpec((SEQ, HB), head(COL_KB)),
            pl.BlockSpec((SEQ, HB), head(COL_VB)),
            pl.BlockSpec((SEQ, HB), head(COL_GB)),
            pl.BlockSpec((1, 3, NB_K, NB_Q), lambda h, b: (h, 0, 0, 0)),
        ],
        out_specs=pl.BlockSpec((SEQ, HB), lambda h, b: (b, h)),
        scratch_shapes=[pltpu.VMEM((HB, NKEYS), BF16)],
        compiler_params=_params(("arbitrary", "arbitrary")),
        name="neighbourhood_attention",
    )(proj_l, proj_c, proj_c, proj_l, proj_l, proj_l, bias_tab)


def _dense_ctx_kernel(q_ref, k_ref, v_ref, g_ref, o_ref):
    scale = HEAD_DIM ** -0.5
    s = lax.dot_general(q_ref[...], k_ref[...], _NT, preferred_element_type=F32) * scale
    p = jnp.exp(s - jnp.max(s, axis=-1, keepdims=True))
    p = p * (1.0 / jnp.sum(p, axis=-1, keepdims=True))
    o = jnp.dot(p.astype(BF16), v_ref[...], preferred_element_type=F32)
    o_ref[...] = (o * _silu(g_ref[...].astype(F32))).astype(o_ref.dtype)


def dense_context_attention(proj_c):
    head = lambda col: (lambda b, h: (b, col + h))
    return pl.pallas_call(
        _dense_ctx_kernel,
        out_shape=jax.ShapeDtypeStruct((BATCH * CTX_LEN, B_WIDTH), BF16),
        grid=(BATCH, B_HEADS),
        in_specs=[pl.BlockSpec((CTX_LEN, HB), head(COL_QB)),
                  pl.BlockSpec((CTX_LEN, HB), head(COL_KB)),
                  pl.BlockSpec((CTX_LEN, HB), head(COL_VB)),
                  pl.BlockSpec((CTX_LEN, HB), head(COL_GB))],
        out_specs=pl.BlockSpec((CTX_LEN, HB), lambda b, h: (b, h)),
        compiler_params=_params(("arbitrary", "arbitrary")),
        name="dense_context_attention",
    )(proj_c, proj_c, proj_c, proj_c)


def _shift_rows(x, d, fill):
    n = x.shape[0]
    if d % SUBLANES == 0:
        pad = jnp.full((abs(d), x.shape[1]), fill, x.dtype)
        return jnp.concatenate([pad, x[:n - d]] if d > 0 else [x[-d:], pad], axis=0)
    rolled = pltpu.roll(x, d % n, axis=0)
    row = lax.broadcasted_iota(jnp.int32, x.shape, 0)
    keep = (row >= d) if d > 0 else (row < n + d)
    return jnp.where(keep, rolled, fill)


def _dwconv(x, w_ref, b_ref):
    acc = b_ref[...] + w_ref[CONV_LEFT:CONV_LEFT + 1, :] * x
    for j in range(CONV_W):
        if j != CONV_LEFT:
            acc = acc + w_ref[j:j + 1, :] * _shift_rows(x, CONV_LEFT - j, 0.0)
    return acc


def _linear_scan(a, b, h0, reverse):
    n = a.shape[0]
    row = lax.broadcasted_iota(jnp.int32, a.shape, 0)
    first = (row == n - 1) if reverse else (row == 0)
    b = b + jnp.where(first, a * h0, 0.0)
    d = 1
    while d < n:
        sd = -d if reverse else d
        b = b + a * _shift_rows(b, sd, 0.0)
        if 2 * d < n:
            a = a * _shift_rows(a, sd, 1.0)
        d *= 2
    return b


def _sigmoid_tanh(x):
    return 0.5 * jnp.tanh(0.5 * x) + 0.5


def _rglru_gates(u, wa_ref, ba_ref, wx_ref, bx_ref, lam_ref, d):
    ub = u.astype(BF16)
    r = _sigmoid_tanh(jnp.dot(ub, wa_ref[0, d], preferred_element_type=F32) + ba_ref[0, d])
    i = _sigmoid_tanh(jnp.dot(ub, wx_ref[0, d], preferred_element_type=F32) + bx_ref[0, d])
    neg_lam = -lam_ref[0, d]
    softplus = jnp.maximum(neg_lam, 0.0) + jnp.log1p(jnp.exp(-jnp.abs(neg_lam)))
    log_a = (-RGLRU_C) * r * softplus
    a = jnp.exp(log_a)
    t = jnp.tanh(log_a)
    b = jnp.sqrt(-2.0 * t / (1.0 - t)) * (i * u)
    return a, b


SCAN_CHUNKS = SUBLANES


def _scan_chunk_len(n):
    return n // SCAN_CHUNKS + 1


def _blocked_scan(a_scr, b_scr, n, h0, reverse, need_states):
    length = _scan_chunk_len(n)
    order = range(length - 1, -1, -1) if reverse else range(length)
    h = jnp.zeros((SCAN_CHUNKS, a_scr.shape[1]), F32)
    p = jnp.ones_like(h)
    hs, ps = [None] * length, [None] * length
    for i in order:
        va = a_scr[pl.ds(i, SCAN_CHUNKS, stride=length), :]
        vb = b_scr[pl.ds(i, SCAN_CHUNKS, stride=length), :]
        h = va * h + vb
        p = va * p
        hs[i], ps[i] = h, p
    rows = [None] * SCAN_CHUNKS
    carry = h0
    for j in (range(SCAN_CHUNKS - 1, -1, -1) if reverse else range(SCAN_CHUNKS)):
        rows[j] = carry
        carry = h[j:j + 1, :] + p[j:j + 1, :] * carry
    if not need_states:
        return None, carry
    start = jnp.concatenate(rows, axis=0)
    return [hs[i] + ps[i] * start for i in range(length)], carry


def _rglru_kernel(*refs, need_ctx):
    if need_ctx:
        (xl_ref, xc_ref, gl_ref, gc_ref, cw_ref, cb_ref, wa_ref, ba_ref, wx_ref, bx_ref, lam_ref,
         ol_ref, oc_ref, al_scr, bl_scr, ac_scr, bc_scr, yl_scr, yc_scr) = refs
    else:
        (xl_ref, xc_ref, gl_ref, cw_ref, cb_ref, wa_ref, ba_ref, wx_ref, bx_ref, lam_ref,
         ol_ref, al_scr, bl_scr, ac_scr, bc_scr, yl_scr, yc_scr) = refs
    ct = al_scr.shape[1]
    u_l = _dwconv(xl_ref[...].astype(F32), cw_ref, cb_ref)
    u_c = _dwconv(xc_ref[...].astype(F32), cw_ref, cb_ref)
    for a_scr, b_scr, n in ((al_scr, bl_scr, SEQ), (ac_scr, bc_scr, CTX_LEN)):
        a_scr[n:n + SUBLANES, :] = jnp.ones((SUBLANES, ct), F32)
        b_scr[n:n + SUBLANES, :] = jnp.zeros((SUBLANES, ct), F32)
    y_l = None
    y_c = None
    for d in range(2):
        rev = d == 1
        a_c, b_c = _rglru_gates(u_c, wa_ref, ba_ref, wx_ref, bx_ref, lam_ref, d)
        ac_scr[0:CTX_LEN, :] = a_c
        bc_scr[0:CTX_LEN, :] = b_c
        h_c, fin = _blocked_scan(ac_scr, bc_scr, CTX_LEN, jnp.zeros((1, ct), F32), rev, need_ctx)
        a_l, b_l = _rglru_gates(u_l, wa_ref, ba_ref, wx_ref, bx_ref, lam_ref, d)
        al_scr[0:SEQ, :] = a_l
        bl_scr[0:SEQ, :] = b_l
        h_l, _ = _blocked_scan(al_scr, bl_scr, SEQ, fin, rev, True)
        y_l = h_l if y_l is None else [f + r for f, r in zip(y_l, h_l)]
        if need_ctx:
            y_c = h_c if y_c is None else [f + r for f, r in zip(y_c, h_c)]
    length = _scan_chunk_len(SEQ)
    for i in range(length):
        yl_scr[pl.ds(i, SCAN_CHUNKS, stride=length), :] = y_l[i]
    ol_ref[...] = (yl_scr[0:SEQ, :] * _silu(gl_ref[...].astype(F32))).astype(ol_ref.dtype)
    if need_ctx:
        length = _scan_chunk_len(CTX_LEN)
        for i in range(length):
            yc_scr[pl.ds(i, SCAN_CHUNKS, stride=length), :] = y_c[i]
        oc_ref[...] = (yc_scr[0:CTX_LEN, :] * _silu(gc_ref[...].astype(F32))).astype(oc_ref.dtype)


def _block_diag(w, per_tile):
    n = w.shape[-1]
    tiles = C_BLOCKS // per_tile
    w = w.reshape(2, tiles, per_tile, n, n)
    eye = jnp.eye(per_tile, dtype=w.dtype)
    bd = jnp.einsum('dtpij,pq->dtpiqj', w, eye).reshape(2, tiles, per_tile * n, per_tile * n)
    return jnp.transpose(bd, (1, 0, 2, 3)).astype(BF16)


def rglru_mixer(proj_l, proj_c, ccols, conv_w, conv_b, rg_wa, rg_ba, rg_wx, rg_bx, rg_lambda, need_ctx):
    ct = 128
    nt = C_WIDTH // ct
    per_tile = ct // C_BLOCK_DIM
    vec = lambda p: jnp.transpose(p.reshape(2, nt, 1, ct), (1, 0, 2, 3))
    wa = _block_diag(rg_wa, per_tile)
    wx = _block_diag(rg_wx, per_tile)
    xcol = lambda col: (lambda b, t: (b, col + t))
    par = lambda b, t: (t, 0, 0, 0)
    out_shape = [jax.ShapeDtypeStruct((BATCH * SEQ, C_WIDTH), BF16)]
    out_specs = [pl.BlockSpec((SEQ, ct), lambda b, t: (b, t))]
    if need_ctx:
        out_shape.append(jax.ShapeDtypeStruct((BATCH * CTX_LEN, C_WIDTH), BF16))
        out_specs.append(pl.BlockSpec((CTX_LEN, ct), lambda b, t: (b, t)))
    in_specs = [pl.BlockSpec((SEQ, ct), xcol(COL_XC)),
                pl.BlockSpec((CTX_LEN, ct), xcol(ccols["xc"])),
                pl.BlockSpec((SEQ, ct), xcol(COL_GC))]
    args = [proj_l, proj_c, proj_l]
    if need_ctx:
        in_specs.append(pl.BlockSpec((CTX_LEN, ct), xcol(ccols["gc"])))
        args.append(proj_c)
    in_specs += [pl.BlockSpec((CONV_W, ct), lambda b, t: (0, t)),
                 pl.BlockSpec((1, ct), lambda b, t: (0, t)),
                 pl.BlockSpec((1, 2, ct, ct), par),
                 pl.BlockSpec((1, 2, 1, ct), par),
                 pl.BlockSpec((1, 2, ct, ct), par),
                 pl.BlockSpec((1, 2, 1, ct), par),
                 pl.BlockSpec((1, 2, 1, ct), par)]
    args += [conv_w, conv_b.reshape(1, C_WIDTH), wa, vec(rg_ba), wx, vec(rg_bx), vec(rg_lambda)]
    outs = pl.pallas_call(
        functools.partial(_rglru_kernel, need_ctx=need_ctx),
        out_shape=out_shape,
        grid=(BATCH, nt),
        in_specs=in_specs,
        out_specs=out_specs,
        scratch_shapes=([pltpu.VMEM((SEQ + SUBLANES, ct), F32)] * 2
                        + [pltpu.VMEM((CTX_LEN + SUBLANES, ct), F32)] * 2
                        + [pltpu.VMEM((SEQ + SUBLANES, ct), F32), pltpu.VMEM((CTX_LEN + SUBLANES, ct), F32)]),
        compiler_params=_params(("arbitrary", "arbitrary")),
        name="rglru_mixer",
    )(*args)
    return outs if need_ctx else (outs[0], None)


def _outproj_kernel(x_ref, gate_ref, ma_ref, mb_ref, mc_ref, wa_ref, wb_ref, wc_ref, o_ref):
    y = jnp.dot(ma_ref[...], wa_ref[...], preferred_element_type=F32)
    y = y + jnp.dot(mb_ref[...], wb_ref[...], preferred_element_type=F32)
    y = y + jnp.dot(mc_ref[...], wc_ref[...], preferred_element_type=F32)
    o_ref[...] = x_ref[...] + gate_ref[0] * y


def output_projection(x2d, mod3, ma, mb, mc, w_bf16, layer, rows_per_mod, mod_row0):
    m = x2d.shape[0]
    tm, tn = 1024, 1024

    def mod_row(i):
        return mod_row0 + (i * tm) // rows_per_mod

    return pl.pallas_call(
        _outproj_kernel,
        out_shape=jax.ShapeDtypeStruct((m, D_MODEL), F32),
        grid=(m // tm, D_MODEL // tn),
        in_specs=[
            pl.BlockSpec((tm, tn), lambda i, j: (i, j)),
            pl.BlockSpec((1, 1, tn), lambda i, j: (mod_row(i), 0, 2 * (D_MODEL // tn) + j)),
            pl.BlockSpec((tm, A_WIDTH), lambda i, j: (i, 0)),
            pl.BlockSpec((tm, B_WIDTH), lambda i, j: (i, 0)),
            pl.BlockSpec((tm, C_WIDTH), lambda i, j: (i, 0)),
            pl.BlockSpec((None, A_WIDTH, tn), lambda i, j: (layer, 0, j)),
            pl.BlockSpec((None, B_WIDTH, tn), lambda i, j: (layer, 1, j)),
            pl.BlockSpec((None, C_WIDTH, tn), lambda i, j: (layer, (A_WIDTH + B_WIDTH) // C_WIDTH, j)),
        ],
        out_specs=pl.BlockSpec((tm, tn), lambda i, j: (i, j)),
        compiler_params=_params(("arbitrary", "arbitrary")),
        name="output_projection",
    )(x2d, mod3, ma, mb, mc, w_bf16, w_bf16, w_bf16)


def _final_norm_kernel(x_ref, w_ref, o_ref):
    x = x_ref[...]
    o_ref[...] = x * lax.rsqrt(jnp.mean(x * x, axis=-1, keepdims=True) + NORM_EPS) * w_ref[...]


def final_norm(x2d, w):
    tm = 256
    return pl.pallas_call(
        _final_norm_kernel,
        out_shape=jax.ShapeDtypeStruct(x2d.shape, F32),
        grid=(x2d.shape[0] // tm,),
        in_specs=[pl.BlockSpec((tm, D_MODEL), lambda i: (i, 0)),
                  pl.BlockSpec((1, D_MODEL), lambda i: (0, 0))],
        out_specs=pl.BlockSpec((tm, D_MODEL), lambda i: (i, 0)),
        compiler_params=_params(("arbitrary",)),
        name="final_norm",
    )(x2d, w.reshape(1, D_MODEL))


def _rope_tables():
    t = np.arange(SEQ)
    pos = np.stack([t // GRID_W, t % GRID_W], axis=0).astype(np.float32)
    lane = np.arange(HEAD_DIM)
    within = lane % A_QK_DIM
    axis = within // (A_QK_DIM // 2)
    k = within % (A_QK_DIM // 2)
    nfreq = A_QK_DIM // 4
    upper = k >= nfreq
    freqs = jnp.asarray(ROPE_THETA, F32) ** (-jnp.arange(nfreq, dtype=F32) / nfreq)
    ang = jnp.asarray(pos)[axis, :].T * freqs[k % nfreq][None, :]
    cos, sin = jnp.cos(ang), jnp.sin(ang)
    sin_lo = jnp.where(upper[None, :], 0.0, -sin)
    sin_hi = jnp.where(upper[None, :], sin, 0.0)
    return cos, sin_lo, sin_hi


def kernel(x, c, ctx, c_ctx, ada_w, ada_b, norm_w, w_in, w_out, lambda_qk, subln_w, rpb,
           conv_w, conv_b, rg_wa, rg_ba, rg_wx, rg_bx, rg_lambda, final_norm_w):
    cvec = jnp.zeros((MOD_ROWS, D_MODEL), F32).at[:BATCH].set(c).at[CTX_MOD_ROW].set(c_ctx)
    mod = adaln_modulation(cvec, ada_w, ada_b)
    mod3 = mod.reshape(DEPTH * MOD_ROWS, 1, 3 * D_MODEL)
    norm_w3 = norm_w.reshape(DEPTH, 1, D_MODEL)
    w_in_bf = w_in.astype(BF16)
    w_out_bf = w_out.astype(BF16)
    rope_tabs = _rope_tables()
    xl = x.reshape(BATCH * SEQ, D_MODEL)
    xc = ctx.reshape(BATCH * CTX_LEN, D_MODEL)
    for l in range(DEPTH):
        need_ctx = l < DEPTH - 1
        lam_init = 0.8 - 0.6 * math.exp(-0.3 * l)
        row_l, row_c = l * MOD_ROWS, l * MOD_ROWS + CTX_MOD_ROW
        proj_l = input_projection(xl, norm_w3, mod3, w_in_bf, l, SEQ, row_l)
        ccols = FULL_COLS if need_ctx else KV_COLS
        proj_c = input_projection(xc, norm_w3, mod3, w_in_bf, l, BATCH * CTX_LEN, row_c, ccols)

        ma = diff_attention_latent(proj_l, proj_c, ccols, rope_tabs, lambda_qk[l], subln_w[l], lam_init)
        mb = neighbourhood_attention(proj_l, proj_c, ccols, neighbourhood_bias_table(rpb[l]))
        mc, mc_c = rglru_mixer(proj_l, proj_c, ccols, conv_w[l], conv_b[l], rg_wa[l], rg_ba[l],
                               rg_wx[l], rg_bx[l], rg_lambda[l], need_ctx)
        xl = output_projection(xl, mod3, ma, mb, mc, w_out_bf, l, SEQ, row_l)
        if need_ctx:
            ma_c = diff_attention_context(proj_c, lambda_qk[l], subln_w[l], lam_init)
            mb_c = dense_context_attention(proj_c)
            xc = output_projection(xc, mod3, ma_c, mb_c, mc_c, w_out_bf, l, BATCH * CTX_LEN, row_c)
    return final_norm(xl, final_norm_w).reshape(BATCH, SEQ, D_MODEL)
```

```python
import functools
import math

import numpy as np
import jax
import jax.numpy as jnp
from jax import lax
from jax.experimental import pallas as pl
from jax.experimental.pallas import tpu as pltpu

F32 = jnp.float32
BF16 = jnp.bfloat16

D_MODEL = 4096
BATCH = 8
SEQ = 2048
DEPTH = 2
GRID_W = 64
GRID_H = SEQ // GRID_W
CTX_LEN = 256
HEAD_DIM = 128
A_WIDTH = (3 * D_MODEL) // 8
A_HEADS = A_WIDTH // HEAD_DIM
A_QK_DIM = HEAD_DIM // 2
B_WIDTH = (3 * D_MODEL) // 8
B_HEADS = B_WIDTH // HEAD_DIM
C_WIDTH = D_MODEL - A_WIDTH - B_WIDTH
C_BLOCKS = 16
C_BLOCK_DIM = C_WIDTH // C_BLOCKS
MIX_WIDTH = A_WIDTH + B_WIDTH + C_WIDTH
IN_WIDTH = 4 * A_WIDTH + 4 * B_WIDTH + 2 * C_WIDTH
NA_KH = 8
NA_KW = 16
ROPE_THETA = 10000.0
RGLRU_C = 8.0
CONV_W = 4
CONV_LEFT = 2
NORM_EPS = 1e-6
SUBLN_EPS = 1e-5
NEG_INF = -1e30

HB = HEAD_DIM
COL_QA, COL_KA, COL_VA, COL_GA = 0, A_HEADS, 2 * A_HEADS, 3 * A_HEADS
COL_QB = 4 * A_HEADS
COL_KB, COL_VB, COL_GB = COL_QB + B_HEADS, COL_QB + 2 * B_HEADS, COL_QB + 3 * B_HEADS
COL_XC = COL_QB + 4 * B_HEADS
COL_GC = COL_XC + C_WIDTH // HB

MOD_ROWS = 16
CTX_MOD_ROW = BATCH

VMEM_LIMIT = 56 * 1024 * 1024
SUBLANES = 8

NB_QROWS = 4
NB_KROWS = 12
NB_GROUPS = GRID_H // NB_QROWS
NB_Q = NB_QROWS * GRID_W
NB_K = NB_KROWS * GRID_W


def _params(sem):
    return pltpu.CompilerParams(dimension_semantics=sem, vmem_limit_bytes=VMEM_LIMIT)


def _sigmoid(x):
    return 1.0 / (1.0 + jnp.exp(-x))


def _silu(x):
    return x * _sigmoid(x)


def _adaln_kernel(c_ref, w_ref, b_ref, o_ref):
    s = _silu(c_ref[...]).astype(BF16)
    w = w_ref[0].astype(BF16)
    o_ref[0] = jnp.dot(s, w, preferred_element_type=F32) + b_ref[0]


def adaln_modulation(cvec, ada_w, ada_b):
    tn = 512
    n3 = 3 * D_MODEL
    return pl.pallas_call(
        _adaln_kernel,
        out_shape=jax.ShapeDtypeStruct((DEPTH, MOD_ROWS, n3), F32),
        grid=(DEPTH, n3 // tn),
        in_specs=[
            pl.BlockSpec((MOD_ROWS, D_MODEL), lambda l, j: (0, 0)),
            pl.BlockSpec((1, D_MODEL, tn), lambda l, j: (l, 0, j)),
            pl.BlockSpec((1, 1, tn), lambda l, j: (l, 0, j)),
        ],
        out_specs=pl.BlockSpec((1, MOD_ROWS, tn), lambda l, j: (l, 0, j)),
        compiler_params=_params(("arbitrary", "arbitrary")),
        name="adaln_modulation",
    )(cvec, ada_w, ada_b.reshape(DEPTH, 1, n3))


_NORM_CHUNK = 32
_NORM_SLICES = 4


def _inproj_kernel(x_ref, nw_ref, shift_ref, scale_ref, w_ref, o_ref, hx0_ref, hx1_ref):
    r, j = pl.program_id(0), pl.program_id(1)
    xs = x_ref.shape[0]
    gain = nw_ref[...] * (1.0 + scale_ref[0])
    add = shift_ref[0]

    def normalise_slice(hx_ref):
        base = jnp.minimum(j, _NORM_SLICES - 1) * xs
        for k in range(xs // _NORM_CHUNK):
            x = x_ref[k * _NORM_CHUNK:(k + 1) * _NORM_CHUNK, :]
            inv = lax.rsqrt(jnp.mean(x * x, axis=-1, keepdims=True) + NORM_EPS)
            rows = pl.ds(pl.multiple_of(base + k * _NORM_CHUNK, _NORM_CHUNK), _NORM_CHUNK)
            hx_ref[rows, :] = (x * inv * gain + add).astype(BF16)

    def step(hx_cur, hx_next):
        normalise_slice(hx_next)
        o_ref[...] = jnp.dot(hx_cur[...], w_ref[...], preferred_element_type=F32).astype(o_ref.dtype)

    @pl.when(r == 0)
    def _():
        normalise_slice(hx0_ref)

    pl.when(r % 2 == 1)(functools.partial(step, hx0_ref, hx1_ref))
    pl.when(jnp.logical_and(r % 2 == 0, r > 0))(functools.partial(step, hx1_ref, hx0_ref))


class ProjCols:
    def __init__(self, groups):
        self.start, pos = {}, 0
        for name, _, width in groups:
            self.start[name] = pos
            pos += width
        self.width = pos * HB
        self.src_blocks = [src + k for _, src, width in groups for k in range(width)]

    def __getitem__(self, name):
        return self.start[name]


_ALL_GROUPS = (("qa", COL_QA, A_HEADS), ("ka", COL_KA, A_HEADS), ("va", COL_VA, A_HEADS),
               ("ga", COL_GA, A_HEADS), ("qb", COL_QB, B_HEADS), ("kb", COL_KB, B_HEADS),
               ("vb", COL_VB, B_HEADS), ("gb", COL_GB, B_HEADS), ("xc", COL_XC, C_WIDTH // HB),
               ("gc", COL_GC, C_WIDTH // HB))
FULL_COLS = ProjCols(_ALL_GROUPS)
KV_COLS = ProjCols(tuple(g for g in _ALL_GROUPS if g[0] in ("ka", "va", "kb", "vb", "xc")))


def input_projection(x2d, norm_w3, mod3, w_bf16, layer, rows_per_mod, mod_row0, cols=FULL_COLS):
    m = x2d.shape[0]
    tm = 512
    tn = 2048 if cols is FULL_COLS else 512
    per_tile = tn // HB
    n_tiles = cols.width // tn
    src_tiles = [cols.src_blocks[t * per_tile] // per_tile for t in range(n_tiles)]
    assert all(cols.src_blocks[t * per_tile + k] == src_tiles[t] * per_tile + k
               for t in range(n_tiles) for k in range(per_tile))

    def src_tile(j):
        t = j + src_tiles[0]
        for k in range(1, n_tiles):
            jump = (src_tiles[k] - k) - (src_tiles[k - 1] - (k - 1))
            if jump:
                t = t + jnp.where(j >= k, jump, 0)
        return t

    n_rows = m // tm
    xs = tm // _NORM_SLICES
    assert n_tiles >= _NORM_SLICES and xs % _NORM_CHUNK == 0

    def norm_tile(r):
        return jnp.minimum(r, n_rows - 1)

    def mod_row(r):
        return mod_row0 + (norm_tile(r) * tm) // rows_per_mod

    return pl.pallas_call(
        _inproj_kernel,
        out_shape=jax.ShapeDtypeStruct((m, cols.width), BF16),
        grid=(n_rows + 1, n_tiles),
        in_specs=[
            pl.BlockSpec((xs, D_MODEL),
                         lambda r, j: (norm_tile(r) * _NORM_SLICES + jnp.minimum(j, _NORM_SLICES - 1), 0)),
            pl.BlockSpec((None, 1, D_MODEL), lambda r, j: (layer, 0, 0)),
            pl.BlockSpec((1, 1, D_MODEL), lambda r, j: (mod_row(r), 0, 0)),
            pl.BlockSpec((1, 1, D_MODEL), lambda r, j: (mod_row(r), 0, 1)),
            pl.BlockSpec((None, D_MODEL, tn), lambda r, j: (layer, 0, src_tile(j))),
        ],
        out_specs=pl.BlockSpec((tm, tn), lambda r, j: (jnp.maximum(r - 1, 0), jnp.where(r == 0, 0, j))),
        scratch_shapes=[pltpu.VMEM((tm, D_MODEL), BF16), pltpu.VMEM((tm, D_MODEL), BF16)],
        compiler_params=_params(("arbitrary", "arbitrary")),
        name="input_projection",
    )(x2d, norm_w3, mod3, mod3, w_bf16)


def _rope(x, cos, sin_lo, sin_hi):
    return (x * cos + pltpu.roll(x, HEAD_DIM - 16, axis=1) * sin_lo
            + pltpu.roll(x, 16, axis=1) * sin_hi)


def _diff_lambda(lq_ref, lam_init):
    lq = lq_ref[...]
    a = jnp.sum(lq[0:1] * lq[1:2], axis=-1, keepdims=True)
    b = jnp.sum(lq[2:3] * lq[3:4], axis=-1, keepdims=True)
    return jnp.exp(a) - jnp.exp(b) + lam_init


LOG2E = 1.4426950408889634
NKEYS = CTX_LEN + SEQ
DA_TQ = 256
DA_W = 2 * DA_TQ
DA_ROWS = 16
DA_KCHUNK = 256


def _transpose_bf16(x):
    return x.astype(F32).T.astype(BF16)


def _diffattn_lat_kernel(one_ref, q_ref, kc_ref, vc_ref, kl_ref, vl_ref, g_ref, cos_ref, slo_ref, shi_ref,
                         lq_ref, sw_ref, o_ref, k_scr, vt_scr, qt_scr, s0_scr, s1_scr, m_scr, p_scr,
                         *, lam_init):
    n_tiles = SEQ // DA_TQ
    n_chunks = NKEYS // DA_KCHUNK
    groups_per_chunk = DA_KCHUNK // DA_ROWS

    def score_chunk(s_scr, qt2, c, m_run):
        rows = slice(c * DA_KCHUNK, (c + 1) * DA_KCHUNK)
        s = jnp.dot(k_scr[rows, :], qt2, preferred_element_type=F32)
        s_scr[rows, :] = s
        parts = [s[i * DA_ROWS:(i + 1) * DA_ROWS, :] for i in range(groups_per_chunk)]
        if m_run is not None:
            parts.append(m_run)
        while len(parts) > 1:
            parts = [jnp.maximum(parts[i], parts[i + 1]) for i in range(0, len(parts) - 1, 2)] + (
                [parts[-1]] if len(parts) % 2 else [])
        return parts[0]

    def colmax(m_run):
        return jnp.broadcast_to(jnp.max(m_run, axis=0, keepdims=True), (DA_ROWS, DA_W))

    def always(k):
        return one_ref[0] > -k

    @pl.when(always(0))
    def _():
        assert DA_KCHUNK == DA_TQ == CTX_LEN
        dim = lax.broadcasted_iota(jnp.int32, (HB, DA_TQ), 0)

        def stage_queries(c):
            rows = slice(c * DA_TQ, (c + 1) * DA_TQ)
            q = _rope(q_ref[rows, :].astype(F32), cos_ref[rows, :], slo_ref[rows, :], shi_ref[rows, :])
            qt = (q * (A_QK_DIM ** -0.5 * LOG2E)).T
            qt_scr[:, c * DA_W:c * DA_W + DA_TQ] = jnp.where(dim < A_QK_DIM, qt, 0.0).astype(BF16)
            qt_scr[:, c * DA_W + DA_TQ:(c + 1) * DA_W] = jnp.where(dim >= A_QK_DIM, qt, 0.0).astype(BF16)

        stage_queries(0)
        qt0 = qt_scr[:, 0:DA_W]
        k_scr[0:CTX_LEN, :] = kc_ref[...]
        vt_scr[:, 0:CTX_LEN] = _transpose_bf16(vc_ref[...])
        m_run = score_chunk(s0_scr, qt0, 0, None)
        for c in range(SEQ // DA_TQ):
            rows = slice(c * DA_TQ, (c + 1) * DA_TQ)
            dst = slice(CTX_LEN + c * DA_TQ, CTX_LEN + (c + 1) * DA_TQ)
            k_scr[dst, :] = _rope(kl_ref[rows, :].astype(F32), cos_ref[rows, :], slo_ref[rows, :],
                                  shi_ref[rows, :]).astype(BF16)
            m_run = score_chunk(s0_scr, qt0, c + 1, m_run)
            vt_scr[:, dst] = _transpose_bf16(vl_ref[rows, :])
            if c > 0:
                stage_queries(c)
        m_scr[...] = colmax(m_run)

    def tile(t, s_scr, s_next_scr):
        rows = slice(t * DA_TQ, (t + 1) * DA_TQ)
        mb = m_scr[...]
        if s_next_scr is not None:
            qt_next = qt_scr[:, (t + 1) * DA_W:(t + 2) * DA_W]
        m_run = None
        l_acc = [None, None]
        acc = None

        def value_chunk(c, acc):
            rows = slice(c * DA_KCHUNK, (c + 1) * DA_KCHUNK)
            part = jnp.dot(vt_scr[:, rows], p_scr[rows, :], preferred_element_type=F32)
            return part if acc is None else acc + part

        for c in range(n_chunks):
            if s_next_scr is not None:
                m_run = score_chunk(s_next_scr, qt_next, c, m_run)
            for r in range(c * groups_per_chunk, (c + 1) * groups_per_chunk):
                grp = slice(r * DA_ROWS, (r + 1) * DA_ROWS)
                p = jnp.exp2(s_scr[grp, :] - mb)
                p_scr[grp, :] = p.astype(BF16)
                l_acc[r % 2] = p if l_acc[r % 2] is None else l_acc[r % 2] + p
            acc = value_chunk(c, acc)
        l = jnp.sum(l_acc[0] + l_acc[1], axis=0, keepdims=True)

        lam = _diff_lambda(lq_ref, lam_init)
        ot = acc[:, :DA_TQ] * (1.0 / l[:, :DA_TQ]) - acc[:, DA_TQ:] * (lam / l[:, DA_TQ:])
        ot = ot * lax.rsqrt(jnp.mean(ot * ot, axis=0, keepdims=True) + SUBLN_EPS)
        o = ot.T * sw_ref[...] * (1.0 - lam_init)
        o_ref[rows, :] = (o * _silu(g_ref[rows, :].astype(F32))).astype(o_ref.dtype)
        if s_next_scr is not None:
            m_scr[...] = colmax(m_run)

    bufs = (s0_scr, s1_scr)
    for t in range(n_tiles):
        nxt = bufs[(t + 1) % 2] if t + 1 < n_tiles else None
        pl.when(always(t + 1))(functools.partial(tile, t, bufs[t % 2], nxt))


_NT = (((1,), (1,)), ((), ()))


def _diffattn_ctx_kernel(q_ref, k_ref, v_ref, g_ref, lq_ref, sw_ref, o_ref, *, lam_init):
    q = q_ref[...].astype(F32) * (A_QK_DIM ** -0.5)
    lane = lax.broadcasted_iota(jnp.int32, q.shape, 1)
    outs = []
    for qn in (jnp.where(lane < A_QK_DIM, q, 0.0), jnp.where(lane >= A_QK_DIM, q, 0.0)):
        s = lax.dot_general(qn.astype(BF16), k_ref[...], _NT, preferred_element_type=F32)
        p = jnp.exp(s - jnp.max(s, axis=-1, keepdims=True))
        acc = jnp.dot(p.astype(BF16), v_ref[...], preferred_element_type=F32)
        outs.append((acc, jnp.sum(p, axis=-1, keepdims=True)))
    lam = _diff_lambda(lq_ref, lam_init)
    o = outs[0][0] * (1.0 / outs[0][1]) - (lam / outs[1][1]) * outs[1][0]
    o = o * lax.rsqrt(jnp.mean(o * o, axis=-1, keepdims=True) + SUBLN_EPS) * sw_ref[...]
    o = o * (1.0 - lam_init)
    o_ref[...] = (o * _silu(g_ref[...].astype(F32))).astype(o_ref.dtype)


def diff_attention_latent(proj_l, proj_c, ccols, rope_tabs, lambda_qk, subln_w, lam_init):
    head = lambda col: (lambda b, h, one: (b, col + h))
    const = lambda b, h, one: (0, 0)
    grid_spec = pltpu.PrefetchScalarGridSpec(
        num_scalar_prefetch=1,
        grid=(BATCH, A_HEADS),
        in_specs=[
            pl.BlockSpec((SEQ, HB), head(COL_QA)),
            pl.BlockSpec((CTX_LEN, HB), head(ccols["ka"])),
            pl.BlockSpec((CTX_LEN, HB), head(ccols["va"])),
            pl.BlockSpec((SEQ, HB), head(COL_KA)),
            pl.BlockSpec((SEQ, HB), head(COL_VA)),
            pl.BlockSpec((SEQ, HB), head(COL_GA)),
            pl.BlockSpec((SEQ, HB), const),
            pl.BlockSpec((SEQ, HB), const),
            pl.BlockSpec((SEQ, HB), const),
            pl.BlockSpec((4, A_QK_DIM), const),
            pl.BlockSpec((1, HB), const),
        ],
        out_specs=pl.BlockSpec((SEQ, HB), lambda b, h, one: (b, h)),
        scratch_shapes=[pltpu.VMEM((NKEYS, HB), BF16), pltpu.VMEM((HB, NKEYS), BF16),
                        pltpu.VMEM((HB, 2 * SEQ), BF16),
                        pltpu.VMEM((NKEYS, DA_W), F32), pltpu.VMEM((NKEYS, DA_W), F32),
                        pltpu.VMEM((DA_ROWS, DA_W), F32), pltpu.VMEM((NKEYS, DA_W), BF16)],
    )
    return pl.pallas_call(
        functools.partial(_diffattn_lat_kernel, lam_init=lam_init),
        out_shape=jax.ShapeDtypeStruct((BATCH * SEQ, A_WIDTH), BF16),
        grid_spec=grid_spec,
        compiler_params=_params(("arbitrary", "arbitrary")),
        name="diff_attention_latent",
    )(jnp.ones((1,), jnp.int32), proj_l, proj_c, proj_c, proj_l, proj_l, proj_l, *rope_tabs,
      lambda_qk, subln_w.reshape(1, HB))


def diff_attention_context(proj_c, lambda_qk, subln_w, lam_init):
    head = lambda col: (lambda b, h: (b, col + h))
    const = lambda b, h: (0, 0)
    return pl.pallas_call(
        functools.partial(_diffattn_ctx_kernel, lam_init=lam_init),
        out_shape=jax.ShapeDtypeStruct((BATCH * CTX_LEN, A_WIDTH), BF16),
        grid=(BATCH, A_HEADS),
        in_specs=[pl.BlockSpec((CTX_LEN, HB), head(COL_QA)),
                  pl.BlockSpec((CTX_LEN, HB), head(COL_KA)),
                  pl.BlockSpec((CTX_LEN, HB), head(COL_VA)),
                  pl.BlockSpec((CTX_LEN, HB), head(COL_GA)),
                  pl.BlockSpec((4, A_QK_DIM), const),
                  pl.BlockSpec((1, HB), const)],
        out_specs=pl.BlockSpec((CTX_LEN, HB), lambda b, h: (b, h)),
        compiler_params=_params(("arbitrary", "arbitrary")),
        name="diff_attention_context",
    )(proj_c, proj_c, proj_c, proj_c, lambda_qk, subln_w.reshape(1, HB))


def _nbr_slab_start(g):
    return min(max(g * NB_QROWS - NA_KH // 2, 0), GRID_H - NB_KROWS)


def _nbr_pattern(g):
    return 0 if g == 0 else (2 if g == NB_GROUPS - 1 else 1)


def _nbr_kernel(q_ref, kc_ref, vc_ref, kl_ref, vl_ref, g_ref, bias_ref, o_ref, vt_scr):
    vt_scr[:, 0:CTX_LEN] = _transpose_bf16(vc_ref[...])
    for c in range(SEQ // NB_Q):
        vt_scr[:, CTX_LEN + c * NB_Q:CTX_LEN + (c + 1) * NB_Q] = _transpose_bf16(vl_ref[c * NB_Q:(c + 1) * NB_Q, :])
    def scores(g):
        rows = slice(g * NB_Q, (g + 1) * NB_Q)
        k0 = _nbr_slab_start(g) * GRID_W
        qt = (q_ref[rows, :].astype(F32) * (HEAD_DIM ** -0.5 * LOG2E)).T.astype(BF16)
        s_c = jnp.dot(kc_ref[...], qt, preferred_element_type=F32)
        s_n = (jnp.dot(kl_ref[k0:k0 + NB_K, :], qt, preferred_element_type=F32)
               + bias_ref[0, _nbr_pattern(g)])
        return s_c, s_n

    def softmax(s_c, s_n):
        m = jnp.maximum(jnp.max(s_c, axis=0, keepdims=True), jnp.max(s_n, axis=0, keepdims=True))
        p_c = jnp.exp2(s_c - m)
        p_n = jnp.exp2(s_n - m)
        l = jnp.sum(p_c, axis=0, keepdims=True) + jnp.sum(p_n, axis=0, keepdims=True)
        return p_c.astype(BF16), p_n.astype(BF16), l

    def finish(g, p_c, p_n, l):
        rows = slice(g * NB_Q, (g + 1) * NB_Q)
        k0 = _nbr_slab_start(g) * GRID_W
        acc = (jnp.dot(vt_scr[:, 0:CTX_LEN], p_c, preferred_element_type=F32)
               + jnp.dot(vt_scr[:, CTX_LEN + k0:CTX_LEN + k0 + NB_K], p_n,
                         preferred_element_type=F32))
        o = (acc * (1.0 / l)).T
        o_ref[rows, :] = (o * _silu(g_ref[rows, :].astype(F32))).astype(o_ref.dtype)

    ahead = scores(0)
    probs = None
    for g in range(NB_GROUPS):
        s_c, s_n = ahead
        if g + 1 < NB_GROUPS:
            ahead = scores(g + 1)
        new_probs = softmax(s_c, s_n)
        if probs is not None:
            finish(g - 1, *probs)
        probs = new_probs
    finish(NB_GROUPS - 1, *probs)


def neighbourhood_bias_table(rpb):
    w = GRID_W
    cq = np.arange(w)
    c0 = np.clip(cq - NA_KW // 2, 0, w - NA_KW)
    ck = np.arange(w)
    col_valid = (ck[None, :] >= c0[:, None]) & (ck[None, :] < c0[:, None] + NA_KW)
    pad = w - NA_KW
    padded = jnp.pad(rpb.astype(F32), ((0, 0), (0, 0), (pad, pad)))
    toep = jnp.stack([padded[:, :, w - 1 - q:2 * w - 1 - q] for q in range(w)], axis=2)
    toep = jnp.where(col_valid[None, None], toep, NEG_INF)
    toep_t = jnp.swapaxes(toep, 2, 3) * LOG2E
    masked = jnp.full((B_HEADS, w, w), NEG_INF, F32)
    tabs = []
    for g in (0, 1, NB_GROUPS - 1):
        slab0 = _nbr_slab_start(g)
        qr = g * NB_QROWS + np.arange(NB_QROWS)
        kr = slab0 + np.arange(NB_KROWS)
        win0 = np.clip(qr - NA_KH // 2, 0, GRID_H - NA_KH)
        row_valid = (kr[None, :] >= win0[:, None]) & (kr[None, :] < win0[:, None] + NA_KH)
        dr_idx = kr[None, :] - qr[:, None] + NA_KH - 1
        rows = [jnp.concatenate([toep_t[:, int(dr_idx[a, k])] if row_valid[a, k] else masked
                                 for a in range(NB_QROWS)], axis=-1) for k in range(NB_KROWS)]
        tabs.append(jnp.concatenate(rows, axis=1))
    return jnp.stack(tabs, axis=1)


def neighbourhood_attention(proj_l, proj_c, ccols, bias_tab):
    head = lambda col: (lambda h, b: (b, col + h))
    return pl.pallas_call(
        _nbr_kernel,
        out_shape=jax.ShapeDtypeStruct((BATCH * SEQ, B_WIDTH), BF16),
        grid=(B_HEADS, BATCH),
        in_specs=[
            pl.BlockSpec((SEQ, HB), head(COL_QB)),
            pl.BlockSpec((CTX_LEN, HB), head(ccols["kb"])),
            pl.BlockSpec((CTX_LEN, HB), head(ccols["vb"])),
            pl.BlockSpec((SEQ, HB), head(COL_KB)),
            pl.BlockSpec((SEQ, HB), head(COL_VB)),
            pl.BlockSpec((SEQ, HB), head(COL_GB)),
            pl.BlockSpec((1, 3, NB_K, NB_Q), lambda h, b: (h, 0, 0, 0)),
        ],
        out_specs=pl.BlockSpec((SEQ, HB), lambda h, b: (b, h)),
        scratch_shapes=[pltpu.VMEM((HB, NKEYS), BF16)],
        compiler_params=_params(("arbitrary", "arbitrary")),
        name="neighbourhood_attention",
    )(proj_l, proj_c, proj_c, proj_l, proj_l, proj_l, bias_tab)


def _dense_ctx_kernel(q_ref, k_ref, v_ref, g_ref, o_ref):
    scale = HEAD_DIM ** -0.5
    s = lax.dot_general(q_ref[...], k_ref[...], _NT, preferred_element_type=F32) * scale
    p = jnp.exp(s - jnp.max(s, axis=-1, keepdims=True))
    p = p * (1.0 / jnp.sum(p, axis=-1, keepdims=True))
    o = jnp.dot(p.astype(BF16), v_ref[...], preferred_element_type=F32)
    o_ref[...] = (o * _silu(g_ref[...].astype(F32))).astype(o_ref.dtype)


def dense_context_attention(proj_c):
    head = lambda col: (lambda b, h: (b, col + h))
    return pl.pallas_call(
        _dense_ctx_kernel,
        out_shape=jax.ShapeDtypeStruct((BATCH * CTX_LEN, B_WIDTH), BF16),
        grid=(BATCH, B_HEADS),
        in_specs=[pl.BlockSpec((CTX_LEN, HB), head(COL_QB)),
                  pl.BlockSpec((CTX_LEN, HB), head(COL_KB)),
                  pl.BlockSpec((CTX_LEN, HB), head(COL_VB)),
                  pl.BlockSpec((CTX_LEN, HB), head(COL_GB))],
        out_specs=pl.BlockSpec((CTX_LEN, HB), lambda b, h: (b, h)),
        compiler_params=_params(("arbitrary", "arbitrary")),
        name="dense_context_attention",
    )(proj_c, proj_c, proj_c, proj_c)


def _shift_rows(x, d, fill):
    n = x.shape[0]
    if d % SUBLANES == 0:
        pad = jnp.full((abs(d), x.shape[1]), fill, x.dtype)
        return jnp.concatenate([pad, x[:n - d]] if d > 0 else [x[-d:], pad], axis=0)
    rolled = pltpu.roll(x, d % n, axis=0)
    row = lax.broadcasted_iota(jnp.int32, x.shape, 0)
    keep = (row >= d) if d > 0 else (row < n + d)
    return jnp.where(keep, rolled, fill)


def _dwconv(x, w_ref, b_ref):
    acc = b_ref[...] + w_ref[CONV_LEFT:CONV_LEFT + 1, :] * x
    for j in range(CONV_W):
        if j != CONV_LEFT:
            acc = acc + w_ref[j:j + 1, :] * _shift_rows(x, CONV_LEFT - j, 0.0)
    return acc


def _linear_scan(a, b, h0, reverse):
    n = a.shape[0]
    row = lax.broadcasted_iota(jnp.int32, a.shape, 0)
    first = (row == n - 1) if reverse else (row == 0)
    b = b + jnp.where(first, a * h0, 0.0)
    d = 1
    while d < n:
        sd = -d if reverse else d
        b = b + a * _shift_rows(b, sd, 0.0)
        if 2 * d < n:
            a = a * _shift_rows(a, sd, 1.0)
        d *= 2
    return b


def _sigmoid_tanh(x):
    return 0.5 * jnp.tanh(0.5 * x) + 0.5


def _rglru_gates(u, wa_ref, ba_ref, wx_ref, bx_ref, lam_ref, d):
    ub = u.astype(BF16)
    r = _sigmoid_tanh(jnp.dot(ub, wa_ref[0, d], preferred_element_type=F32) + ba_ref[0, d])
    i = _sigmoid_tanh(jnp.dot(ub, wx_ref[0, d], preferred_element_type=F32) + bx_ref[0, d])
    neg_lam = -lam_ref[0, d]
    softplus = jnp.maximum(neg_lam, 0.0) + jnp.log1p(jnp.exp(-jnp.abs(neg_lam)))
    log_a = (-RGLRU_C) * r * softplus
    a = jnp.exp(log_a)
    t = jnp.tanh(log_a)
    b = jnp.sqrt(-2.0 * t / (1.0 - t)) * (i * u)
    return a, b


SCAN_CHUNKS = SUBLANES


def _scan_chunk_len(n):
    return n // SCAN_CHUNKS + 1


def _blocked_scan(a_scr, b_scr, n, h0, reverse, need_states):
    length = _scan_chunk_len(n)
    order = range(length - 1, -1, -1) if reverse else range(length)
    h = jnp.zeros((SCAN_CHUNKS, a_scr.shape[1]), F32)
    p = jnp.ones_like(h)
    hs, ps = [None] * length, [None] * length
    for i in order:
        va = a_scr[pl.ds(i, SCAN_CHUNKS, stride=length), :]
        vb = b_scr[pl.ds(i, SCAN_CHUNKS, stride=length), :]
        h = va * h + vb
        p = va * p
        hs[i], ps[i] = h, p
    rows = [None] * SCAN_CHUNKS
    carry = h0
    for j in (range(SCAN_CHUNKS - 1, -1, -1) if reverse else range(SCAN_CHUNKS)):
        rows[j] = carry
        carry = h[j:j + 1, :] + p[j:j + 1, :] * carry
    if not need_states:
        return None, carry
    start = jnp.concatenate(rows, axis=0)
    return [hs[i] + ps[i] * start for i in range(length)], carry


def _rglru_kernel(*refs, need_ctx):
    if need_ctx:
        (xl_ref, xc_ref, gl_ref, gc_ref, cw_ref, cb_ref, wa_ref, ba_ref, wx_ref, bx_ref, lam_ref,
         ol_ref, oc_ref, al_scr, bl_scr, ac_scr, bc_scr, yl_scr, yc_scr) = refs
    else:
        (xl_ref, xc_ref, gl_ref, cw_ref, cb_ref, wa_ref, ba_ref, wx_ref, bx_ref, lam_ref,
         ol_ref, al_scr, bl_scr, ac_scr, bc_scr, yl_scr, yc_scr) = refs
    ct = al_scr.shape[1]
    u_l = _dwconv(xl_ref[...].astype(F32), cw_ref, cb_ref)
    u_c = _dwconv(xc_ref[...].astype(F32), cw_ref, cb_ref)
    for a_scr, b_scr, n in ((al_scr, bl_scr, SEQ), (ac_scr, bc_scr, CTX_LEN)):
        a_scr[n:n + SUBLANES, :] = jnp.ones((SUBLANES, ct), F32)
        b_scr[n:n + SUBLANES, :] = jnp.zeros((SUBLANES, ct), F32)
    y_l = None
    y_c = None
    for d in range(2):
        rev = d == 1
        a_c, b_c = _rglru_gates(u_c, wa_ref, ba_ref, wx_ref, bx_ref, lam_ref, d)
        ac_scr[0:CTX_LEN, :] = a_c
        bc_scr[0:CTX_LEN, :] = b_c
        h_c, fin = _blocked_scan(ac_scr, bc_scr, CTX_LEN, jnp.zeros((1, ct), F32), rev, need_ctx)
        a_l, b_l = _rglru_gates(u_l, wa_ref, ba_ref, wx_ref, bx_ref, lam_ref, d)
        al_scr[0:SEQ, :] = a_l
        bl_scr[0:SEQ, :] = b_l
        h_l, _ = _blocked_scan(al_scr, bl_scr, SEQ, fin, rev, True)
        y_l = h_l if y_l is None else [f + r for f, r in zip(y_l, h_l)]
        if need_ctx:
            y_c = h_c if y_c is None else [f + r for f, r in zip(y_c, h_c)]
    length = _scan_chunk_len(SEQ)
    for i in range(length):
        yl_scr[pl.ds(i, SCAN_CHUNKS, stride=length), :] = y_l[i]
    ol_ref[...] = (yl_scr[0:SEQ, :] * _silu(gl_ref[...].astype(F32))).astype(ol_ref.dtype)
    if need_ctx:
        length = _scan_chunk_len(CTX_LEN)
        for i in range(length):
            yc_scr[pl.ds(i, SCAN_CHUNKS, stride=length), :] = y_c[i]
        oc_ref[...] = (yc_scr[0:CTX_LEN, :] * _silu(gc_ref[...].astype(F32))).astype(oc_ref.dtype)


def _block_diag(w, per_tile):
    n = w.shape[-1]
    tiles = C_BLOCKS // per_tile
    w = w.reshape(2, tiles, per_tile, n, n)
    eye = jnp.eye(per_tile, dtype=w.dtype)
    bd = jnp.einsum('dtpij,pq->dtpiqj', w, eye).reshape(2, tiles, per_tile * n, per_tile * n)
    return jnp.transpose(bd, (1, 0, 2, 3)).astype(BF16)


def rglru_mixer(proj_l, proj_c, ccols, conv_w, conv_b, rg_wa, rg_ba, rg_wx, rg_bx, rg_lambda, need_ctx):
    ct = 128
    nt = C_WIDTH // ct
    per_tile = ct // C_BLOCK_DIM
    vec = lambda p: jnp.transpose(p.reshape(2, nt, 1, ct), (1, 0, 2, 3))
    wa = _block_diag(rg_wa, per_tile)
    wx = _block_diag(rg_wx, per_tile)
    xcol = lambda col: (lambda b, t: (b, col + t))
    par = lambda b, t: (t, 0, 0, 0)
    out_shape = [jax.ShapeDtypeStruct((BATCH * SEQ, C_WIDTH), BF16)]
    out_specs = [pl.BlockSpec((SEQ, ct), lambda b, t: (b, t))]
    if need_ctx:
        out_shape.append(jax.ShapeDtypeStruct((BATCH * CTX_LEN, C_WIDTH), BF16))
        out_specs.append(pl.BlockSpec((CTX_LEN, ct), lambda b, t: (b, t)))
    in_specs = [pl.BlockSpec((SEQ, ct), xcol(COL_XC)),
                pl.BlockSpec((CTX_LEN, ct), xcol(ccols["xc"])),
                pl.BlockSpec((SEQ, ct), xcol(COL_GC))]
    args = [proj_l, proj_c, proj_l]
    if need_ctx:
        in_specs.append(pl.BlockSpec((CTX_LEN, ct), xcol(ccols["gc"])))
        args.append(proj_c)
    in_specs += [pl.BlockSpec((CONV_W, ct), lambda b, t: (0, t)),
                 pl.BlockSpec((1, ct), lambda b, t: (0, t)),
                 pl.BlockSpec((1, 2, ct, ct), par),
                 pl.BlockSpec((1, 2, 1, ct), par),
                 pl.BlockSpec((1, 2, ct, ct), par),
                 pl.BlockSpec((1, 2, 1, ct), par),
                 pl.BlockSpec((1, 2, 1, ct), par)]
    args += [conv_w, conv_b.reshape(1, C_WIDTH), wa, vec(rg_ba), wx, vec(rg_bx), vec(rg_lambda)]
    outs = pl.pallas_call(
        functools.partial(_rglru_kernel, need_ctx=need_ctx),
        out_shape=out_shape,
        grid=(BATCH, nt),
        in_specs=in_specs,
        out_specs=out_specs,
        scratch_shapes=([pltpu.VMEM((SEQ + SUBLANES, ct), F32)] * 2
                        + [pltpu.VMEM((CTX_LEN + SUBLANES, ct), F32)] * 2
                        + [pltpu.VMEM((SEQ + SUBLANES, ct), F32), pltpu.VMEM((CTX_LEN + SUBLANES, ct), F32)]),
        compiler_params=_params(("arbitrary", "arbitrary")),
        name="rglru_mixer",
    )(*args)
    return outs if need_ctx else (outs[0], None)


def _outproj_kernel(x_ref, gate_ref, ma_ref, mb_ref, mc_ref, wa_ref, wb_ref, wc_ref, o_ref):
    y = jnp.dot(ma_ref[...], wa_ref[...], preferred_element_type=F32)
    y = y + jnp.dot(mb_ref[...], wb_ref[...], preferred_element_type=F32)
    y = y + jnp.dot(mc_ref[...], wc_ref[...], preferred_element_type=F32)
    o_ref[...] = x_ref[...] + gate_ref[0] * y


def output_projection(x2d, mod3, ma, mb, mc, w_bf16, layer, rows_per_mod, mod_row0):
    m = x2d.shape[0]
    tm, tn = 1024, 1024

    def mod_row(i):
        return mod_row0 + (i * tm) // rows_per_mod

    return pl.pallas_call(
        _outproj_kernel,
        out_shape=jax.ShapeDtypeStruct((m, D_MODEL), F32),
        grid=(m // tm, D_MODEL // tn),
        in_specs=[
            pl.BlockSpec((tm, tn), lambda i, j: (i, j)),
            pl.BlockSpec((1, 1, tn), lambda i, j: (mod_row(i), 0, 2 * (D_MODEL // tn) + j)),
            pl.BlockSpec((tm, A_WIDTH), lambda i, j: (i, 0)),
            pl.BlockSpec((tm, B_WIDTH), lambda i, j: (i, 0)),
            pl.BlockSpec((tm, C_WIDTH), lambda i, j: (i, 0)),
            pl.BlockSpec((None, A_WIDTH, tn), lambda i, j: (layer, 0, j)),
            pl.BlockSpec((None, B_WIDTH, tn), lambda i, j: (layer, 1, j)),
            pl.BlockSpec((None, C_WIDTH, tn), lambda i, j: (layer, (A_WIDTH + B_WIDTH) // C_WIDTH, j)),
        ],
        out_specs=pl.BlockSpec((tm, tn), lambda i, j: (i, j)),
        compiler_params=_params(("arbitrary", "arbitrary")),
        name="output_projection",
    )(x2d, mod3, ma, mb, mc, w_bf16, w_bf16, w_bf16)


def _final_norm_kernel(x_ref, w_ref, o_ref):
    x = x_ref[...]
    o_ref[...] = x * lax.rsqrt(jnp.mean(x * x, axis=-1, keepdims=True) + NORM_EPS) * w_ref[...]


def final_norm(x2d, w):
    tm = 256
    return pl.pallas_call(
        _final_norm_kernel,
        out_shape=jax.ShapeDtypeStruct(x2d.shape, F32),
        grid=(x2d.shape[0] // tm,),
        in_specs=[pl.BlockSpec((tm, D_MODEL), lambda i: (i, 0)),
                  pl.BlockSpec((1, D_MODEL), lambda i: (0, 0))],
        out_specs=pl.BlockSpec((tm, D_MODEL), lambda i: (i, 0)),
        compiler_params=_params(("arbitrary",)),
        name="final_norm",
    )(x2d, w.reshape(1, D_MODEL))


def _rope_tables():
    t = np.arange(SEQ)
    pos = np.stack([t // GRID_W, t % GRID_W], axis=0).astype(np.float32)
    lane = np.arange(HEAD_DIM)
    within = lane % A_QK_DIM
    axis = within // (A_QK_DIM // 2)
    k = within % (A_QK_DIM // 2)
    nfreq = A_QK_DIM // 4
    upper = k >= nfreq
    freqs = jnp.asarray(ROPE_THETA, F32) ** (-jnp.arange(nfreq, dtype=F32) / nfreq)
    ang = jnp.asarray(pos)[axis, :].T * freqs[k % nfreq][None, :]
    cos, sin = jnp.cos(ang), jnp.sin(ang)
    sin_lo = jnp.where(upper[None, :], 0.0, -sin)
    sin_hi = jnp.where(upper[None, :], sin, 0.0)
    return cos, sin_lo, sin_hi


def kernel(x, c, ctx, c_ctx, ada_w, ada_b, norm_w, w_in, w_out, lambda_qk, subln_w, rpb,
           conv_w, conv_b, rg_wa, rg_ba, rg_wx, rg_bx, rg_lambda, final_norm_w):
    cvec = jnp.zeros((MOD_ROWS, D_MODEL), F32).at[:BATCH].set(c).at[CTX_MOD_ROW].set(c_ctx)
    mod = adaln_modulation(cvec, ada_w, ada_b)
    mod3 = mod.reshape(DEPTH * MOD_ROWS, 1, 3 * D_MODEL)
    norm_w3 = norm_w.reshape(DEPTH, 1, D_MODEL)
    w_in_bf = w_in.astype(BF16)
    w_out_bf = w_out.astype(BF16)
    rope_tabs = _rope_tables()
    xl = x.reshape(BATCH * SEQ, D_MODEL)
    xc = ctx.reshape(BATCH * CTX_LEN, D_MODEL)
    for l in range(DEPTH):
        need_ctx = l < DEPTH - 1
        lam_init = 0.8 - 0.6 * math.exp(-0.3 * l)
        row_l, row_c = l * MOD_ROWS, l * MOD_ROWS + CTX_MOD_ROW
        proj_l = input_projection(xl, norm_w3, mod3, w_in_bf, l, SEQ, row_l)
        ccols = FULL_COLS if need_ctx else KV_COLS
        proj_c = input_projection(xc, norm_w3, mod3, w_in_bf, l, BATCH * CTX_LEN, row_c, ccols)

        ma = diff_attention_latent(proj_l, proj_c, ccols, rope_tabs, lambda_qk[l], subln_w[l], lam_init)
        mb = neighbourhood_attention(proj_l, proj_c, ccols, neighbourhood_bias_table(rpb[l]))
        mc, mc_c = rglru_mixer(proj_l, proj_c, ccols, conv_w[l], conv_b[l], rg_wa[l], rg_ba[l],
                               rg_wx[l], rg_bx[l], rg_lambda[l], need_ctx)
        xl = output_projection(xl, mod3, ma, mb, mc, w_out_bf, l, SEQ, row_l)
        if need_ctx:
            ma_c = diff_attention_context(proj_c, lambda_qk[l], subln_w[l], lam_init)
            mb_c = dense_context_attention(proj_c)
            xc = output_projection(xc, mod3, ma_c, mb_c, mc_c, w_out_bf, l, BATCH * CTX_LEN, row_c)
    return final_norm(xl, final_norm_w).reshape(BATCH, SEQ, D_MODEL)
```

```python
import functools
import math

import numpy as np
import jax
import jax.numpy as jnp
from jax import lax
from jax.experimental import pallas as pl
from jax.experimental.pallas import tpu as pltpu

F32 = jnp.float32
BF16 = jnp.bfloat16

D_MODEL = 4096
BATCH = 8
SEQ = 2048
DEPTH = 2
GRID_W = 64
GRID_H = SEQ // GRID_W
CTX_LEN = 256
HEAD_DIM = 128
A_WIDTH = (3 * D_MODEL) // 8
A_HEADS = A_WIDTH // HEAD_DIM
A_QK_DIM = HEAD_DIM // 2
B_WIDTH = (3 * D_MODEL) // 8
B_HEADS = B_WIDTH // HEAD_DIM
C_WIDTH = D_MODEL - A_WIDTH - B_WIDTH
C_BLOCKS = 16
C_BLOCK_DIM = C_WIDTH // C_BLOCKS
MIX_WIDTH = A_WIDTH + B_WIDTH + C_WIDTH
IN_WIDTH = 4 * A_WIDTH + 4 * B_WIDTH + 2 * C_WIDTH
NA_KH = 8
NA_KW = 16
ROPE_THETA = 10000.0
RGLRU_C = 8.0
CONV_W = 4
CONV_LEFT = 2
NORM_EPS = 1e-6
SUBLN_EPS = 1e-5
NEG_INF = -1e30

HB = HEAD_DIM
COL_QA, COL_KA, COL_VA, COL_GA = 0, A_HEADS, 2 * A_HEADS, 3 * A_HEADS
COL_QB = 4 * A_HEADS
COL_KB, COL_VB, COL_GB = COL_QB + B_HEADS, COL_QB + 2 * B_HEADS, COL_QB + 3 * B_HEADS
COL_XC = COL_QB + 4 * B_HEADS
COL_GC = COL_XC + C_WIDTH // HB

MOD_ROWS = 16
CTX_MOD_ROW = BATCH

VMEM_LIMIT = 56 * 1024 * 1024
SUBLANES = 8

NB_QROWS = 4
NB_KROWS = 12
NB_GROUPS = GRID_H // NB_QROWS
NB_Q = NB_QROWS * GRID_W
NB_K = NB_KROWS * GRID_W


def _params(sem):
    return pltpu.CompilerParams(dimension_semantics=sem, vmem_limit_bytes=VMEM_LIMIT)


def _sigmoid(x):
    return 1.0 / (1.0 + jnp.exp(-x))


def _silu(x):
    return x * _sigmoid(x)


def _adaln_kernel(c_ref, w_ref, b_ref, o_ref):
    s = _silu(c_ref[...]).astype(BF16)
    w = w_ref[0].astype(BF16)
    o_ref[0] = jnp.dot(s, w, preferred_element_type=F32) + b_ref[0]


def adaln_modulation(cvec, ada_w, ada_b):
    tn = 512
    n3 = 3 * D_MODEL
    return pl.pallas_call(
        _adaln_kernel,
        out_shape=jax.ShapeDtypeStruct((DEPTH, MOD_ROWS, n3), F32),
        grid=(DEPTH, n3 // tn),
        in_specs=[
            pl.BlockSpec((MOD_ROWS, D_MODEL), lambda l, j: (0, 0)),
            pl.BlockSpec((1, D_MODEL, tn), lambda l, j: (l, 0, j)),
            pl.BlockSpec((1, 1, tn), lambda l, j: (l, 0, j)),
        ],
        out_specs=pl.BlockSpec((1, MOD_ROWS, tn), lambda l, j: (l, 0, j)),
        compiler_params=_params(("arbitrary", "arbitrary")),
        name="adaln_modulation",
    )(cvec, ada_w, ada_b.reshape(DEPTH, 1, n3))


_NORM_CHUNK = 32
_NORM_SLICES = 4


def _inproj_kernel(x_ref, nw_ref, shift_ref, scale_ref, w_ref, o_ref, hx0_ref, hx1_ref):
    r, j = pl.program_id(0), pl.program_id(1)
    xs = x_ref.shape[0]
    gain = nw_ref[...] * (1.0 + scale_ref[0])
    add = shift_ref[0]

    def normalise_slice(hx_ref):
        base = jnp.minimum(j, _NORM_SLICES - 1) * xs
        for k in range(xs // _NORM_CHUNK):
            x = x_ref[k * _NORM_CHUNK:(k + 1) * _NORM_CHUNK, :]
            inv = lax.rsqrt(jnp.mean(x * x, axis=-1, keepdims=True) + NORM_EPS)
            rows = pl.ds(pl.multiple_of(base + k * _NORM_CHUNK, _NORM_CHUNK), _NORM_CHUNK)
            hx_ref[rows, :] = (x * inv * gain + add).astype(BF16)

    def step(hx_cur, hx_next):
        normalise_slice(hx_next)
        o_ref[...] = jnp.dot(hx_cur[...], w_ref[...], preferred_element_type=F32).astype(o_ref.dtype)

    @pl.when(r == 0)
    def _():
        normalise_slice(hx0_ref)

    pl.when(r % 2 == 1)(functools.partial(step, hx0_ref, hx1_ref))
    pl.when(jnp.logical_and(r % 2 == 0, r > 0))(functools.partial(step, hx1_ref, hx0_ref))


class ProjCols:
    def __init__(self, groups):
        self.start, pos = {}, 0
        for name, _, width in groups:
            self.start[name] = pos
            pos += width
        self.width = pos * HB
        self.src_blocks = [src + k for _, src, width in groups for k in range(width)]

    def __getitem__(self, name):
        return self.start[name]


_ALL_GROUPS = (("qa", COL_QA, A_HEADS), ("ka", COL_KA, A_HEADS), ("va", COL_VA, A_HEADS),
               ("ga", COL_GA, A_HEADS), ("qb", COL_QB, B_HEADS), ("kb", COL_KB, B_HEADS),
               ("vb", COL_VB, B_HEADS), ("gb", COL_GB, B_HEADS), ("xc", COL_XC, C_WIDTH // HB),
               ("gc", COL_GC, C_WIDTH // HB))
FULL_COLS = ProjCols(_ALL_GROUPS)
KV_COLS = ProjCols(tuple(g for g in _ALL_GROUPS if g[0] in ("ka", "va", "kb", "vb", "xc")))


def input_projection(x2d, norm_w3, mod3, w_bf16, layer, rows_per_mod, mod_row0, cols=FULL_COLS):
    m = x2d.shape[0]
    tm = 512
    tn = 2048 if cols is FULL_COLS else 512
    per_tile = tn // HB
    n_tiles = cols.width // tn
    src_tiles = [cols.src_blocks[t * per_tile] // per_tile for t in range(n_tiles)]
    assert all(cols.src_blocks[t * per_tile + k] == src_tiles[t] * per_tile + k
               for t in range(n_tiles) for k in range(per_tile))

    def src_tile(j):
        t = j + src_tiles[0]
        for k in range(1, n_tiles):
            jump = (src_tiles[k] - k) - (src_tiles[k - 1] - (k - 1))
            if jump:
                t = t + jnp.where(j >= k, jump, 0)
        return t

    n_rows = m // tm
    xs = tm // _NORM_SLICES
    assert n_tiles >= _NORM_SLICES and xs % _NORM_CHUNK == 0

    def norm_tile(r):
        return jnp.minimum(r, n_rows - 1)

    def mod_row(r):
        return mod_row0 + (norm_tile(r) * tm) // rows_per_mod

    def parked(r, j):
        return jnp.where(r == 0, 0, j)

    return pl.pallas_call(
        _inproj_kernel,
        out_shape=jax.ShapeDtypeStruct((m, cols.width), BF16),
        grid=(n_rows + 1, n_tiles),
        in_specs=[
            pl.BlockSpec((xs, D_MODEL),
                         lambda r, j: (norm_tile(r) * _NORM_SLICES + jnp.minimum(j, _NORM_SLICES - 1), 0)),
            pl.BlockSpec((None, 1, D_MODEL), lambda r, j: (layer, 0, 0)),
            pl.BlockSpec((1, 1, D_MODEL), lambda r, j: (mod_row(r), 0, 0)),
            pl.BlockSpec((1, 1, D_MODEL), lambda r, j: (mod_row(r), 0, 1)),
            pl.BlockSpec((None, D_MODEL, tn), lambda r, j: (layer, 0, src_tile(parked(r, j)))),
        ],
        out_specs=pl.BlockSpec((tm, tn), lambda r, j: (jnp.maximum(r - 1, 0), parked(r, j))),
        scratch_shapes=[pltpu.VMEM((tm, D_MODEL), BF16), pltpu.VMEM((tm, D_MODEL), BF16)],
        compiler_params=_params(("arbitrary", "arbitrary")),
        name="input_projection",
    )(x2d, norm_w3, mod3, mod3, w_bf16)


def _rope(x, cos, sin_lo, sin_hi):
    return (x * cos + pltpu.roll(x, HEAD_DIM - 16, axis=1) * sin_lo
            + pltpu.roll(x, 16, axis=1) * sin_hi)


def _diff_lambda(lq_ref, lam_init):
    lq = lq_ref[...]
    a = jnp.sum(lq[0:1] * lq[1:2], axis=-1, keepdims=True)
    b = jnp.sum(lq[2:3] * lq[3:4], axis=-1, keepdims=True)
    return jnp.exp(a) - jnp.exp(b) + lam_init


LOG2E = 1.4426950408889634
NKEYS = CTX_LEN + SEQ
DA_TQ = 256
DA_W = 2 * DA_TQ
DA_ROWS = 16
DA_KCHUNK = 256


def _transpose_bf16(x):
    return x.astype(F32).T.astype(BF16)


def _diffattn_lat_kernel(one_ref, q_ref, kc_ref, vc_ref, kl_ref, vl_ref, g_ref, cos_ref, slo_ref, shi_ref,
                         lq_ref, sw_ref, o_ref, k_scr, vt_scr, qt_scr, s0_scr, s1_scr, m_scr, p_scr,
                         *, lam_init):
    n_tiles = SEQ // DA_TQ
    n_chunks = NKEYS // DA_KCHUNK
    groups_per_chunk = DA_KCHUNK // DA_ROWS

    def score_chunk(s_scr, qt2, c, m_run):
        rows = slice(c * DA_KCHUNK, (c + 1) * DA_KCHUNK)
        s = jnp.dot(k_scr[rows, :], qt2, preferred_element_type=F32)
        s_scr[rows, :] = s
        parts = [s[i * DA_ROWS:(i + 1) * DA_ROWS, :] for i in range(groups_per_chunk)]
        if m_run is not None:
            parts.append(m_run)
        while len(parts) > 1:
            parts = [jnp.maximum(parts[i], parts[i + 1]) for i in range(0, len(parts) - 1, 2)] + (
                [parts[-1]] if len(parts) % 2 else [])
        return parts[0]

    def colmax(m_run):
        return jnp.broadcast_to(jnp.max(m_run, axis=0, keepdims=True), (DA_ROWS, DA_W))

    def always(k):
        return one_ref[0] > -k

    @pl.when(always(0))
    def _():
        assert DA_KCHUNK == DA_TQ == CTX_LEN
        dim = lax.broadcasted_iota(jnp.int32, (HB, DA_TQ), 0)

        def stage_queries(c):
            rows = slice(c * DA_TQ, (c + 1) * DA_TQ)
            q = _rope(q_ref[rows, :].astype(F32), cos_ref[rows, :], slo_ref[rows, :], shi_ref[rows, :])
            qt = (q * (A_QK_DIM ** -0.5 * LOG2E)).T
            qt_scr[:, c * DA_W:c * DA_W + DA_TQ] = jnp.where(dim < A_QK_DIM, qt, 0.0).astype(BF16)
            qt_scr[:, c * DA_W + DA_TQ:(c + 1) * DA_W] = jnp.where(dim >= A_QK_DIM, qt, 0.0).astype(BF16)

        stage_queries(0)
        qt0 = qt_scr[:, 0:DA_W]
        k_scr[0:CTX_LEN, :] = kc_ref[...]
        vt_scr[:, 0:CTX_LEN] = _transpose_bf16(vc_ref[...])
        m_run = score_chunk(s0_scr, qt0, 0, None)
        for c in range(SEQ // DA_TQ):
            rows = slice(c * DA_TQ, (c + 1) * DA_TQ)
            dst = slice(CTX_LEN + c * DA_TQ, CTX_LEN + (c + 1) * DA_TQ)
            k_scr[dst, :] = _rope(kl_ref[rows, :].astype(F32), cos_ref[rows, :], slo_ref[rows, :],
                                  shi_ref[rows, :]).astype(BF16)
            m_run = score_chunk(s0_scr, qt0, c + 1, m_run)
            vt_scr[:, dst] = _transpose_bf16(vl_ref[rows, :])
            if c > 0:
                stage_queries(c)
        m_scr[...] = colmax(m_run)

    def tile(t, s_scr, s_next_scr):
        rows = slice(t * DA_TQ, (t + 1) * DA_TQ)
        mb = m_scr[...]
        if s_next_scr is not None:
            qt_next = qt_scr[:, (t + 1) * DA_W:(t + 2) * DA_W]
        m_run = None
        l_acc = [None, None]
        acc = None

        def value_chunk(c, acc):
            rows = slice(c * DA_KCHUNK, (c + 1) * DA_KCHUNK)
            part = jnp.dot(vt_scr[:, rows], p_scr[rows, :], preferred_element_type=F32)
            return part if acc is None else acc + part

        for c in range(n_chunks):
            if s_next_scr is not None:
                m_run = score_chunk(s_next_scr, qt_next, c, m_run)
            for r in range(c * groups_per_chunk, (c + 1) * groups_per_chunk):
                grp = slice(r * DA_ROWS, (r + 1) * DA_ROWS)
                p = jnp.exp2(s_scr[grp, :] - mb)
                p_scr[grp, :] = p.astype(BF16)
                l_acc[r % 2] = p if l_acc[r % 2] is None else l_acc[r % 2] + p
            acc = value_chunk(c, acc)
        l = jnp.sum(l_acc[0] + l_acc[1], axis=0, keepdims=True)

        lam = _diff_lambda(lq_ref, lam_init)
        ot = acc[:, :DA_TQ] * (1.0 / l[:, :DA_TQ]) - acc[:, DA_TQ:] * (lam / l[:, DA_TQ:])
        ot = ot * lax.rsqrt(jnp.mean(ot * ot, axis=0, keepdims=True) + SUBLN_EPS)
        o = ot.T * sw_ref[...] * (1.0 - lam_init)
        o_ref[rows, :] = (o * _silu(g_ref[rows, :].astype(F32))).astype(o_ref.dtype)
        if s_next_scr is not None:
            m_scr[...] = colmax(m_run)

    bufs = (s0_scr, s1_scr)
    for t in range(n_tiles):
        nxt = bufs[(t + 1) % 2] if t + 1 < n_tiles else None
        pl.when(always(t + 1))(functools.partial(tile, t, bufs[t % 2], nxt))


_NT = (((1,), (1,)), ((), ()))


def _diffattn_ctx_kernel(q_ref, k_ref, v_ref, g_ref, lq_ref, sw_ref, o_ref, *, lam_init):
    q = q_ref[...].astype(F32) * (A_QK_DIM ** -0.5)
    lane = lax.broadcasted_iota(jnp.int32, q.shape, 1)
    outs = []
    for qn in (jnp.where(lane < A_QK_DIM, q, 0.0), jnp.where(lane >= A_QK_DIM, q, 0.0)):
        s = lax.dot_general(qn.astype(BF16), k_ref[...], _NT, preferred_element_type=F32)
        p = jnp.exp(s - jnp.max(s, axis=-1, keepdims=True))
        acc = jnp.dot(p.astype(BF16), v_ref[...], preferred_element_type=F32)
        outs.append((acc, jnp.sum(p, axis=-1, keepdims=True)))
    lam = _diff_lambda(lq_ref, lam_init)
    o = outs[0][0] * (1.0 / outs[0][1]) - (lam / outs[1][1]) * outs[1][0]
    o = o * lax.rsqrt(jnp.mean(o * o, axis=-1, keepdims=True) + SUBLN_EPS) * sw_ref[...]
    o = o * (1.0 - lam_init)
    o_ref[...] = (o * _silu(g_ref[...].astype(F32))).astype(o_ref.dtype)


def diff_attention_latent(proj_l, proj_c, ccols, rope_tabs, lambda_qk, subln_w, lam_init):
    head = lambda col: (lambda b, h, one: (b, col + h))
    const = lambda b, h, one: (0, 0)
    grid_spec = pltpu.PrefetchScalarGridSpec(
        num_scalar_prefetch=1,
        grid=(BATCH, A_HEADS),
        in_specs=[
            pl.BlockSpec((SEQ, HB), head(COL_QA)),
            pl.BlockSpec((CTX_LEN, HB), head(ccols["ka"])),
            pl.BlockSpec((CTX_LEN, HB), head(ccols["va"])),
            pl.BlockSpec((SEQ, HB), head(COL_KA)),
            pl.BlockSpec((SEQ, HB), head(COL_VA)),
            pl.BlockSpec((SEQ, HB), head(COL_GA)),
            pl.BlockSpec((SEQ, HB), const),
            pl.BlockSpec((SEQ, HB), const),
            pl.BlockSpec((SEQ, HB), const),
            pl.BlockSpec((4, A_QK_DIM), const),
            pl.BlockSpec((1, HB), const),
        ],
        out_specs=pl.BlockSpec((SEQ, HB), lambda b, h, one: (b, h)),
        scratch_shapes=[pltpu.VMEM((NKEYS, HB), BF16), pltpu.VMEM((HB, NKEYS), BF16),
                        pltpu.VMEM((HB, 2 * SEQ), BF16),
                        pltpu.VMEM((NKEYS, DA_W), F32), pltpu.VMEM((NKEYS, DA_W), F32),
                        pltpu.VMEM((DA_ROWS, DA_W), F32), pltpu.VMEM((NKEYS, DA_W), BF16)],
    )
    return pl.pallas_call(
        functools.partial(_diffattn_lat_kernel, lam_init=lam_init),
        out_shape=jax.ShapeDtypeStruct((BATCH * SEQ, A_WIDTH), BF16),
        grid_spec=grid_spec,
        compiler_params=_params(("arbitrary", "arbitrary")),
        name="diff_attention_latent",
    )(jnp.ones((1,), jnp.int32), proj_l, proj_c, proj_c, proj_l, proj_l, proj_l, *rope_tabs,
      lambda_qk, subln_w.reshape(1, HB))


def diff_attention_context(proj_c, lambda_qk, subln_w, lam_init):
    head = lambda col: (lambda b, h: (b, col + h))
    const = lambda b, h: (0, 0)
    return pl.pallas_call(
        functools.partial(_diffattn_ctx_kernel, lam_init=lam_init),
        out_shape=jax.ShapeDtypeStruct((BATCH * CTX_LEN, A_WIDTH), BF16),
        grid=(BATCH, A_HEADS),
        in_specs=[pl.BlockSpec((CTX_LEN, HB), head(COL_QA)),
                  pl.BlockSpec((CTX_LEN, HB), head(COL_KA)),
                  pl.BlockSpec((CTX_LEN, HB), head(COL_VA)),
                  pl.BlockSpec((CTX_LEN, HB), head(COL_GA)),
                  pl.BlockSpec((4, A_QK_DIM), const),
                  pl.BlockSpec((1, HB), const)],
        out_specs=pl.BlockSpec((CTX_LEN, HB), lambda b, h: (b, h)),
        compiler_params=_params(("arbitrary", "arbitrary")),
        name="diff_attention_context",
    )(proj_c, proj_c, proj_c, proj_c, lambda_qk, subln_w.reshape(1, HB))


def _nbr_slab_start(g):
    return min(max(g * NB_QROWS - NA_KH // 2, 0), GRID_H - NB_KROWS)


def _nbr_pattern(g):
    return 0 if g == 0 else (2 if g == NB_GROUPS - 1 else 1)


def _nbr_kernel(q_ref, kc_ref, vc_ref, kl_ref, vl_ref, g_ref, bias_ref, o_ref, vt_scr):
    vt_scr[:, 0:CTX_LEN] = _transpose_bf16(vc_ref[...])
    for c in range(SEQ // NB_Q):
        vt_scr[:, CTX_LEN + c * NB_Q:CTX_LEN + (c + 1) * NB_Q] = _transpose_bf16(vl_ref[c * NB_Q:(c + 1) * NB_Q, :])
    def scores(g):
        rows = slice(g * NB_Q, (g + 1) * NB_Q)
        k0 = _nbr_slab_start(g) * GRID_W
        qt = (q_ref[rows, :].astype(F32) * (HEAD_DIM ** -0.5 * LOG2E)).T.astype(BF16)
        s_c = jnp.dot(kc_ref[...], qt, preferred_element_type=F32)
        s_n = (jnp.dot(kl_ref[k0:k0 + NB_K, :], qt, preferred_element_type=F32)
               + bias_ref[0, _nbr_pattern(g)])
        return s_c, s_n

    def softmax(s_c, s_n):
        m = jnp.maximum(jnp.max(s_c, axis=0, keepdims=True), jnp.max(s_n, axis=0, keepdims=True))
        p_c = jnp.exp2(s_c - m)
        p_n = jnp.exp2(s_n - m)
        l = jnp.sum(p_c, axis=0, keepdims=True) + jnp.sum(p_n, axis=0, keepdims=True)
        return p_c.astype(BF16), p_n.astype(BF16), l

    def finish(g, p_c, p_n, l):
        rows = slice(g * NB_Q, (g + 1) * NB_Q)
        k0 = _nbr_slab_start(g) * GRID_W
        acc = (jnp.dot(vt_scr[:, 0:CTX_LEN], p_c, preferred_element_type=F32)
               + jnp.dot(vt_scr[:, CTX_LEN + k0:CTX_LEN + k0 + NB_K], p_n,
                         preferred_element_type=F32))
        o = (acc * (1.0 / l)).T
        o_ref[rows, :] = (o * _silu(g_ref[rows, :].astype(F32))).astype(o_ref.dtype)

    ahead = scores(0)
    probs = None
    for g in range(NB_GROUPS):
        s_c, s_n = ahead
        if g + 1 < NB_GROUPS:
            ahead = scores(g + 1)
        new_probs = softmax(s_c, s_n)
        if probs is not None:
            finish(g - 1, *probs)
        probs = new_probs
    finish(NB_GROUPS - 1, *probs)


def neighbourhood_bias_table(rpb):
    w = GRID_W
    cq = np.arange(w)
    c0 = np.clip(cq - NA_KW // 2, 0, w - NA_KW)
    ck = np.arange(w)
    col_valid = (ck[None, :] >= c0[:, None]) & (ck[None, :] < c0[:, None] + NA_KW)
    pad = w - NA_KW
    padded = jnp.pad(rpb.astype(F32), ((0, 0), (0, 0), (pad, pad)))
    toep = jnp.stack([padded[:, :, w - 1 - q:2 * w - 1 - q] for q in range(w)], axis=2)
    toep = jnp.where(col_valid[None, None], toep, NEG_INF)
    toep_t = jnp.swapaxes(toep, 2, 3) * LOG2E
    masked = jnp.full((B_HEADS, w, w), NEG_INF, F32)
    tabs = []
    for g in (0, 1, NB_GROUPS - 1):
        slab0 = _nbr_slab_start(g)
        qr = g * NB_QROWS + np.arange(NB_QROWS)
        kr = slab0 + np.arange(NB_KROWS)
        win0 = np.clip(qr - NA_KH // 2, 0, GRID_H - NA_KH)
        row_valid = (kr[None, :] >= win0[:, None]) & (kr[None, :] < win0[:, None] + NA_KH)
        dr_idx = kr[None, :] - qr[:, None] + NA_KH - 1
        rows = [jnp.concatenate([toep_t[:, int(dr_idx[a, k])] if row_valid[a, k] else masked
                                 for a in range(NB_QROWS)], axis=-1) for k in range(NB_KROWS)]
        tabs.append(jnp.concatenate(rows, axis=1))
    return jnp.stack(tabs, axis=1)


def neighbourhood_attention(proj_l, proj_c, ccols, bias_tab):
    head = lambda col: (lambda h, b: (b, col + h))
    return pl.pallas_call(
        _nbr_kernel,
        out_shape=jax.ShapeDtypeStruct((BATCH * SEQ, B_WIDTH), BF16),
        grid=(B_HEADS, BATCH),
        in_specs=[
            pl.BlockSpec((SEQ, HB), head(COL_QB)),
            pl.BlockSpec((CTX_LEN, HB), head(ccols["kb"])),
            pl.BlockSpec((CTX_LEN, HB), head(ccols["vb"])),
            pl.BlockSpec((SEQ, HB), head(COL_KB)),
            pl.BlockSpec((SEQ, HB), head(COL_VB)),
            pl.BlockSpec((SEQ, HB), head(COL_GB)),
            pl.BlockSpec((1, 3, NB_K, NB_Q), lambda h, b: (h, 0, 0, 0)),
        ],
        out_specs=pl.BlockSpec((SEQ, HB), lambda h, b: (b, h)),
        scratch_shapes=[pltpu.VMEM((HB, NKEYS), BF16)],
        compiler_params=_params(("arbitrary", "arbitrary")),
        name="neighbourhood_attention",
    )(proj_l, proj_c, proj_c, proj_l, proj_l, proj_l, bias_tab)


def _dense_ctx_kernel(q_ref, k_ref, v_ref, g_ref, o_ref):
    scale = HEAD_DIM ** -0.5
    s = lax.dot_general(q_ref[...], k_ref[...], _NT, preferred_element_type=F32) * scale
    p = jnp.exp(s - jnp.max(s, axis=-1, keepdims=True))
    p = p * (1.0 / jnp.sum(p, axis=-1, keepdims=True))
    o = jnp.dot(p.astype(BF16), v_ref[...], preferred_element_type=F32)
    o_ref[...] = (o * _silu(g_ref[...].astype(F32))).astype(o_ref.dtype)


def dense_context_attention(proj_c):
    head = lambda col: (lambda b, h: (b, col + h))
    return pl.pallas_call(
        _dense_ctx_kernel,
        out_shape=jax.ShapeDtypeStruct((BATCH * CTX_LEN, B_WIDTH), BF16),
        grid=(BATCH, B_HEADS),
        in_specs=[pl.BlockSpec((CTX_LEN, HB), head(COL_QB)),
                  pl.BlockSpec((CTX_LEN, HB), head(COL_KB)),
                  pl.BlockSpec((CTX_LEN, HB), head(COL_VB)),
                  pl.BlockSpec((CTX_LEN, HB), head(COL_GB))],
        out_specs=pl.BlockSpec((CTX_LEN, HB), lambda b, h: (b, h)),
        compiler_params=_params(("arbitrary", "arbitrary")),
        name="dense_context_attention",
    )(proj_c, proj_c, proj_c, proj_c)


def _shift_rows(x, d, fill):
    n = x.shape[0]
    if d % SUBLANES == 0:
        pad = jnp.full((abs(d), x.shape[1]), fill, x.dtype)
        return jnp.concatenate([pad, x[:n - d]] if d > 0 else [x[-d:], pad], axis=0)
    rolled = pltpu.roll(x, d % n, axis=0)
    row = lax.broadcasted_iota(jnp.int32, x.shape, 0)
    keep = (row >= d) if d > 0 else (row < n + d)
    return jnp.where(keep, rolled, fill)


def _dwconv(x, w_ref, b_ref):
    acc = b_ref[...] + w_ref[CONV_LEFT:CONV_LEFT + 1, :] * x
    for j in range(CONV_W):
        if j != CONV_LEFT:
            acc = acc + w_ref[j:j + 1, :] * _shift_rows(x, CONV_LEFT - j, 0.0)
    return acc


def _linear_scan(a, b, h0, reverse):
    n = a.shape[0]
    row = lax.broadcasted_iota(jnp.int32, a.shape, 0)
    first = (row == n - 1) if reverse else (row == 0)
    b = b + jnp.where(first, a * h0, 0.0)
    d = 1
    while d < n:
        sd = -d if reverse else d
        b = b + a * _shift_rows(b, sd, 0.0)
        if 2 * d < n:
            a = a * _shift_rows(a, sd, 1.0)
        d *= 2
    return b


def _sigmoid_tanh(x):
    return 0.5 * jnp.tanh(0.5 * x) + 0.5


def _rglru_gates(u, wa_ref, ba_ref, wx_ref, bx_ref, lam_ref, d):
    ub = u.astype(BF16)
    r = _sigmoid_tanh(jnp.dot(ub, wa_ref[0, d], preferred_element_type=F32) + ba_ref[0, d])
    i = _sigmoid_tanh(jnp.dot(ub, wx_ref[0, d], preferred_element_type=F32) + bx_ref[0, d])
    neg_lam = -lam_ref[0, d]
    softplus = jnp.maximum(neg_lam, 0.0) + jnp.log1p(jnp.exp(-jnp.abs(neg_lam)))
    log_a = (-RGLRU_C) * r * softplus
    a = jnp.exp(log_a)
    t = jnp.tanh(log_a)
    b = jnp.sqrt(-2.0 * t / (1.0 - t)) * (i * u)
    return a, b


SCAN_CHUNKS = SUBLANES


def _scan_chunk_len(n):
    return n // SCAN_CHUNKS + 1


def _blocked_scan(a_scr, b_scr, n, h0, reverse, need_states):
    length = _scan_chunk_len(n)
    order = range(length - 1, -1, -1) if reverse else range(length)
    h = jnp.zeros((SCAN_CHUNKS, a_scr.shape[1]), F32)
    p = jnp.ones_like(h)
    hs, ps = [None] * length, [None] * length
    for i in order:
        va = a_scr[pl.ds(i, SCAN_CHUNKS, stride=length), :]
        vb = b_scr[pl.ds(i, SCAN_CHUNKS, stride=length), :]
        h = va * h + vb
        p = va * p
        hs[i], ps[i] = h, p
    rows = [None] * SCAN_CHUNKS
    carry = h0
    for j in (range(SCAN_CHUNKS - 1, -1, -1) if reverse else range(SCAN_CHUNKS)):
        rows[j] = carry
        carry = h[j:j + 1, :] + p[j:j + 1, :] * carry
    if not need_states:
        return None, carry
    start = jnp.concatenate(rows, axis=0)
    return [hs[i] + ps[i] * start for i in range(length)], carry


def _rglru_kernel(*refs, need_ctx):
    if need_ctx:
        (xl_ref, xc_ref, gl_ref, gc_ref, cw_ref, cb_ref, wa_ref, ba_ref, wx_ref, bx_ref, lam_ref,
         ol_ref, oc_ref, al_scr, bl_scr, ac_scr, bc_scr, yl_scr, yc_scr) = refs
    else:
        (xl_ref, xc_ref, gl_ref, cw_ref, cb_ref, wa_ref, ba_ref, wx_ref, bx_ref, lam_ref,
         ol_ref, al_scr, bl_scr, ac_scr, bc_scr, yl_scr, yc_scr) = refs
    ct = al_scr.shape[1]
    u_l = _dwconv(xl_ref[...].astype(F32), cw_ref, cb_ref)
    u_c = _dwconv(xc_ref[...].astype(F32), cw_ref, cb_ref)
    for a_scr, b_scr, n in ((al_scr, bl_scr, SEQ), (ac_scr, bc_scr, CTX_LEN)):
        a_scr[n:n + SUBLANES, :] = jnp.ones((SUBLANES, ct), F32)
        b_scr[n:n + SUBLANES, :] = jnp.zeros((SUBLANES, ct), F32)
    y_l = None
    y_c = None
    for d in range(2):
        rev = d == 1
        a_c, b_c = _rglru_gates(u_c, wa_ref, ba_ref, wx_ref, bx_ref, lam_ref, d)
        ac_scr[0:CTX_LEN, :] = a_c
        bc_scr[0:CTX_LEN, :] = b_c
        h_c, fin = _blocked_scan(ac_scr, bc_scr, CTX_LEN, jnp.zeros((1, ct), F32), rev, need_ctx)
        a_l, b_l = _rglru_gates(u_l, wa_ref, ba_ref, wx_ref, bx_ref, lam_ref, d)
        al_scr[0:SEQ, :] = a_l
        bl_scr[0:SEQ, :] = b_l
        h_l, _ = _blocked_scan(al_scr, bl_scr, SEQ, fin, rev, True)
        y_l = h_l if y_l is None else [f + r for f, r in zip(y_l, h_l)]
        if need_ctx:
            y_c = h_c if y_c is None else [f + r for f, r in zip(y_c, h_c)]
    length = _scan_chunk_len(SEQ)
    for i in range(length):
        yl_scr[pl.ds(i, SCAN_CHUNKS, stride=length), :] = y_l[i]
    ol_ref[...] = (yl_scr[0:SEQ, :] * _silu(gl_ref[...].astype(F32))).astype(ol_ref.dtype)
    if need_ctx:
        length = _scan_chunk_len(CTX_LEN)
        for i in range(length):
            yc_scr[pl.ds(i, SCAN_CHUNKS, stride=length), :] = y_c[i]
        oc_ref[...] = (yc_scr[0:CTX_LEN, :] * _silu(gc_ref[...].astype(F32))).astype(oc_ref.dtype)


def _block_diag(w, per_tile):
    n = w.shape[-1]
    tiles = C_BLOCKS // per_tile
    w = w.reshape(2, tiles, per_tile, n, n)
    eye = jnp.eye(per_tile, dtype=w.dtype)
    bd = jnp.einsum('dtpij,pq->dtpiqj', w, eye).reshape(2, tiles, per_tile * n, per_tile * n)
    return jnp.transpose(bd, (1, 0, 2, 3)).astype(BF16)


def rglru_mixer(proj_l, proj_c, ccols, conv_w, conv_b, rg_wa, rg_ba, rg_wx, rg_bx, rg_lambda, need_ctx):
    ct = 128
    nt = C_WIDTH // ct
    per_tile = ct // C_BLOCK_DIM
    vec = lambda p: jnp.transpose(p.reshape(2, nt, 1, ct), (1, 0, 2, 3))
    wa = _block_diag(rg_wa, per_tile)
    wx = _block_diag(rg_wx, per_tile)
    xcol = lambda col: (lambda b, t: (b, col + t))
    par = lambda b, t: (t, 0, 0, 0)
    out_shape = [jax.ShapeDtypeStruct((BATCH * SEQ, C_WIDTH), BF16)]
    out_specs = [pl.BlockSpec((SEQ, ct), lambda b, t: (b, t))]
    if need_ctx:
        out_shape.append(jax.ShapeDtypeStruct((BATCH * CTX_LEN, C_WIDTH), BF16))
        out_specs.append(pl.BlockSpec((CTX_LEN, ct), lambda b, t: (b, t)))
    in_specs = [pl.BlockSpec((SEQ, ct), xcol(COL_XC)),
                pl.BlockSpec((CTX_LEN, ct), xcol(ccols["xc"])),
                pl.BlockSpec((SEQ, ct), xcol(COL_GC))]
    args = [proj_l, proj_c, proj_l]
    if need_ctx:
        in_specs.append(pl.BlockSpec((CTX_LEN, ct), xcol(ccols["gc"])))
        args.append(proj_c)
    in_specs += [pl.BlockSpec((CONV_W, ct), lambda b, t: (0, t)),
                 pl.BlockSpec((1, ct), lambda b, t: (0, t)),
                 pl.BlockSpec((1, 2, ct, ct), par),
                 pl.BlockSpec((1, 2, 1, ct), par),
                 pl.BlockSpec((1, 2, ct, ct), par),
                 pl.BlockSpec((1, 2, 1, ct), par),
                 pl.BlockSpec((1, 2, 1, ct), par)]
    args += [conv_w, conv_b.reshape(1, C_WIDTH), wa, vec(rg_ba), wx, vec(rg_bx), vec(rg_lambda)]
    outs = pl.pallas_call(
        functools.partial(_rglru_kernel, need_ctx=need_ctx),
        out_shape=out_shape,
        grid=(BATCH, nt),
        in_specs=in_specs,
        out_specs=out_specs,
        scratch_shapes=([pltpu.VMEM((SEQ + SUBLANES, ct), F32)] * 2
                        + [pltpu.VMEM((CTX_LEN + SUBLANES, ct), F32)] * 2
                        + [pltpu.VMEM((SEQ + SUBLANES, ct), F32), pltpu.VMEM((CTX_LEN + SUBLANES, ct), F32)]),
        compiler_params=_params(("arbitrary", "arbitrary")),
        name="rglru_mixer",
    )(*args)
    return outs if need_ctx else (outs[0], None)


def _outproj_kernel(x_ref, gate_ref, ma_ref, mb_ref, mc_ref, wa_ref, wb_ref, wc_ref, o_ref):
    y = jnp.dot(ma_ref[...], wa_ref[...], preferred_element_type=F32)
    y = y + jnp.dot(mb_ref[...], wb_ref[...], preferred_element_type=F32)
    y = y + jnp.dot(mc_ref[...], wc_ref[...], preferred_element_type=F32)
    o_ref[...] = x_ref[...] + gate_ref[0] * y


def output_projection(x2d, mod3, ma, mb, mc, w_bf16, layer, rows_per_mod, mod_row0):
    m = x2d.shape[0]
    tm, tn = 1024, 1024

    def mod_row(i):
        return mod_row0 + (i * tm) // rows_per_mod

    return pl.pallas_call(
        _outproj_kernel,
        out_shape=jax.ShapeDtypeStruct((m, D_MODEL), F32),
        grid=(m // tm, D_MODEL // tn),
        in_specs=[
            pl.BlockSpec((tm, tn), lambda i, j: (i, j)),
            pl.BlockSpec((1, 1, tn), lambda i, j: (mod_row(i), 0, 2 * (D_MODEL // tn) + j)),
            pl.BlockSpec((tm, A_WIDTH), lambda i, j: (i, 0)),
            pl.BlockSpec((tm, B_WIDTH), lambda i, j: (i, 0)),
            pl.BlockSpec((tm, C_WIDTH), lambda i, j: (i, 0)),
            pl.BlockSpec((None, A_WIDTH, tn), lambda i, j: (layer, 0, j)),
            pl.BlockSpec((None, B_WIDTH, tn), lambda i, j: (layer, 1, j)),
            pl.BlockSpec((None, C_WIDTH, tn), lambda i, j: (layer, (A_WIDTH + B_WIDTH) // C_WIDTH, j)),
        ],
        out_specs=pl.BlockSpec((tm, tn), lambda i, j: (i, j)),
        compiler_params=_params(("arbitrary", "arbitrary")),
        name="output_projection",
    )(x2d, mod3, ma, mb, mc, w_bf16, w_bf16, w_bf16)


def _final_norm_kernel(x_ref, w_ref, o_ref):
    x = x_ref[...]
    o_ref[...] = x * lax.rsqrt(jnp.mean(x * x, axis=-1, keepdims=True) + NORM_EPS) * w_ref[...]


def final_norm(x2d, w):
    tm = 256
    return pl.pallas_call(
        _final_norm_kernel,
        out_shape=jax.ShapeDtypeStruct(x2d.shape, F32),
        grid=(x2d.shape[0] // tm,),
        in_specs=[pl.BlockSpec((tm, D_MODEL), lambda i: (i, 0)),
                  pl.BlockSpec((1, D_MODEL), lambda i: (0, 0))],
        out_specs=pl.BlockSpec((tm, D_MODEL), lambda i: (i, 0)),
        compiler_params=_params(("arbitrary",)),
        name="final_norm",
    )(x2d, w.reshape(1, D_MODEL))


def _rope_tables():
    t = np.arange(SEQ)
    pos = np.stack([t // GRID_W, t % GRID_W], axis=0).astype(np.float32)
    lane = np.arange(HEAD_DIM)
    within = lane % A_QK_DIM
    axis = within // (A_QK_DIM // 2)
    k = within % (A_QK_DIM // 2)
    nfreq = A_QK_DIM // 4
    upper = k >= nfreq
    freqs = jnp.asarray(ROPE_THETA, F32) ** (-jnp.arange(nfreq, dtype=F32) / nfreq)
    ang = jnp.asarray(pos)[axis, :].T * freqs[k % nfreq][None, :]
    cos, sin = jnp.cos(ang), jnp.sin(ang)
    sin_lo = jnp.where(upper[None, :], 0.0, -sin)
    sin_hi = jnp.where(upper[None, :], sin, 0.0)
    return cos, sin_lo, sin_hi


def kernel(x, c, ctx, c_ctx, ada_w, ada_b, norm_w, w_in, w_out, lambda_qk, subln_w, rpb,
           conv_w, conv_b, rg_wa, rg_ba, rg_wx, rg_bx, rg_lambda, final_norm_w):
    cvec = jnp.zeros((MOD_ROWS, D_MODEL), F32).at[:BATCH].set(c).at[CTX_MOD_ROW].set(c_ctx)
    mod = adaln_modulation(cvec, ada_w, ada_b)
    mod3 = mod.reshape(DEPTH * MOD_ROWS, 1, 3 * D_MODEL)
    norm_w3 = norm_w.reshape(DEPTH, 1, D_MODEL)
    w_in_bf = w_in.astype(BF16)
    w_out_bf = w_out.astype(BF16)
    rope_tabs = _rope_tables()
    xl = x.reshape(BATCH * SEQ, D_MODEL)
    xc = ctx.reshape(BATCH * CTX_LEN, D_MODEL)
    for l in range(DEPTH):
        need_ctx = l < DEPTH - 1
        lam_init = 0.8 - 0.6 * math.exp(-0.3 * l)
        row_l, row_c = l * MOD_ROWS, l * MOD_ROWS + CTX_MOD_ROW
        proj_l = input_projection(xl, norm_w3, mod3, w_in_bf, l, SEQ, row_l)
        ccols = FULL_COLS if need_ctx else KV_COLS
        proj_c = input_projection(xc, norm_w3, mod3, w_in_bf, l, BATCH * CTX_LEN, row_c, ccols)

        ma = diff_attention_latent(proj_l, proj_c, ccols, rope_tabs, lambda_qk[l], subln_w[l], lam_init)
        mb = neighbourhood_attention(proj_l, proj_c, ccols, neighbourhood_bias_table(rpb[l]))
        mc, mc_c = rglru_mixer(proj_l, proj_c, ccols, conv_w[l], conv_b[l], rg_wa[l], rg_ba[l],
                               rg_wx[l], rg_bx[l], rg_lambda[l], need_ctx)
        xl = output_projection(xl, mod3, ma, mb, mc, w_out_bf, l, SEQ, row_l)
        if need_ctx:
            ma_c = diff_attention_context(proj_c, lambda_qk[l], subln_w[l], lam_init)
            mb_c = dense_context_attention(proj_c)
            xc = output_projection(xc, mod3, ma_c, mb_c, mc_c, w_out_bf, l, BATCH * CTX_LEN, row_c)
    return final_norm(xl, final_norm_w).reshape(BATCH, SEQ, D_MODEL)
```

```python
import functools
import math

import numpy as np
import jax
import jax.numpy as jnp
from jax import lax
from jax.experimental import pallas as pl
from jax.experimental.pallas import tpu as pltpu

F32 = jnp.float32
BF16 = jnp.bfloat16

D_MODEL = 4096
BATCH = 8
SEQ = 2048
DEPTH = 2
GRID_W = 64
GRID_H = SEQ // GRID_W
CTX_LEN = 256
HEAD_DIM = 128
A_WIDTH = (3 * D_MODEL) // 8
A_HEADS = A_WIDTH // HEAD_DIM
A_QK_DIM = HEAD_DIM // 2
B_WIDTH = (3 * D_MODEL) // 8
B_HEADS = B_WIDTH // HEAD_DIM
C_WIDTH = D_MODEL - A_WIDTH - B_WIDTH
C_BLOCKS = 16
C_BLOCK_DIM = C_WIDTH // C_BLOCKS
MIX_WIDTH = A_WIDTH + B_WIDTH + C_WIDTH
IN_WIDTH = 4 * A_WIDTH + 4 * B_WIDTH + 2 * C_WIDTH
NA_KH = 8
NA_KW = 16
ROPE_THETA = 10000.0
RGLRU_C = 8.0
CONV_W = 4
CONV_LEFT = 2
NORM_EPS = 1e-6
SUBLN_EPS = 1e-5
NEG_INF = -1e30

HB = HEAD_DIM
COL_QA, COL_KA, COL_VA, COL_GA = 0, A_HEADS, 2 * A_HEADS, 3 * A_HEADS
COL_QB = 4 * A_HEADS
COL_KB, COL_VB, COL_GB = COL_QB + B_HEADS, COL_QB + 2 * B_HEADS, COL_QB + 3 * B_HEADS
COL_XC = COL_QB + 4 * B_HEADS
COL_GC = COL_XC + C_WIDTH // HB

MOD_ROWS = 16
CTX_MOD_ROW = BATCH

VMEM_LIMIT = 56 * 1024 * 1024
SUBLANES = 8

NB_QROWS = 4
NB_KROWS = 12
NB_GROUPS = GRID_H // NB_QROWS
NB_Q = NB_QROWS * GRID_W
NB_K = NB_KROWS * GRID_W


def _params(sem):
    return pltpu.CompilerParams(dimension_semantics=sem, vmem_limit_bytes=VMEM_LIMIT)


def _sigmoid(x):
    return 1.0 / (1.0 + jnp.exp(-x))


def _silu(x):
    return x * _sigmoid(x)


def _adaln_kernel(c_ref, w_ref, b_ref, o_ref):
    s = _silu(c_ref[...]).astype(BF16)
    w = w_ref[0].astype(BF16)
    o_ref[0] = jnp.dot(s, w, preferred_element_type=F32) + b_ref[0]


def adaln_modulation(cvec, ada_w, ada_b):
    tn = 512
    n3 = 3 * D_MODEL
    return pl.pallas_call(
        _adaln_kernel,
        out_shape=jax.ShapeDtypeStruct((DEPTH, MOD_ROWS, n3), F32),
        grid=(DEPTH, n3 // tn),
        in_specs=[
            pl.BlockSpec((MOD_ROWS, D_MODEL), lambda l, j: (0, 0)),
            pl.BlockSpec((1, D_MODEL, tn), lambda l, j: (l, 0, j)),
            pl.BlockSpec((1, 1, tn), lambda l, j: (l, 0, j)),
        ],
        out_specs=pl.BlockSpec((1, MOD_ROWS, tn), lambda l, j: (l, 0, j)),
        compiler_params=_params(("arbitrary", "arbitrary")),
        name="adaln_modulation",
    )(cvec, ada_w, ada_b.reshape(DEPTH, 1, n3))


_NORM_CHUNK = 32
_NORM_SLICES = 4


def _inproj_kernel(x_ref, nw_ref, shift_ref, scale_ref, w_ref, o_ref, hx0_ref, hx1_ref):
    r, j = pl.program_id(0), pl.program_id(1)
    xs = x_ref.shape[0]
    gain = nw_ref[...] * (1.0 + scale_ref[0])
    add = shift_ref[0]

    def normalise_slice(hx_ref):
        base = jnp.minimum(j, _NORM_SLICES - 1) * xs
        for k in range(xs // _NORM_CHUNK):
            x = x_ref[k * _NORM_CHUNK:(k + 1) * _NORM_CHUNK, :]
            inv = lax.rsqrt(jnp.mean(x * x, axis=-1, keepdims=True) + NORM_EPS)
            rows = pl.ds(pl.multiple_of(base + k * _NORM_CHUNK, _NORM_CHUNK), _NORM_CHUNK)
            hx_ref[rows, :] = (x * inv * gain + add).astype(BF16)

    def step(hx_cur, hx_next):
        normalise_slice(hx_next)
        o_ref[...] = jnp.dot(hx_cur[...], w_ref[...], preferred_element_type=F32).astype(o_ref.dtype)

    @pl.when(r == 0)
    def _():
        normalise_slice(hx0_ref)

    pl.when(r % 2 == 1)(functools.partial(step, hx0_ref, hx1_ref))
    pl.when(jnp.logical_and(r % 2 == 0, r > 0))(functools.partial(step, hx1_ref, hx0_ref))


class ProjCols:
    def __init__(self, groups):
        self.start, pos = {}, 0
        for name, _, width in groups:
            self.start[name] = pos
            pos += width
        self.width = pos * HB
        self.src_blocks = [src + k for _, src, width in groups for k in range(width)]

    def __getitem__(self, name):
        return self.start[name]


_ALL_GROUPS = (("qa", COL_QA, A_HEADS), ("ka", COL_KA, A_HEADS), ("va", COL_VA, A_HEADS),
               ("ga", COL_GA, A_HEADS), ("qb", COL_QB, B_HEADS), ("kb", COL_KB, B_HEADS),
               ("vb", COL_VB, B_HEADS), ("gb", COL_GB, B_HEADS), ("xc", COL_XC, C_WIDTH // HB),
               ("gc", COL_GC, C_WIDTH // HB))
FULL_COLS = ProjCols(_ALL_GROUPS)
KV_COLS = ProjCols(tuple(g for g in _ALL_GROUPS if g[0] in ("ka", "va", "kb", "vb", "xc")))


def input_projection(x2d, norm_w3, mod3, w_bf16, layer, rows_per_mod, mod_row0, cols=FULL_COLS):
    m = x2d.shape[0]
    tm = 512
    tn = 2048 if cols is FULL_COLS else 512
    per_tile = tn // HB
    n_tiles = cols.width // tn
    src_tiles = [cols.src_blocks[t * per_tile] // per_tile for t in range(n_tiles)]
    assert all(cols.src_blocks[t * per_tile + k] == src_tiles[t] * per_tile + k
               for t in range(n_tiles) for k in range(per_tile))

    def src_tile(j):
        t = j + src_tiles[0]
        for k in range(1, n_tiles):
            jump = (src_tiles[k] - k) - (src_tiles[k - 1] - (k - 1))
            if jump:
                t = t + jnp.where(j >= k, jump, 0)
        return t

    n_rows = m // tm
    xs = tm // _NORM_SLICES
    assert n_tiles >= _NORM_SLICES and xs % _NORM_CHUNK == 0

    def norm_tile(r):
        return jnp.minimum(r, n_rows - 1)

    def mod_row(r):
        return mod_row0 + (norm_tile(r) * tm) // rows_per_mod

    def parked(r, j):
        return jnp.where(r == 0, 0, j)

    return pl.pallas_call(
        _inproj_kernel,
        out_shape=jax.ShapeDtypeStruct((m, cols.width), BF16),
        grid=(n_rows + 1, n_tiles),
        in_specs=[
            pl.BlockSpec((xs, D_MODEL),
                         lambda r, j: (norm_tile(r) * _NORM_SLICES + jnp.minimum(j, _NORM_SLICES - 1), 0)),
            pl.BlockSpec((None, 1, D_MODEL), lambda r, j: (layer, 0, 0)),
            pl.BlockSpec((1, 1, D_MODEL), lambda r, j: (mod_row(r), 0, 0)),
            pl.BlockSpec((1, 1, D_MODEL), lambda r, j: (mod_row(r), 0, 1)),
            pl.BlockSpec((None, D_MODEL, tn), lambda r, j: (layer, 0, src_tile(parked(r, j)))),
        ],
        out_specs=pl.BlockSpec((tm, tn), lambda r, j: (jnp.maximum(r - 1, 0), parked(r, j))),
        scratch_shapes=[pltpu.VMEM((tm, D_MODEL), BF16), pltpu.VMEM((tm, D_MODEL), BF16)],
        compiler_params=_params(("arbitrary", "arbitrary")),
        name="input_projection",
    )(x2d, norm_w3, mod3, mod3, w_bf16)


def _rope(x, cos, sin_lo, sin_hi):
    return (x * cos + pltpu.roll(x, HEAD_DIM - 16, axis=1) * sin_lo
            + pltpu.roll(x, 16, axis=1) * sin_hi)


def _diff_lambda(lq_ref, lam_init):
    lq = lq_ref[...]
    a = jnp.sum(lq[0:1] * lq[1:2], axis=-1, keepdims=True)
    b = jnp.sum(lq[2:3] * lq[3:4], axis=-1, keepdims=True)
    return jnp.exp(a) - jnp.exp(b) + lam_init


LOG2E = 1.4426950408889634
NKEYS = CTX_LEN + SEQ
DA_TQ = 256
DA_W = 2 * DA_TQ
DA_ROWS = 16
DA_KCHUNK = 256


def _transpose_bf16(x):
    return x.astype(F32).T.astype(BF16)


def _diffattn_lat_kernel(one_ref, q_ref, kc_ref, vc_ref, kl_ref, vl_ref, g_ref, cos_ref, slo_ref, shi_ref,
                         lq_ref, sw_ref, o_ref, k_scr, vt_scr, qt_scr, s0_scr, s1_scr, m_scr, p_scr,
                         *, lam_init):
    n_tiles = SEQ // DA_TQ
    n_chunks = NKEYS // DA_KCHUNK
    groups_per_chunk = DA_KCHUNK // DA_ROWS

    def score_chunk(s_scr, qt2, c, m_run):
        rows = slice(c * DA_KCHUNK, (c + 1) * DA_KCHUNK)
        s = jnp.dot(k_scr[rows, :], qt2, preferred_element_type=F32)
        s_scr[rows, :] = s
        parts = [s[i * DA_ROWS:(i + 1) * DA_ROWS, :] for i in range(groups_per_chunk)]
        if m_run is not None:
            parts.append(m_run)
        while len(parts) > 1:
            parts = [jnp.maximum(parts[i], parts[i + 1]) for i in range(0, len(parts) - 1, 2)] + (
                [parts[-1]] if len(parts) % 2 else [])
        return parts[0]

    def colmax(m_run):
        return jnp.broadcast_to(jnp.max(m_run, axis=0, keepdims=True), (DA_ROWS, DA_W))

    def always(k):
        return one_ref[0] > -k

    @pl.when(always(0))
    def _():
        assert DA_KCHUNK == DA_TQ == CTX_LEN
        dim = lax.broadcasted_iota(jnp.int32, (HB, DA_TQ), 0)

        def stage_queries(c):
            rows = slice(c * DA_TQ, (c + 1) * DA_TQ)
            q = _rope(q_ref[rows, :].astype(F32), cos_ref[rows, :], slo_ref[rows, :], shi_ref[rows, :])
            qt = (q * (A_QK_DIM ** -0.5 * LOG2E)).T
            qt_scr[:, c * DA_W:c * DA_W + DA_TQ] = jnp.where(dim < A_QK_DIM, qt, 0.0).astype(BF16)
            qt_scr[:, c * DA_W + DA_TQ:(c + 1) * DA_W] = jnp.where(dim >= A_QK_DIM, qt, 0.0).astype(BF16)

        stage_queries(0)
        qt0 = qt_scr[:, 0:DA_W]
        k_scr[0:CTX_LEN, :] = kc_ref[...]
        vt_scr[:, 0:CTX_LEN] = _transpose_bf16(vc_ref[...])
        m_run = score_chunk(s0_scr, qt0, 0, None)
        for c in range(SEQ // DA_TQ):
            rows = slice(c * DA_TQ, (c + 1) * DA_TQ)
            dst = slice(CTX_LEN + c * DA_TQ, CTX_LEN + (c + 1) * DA_TQ)
            k_scr[dst, :] = _rope(kl_ref[rows, :].astype(F32), cos_ref[rows, :], slo_ref[rows, :],
                                  shi_ref[rows, :]).astype(BF16)
            m_run = score_chunk(s0_scr, qt0, c + 1, m_run)
            vt_scr[:, dst] = _transpose_bf16(vl_ref[rows, :])
            if c > 0:
                stage_queries(c)
        m_scr[...] = colmax(m_run)

    def tile(t, s_scr, s_next_scr):
        rows = slice(t * DA_TQ, (t + 1) * DA_TQ)
        mb = m_scr[...]
        if s_next_scr is not None:
            qt_next = qt_scr[:, (t + 1) * DA_W:(t + 2) * DA_W]
        m_run = None
        l_acc = [None, None]
        acc = None

        def value_chunk(c, acc):
            rows = slice(c * DA_KCHUNK, (c + 1) * DA_KCHUNK)
            part = jnp.dot(vt_scr[:, rows], p_scr[rows, :], preferred_element_type=F32)
            return part if acc is None else acc + part

        for c in range(n_chunks):
            if s_next_scr is not None:
                m_run = score_chunk(s_next_scr, qt_next, c, m_run)
            for r in range(c * groups_per_chunk, (c + 1) * groups_per_chunk):
                grp = slice(r * DA_ROWS, (r + 1) * DA_ROWS)
                p = jnp.exp2(s_scr[grp, :] - mb)
                p_scr[grp, :] = p.astype(BF16)
                l_acc[r % 2] = p if l_acc[r % 2] is None else l_acc[r % 2] + p
            acc = value_chunk(c, acc)
        l = jnp.sum(l_acc[0] + l_acc[1], axis=0, keepdims=True)

        lam = _diff_lambda(lq_ref, lam_init)
        ot = acc[:, :DA_TQ] * (1.0 / l[:, :DA_TQ]) - acc[:, DA_TQ:] * (lam / l[:, DA_TQ:])
        ot = ot * lax.rsqrt(jnp.mean(ot * ot, axis=0, keepdims=True) + SUBLN_EPS)
        o = ot.T * sw_ref[...] * (1.0 - lam_init)
        o_ref[rows, :] = (o * _silu(g_ref[rows, :].astype(F32))).astype(o_ref.dtype)
        if s_next_scr is not None:
            m_scr[...] = colmax(m_run)

    bufs = (s0_scr, s1_scr)
    for t in range(n_tiles):
        nxt = bufs[(t + 1) % 2] if t + 1 < n_tiles else None
        pl.when(always(t + 1))(functools.partial(tile, t, bufs[t % 2], nxt))


_NT = (((1,), (1,)), ((), ()))


def _diffattn_ctx_kernel(q_ref, k_ref, v_ref, g_ref, lq_ref, sw_ref, o_ref, *, lam_init):
    q = q_ref[...].astype(F32) * (A_QK_DIM ** -0.5)
    lane = lax.broadcasted_iota(jnp.int32, q.shape, 1)
    outs = []
    for qn in (jnp.where(lane < A_QK_DIM, q, 0.0), jnp.where(lane >= A_QK_DIM, q, 0.0)):
        s = lax.dot_general(qn.astype(BF16), k_ref[...], _NT, preferred_element_type=F32)
        p = jnp.exp(s - jnp.max(s, axis=-1, keepdims=True))
        acc = jnp.dot(p.astype(BF16), v_ref[...], preferred_element_type=F32)
        outs.append((acc, jnp.sum(p, axis=-1, keepdims=True)))
    lam = _diff_lambda(lq_ref, lam_init)
    o = outs[0][0] * (1.0 / outs[0][1]) - (lam / outs[1][1]) * outs[1][0]
    o = o * lax.rsqrt(jnp.mean(o * o, axis=-1, keepdims=True) + SUBLN_EPS) * sw_ref[...]
    o = o * (1.0 - lam_init)
    o_ref[...] = (o * _silu(g_ref[...].astype(F32))).astype(o_ref.dtype)


def diff_attention_latent(proj_l, proj_c, ccols, rope_tabs, lambda_qk, subln_w, lam_init):
    head = lambda col: (lambda b, h, one: (b, col + h))
    const = lambda b, h, one: (0, 0)
    grid_spec = pltpu.PrefetchScalarGridSpec(
        num_scalar_prefetch=1,
        grid=(BATCH, A_HEADS),
        in_specs=[
            pl.BlockSpec((SEQ, HB), head(COL_QA)),
            pl.BlockSpec((CTX_LEN, HB), head(ccols["ka"])),
            pl.BlockSpec((CTX_LEN, HB), head(ccols["va"])),
            pl.BlockSpec((SEQ, HB), head(COL_KA)),
            pl.BlockSpec((SEQ, HB), head(COL_VA)),
            pl.BlockSpec((SEQ, HB), head(COL_GA)),
            pl.BlockSpec((SEQ, HB), const),
            pl.BlockSpec((SEQ, HB), const),
            pl.BlockSpec((SEQ, HB), const),
            pl.BlockSpec((4, A_QK_DIM), const),
            pl.BlockSpec((1, HB), const),
        ],
        out_specs=pl.BlockSpec((SEQ, HB), lambda b, h, one: (b, h)),
        scratch_shapes=[pltpu.VMEM((NKEYS, HB), BF16), pltpu.VMEM((HB, NKEYS), BF16),
                        pltpu.VMEM((HB, 2 * SEQ), BF16),
                        pltpu.VMEM((NKEYS, DA_W), F32), pltpu.VMEM((NKEYS, DA_W), F32),
                        pltpu.VMEM((DA_ROWS, DA_W), F32), pltpu.VMEM((NKEYS, DA_W), BF16)],
    )
    return pl.pallas_call(
        functools.partial(_diffattn_lat_kernel, lam_init=lam_init),
        out_shape=jax.ShapeDtypeStruct((BATCH * SEQ, A_WIDTH), BF16),
        grid_spec=grid_spec,
        compiler_params=_params(("arbitrary", "arbitrary")),
        name="diff_attention_latent",
    )(jnp.ones((1,), jnp.int32), proj_l, proj_c, proj_c, proj_l, proj_l, proj_l, *rope_tabs,
      lambda_qk, subln_w.reshape(1, HB))


def diff_attention_context(proj_c, lambda_qk, subln_w, lam_init):
    head = lambda col: (lambda b, h: (b, col + h))
    const = lambda b, h: (0, 0)
    return pl.pallas_call(
        functools.partial(_diffattn_ctx_kernel, lam_init=lam_init),
        out_shape=jax.ShapeDtypeStruct((BATCH * CTX_LEN, A_WIDTH), BF16),
        grid=(BATCH, A_HEADS),
        in_specs=[pl.BlockSpec((CTX_LEN, HB), head(COL_QA)),
                  pl.BlockSpec((CTX_LEN, HB), head(COL_KA)),
                  pl.BlockSpec((CTX_LEN, HB), head(COL_VA)),
                  pl.BlockSpec((CTX_LEN, HB), head(COL_GA)),
                  pl.BlockSpec((4, A_QK_DIM), const),
                  pl.BlockSpec((1, HB), const)],
        out_specs=pl.BlockSpec((CTX_LEN, HB), lambda b, h: (b, h)),
        compiler_params=_params(("arbitrary", "arbitrary")),
        name="diff_attention_context",
    )(proj_c, proj_c, proj_c, proj_c, lambda_qk, subln_w.reshape(1, HB))


def _nbr_slab_start(g):
    return min(max(g * NB_QROWS - NA_KH // 2, 0), GRID_H - NB_KROWS)


def _nbr_pattern(g):
    return 0 if g == 0 else (2 if g == NB_GROUPS - 1 else 1)


def _nbr_kernel(q_ref, kc_ref, vc_ref, kl_ref, vl_ref, g_ref, bias_ref, o_ref, vt_scr):
    vt_scr[:, 0:CTX_LEN] = _transpose_bf16(vc_ref[...])
    for c in range(SEQ // NB_Q):
        vt_scr[:, CTX_LEN + c * NB_Q:CTX_LEN + (c + 1) * NB_Q] = _transpose_bf16(vl_ref[c * NB_Q:(c + 1) * NB_Q, :])
    def scores(g):
        rows = slice(g * NB_Q, (g + 1) * NB_Q)
        k0 = _nbr_slab_start(g) * GRID_W
        qt = (q_ref[rows, :].astype(F32) * (HEAD_DIM ** -0.5 * LOG2E)).T.astype(BF16)
        s_c = jnp.dot(kc_ref[...], qt, preferred_element_type=F32)
        s_n = (jnp.dot(kl_ref[k0:k0 + NB_K, :], qt, preferred_element_type=F32)
               + bias_ref[0, _nbr_pattern(g)])
        return s_c, s_n

    def softmax(s_c, s_n):
        m = jnp.maximum(jnp.max(s_c, axis=0, keepdims=True), jnp.max(s_n, axis=0, keepdims=True))
        p_c = jnp.exp2(s_c - m)
        p_n = jnp.exp2(s_n - m)
        l = jnp.sum(p_c, axis=0, keepdims=True) + jnp.sum(p_n, axis=0, keepdims=True)
        return p_c.astype(BF16), p_n.astype(BF16), l

    def finish(g, p_c, p_n, l):
        rows = slice(g * NB_Q, (g + 1) * NB_Q)
        k0 = _nbr_slab_start(g) * GRID_W
        acc = (jnp.dot(vt_scr[:, 0:CTX_LEN], p_c, preferred_element_type=F32)
               + jnp.dot(vt_scr[:, CTX_LEN + k0:CTX_LEN + k0 + NB_K], p_n,
                         preferred_element_type=F32))
        o = (acc * (1.0 / l)).T
        o_ref[rows, :] = (o * _silu(g_ref[rows, :].astype(F32))).astype(o_ref.dtype)

    ahead = scores(0)
    probs = None
    for g in range(NB_GROUPS):
        s_c, s_n = ahead
        if g + 1 < NB_GROUPS:
            ahead = scores(g + 1)
        new_probs = softmax(s_c, s_n)
        if probs is not None:
            finish(g - 1, *probs)
        probs = new_probs
    finish(NB_GROUPS - 1, *probs)


def neighbourhood_bias_table(rpb):
    w = GRID_W
    cq = np.arange(w)
    c0 = np.clip(cq - NA_KW // 2, 0, w - NA_KW)
    ck = np.arange(w)
    col_valid = (ck[None, :] >= c0[:, None]) & (ck[None, :] < c0[:, None] + NA_KW)
    pad = w - NA_KW
    padded = jnp.pad(rpb.astype(F32), ((0, 0), (0, 0), (pad, pad)))
    toep = jnp.stack([padded[:, :, w - 1 - q:2 * w - 1 - q] for q in range(w)], axis=2)
    toep = jnp.where(col_valid[None, None], toep, NEG_INF)
    toep_t = jnp.swapaxes(toep, 2, 3) * LOG2E
    masked = jnp.full((B_HEADS, w, w), NEG_INF, F32)
    tabs = []
    for g in (0, 1, NB_GROUPS - 1):
        slab0 = _nbr_slab_start(g)
        qr = g * NB_QROWS + np.arange(NB_QROWS)
        kr = slab0 + np.arange(NB_KROWS)
        win0 = np.clip(qr - NA_KH // 2, 0, GRID_H - NA_KH)
        row_valid = (kr[None, :] >= win0[:, None]) & (kr[None, :] < win0[:, None] + NA_KH)
        dr_idx = kr[None, :] - qr[:, None] + NA_KH - 1
        rows = [jnp.concatenate([toep_t[:, int(dr_idx[a, k])] if row_valid[a, k] else masked
                                 for a in range(NB_QROWS)], axis=-1) for k in range(NB_KROWS)]
        tabs.append(jnp.concatenate(rows, axis=1))
    return jnp.stack(tabs, axis=1)


def neighbourhood_attention(proj_l, proj_c, ccols, bias_tab):
    head = lambda col: (lambda h, b: (b, col + h))
    return pl.pallas_call(
        _nbr_kernel,
        out_shape=jax.ShapeDtypeStruct((BATCH * SEQ, B_WIDTH), BF16),
        grid=(B_HEADS, BATCH),
        in_specs=[
            pl.BlockSpec((SEQ, HB), head(COL_QB)),
            pl.BlockSpec((CTX_LEN, HB), head(ccols["kb"])),
            pl.BlockSpec((CTX_LEN, HB), head(ccols["vb"])),
            pl.BlockSpec((SEQ, HB), head(COL_KB)),
            pl.BlockSpec((SEQ, HB), head(COL_VB)),
            pl.BlockSpec((SEQ, HB), head(COL_GB)),
            pl.BlockSpec((1, 3, NB_K, NB_Q), lambda h, b: (h, 0, 0, 0)),
        ],
        out_specs=pl.BlockSpec((SEQ, HB), lambda h, b: (b, h)),
        scratch_shapes=[pltpu.VMEM((HB, NKEYS), BF16)],
        compiler_params=_params(("arbitrary", "arbitrary")),
        name="neighbourhood_attention",
    )(proj_l, proj_c, proj_c, proj_l, proj_l, proj_l, bias_tab)


def _dense_ctx_kernel(q_ref, k_ref, v_ref, g_ref, o_ref):
    scale = HEAD_DIM ** -0.5
    s = lax.dot_general(q_ref[...], k_ref[...], _NT, preferred_element_type=F32) * scale
    p = jnp.exp(s - jnp.max(s, axis=-1, keepdims=True))
    p = p * (1.0 / jnp.sum(p, axis=-1, keepdims=True))
    o = jnp.dot(p.astype(BF16), v_ref[...], preferred_element_type=F32)
    o_ref[...] = (o * _silu(g_ref[...].astype(F32))).astype(o_ref.dtype)


def dense_context_attention(proj_c):
    head = lambda col: (lambda b, h: (b, col + h))
    return pl.pallas_call(
        _dense_ctx_kernel,
        out_shape=jax.ShapeDtypeStruct((BATCH * CTX_LEN, B_WIDTH), BF16),
        grid=(BATCH, B_HEADS),
        in_specs=[pl.BlockSpec((CTX_LEN, HB), head(COL_QB)),
                  pl.BlockSpec((CTX_LEN, HB), head(COL_KB)),
                  pl.BlockSpec((CTX_LEN, HB), head(COL_VB)),
                  pl.BlockSpec((CTX_LEN, HB), head(COL_GB))],
        out_specs=pl.BlockSpec((CTX_LEN, HB), lambda b, h: (b, h)),
        compiler_params=_params(("arbitrary", "arbitrary")),
        name="dense_context_attention",
    )(proj_c, proj_c, proj_c, proj_c)


def _shift_rows(x, d, fill):
    n = x.shape[0]
    if d % SUBLANES == 0:
        pad = jnp.full((abs(d), x.shape[1]), fill, x.dtype)
        return jnp.concatenate([pad, x[:n - d]] if d > 0 else [x[-d:], pad], axis=0)
    rolled = pltpu.roll(x, d % n, axis=0)
    row = lax.broadcasted_iota(jnp.int32, x.shape, 0)
    keep = (row >= d) if d > 0 else (row < n + d)
    return jnp.where(keep, rolled, fill)


def _dwconv(x, w_ref, b_ref):
    acc = b_ref[...] + w_ref[CONV_LEFT:CONV_LEFT + 1, :] * x
    for j in range(CONV_W):
        if j != CONV_LEFT:
            acc = acc + w_ref[j:j + 1, :] * _shift_rows(x, CONV_LEFT - j, 0.0)
    return acc


def _linear_scan(a, b, h0, reverse):
    n = a.shape[0]
    row = lax.broadcasted_iota(jnp.int32, a.shape, 0)
    first = (row == n - 1) if reverse else (row == 0)
    b = b + jnp.where(first, a * h0, 0.0)
    d = 1
    while d < n:
        sd = -d if reverse else d
        b = b + a * _shift_rows(b, sd, 0.0)
        if 2 * d < n:
            a = a * _shift_rows(a, sd, 1.0)
        d *= 2
    return b


def _sigmoid_tanh(x):
    return 0.5 * jnp.tanh(0.5 * x) + 0.5


def _rglru_gates(u, wa_ref, ba_ref, wx_ref, bx_ref, lam_ref, d):
    ub = u.astype(BF16)
    r = _sigmoid_tanh(jnp.dot(ub, wa_ref[0, d], preferred_element_type=F32) + ba_ref[0, d])
    i = _sigmoid_tanh(jnp.dot(ub, wx_ref[0, d], preferred_element_type=F32) + bx_ref[0, d])
    neg_lam = -lam_ref[0, d]
    softplus = jnp.maximum(neg_lam, 0.0) + jnp.log1p(jnp.exp(-jnp.abs(neg_lam)))
    log_a = (-RGLRU_C) * r * softplus
    a = jnp.exp(log_a)
    t = jnp.tanh(log_a)
    b = jnp.sqrt(-2.0 * t / (1.0 - t)) * (i * u)
    return a, b


SCAN_CHUNKS = SUBLANES


def _scan_chunk_len(n):
    return n // SCAN_CHUNKS + 1


def _blocked_scan(a_scr, b_scr, n, h0, reverse, need_states):
    length = _scan_chunk_len(n)
    order = range(length - 1, -1, -1) if reverse else range(length)
    h = jnp.zeros((SCAN_CHUNKS, a_scr.shape[1]), F32)
    p = jnp.ones_like(h)
    hs, ps = [None] * length, [None] * length
    for i in order:
        va = a_scr[pl.ds(i, SCAN_CHUNKS, stride=length), :]
        vb = b_scr[pl.ds(i, SCAN_CHUNKS, stride=length), :]
        h = va * h + vb
        p = va * p
        hs[i], ps[i] = h, p
    rows = [None] * SCAN_CHUNKS
    carry = h0
    for j in (range(SCAN_CHUNKS - 1, -1, -1) if reverse else range(SCAN_CHUNKS)):
        rows[j] = carry
        carry = h[j:j + 1, :] + p[j:j + 1, :] * carry
    if not need_states:
        return None, carry
    start = jnp.concatenate(rows, axis=0)
    return [hs[i] + ps[i] * start for i in range(length)], carry


def _rglru_kernel(*refs, need_ctx):
    if need_ctx:
        (xl_ref, xc_ref, gl_ref, gc_ref, cw_ref, cb_ref, wa_ref, ba_ref, wx_ref, bx_ref, lam_ref,
         ol_ref, oc_ref, al_scr, bl_scr, ac_scr, bc_scr, yl_scr, yc_scr) = refs
    else:
        (xl_ref, xc_ref, gl_ref, cw_ref, cb_ref, wa_ref, ba_ref, wx_ref, bx_ref, lam_ref,
         ol_ref, al_scr, bl_scr, ac_scr, bc_scr, yl_scr, yc_scr) = refs
    ct = al_scr.shape[1]
    u_l = _dwconv(xl_ref[...].astype(F32), cw_ref, cb_ref)
    u_c = _dwconv(xc_ref[...].astype(F32), cw_ref, cb_ref)
    for a_scr, b_scr, n in ((al_scr, bl_scr, SEQ), (ac_scr, bc_scr, CTX_LEN)):
        a_scr[n:n + SUBLANES, :] = jnp.ones((SUBLANES, ct), F32)
        b_scr[n:n + SUBLANES, :] = jnp.zeros((SUBLANES, ct), F32)
    y_l = None
    y_c = None
    for d in range(2):
        rev = d == 1
        a_c, b_c = _rglru_gates(u_c, wa_ref, ba_ref, wx_ref, bx_ref, lam_ref, d)
        ac_scr[0:CTX_LEN, :] = a_c
        bc_scr[0:CTX_LEN, :] = b_c
        h_c, fin = _blocked_scan(ac_scr, bc_scr, CTX_LEN, jnp.zeros((1, ct), F32), rev, need_ctx)
        a_l, b_l = _rglru_gates(u_l, wa_ref, ba_ref, wx_ref, bx_ref, lam_ref, d)
        al_scr[0:SEQ, :] = a_l
        bl_scr[0:SEQ, :] = b_l
        h_l, _ = _blocked_scan(al_scr, bl_scr, SEQ, fin, rev, True)
        y_l = h_l if y_l is None else [f + r for f, r in zip(y_l, h_l)]
        if need_ctx:
            y_c = h_c if y_c is None else [f + r for f, r in zip(y_c, h_c)]
    length = _scan_chunk_len(SEQ)
    for i in range(length):
        yl_scr[pl.ds(i, SCAN_CHUNKS, stride=length), :] = y_l[i]
    ol_ref[...] = (yl_scr[0:SEQ, :] * _silu(gl_ref[...].astype(F32))).astype(ol_ref.dtype)
    if need_ctx:
        length = _scan_chunk_len(CTX_LEN)
        for i in range(length):
            yc_scr[pl.ds(i, SCAN_CHUNKS, stride=length), :] = y_c[i]
        oc_ref[...] = (yc_scr[0:CTX_LEN, :] * _silu(gc_ref[...].astype(F32))).astype(oc_ref.dtype)


def _block_diag(w, per_tile):
    n = w.shape[-1]
    tiles = C_BLOCKS // per_tile
    w = w.reshape(2, tiles, per_tile, n, n)
    eye = jnp.eye(per_tile, dtype=w.dtype)
    bd = jnp.einsum('dtpij,pq->dtpiqj', w, eye).reshape(2, tiles, per_tile * n, per_tile * n)
    return jnp.transpose(bd, (1, 0, 2, 3)).astype(BF16)


def rglru_mixer(proj_l, proj_c, ccols, conv_w, conv_b, rg_wa, rg_ba, rg_wx, rg_bx, rg_lambda, need_ctx):
    ct = 128
    nt = C_WIDTH // ct
    per_tile = ct // C_BLOCK_DIM
    vec = lambda p: jnp.transpose(p.reshape(2, nt, 1, ct), (1, 0, 2, 3))
    wa = _block_diag(rg_wa, per_tile)
    wx = _block_diag(rg_wx, per_tile)
    xcol = lambda col: (lambda b, t: (b, col + t))
    par = lambda b, t: (t, 0, 0, 0)
    out_shape = [jax.ShapeDtypeStruct((BATCH * SEQ, C_WIDTH), BF16)]
    out_specs = [pl.BlockSpec((SEQ, ct), lambda b, t: (b, t))]
    if need_ctx:
        out_shape.append(jax.ShapeDtypeStruct((BATCH * CTX_LEN, C_WIDTH), BF16))
        out_specs.append(pl.BlockSpec((CTX_LEN, ct), lambda b, t: (b, t)))
    in_specs = [pl.BlockSpec((SEQ, ct), xcol(COL_XC)),
                pl.BlockSpec((CTX_LEN, ct), xcol(ccols["xc"])),
                pl.BlockSpec((SEQ, ct), xcol(COL_GC))]
    args = [proj_l, proj_c, proj_l]
    if need_ctx:
        in_specs.append(pl.BlockSpec((CTX_LEN, ct), xcol(ccols["gc"])))
        args.append(proj_c)
    in_specs += [pl.BlockSpec((CONV_W, ct), lambda b, t: (0, t)),
                 pl.BlockSpec((1, ct), lambda b, t: (0, t)),
                 pl.BlockSpec((1, 2, ct, ct), par),
                 pl.BlockSpec((1, 2, 1, ct), par),
                 pl.BlockSpec((1, 2, ct, ct), par),
                 pl.BlockSpec((1, 2, 1, ct), par),
                 pl.BlockSpec((1, 2, 1, ct), par)]
    args += [conv_w, conv_b.reshape(1, C_WIDTH), wa, vec(rg_ba), wx, vec(rg_bx), vec(rg_lambda)]
    outs = pl.pallas_call(
        functools.partial(_rglru_kernel, need_ctx=need_ctx),
        out_shape=out_shape,
        grid=(BATCH, nt),
        in_specs=in_specs,
        out_specs=out_specs,
        scratch_shapes=([pltpu.VMEM((SEQ + SUBLANES, ct), F32)] * 2
                        + [pltpu.VMEM((CTX_LEN + SUBLANES, ct), F32)] * 2
                        + [pltpu.VMEM((SEQ + SUBLANES, ct), F32), pltpu.VMEM((CTX_LEN + SUBLANES, ct), F32)]),
        compiler_params=_params(("arbitrary", "arbitrary")),
        name="rglru_mixer",
    )(*args)
    return outs if need_ctx else (outs[0], None)


def _outproj_kernel(x_ref, gate_ref, ma_ref, mb_ref, mc_ref, wa_ref, wb_ref, wc_ref, o_ref):
    y = jnp.dot(ma_ref[...], wa_ref[...], preferred_element_type=F32)
    y = y + jnp.dot(mb_ref[...], wb_ref[...], preferred_element_type=F32)
    y = y + jnp.dot(mc_ref[...], wc_ref[...], preferred_element_type=F32)
    o_ref[...] = x_ref[...] + gate_ref[0] * y


def output_projection(x2d, mod3, ma, mb, mc, w_bf16, layer, rows_per_mod, mod_row0):
    m = x2d.shape[0]
    tm, tn = 1024, 1024

    def mod_row(i):
        return mod_row0 + (i * tm) // rows_per_mod

    return pl.pallas_call(
        _outproj_kernel,
        out_shape=jax.ShapeDtypeStruct((m, D_MODEL), F32),
        grid=(m // tm, D_MODEL // tn),
        in_specs=[
            pl.BlockSpec((tm, tn), lambda i, j: (i, j)),
            pl.BlockSpec((1, 1, tn), lambda i, j: (mod_row(i), 0, 2 * (D_MODEL // tn) + j)),
            pl.BlockSpec((tm, A_WIDTH), lambda i, j: (i, 0)),
            pl.BlockSpec((tm, B_WIDTH), lambda i, j: (i, 0)),
            pl.BlockSpec((tm, C_WIDTH), lambda i, j: (i, 0)),
            pl.BlockSpec((None, A_WIDTH, tn), lambda i, j: (layer, 0, j)),
            pl.BlockSpec((None, B_WIDTH, tn), lambda i, j: (layer, 1, j)),
            pl.BlockSpec((None, C_WIDTH, tn), lambda i, j: (layer, (A_WIDTH + B_WIDTH) // C_WIDTH, j)),
        ],
        out_specs=pl.BlockSpec((tm, tn), lambda i, j: (i, j)),
        compiler_params=_params(("arbitrary", "arbitrary")),
        name="output_projection",
    )(x2d, mod3, ma, mb, mc, w_bf16, w_bf16, w_bf16)


def _outproj_norm_kernel(x_ref, gate_ref, ma_ref, mb_ref, mc_ref, wa_ref, wb_ref, wc_ref, fw_ref, o_ref):
    j = pl.program_id(1)
    tn = x_ref.shape[1]
    y = jnp.dot(ma_ref[...], wa_ref[...], preferred_element_type=F32)
    y = y + jnp.dot(mb_ref[...], wb_ref[...], preferred_element_type=F32)
    y = y + jnp.dot(mc_ref[...], wc_ref[...], preferred_element_type=F32)
    o_ref[:, pl.ds(pl.multiple_of(j * tn, tn), tn)] = x_ref[...] + gate_ref[0] * y

    @pl.when(j == pl.num_programs(1) - 1)
    def _():
        fw = fw_ref[...]

        def body(r, carry):
            rows = pl.ds(pl.multiple_of(r * _NORM_CHUNK, _NORM_CHUNK), _NORM_CHUNK)
            v = o_ref[rows, :]
            o_ref[rows, :] = v * lax.rsqrt(jnp.mean(v * v, axis=-1, keepdims=True) + NORM_EPS) * fw
            return carry

        lax.fori_loop(0, o_ref.shape[0] // _NORM_CHUNK, body, 0)


def output_projection_final(x2d, mod3, ma, mb, mc, w_bf16, layer, rows_per_mod, mod_row0, final_w):
    m = x2d.shape[0]
    tm, tn = 512, 1024

    def mod_row(i):
        return mod_row0 + (i * tm) // rows_per_mod

    return pl.pallas_call(
        _outproj_norm_kernel,
        out_shape=jax.ShapeDtypeStruct((m, D_MODEL), F32),
        grid=(m // tm, D_MODEL // tn),
        in_specs=[
            pl.BlockSpec((tm, tn), lambda i, j: (i, j)),
            pl.BlockSpec((1, 1, tn), lambda i, j: (mod_row(i), 0, 2 * (D_MODEL // tn) + j)),
            pl.BlockSpec((tm, A_WIDTH), lambda i, j: (i, 0)),
            pl.BlockSpec((tm, B_WIDTH), lambda i, j: (i, 0)),
            pl.BlockSpec((tm, C_WIDTH), lambda i, j: (i, 0)),
            pl.BlockSpec((None, A_WIDTH, tn), lambda i, j: (layer, 0, j)),
            pl.BlockSpec((None, B_WIDTH, tn), lambda i, j: (layer, 1, j)),
            pl.BlockSpec((None, C_WIDTH, tn), lambda i, j: (layer, (A_WIDTH + B_WIDTH) // C_WIDTH, j)),
            pl.BlockSpec((1, D_MODEL), lambda i, j: (0, 0)),
        ],
        out_specs=pl.BlockSpec((tm, D_MODEL), lambda i, j: (i, 0)),
        compiler_params=_params(("arbitrary", "arbitrary")),
        name="output_projection_final",
    )(x2d, mod3, ma, mb, mc, w_bf16, w_bf16, w_bf16, final_w.reshape(1, D_MODEL))


def _rope_tables():
    t = np.arange(SEQ)
    pos = np.stack([t // GRID_W, t % GRID_W], axis=0).astype(np.float32)
    lane = np.arange(HEAD_DIM)
    within = lane % A_QK_DIM
    axis = within // (A_QK_DIM // 2)
    k = within % (A_QK_DIM // 2)
    nfreq = A_QK_DIM // 4
    upper = k >= nfreq
    freqs = jnp.asarray(ROPE_THETA, F32) ** (-jnp.arange(nfreq, dtype=F32) / nfreq)
    ang = jnp.asarray(pos)[axis, :].T * freqs[k % nfreq][None, :]
    cos, sin = jnp.cos(ang), jnp.sin(ang)
    sin_lo = jnp.where(upper[None, :], 0.0, -sin)
    sin_hi = jnp.where(upper[None, :], sin, 0.0)
    return cos, sin_lo, sin_hi


def kernel(x, c, ctx, c_ctx, ada_w, ada_b, norm_w, w_in, w_out, lambda_qk, subln_w, rpb,
           conv_w, conv_b, rg_wa, rg_ba, rg_wx, rg_bx, rg_lambda, final_norm_w):
    cvec = jnp.zeros((MOD_ROWS, D_MODEL), F32).at[:BATCH].set(c).at[CTX_MOD_ROW].set(c_ctx)
    mod = adaln_modulation(cvec, ada_w, ada_b)
    mod3 = mod.reshape(DEPTH * MOD_ROWS, 1, 3 * D_MODEL)
    norm_w3 = norm_w.reshape(DEPTH, 1, D_MODEL)
    w_in_bf = w_in.astype(BF16)
    w_out_bf = w_out.astype(BF16)
    rope_tabs = _rope_tables()
    xl = x.reshape(BATCH * SEQ, D_MODEL)
    xc = ctx.reshape(BATCH * CTX_LEN, D_MODEL)
    for l in range(DEPTH):
        need_ctx = l < DEPTH - 1
        lam_init = 0.8 - 0.6 * math.exp(-0.3 * l)
        row_l, row_c = l * MOD_ROWS, l * MOD_ROWS + CTX_MOD_ROW
        proj_l = input_projection(xl, norm_w3, mod3, w_in_bf, l, SEQ, row_l)
        ccols = FULL_COLS if need_ctx else KV_COLS
        proj_c = input_projection(xc, norm_w3, mod3, w_in_bf, l, BATCH * CTX_LEN, row_c, ccols)

        ma = diff_attention_latent(proj_l, proj_c, ccols, rope_tabs, lambda_qk[l], subln_w[l], lam_init)
        mb = neighbourhood_attention(proj_l, proj_c, ccols, neighbourhood_bias_table(rpb[l]))
        mc, mc_c = rglru_mixer(proj_l, proj_c, ccols, conv_w[l], conv_b[l], rg_wa[l], rg_ba[l],
                               rg_wx[l], rg_bx[l], rg_lambda[l], need_ctx)
        if need_ctx:
            xl = output_projection(xl, mod3, ma, mb, mc, w_out_bf, l, SEQ, row_l)
            ma_c = diff_attention_context(proj_c, lambda_qk[l], subln_w[l], lam_init)
            mb_c = dense_context_attention(proj_c)
            xc = output_projection(xc, mod3, ma_c, mb_c, mc_c, w_out_bf, l, BATCH * CTX_LEN, row_c)
        else:
            xl = output_projection_final(xl, mod3, ma, mb, mc, w_out_bf, l, SEQ, row_l, final_norm_w)
    return xl.reshape(BATCH, SEQ, D_MODEL)
```

```python
import functools
import math

import numpy as np
import jax
import jax.numpy as jnp
from jax import lax
from jax.experimental import pallas as pl
from jax.experimental.pallas import tpu as pltpu

F32 = jnp.float32
BF16 = jnp.bfloat16

D_MODEL = 4096
BATCH = 8
SEQ = 2048
DEPTH = 2
GRID_W = 64
GRID_H = SEQ // GRID_W
CTX_LEN = 256
HEAD_DIM = 128
A_WIDTH = (3 * D_MODEL) // 8
A_HEADS = A_WIDTH // HEAD_DIM
A_QK_DIM = HEAD_DIM // 2
B_WIDTH = (3 * D_MODEL) // 8
B_HEADS = B_WIDTH // HEAD_DIM
C_WIDTH = D_MODEL - A_WIDTH - B_WIDTH
C_BLOCKS = 16
C_BLOCK_DIM = C_WIDTH // C_BLOCKS
MIX_WIDTH = A_WIDTH + B_WIDTH + C_WIDTH
IN_WIDTH = 4 * A_WIDTH + 4 * B_WIDTH + 2 * C_WIDTH
NA_KH = 8
NA_KW = 16
ROPE_THETA = 10000.0
RGLRU_C = 8.0
CONV_W = 4
CONV_LEFT = 2
NORM_EPS = 1e-6
SUBLN_EPS = 1e-5
NEG_INF = -1e30

HB = HEAD_DIM
COL_QA, COL_KA, COL_VA, COL_GA = 0, A_HEADS, 2 * A_HEADS, 3 * A_HEADS
COL_QB = 4 * A_HEADS
COL_KB, COL_VB, COL_GB = COL_QB + B_HEADS, COL_QB + 2 * B_HEADS, COL_QB + 3 * B_HEADS
COL_XC = COL_QB + 4 * B_HEADS
COL_GC = COL_XC + C_WIDTH // HB

MOD_ROWS = 16
CTX_MOD_ROW = BATCH

VMEM_LIMIT = 56 * 1024 * 1024
SUBLANES = 8

NB_QROWS = 4
NB_KROWS = 12
NB_GROUPS = GRID_H // NB_QROWS
NB_Q = NB_QROWS * GRID_W
NB_K = NB_KROWS * GRID_W


def _params(sem):
    return pltpu.CompilerParams(dimension_semantics=sem, vmem_limit_bytes=VMEM_LIMIT)


def _sigmoid(x):
    return 1.0 / (1.0 + jnp.exp(-x))


def _silu(x):
    return x * _sigmoid(x)


def _adaln_kernel(c_ref, w_ref, b_ref, o_ref):
    s = _silu(c_ref[...]).astype(BF16)
    w = w_ref[0].astype(BF16)
    o_ref[0] = jnp.dot(s, w, preferred_element_type=F32) + b_ref[0]


def adaln_modulation(cvec, ada_w, ada_b):
    tn = 512
    n3 = 3 * D_MODEL
    return pl.pallas_call(
        _adaln_kernel,
        out_shape=jax.ShapeDtypeStruct((DEPTH, MOD_ROWS, n3), F32),
        grid=(DEPTH, n3 // tn),
        in_specs=[
            pl.BlockSpec((MOD_ROWS, D_MODEL), lambda l, j: (0, 0)),
            pl.BlockSpec((1, D_MODEL, tn), lambda l, j: (l, 0, j)),
            pl.BlockSpec((1, 1, tn), lambda l, j: (l, 0, j)),
        ],
        out_specs=pl.BlockSpec((1, MOD_ROWS, tn), lambda l, j: (l, 0, j)),
        compiler_params=_params(("arbitrary", "arbitrary")),
        name="adaln_modulation",
    )(cvec, ada_w, ada_b.reshape(DEPTH, 1, n3))


_NORM_CHUNK = 32
_NORM_SLICES = 4


def _inproj_kernel(*refs, cast_next):
    if cast_next:
        x_ref, nw_ref, shift_ref, scale_ref, w_ref, wsrc_ref, o_ref, wdst_ref, hx0_ref, hx1_ref = refs
    else:
        x_ref, nw_ref, shift_ref, scale_ref, w_ref, o_ref, hx0_ref, hx1_ref = refs
    r, j = pl.program_id(0), pl.program_id(1)

    def cast_block():
        if cast_next:
            wdst_ref[...] = wsrc_ref[...].astype(BF16)
    xs = x_ref.shape[0]
    gain = nw_ref[...] * (1.0 + scale_ref[0])
    add = shift_ref[0]

    def normalise_slice(hx_ref):
        base = jnp.minimum(j, _NORM_SLICES - 1) * xs
        for k in range(xs // _NORM_CHUNK):
            x = x_ref[k * _NORM_CHUNK:(k + 1) * _NORM_CHUNK, :]
            inv = lax.rsqrt(jnp.mean(x * x, axis=-1, keepdims=True) + NORM_EPS)
            rows = pl.ds(pl.multiple_of(base + k * _NORM_CHUNK, _NORM_CHUNK), _NORM_CHUNK)
            hx_ref[rows, :] = (x * inv * gain + add).astype(BF16)

    def step(hx_cur, hx_next):
        normalise_slice(hx_next)
        cast_block()
        o_ref[...] = jnp.dot(hx_cur[...], w_ref[...], preferred_element_type=F32).astype(o_ref.dtype)

    @pl.when(r == 0)
    def _():
        normalise_slice(hx0_ref)
        cast_block()

    pl.when(r % 2 == 1)(functools.partial(step, hx0_ref, hx1_ref))
    pl.when(jnp.logical_and(r % 2 == 0, r > 0))(functools.partial(step, hx1_ref, hx0_ref))


class ProjCols:
    def __init__(self, groups):
        self.start, pos = {}, 0
        for name, _, width in groups:
            self.start[name] = pos
            pos += width
        self.width = pos * HB
        self.src_blocks = [src + k for _, src, width in groups for k in range(width)]

    def __getitem__(self, name):
        return self.start[name]


_ALL_GROUPS = (("qa", COL_QA, A_HEADS), ("ka", COL_KA, A_HEADS), ("va", COL_VA, A_HEADS),
               ("ga", COL_GA, A_HEADS), ("qb", COL_QB, B_HEADS), ("kb", COL_KB, B_HEADS),
               ("vb", COL_VB, B_HEADS), ("gb", COL_GB, B_HEADS), ("xc", COL_XC, C_WIDTH // HB),
               ("gc", COL_GC, C_WIDTH // HB))
FULL_COLS = ProjCols(_ALL_GROUPS)
KV_COLS = ProjCols(tuple(g for g in _ALL_GROUPS if g[0] in ("ka", "va", "kb", "vb", "xc")))


def input_projection(x2d, norm_w3, mod3, w_bf16, layer, rows_per_mod, mod_row0, cols=FULL_COLS,
                     cast_next=None):
    m = x2d.shape[0]
    tm = 512
    tn = 2048 if cols is FULL_COLS else 512
    per_tile = tn // HB
    n_tiles = cols.width // tn
    src_tiles = [cols.src_blocks[t * per_tile] // per_tile for t in range(n_tiles)]
    assert all(cols.src_blocks[t * per_tile + k] == src_tiles[t] * per_tile + k
               for t in range(n_tiles) for k in range(per_tile))

    def src_tile(j):
        t = j + src_tiles[0]
        for k in range(1, n_tiles):
            jump = (src_tiles[k] - k) - (src_tiles[k - 1] - (k - 1))
            if jump:
                t = t + jnp.where(j >= k, jump, 0)
        return t

    n_rows = m // tm
    xs = tm // _NORM_SLICES
    assert n_tiles >= _NORM_SLICES and xs % _NORM_CHUNK == 0

    def norm_tile(r):
        return jnp.minimum(r, n_rows - 1)

    def mod_row(r):
        return mod_row0 + (norm_tile(r) * tm) // rows_per_mod

    def parked(r, j):
        return jnp.where(r == 0, 0, j)

    in_specs = [
        pl.BlockSpec((xs, D_MODEL),
                     lambda r, j: (norm_tile(r) * _NORM_SLICES + jnp.minimum(j, _NORM_SLICES - 1), 0)),
        pl.BlockSpec((None, 1, D_MODEL), lambda r, j: (layer, 0, 0)),
        pl.BlockSpec((1, 1, D_MODEL), lambda r, j: (mod_row(r), 0, 0)),
        pl.BlockSpec((1, 1, D_MODEL), lambda r, j: (mod_row(r), 0, 1)),
        pl.BlockSpec((None, D_MODEL, tn), lambda r, j: (0, 0, src_tile(parked(r, j)))),
    ]
    args = [x2d, norm_w3, mod3, mod3, w_bf16]
    out_shape = [jax.ShapeDtypeStruct((m, cols.width), BF16)]
    out_specs = [pl.BlockSpec((tm, tn), lambda r, j: (jnp.maximum(r - 1, 0), parked(r, j)))]
    if cast_next is not None:
        w_f32, next_layer = cast_next
        cast_rows, cast_cols = D_MODEL // n_rows, IN_WIDTH // n_tiles
        assert D_MODEL % n_rows == 0 and cast_rows % 16 == 0 and IN_WIDTH % n_tiles == 0
        cast_map = lambda r, j: (jnp.minimum(r, n_rows - 1), jnp.where(r == n_rows, n_tiles - 1, j))
        in_specs.append(pl.BlockSpec((None, cast_rows, cast_cols), lambda r, j: (next_layer, *cast_map(r, j))))
        args.append(w_f32)
        out_shape.append(jax.ShapeDtypeStruct((1, D_MODEL, IN_WIDTH), BF16))
        out_specs.append(pl.BlockSpec((None, cast_rows, cast_cols), lambda r, j: (0, *cast_map(r, j))))
    outs = pl.pallas_call(
        functools.partial(_inproj_kernel, cast_next=cast_next is not None),
        out_shape=out_shape,
        grid=(n_rows + 1, n_tiles),
        in_specs=in_specs,
        out_specs=out_specs,
        scratch_shapes=[pltpu.VMEM((tm, D_MODEL), BF16), pltpu.VMEM((tm, D_MODEL), BF16)],
        compiler_params=_params(("arbitrary", "arbitrary")),
        name="input_projection",
    )(*args)
    return outs if cast_next is not None else outs[0]


def _rope(x, cos, sin_lo, sin_hi):
    return (x * cos + pltpu.roll(x, HEAD_DIM - 16, axis=1) * sin_lo
            + pltpu.roll(x, 16, axis=1) * sin_hi)


def _diff_lambda(lq_ref, lam_init):
    lq = lq_ref[...]
    a = jnp.sum(lq[0:1] * lq[1:2], axis=-1, keepdims=True)
    b = jnp.sum(lq[2:3] * lq[3:4], axis=-1, keepdims=True)
    return jnp.exp(a) - jnp.exp(b) + lam_init


LOG2E = 1.4426950408889634
NKEYS = CTX_LEN + SEQ
DA_TQ = 256
DA_W = 2 * DA_TQ
DA_ROWS = 16
DA_KCHUNK = 256


def _transpose_bf16(x):
    return x.astype(F32).T.astype(BF16)


def _diffattn_lat_kernel(one_ref, q_ref, kc_ref, vc_ref, kl_ref, vl_ref, g_ref, cos_ref, slo_ref, shi_ref,
                         lq_ref, sw_ref, o_ref, k_scr, vt_scr, qt_scr, s0_scr, s1_scr, m_scr, p_scr,
                         *, lam_init):
    n_tiles = SEQ // DA_TQ
    n_chunks = NKEYS // DA_KCHUNK
    groups_per_chunk = DA_KCHUNK // DA_ROWS

    def score_chunk(s_scr, qt2, c, m_run):
        rows = slice(c * DA_KCHUNK, (c + 1) * DA_KCHUNK)
        s = jnp.dot(k_scr[rows, :], qt2, preferred_element_type=F32)
        s_scr[rows, :] = s
        parts = [s[i * DA_ROWS:(i + 1) * DA_ROWS, :] for i in range(groups_per_chunk)]
        if m_run is not None:
            parts.append(m_run)
        while len(parts) > 1:
            parts = [jnp.maximum(parts[i], parts[i + 1]) for i in range(0, len(parts) - 1, 2)] + (
                [parts[-1]] if len(parts) % 2 else [])
        return parts[0]

    def colmax(m_run):
        return jnp.broadcast_to(jnp.max(m_run, axis=0, keepdims=True), (DA_ROWS, DA_W))

    def always(k):
        return one_ref[0] > -k

    @pl.when(always(0))
    def _():
        assert DA_KCHUNK == DA_TQ == CTX_LEN
        dim = lax.broadcasted_iota(jnp.int32, (HB, DA_TQ), 0)

        def stage_queries(c):
            rows = slice(c * DA_TQ, (c + 1) * DA_TQ)
            q = _rope(q_ref[rows, :].astype(F32), cos_ref[rows, :], slo_ref[rows, :], shi_ref[rows, :])
            qt = (q * (A_QK_DIM ** -0.5 * LOG2E)).T
            qt_scr[:, c * DA_W:c * DA_W + DA_TQ] = jnp.where(dim < A_QK_DIM, qt, 0.0).astype(BF16)
            qt_scr[:, c * DA_W + DA_TQ:(c + 1) * DA_W] = jnp.where(dim >= A_QK_DIM, qt, 0.0).astype(BF16)

        stage_queries(0)
        qt0 = qt_scr[:, 0:DA_W]
        k_scr[0:CTX_LEN, :] = kc_ref[...]
        vt_scr[:, 0:CTX_LEN] = _transpose_bf16(vc_ref[...])
        m_run = score_chunk(s0_scr, qt0, 0, None)
        for c in range(SEQ // DA_TQ):
            rows = slice(c * DA_TQ, (c + 1) * DA_TQ)
            dst = slice(CTX_LEN + c * DA_TQ, CTX_LEN + (c + 1) * DA_TQ)
            k_scr[dst, :] = _rope(kl_ref[rows, :].astype(F32), cos_ref[rows, :], slo_ref[rows, :],
                                  shi_ref[rows, :]).astype(BF16)
            m_run = score_chunk(s0_scr, qt0, c + 1, m_run)
            vt_scr[:, dst] = _transpose_bf16(vl_ref[rows, :])
            if c > 0:
                stage_queries(c)
        m_scr[...] = colmax(m_run)

    def tile(t, s_scr, s_next_scr):
        rows = slice(t * DA_TQ, (t + 1) * DA_TQ)
        mb = m_scr[...]
        if s_next_scr is not None:
            qt_next = qt_scr[:, (t + 1) * DA_W:(t + 2) * DA_W]
        m_run = None
        l_acc = [None, None]
        acc = None

        def value_chunk(c, acc):
            rows = slice(c * DA_KCHUNK, (c + 1) * DA_KCHUNK)
            part = jnp.dot(vt_scr[:, rows], p_scr[rows, :], preferred_element_type=F32)
            return part if acc is None else acc + part

        for c in range(n_chunks):
            if s_next_scr is not None:
                m_run = score_chunk(s_next_scr, qt_next, c, m_run)
            for r in range(c * groups_per_chunk, (c + 1) * groups_per_chunk):
                grp = slice(r * DA_ROWS, (r + 1) * DA_ROWS)
                p = jnp.exp2(s_scr[grp, :] - mb)
                p_scr[grp, :] = p.astype(BF16)
                l_acc[r % 2] = p if l_acc[r % 2] is None else l_acc[r % 2] + p
            acc = value_chunk(c, acc)
        l = jnp.sum(l_acc[0] + l_acc[1], axis=0, keepdims=True)

        lam = _diff_lambda(lq_ref, lam_init)
        ot = acc[:, :DA_TQ] * (1.0 / l[:, :DA_TQ]) - acc[:, DA_TQ:] * (lam / l[:, DA_TQ:])
        ot = ot * lax.rsqrt(jnp.mean(ot * ot, axis=0, keepdims=True) + SUBLN_EPS)
        o = ot.T * sw_ref[...] * (1.0 - lam_init)
        o_ref[rows, :] = (o * _silu(g_ref[rows, :].astype(F32))).astype(o_ref.dtype)
        if s_next_scr is not None:
            m_scr[...] = colmax(m_run)

    bufs = (s0_scr, s1_scr)
    for t in range(n_tiles):
        nxt = bufs[(t + 1) % 2] if t + 1 < n_tiles else None
        pl.when(always(t + 1))(functools.partial(tile, t, bufs[t % 2], nxt))


_NT = (((1,), (1,)), ((), ()))


def _diffattn_ctx_kernel(q_ref, k_ref, v_ref, g_ref, lq_ref, sw_ref, o_ref, *, lam_init):
    q = q_ref[...].astype(F32) * (A_QK_DIM ** -0.5)
    lane = lax.broadcasted_iota(jnp.int32, q.shape, 1)
    outs = []
    for qn in (jnp.where(lane < A_QK_DIM, q, 0.0), jnp.where(lane >= A_QK_DIM, q, 0.0)):
        s = lax.dot_general(qn.astype(BF16), k_ref[...], _NT, preferred_element_type=F32)
        p = jnp.exp(s - jnp.max(s, axis=-1, keepdims=True))
        acc = jnp.dot(p.astype(BF16), v_ref[...], preferred_element_type=F32)
        outs.append((acc, jnp.sum(p, axis=-1, keepdims=True)))
    lam = _diff_lambda(lq_ref, lam_init)
    o = outs[0][0] * (1.0 / outs[0][1]) - (lam / outs[1][1]) * outs[1][0]
    o = o * lax.rsqrt(jnp.mean(o * o, axis=-1, keepdims=True) + SUBLN_EPS) * sw_ref[...]
    o = o * (1.0 - lam_init)
    o_ref[...] = (o * _silu(g_ref[...].astype(F32))).astype(o_ref.dtype)


def diff_attention_latent(proj_l, proj_c, ccols, rope_tabs, lambda_qk, subln_w, lam_init):
    head = lambda col: (lambda b, h, one: (b, col + h))
    const = lambda b, h, one: (0, 0)
    grid_spec = pltpu.PrefetchScalarGridSpec(
        num_scalar_prefetch=1,
        grid=(BATCH, A_HEADS),
        in_specs=[
            pl.BlockSpec((SEQ, HB), head(COL_QA)),
            pl.BlockSpec((CTX_LEN, HB), head(ccols["ka"])),
            pl.BlockSpec((CTX_LEN, HB), head(ccols["va"])),
            pl.BlockSpec((SEQ, HB), head(COL_KA)),
            pl.BlockSpec((SEQ, HB), head(COL_VA)),
            pl.BlockSpec((SEQ, HB), head(COL_GA)),
            pl.BlockSpec((SEQ, HB), const),
            pl.BlockSpec((SEQ, HB), const),
            pl.BlockSpec((SEQ, HB), const),
            pl.BlockSpec((4, A_QK_DIM), const),
            pl.BlockSpec((1, HB), const),
        ],
        out_specs=pl.BlockSpec((SEQ, HB), lambda b, h, one: (b, h)),
        scratch_shapes=[pltpu.VMEM((NKEYS, HB), BF16), pltpu.VMEM((HB, NKEYS), BF16),
                        pltpu.VMEM((HB, 2 * SEQ), BF16),
                        pltpu.VMEM((NKEYS, DA_W), F32), pltpu.VMEM((NKEYS, DA_W), F32),
                        pltpu.VMEM((DA_ROWS, DA_W), F32), pltpu.VMEM((NKEYS, DA_W), BF16)],
    )
    return pl.pallas_call(
        functools.partial(_diffattn_lat_kernel, lam_init=lam_init),
        out_shape=jax.ShapeDtypeStruct((BATCH * SEQ, A_WIDTH), BF16),
        grid_spec=grid_spec,
        compiler_params=_params(("arbitrary", "arbitrary")),
        name="diff_attention_latent",
    )(jnp.ones((1,), jnp.int32), proj_l, proj_c, proj_c, proj_l, proj_l, proj_l, *rope_tabs,
      lambda_qk, subln_w.reshape(1, HB))


def diff_attention_context(proj_c, lambda_qk, subln_w, lam_init):
    head = lambda col: (lambda b, h: (b, col + h))
    const = lambda b, h: (0, 0)
    return pl.pallas_call(
        functools.partial(_diffattn_ctx_kernel, lam_init=lam_init),
        out_shape=jax.ShapeDtypeStruct((BATCH * CTX_LEN, A_WIDTH), BF16),
        grid=(BATCH, A_HEADS),
        in_specs=[pl.BlockSpec((CTX_LEN, HB), head(COL_QA)),
                  pl.BlockSpec((CTX_LEN, HB), head(COL_KA)),
                  pl.BlockSpec((CTX_LEN, HB), head(COL_VA)),
                  pl.BlockSpec((CTX_LEN, HB), head(COL_GA)),
                  pl.BlockSpec((4, A_QK_DIM), const),
                  pl.BlockSpec((1, HB), const)],
        out_specs=pl.BlockSpec((CTX_LEN, HB), lambda b, h: (b, h)),
        compiler_params=_params(("arbitrary", "arbitrary")),
        name="diff_attention_context",
    )(proj_c, proj_c, proj_c, proj_c, lambda_qk, subln_w.reshape(1, HB))


def _nbr_slab_start(g):
    return min(max(g * NB_QROWS - NA_KH // 2, 0), GRID_H - NB_KROWS)


def _nbr_pattern(g):
    return 0 if g == 0 else (2 if g == NB_GROUPS - 1 else 1)


def _nbr_kernel(q_ref, kc_ref, vc_ref, kl_ref, vl_ref, g_ref, bias_ref, o_ref, vt_scr):
    vt_scr[:, 0:CTX_LEN] = _transpose_bf16(vc_ref[...])
    for c in range(SEQ // NB_Q):
        vt_scr[:, CTX_LEN + c * NB_Q:CTX_LEN + (c + 1) * NB_Q] = _transpose_bf16(vl_ref[c * NB_Q:(c + 1) * NB_Q, :])
    def scores(g):
        rows = slice(g * NB_Q, (g + 1) * NB_Q)
        k0 = _nbr_slab_start(g) * GRID_W
        qt = (q_ref[rows, :].astype(F32) * (HEAD_DIM ** -0.5 * LOG2E)).T.astype(BF16)
        s_c = jnp.dot(kc_ref[...], qt, preferred_element_type=F32)
        s_n = (jnp.dot(kl_ref[k0:k0 + NB_K, :], qt, preferred_element_type=F32)
               + bias_ref[0, _nbr_pattern(g)])
        return s_c, s_n

    def softmax(s_c, s_n):
        m = jnp.maximum(jnp.max(s_c, axis=0, keepdims=True), jnp.max(s_n, axis=0, keepdims=True))
        p_c = jnp.exp2(s_c - m)
        p_n = jnp.exp2(s_n - m)
        l = jnp.sum(p_c, axis=0, keepdims=True) + jnp.sum(p_n, axis=0, keepdims=True)
        return p_c.astype(BF16), p_n.astype(BF16), l

    def finish(g, p_c, p_n, l):
        rows = slice(g * NB_Q, (g + 1) * NB_Q)
        k0 = _nbr_slab_start(g) * GRID_W
        acc = (jnp.dot(vt_scr[:, 0:CTX_LEN], p_c, preferred_element_type=F32)
               + jnp.dot(vt_scr[:, CTX_LEN + k0:CTX_LEN + k0 + NB_K], p_n,
                         preferred_element_type=F32))
        o = (acc * (1.0 / l)).T
        o_ref[rows, :] = (o * _silu(g_ref[rows, :].astype(F32))).astype(o_ref.dtype)

    ahead = scores(0)
    probs = None
    for g in range(NB_GROUPS):
        s_c, s_n = ahead
        if g + 1 < NB_GROUPS:
            ahead = scores(g + 1)
        new_probs = softmax(s_c, s_n)
        if probs is not None:
            finish(g - 1, *probs)
        probs = new_probs
    finish(NB_GROUPS - 1, *probs)


def neighbourhood_bias_table(rpb):
    w = GRID_W
    cq = np.arange(w)
    c0 = np.clip(cq - NA_KW // 2, 0, w - NA_KW)
    ck = np.arange(w)
    col_valid = (ck[None, :] >= c0[:, None]) & (ck[None, :] < c0[:, None] + NA_KW)
    pad = w - NA_KW
    padded = jnp.pad(rpb.astype(F32), ((0, 0), (0, 0), (pad, pad)))
    toep = jnp.stack([padded[:, :, w - 1 - q:2 * w - 1 - q] for q in range(w)], axis=2)
    toep = jnp.where(col_valid[None, None], toep, NEG_INF)
    toep_t = jnp.swapaxes(toep, 2, 3) * LOG2E
    masked = jnp.full((B_HEADS, w, w), NEG_INF, F32)
    tabs = []
    for g in (0, 1, NB_GROUPS - 1):
        slab0 = _nbr_slab_start(g)
        qr = g * NB_QROWS + np.arange(NB_QROWS)
        kr = slab0 + np.arange(NB_KROWS)
        win0 = np.clip(qr - NA_KH // 2, 0, GRID_H - NA_KH)
        row_valid = (kr[None, :] >= win0[:, None]) & (kr[None, :] < win0[:, None] + NA_KH)
        dr_idx = kr[None, :] - qr[:, None] + NA_KH - 1
        rows = [jnp.concatenate([toep_t[:, int(dr_idx[a, k])] if row_valid[a, k] else masked
                                 for a in range(NB_QROWS)], axis=-1) for k in range(NB_KROWS)]
        tabs.append(jnp.concatenate(rows, axis=1))
    return jnp.stack(tabs, axis=1)


def neighbourhood_attention(proj_l, proj_c, ccols, bias_tab):
    head = lambda col: (lambda h, b: (b, col + h))
    return pl.pallas_call(
        _nbr_kernel,
        out_shape=jax.ShapeDtypeStruct((BATCH * SEQ, B_WIDTH), BF16),
        grid=(B_HEADS, BATCH),
        in_specs=[
            pl.BlockSpec((SEQ, HB), head(COL_QB)),
            pl.BlockSpec((CTX_LEN, HB), head(ccols["kb"])),
            pl.BlockSpec((CTX_LEN, HB), head(ccols["vb"])),
            pl.BlockSpec((SEQ, HB), head(COL_KB)),
            pl.BlockSpec((SEQ, HB), head(COL_VB)),
            pl.BlockSpec((SEQ, HB), head(COL_GB)),
            pl.BlockSpec((1, 3, NB_K, NB_Q), lambda h, b: (h, 0, 0, 0)),
        ],
        out_specs=pl.BlockSpec((SEQ, HB), lambda h, b: (b, h)),
        scratch_shapes=[pltpu.VMEM((HB, NKEYS), BF16)],
        compiler_params=_params(("arbitrary", "arbitrary")),
        name="neighbourhood_attention",
    )(proj_l, proj_c, proj_c, proj_l, proj_l, proj_l, bias_tab)


def _dense_ctx_kernel(q_ref, k_ref, v_ref, g_ref, o_ref):
    scale = HEAD_DIM ** -0.5
    s = lax.dot_general(q_ref[...], k_ref[...], _NT, preferred_element_type=F32) * scale
    p = jnp.exp(s - jnp.max(s, axis=-1, keepdims=True))
    p = p * (1.0 / jnp.sum(p, axis=-1, keepdims=True))
    o = jnp.dot(p.astype(BF16), v_ref[...], preferred_element_type=F32)
    o_ref[...] = (o * _silu(g_ref[...].astype(F32))).astype(o_ref.dtype)


def dense_context_attention(proj_c):
    head = lambda col: (lambda b, h: (b, col + h))
    return pl.pallas_call(
        _dense_ctx_kernel,
        out_shape=jax.ShapeDtypeStruct((BATCH * CTX_LEN, B_WIDTH), BF16),
        grid=(BATCH, B_HEADS),
        in_specs=[pl.BlockSpec((CTX_LEN, HB), head(COL_QB)),
                  pl.BlockSpec((CTX_LEN, HB), head(COL_KB)),
                  pl.BlockSpec((CTX_LEN, HB), head(COL_VB)),
                  pl.BlockSpec((CTX_LEN, HB), head(COL_GB))],
        out_specs=pl.BlockSpec((CTX_LEN, HB), lambda b, h: (b, h)),
        compiler_params=_params(("arbitrary", "arbitrary")),
        name="dense_context_attention",
    )(proj_c, proj_c, proj_c, proj_c)


def _shift_rows(x, d, fill):
    n = x.shape[0]
    if d % SUBLANES == 0:
        pad = jnp.full((abs(d), x.shape[1]), fill, x.dtype)
        return jnp.concatenate([pad, x[:n - d]] if d > 0 else [x[-d:], pad], axis=0)
    rolled = pltpu.roll(x, d % n, axis=0)
    row = lax.broadcasted_iota(jnp.int32, x.shape, 0)
    keep = (row >= d) if d > 0 else (row < n + d)
    return jnp.where(keep, rolled, fill)


def _dwconv(x, w_ref, b_ref):
    acc = b_ref[...] + w_ref[CONV_LEFT:CONV_LEFT + 1, :] * x
    for j in range(CONV_W):
        if j != CONV_LEFT:
            acc = acc + w_ref[j:j + 1, :] * _shift_rows(x, CONV_LEFT - j, 0.0)
    return acc


def _linear_scan(a, b, h0, reverse):
    n = a.shape[0]
    row = lax.broadcasted_iota(jnp.int32, a.shape, 0)
    first = (row == n - 1) if reverse else (row == 0)
    b = b + jnp.where(first, a * h0, 0.0)
    d = 1
    while d < n:
        sd = -d if reverse else d
        b = b + a * _shift_rows(b, sd, 0.0)
        if 2 * d < n:
            a = a * _shift_rows(a, sd, 1.0)
        d *= 2
    return b


def _sigmoid_tanh(x):
    return 0.5 * jnp.tanh(0.5 * x) + 0.5


def _rglru_gates(u, wa_ref, ba_ref, wx_ref, bx_ref, lam_ref, d):
    ub = u.astype(BF16)
    r = _sigmoid_tanh(jnp.dot(ub, wa_ref[0, d], preferred_element_type=F32) + ba_ref[0, d])
    i = _sigmoid_tanh(jnp.dot(ub, wx_ref[0, d], preferred_element_type=F32) + bx_ref[0, d])
    neg_lam = -lam_ref[0, d]
    softplus = jnp.maximum(neg_lam, 0.0) + jnp.log1p(jnp.exp(-jnp.abs(neg_lam)))
    log_a = (-RGLRU_C) * r * softplus
    a = jnp.exp(log_a)
    t = jnp.tanh(log_a)
    b = jnp.sqrt(-2.0 * t / (1.0 - t)) * (i * u)
    return a, b


SCAN_CHUNKS = SUBLANES


def _scan_chunk_len(n):
    return n // SCAN_CHUNKS + 1


def _blocked_scan(a_scr, b_scr, n, h0, reverse, need_states):
    length = _scan_chunk_len(n)
    order = range(length - 1, -1, -1) if reverse else range(length)
    h = jnp.zeros((SCAN_CHUNKS, a_scr.shape[1]), F32)
    p = jnp.ones_like(h)
    hs, ps = [None] * length, [None] * length
    for i in order:
        va = a_scr[pl.ds(i, SCAN_CHUNKS, stride=length), :]
        vb = b_scr[pl.ds(i, SCAN_CHUNKS, stride=length), :]
        h = va * h + vb
        p = va * p
        hs[i], ps[i] = h, p
    rows = [None] * SCAN_CHUNKS
    carry = h0
    for j in (range(SCAN_CHUNKS - 1, -1, -1) if reverse else range(SCAN_CHUNKS)):
        rows[j] = carry
        carry = h[j:j + 1, :] + p[j:j + 1, :] * carry
    if not need_states:
        return None, carry
    start = jnp.concatenate(rows, axis=0)
    return [hs[i] + ps[i] * start for i in range(length)], carry


def _rglru_kernel(*refs, need_ctx):
    if need_ctx:
        (xl_ref, xc_ref, gl_ref, gc_ref, cw_ref, cb_ref, wa_ref, ba_ref, wx_ref, bx_ref, lam_ref,
         ol_ref, oc_ref, al_scr, bl_scr, ac_scr, bc_scr, yl_scr, yc_scr) = refs
    else:
        (xl_ref, xc_ref, gl_ref, cw_ref, cb_ref, wa_ref, ba_ref, wx_ref, bx_ref, lam_ref,
         ol_ref, al_scr, bl_scr, ac_scr, bc_scr, yl_scr, yc_scr) = refs
    ct = al_scr.shape[1]
    u_l = _dwconv(xl_ref[...].astype(F32), cw_ref, cb_ref)
    u_c = _dwconv(xc_ref[...].astype(F32), cw_ref, cb_ref)
    for a_scr, b_scr, n in ((al_scr, bl_scr, SEQ), (ac_scr, bc_scr, CTX_LEN)):
        a_scr[n:n + SUBLANES, :] = jnp.ones((SUBLANES, ct), F32)
        b_scr[n:n + SUBLANES, :] = jnp.zeros((SUBLANES, ct), F32)
    y_l = None
    y_c = None
    for d in range(2):
        rev = d == 1
        a_c, b_c = _rglru_gates(u_c, wa_ref, ba_ref, wx_ref, bx_ref, lam_ref, d)
        ac_scr[0:CTX_LEN, :] = a_c
        bc_scr[0:CTX_LEN, :] = b_c
        h_c, fin = _blocked_scan(ac_scr, bc_scr, CTX_LEN, jnp.zeros((1, ct), F32), rev, need_ctx)
        a_l, b_l = _rglru_gates(u_l, wa_ref, ba_ref, wx_ref, bx_ref, lam_ref, d)
        al_scr[0:SEQ, :] = a_l
        bl_scr[0:SEQ, :] = b_l
        h_l, _ = _blocked_scan(al_scr, bl_scr, SEQ, fin, rev, True)
        y_l = h_l if y_l is None else [f + r for f, r in zip(y_l, h_l)]
        if need_ctx:
            y_c = h_c if y_c is None else [f + r for f, r in zip(y_c, h_c)]
    length = _scan_chunk_len(SEQ)
    for i in range(length):
        yl_scr[pl.ds(i, SCAN_CHUNKS, stride=length), :] = y_l[i]
    ol_ref[...] = (yl_scr[0:SEQ, :] * _silu(gl_ref[...].astype(F32))).astype(ol_ref.dtype)
    if need_ctx:
        length = _scan_chunk_len(CTX_LEN)
        for i in range(length):
            yc_scr[pl.ds(i, SCAN_CHUNKS, stride=length), :] = y_c[i]
        oc_ref[...] = (yc_scr[0:CTX_LEN, :] * _silu(gc_ref[...].astype(F32))).astype(oc_ref.dtype)


def _block_diag(w, per_tile):
    n = w.shape[-1]
    tiles = C_BLOCKS // per_tile
    w = w.reshape(2, tiles, per_tile, n, n)
    eye = jnp.eye(per_tile, dtype=w.dtype)
    bd = jnp.einsum('dtpij,pq->dtpiqj', w, eye).reshape(2, tiles, per_tile * n, per_tile * n)
    return jnp.transpose(bd, (1, 0, 2, 3)).astype(BF16)


def rglru_mixer(proj_l, proj_c, ccols, conv_w, conv_b, rg_wa, rg_ba, rg_wx, rg_bx, rg_lambda, need_ctx):
    ct = 128
    nt = C_WIDTH // ct
    per_tile = ct // C_BLOCK_DIM
    vec = lambda p: jnp.transpose(p.reshape(2, nt, 1, ct), (1, 0, 2, 3))
    wa = _block_diag(rg_wa, per_tile)
    wx = _block_diag(rg_wx, per_tile)
    xcol = lambda col: (lambda b, t: (b, col + t))
    par = lambda b, t: (t, 0, 0, 0)
    out_shape = [jax.ShapeDtypeStruct((BATCH * SEQ, C_WIDTH), BF16)]
    out_specs = [pl.BlockSpec((SEQ, ct), lambda b, t: (b, t))]
    if need_ctx:
        out_shape.append(jax.ShapeDtypeStruct((BATCH * CTX_LEN, C_WIDTH), BF16))
        out_specs.append(pl.BlockSpec((CTX_LEN, ct), lambda b, t: (b, t)))
    in_specs = [pl.BlockSpec((SEQ, ct), xcol(COL_XC)),
                pl.BlockSpec((CTX_LEN, ct), xcol(ccols["xc"])),
                pl.BlockSpec((SEQ, ct), xcol(COL_GC))]
    args = [proj_l, proj_c, proj_l]
    if need_ctx:
        in_specs.append(pl.BlockSpec((CTX_LEN, ct), xcol(ccols["gc"])))
        args.append(proj_c)
    in_specs += [pl.BlockSpec((CONV_W, ct), lambda b, t: (0, t)),
                 pl.BlockSpec((1, ct), lambda b, t: (0, t)),
                 pl.BlockSpec((1, 2, ct, ct), par),
                 pl.BlockSpec((1, 2, 1, ct), par),
                 pl.BlockSpec((1, 2, ct, ct), par),
                 pl.BlockSpec((1, 2, 1, ct), par),
                 pl.BlockSpec((1, 2, 1, ct), par)]
    args += [conv_w, conv_b.reshape(1, C_WIDTH), wa, vec(rg_ba), wx, vec(rg_bx), vec(rg_lambda)]
    outs = pl.pallas_call(
        functools.partial(_rglru_kernel, need_ctx=need_ctx),
        out_shape=out_shape,
        grid=(BATCH, nt),
        in_specs=in_specs,
        out_specs=out_specs,
        scratch_shapes=([pltpu.VMEM((SEQ + SUBLANES, ct), F32)] * 2
                        + [pltpu.VMEM((CTX_LEN + SUBLANES, ct), F32)] * 2
                        + [pltpu.VMEM((SEQ + SUBLANES, ct), F32), pltpu.VMEM((CTX_LEN + SUBLANES, ct), F32)]),
        compiler_params=_params(("arbitrary", "arbitrary")),
        name="rglru_mixer",
    )(*args)
    return outs if need_ctx else (outs[0], None)


def _outproj_kernel(x_ref, gate_ref, ma_ref, mb_ref, mc_ref, wa_ref, wb_ref, wc_ref, o_ref):
    y = jnp.dot(ma_ref[...], wa_ref[...], preferred_element_type=F32)
    y = y + jnp.dot(mb_ref[...], wb_ref[...], preferred_element_type=F32)
    y = y + jnp.dot(mc_ref[...], wc_ref[...], preferred_element_type=F32)
    o_ref[...] = x_ref[...] + gate_ref[0] * y


def output_projection(x2d, mod3, ma, mb, mc, w_bf16, layer, rows_per_mod, mod_row0):
    m = x2d.shape[0]
    tm, tn = 1024, 1024

    def mod_row(i):
        return mod_row0 + (i * tm) // rows_per_mod

    return pl.pallas_call(
        _outproj_kernel,
        out_shape=jax.ShapeDtypeStruct((m, D_MODEL), F32),
        grid=(m // tm, D_MODEL // tn),
        in_specs=[
            pl.BlockSpec((tm, tn), lambda i, j: (i, j)),
            pl.BlockSpec((1, 1, tn), lambda i, j: (mod_row(i), 0, 2 * (D_MODEL // tn) + j)),
            pl.BlockSpec((tm, A_WIDTH), lambda i, j: (i, 0)),
            pl.BlockSpec((tm, B_WIDTH), lambda i, j: (i, 0)),
            pl.BlockSpec((tm, C_WIDTH), lambda i, j: (i, 0)),
            pl.BlockSpec((None, A_WIDTH, tn), lambda i, j: (layer, 0, j)),
            pl.BlockSpec((None, B_WIDTH, tn), lambda i, j: (layer, 1, j)),
            pl.BlockSpec((None, C_WIDTH, tn), lambda i, j: (layer, (A_WIDTH + B_WIDTH) // C_WIDTH, j)),
        ],
        out_specs=pl.BlockSpec((tm, tn), lambda i, j: (i, j)),
        compiler_params=_params(("arbitrary", "arbitrary")),
        name="output_projection",
    )(x2d, mod3, ma, mb, mc, w_bf16, w_bf16, w_bf16)


def _final_norm_kernel(x_ref, w_ref, o_ref):
    x = x_ref[...]
    o_ref[...] = x * lax.rsqrt(jnp.mean(x * x, axis=-1, keepdims=True) + NORM_EPS) * w_ref[...]


def final_norm(x2d, w):
    tm = 256
    return pl.pallas_call(
        _final_norm_kernel,
        out_shape=jax.ShapeDtypeStruct(x2d.shape, F32),
        grid=(x2d.shape[0] // tm,),
        in_specs=[pl.BlockSpec((tm, D_MODEL), lambda i: (i, 0)),
                  pl.BlockSpec((1, D_MODEL), lambda i: (0, 0))],
        out_specs=pl.BlockSpec((tm, D_MODEL), lambda i: (i, 0)),
        compiler_params=_params(("arbitrary",)),
        name="final_norm",
    )(x2d, w.reshape(1, D_MODEL))


def _rope_tables():
    t = np.arange(SEQ)
    pos = np.stack([t // GRID_W, t % GRID_W], axis=0).astype(np.float32)
    lane = np.arange(HEAD_DIM)
    within = lane % A_QK_DIM
    axis = within // (A_QK_DIM // 2)
    k = within % (A_QK_DIM // 2)
    nfreq = A_QK_DIM // 4
    upper = k >= nfreq
    freqs = jnp.asarray(ROPE_THETA, F32) ** (-jnp.arange(nfreq, dtype=F32) / nfreq)
    ang = jnp.asarray(pos)[axis, :].T * freqs[k % nfreq][None, :]
    cos, sin = jnp.cos(ang), jnp.sin(ang)
    sin_lo = jnp.where(upper[None, :], 0.0, -sin)
    sin_hi = jnp.where(upper[None, :], sin, 0.0)
    return cos, sin_lo, sin_hi


def kernel(x, c, ctx, c_ctx, ada_w, ada_b, norm_w, w_in, w_out, lambda_qk, subln_w, rpb,
           conv_w, conv_b, rg_wa, rg_ba, rg_wx, rg_bx, rg_lambda, final_norm_w):
    cvec = jnp.zeros((MOD_ROWS, D_MODEL), F32).at[:BATCH].set(c).at[CTX_MOD_ROW].set(c_ctx)
    mod = adaln_modulation(cvec, ada_w, ada_b)
    mod3 = mod.reshape(DEPTH * MOD_ROWS, 1, 3 * D_MODEL)
    norm_w3 = norm_w.reshape(DEPTH, 1, D_MODEL)
    w_in_bf = w_in[0:1].astype(BF16)
    w_out_bf = w_out.astype(BF16)
    rope_tabs = _rope_tables()
    xl = x.reshape(BATCH * SEQ, D_MODEL)
    xc = ctx.reshape(BATCH * CTX_LEN, D_MODEL)
    for l in range(DEPTH):
        need_ctx = l < DEPTH - 1
        lam_init = 0.8 - 0.6 * math.exp(-0.3 * l)
        row_l, row_c = l * MOD_ROWS, l * MOD_ROWS + CTX_MOD_ROW
        ccols = FULL_COLS if need_ctx else KV_COLS
        proj_c = input_projection(xc, norm_w3, mod3, w_in_bf, l, BATCH * CTX_LEN, row_c, ccols)
        if l + 1 < DEPTH:
            proj_l, w_in_bf = input_projection(xl, norm_w3, mod3, w_in_bf, l, SEQ, row_l,
                                               cast_next=(w_in, l + 1))
        else:
            proj_l = input_projection(xl, norm_w3, mod3, w_in_bf, l, SEQ, row_l)

        ma = diff_attention_latent(proj_l, proj_c, ccols, rope_tabs, lambda_qk[l], subln_w[l], lam_init)
        mb = neighbourhood_attention(proj_l, proj_c, ccols, neighbourhood_bias_table(rpb[l]))
        mc, mc_c = rglru_mixer(proj_l, proj_c, ccols, conv_w[l], conv_b[l], rg_wa[l], rg_ba[l],
                               rg_wx[l], rg_bx[l], rg_lambda[l], need_ctx)
        xl = output_projection(xl, mod3, ma, mb, mc, w_out_bf, l, SEQ, row_l)
        if need_ctx:
            ma_c = diff_attention_context(proj_c, lambda_qk[l], subln_w[l], lam_init)
            mb_c = dense_context_attention(proj_c)
            xc = output_projection(xc, mod3, ma_c, mb_c, mc_c, w_out_bf, l, BATCH * CTX_LEN, row_c)
    return final_norm(xl, final_norm_w).reshape(BATCH, SEQ, D_MODEL)
```

```python
import functools
import math

import numpy as np
import jax
import jax.numpy as jnp
from jax import lax
from jax.experimental import pallas as pl
from jax.experimental.pallas import tpu as pltpu

F32 = jnp.float32
BF16 = jnp.bfloat16

D_MODEL = 4096
BATCH = 8
SEQ = 2048
DEPTH = 2
GRID_W = 64
GRID_H = SEQ // GRID_W
CTX_LEN = 256
HEAD_DIM = 128
A_WIDTH = (3 * D_MODEL) // 8
A_HEADS = A_WIDTH // HEAD_DIM
A_QK_DIM = HEAD_DIM // 2
B_WIDTH = (3 * D_MODEL) // 8
B_HEADS = B_WIDTH // HEAD_DIM
C_WIDTH = D_MODEL - A_WIDTH - B_WIDTH
C_BLOCKS = 16
C_BLOCK_DIM = C_WIDTH // C_BLOCKS
MIX_WIDTH = A_WIDTH + B_WIDTH + C_WIDTH
IN_WIDTH = 4 * A_WIDTH + 4 * B_WIDTH + 2 * C_WIDTH
NA_KH = 8
NA_KW = 16
ROPE_THETA = 10000.0
RGLRU_C = 8.0
CONV_W = 4
CONV_LEFT = 2
NORM_EPS = 1e-6
SUBLN_EPS = 1e-5
NEG_INF = -1e30

HB = HEAD_DIM
COL_QA, COL_KA, COL_VA, COL_GA = 0, A_HEADS, 2 * A_HEADS, 3 * A_HEADS
COL_QB = 4 * A_HEADS
COL_KB, COL_VB, COL_GB = COL_QB + B_HEADS, COL_QB + 2 * B_HEADS, COL_QB + 3 * B_HEADS
COL_XC = COL_QB + 4 * B_HEADS
COL_GC = COL_XC + C_WIDTH // HB

MOD_ROWS = 16
CTX_MOD_ROW = BATCH

VMEM_LIMIT = 56 * 1024 * 1024
SUBLANES = 8

NB_QROWS = 4
NB_KROWS = 12
NB_GROUPS = GRID_H // NB_QROWS
NB_Q = NB_QROWS * GRID_W
NB_K = NB_KROWS * GRID_W


def _params(sem):
    return pltpu.CompilerParams(dimension_semantics=sem, vmem_limit_bytes=VMEM_LIMIT)


def _sigmoid(x):
    return 1.0 / (1.0 + jnp.exp(-x))


def _silu(x):
    return x * _sigmoid(x)


def _adaln_kernel(c_ref, w_ref, b_ref, o_ref):
    s = _silu(c_ref[...]).astype(BF16)
    w = w_ref[0].astype(BF16)
    o_ref[0] = jnp.dot(s, w, preferred_element_type=F32) + b_ref[0]


def adaln_modulation(cvec, ada_w, ada_b):
    tn = 512
    n3 = 3 * D_MODEL
    return pl.pallas_call(
        _adaln_kernel,
        out_shape=jax.ShapeDtypeStruct((DEPTH, MOD_ROWS, n3), F32),
        grid=(DEPTH, n3 // tn),
        in_specs=[
            pl.BlockSpec((MOD_ROWS, D_MODEL), lambda l, j: (0, 0)),
            pl.BlockSpec((1, D_MODEL, tn), lambda l, j: (l, 0, j)),
            pl.BlockSpec((1, 1, tn), lambda l, j: (l, 0, j)),
        ],
        out_specs=pl.BlockSpec((1, MOD_ROWS, tn), lambda l, j: (l, 0, j)),
        compiler_params=_params(("arbitrary", "arbitrary")),
        name="adaln_modulation",
    )(cvec, ada_w, ada_b.reshape(DEPTH, 1, n3))


_NORM_CHUNK = 32
_NORM_SLICES = 4


def _inproj_kernel(*refs, cast_next):
    if cast_next:
        x_ref, nw_ref, shift_ref, scale_ref, w_ref, wsrc_ref, o_ref, wdst_ref, hx0_ref, hx1_ref = refs
    else:
        x_ref, nw_ref, shift_ref, scale_ref, w_ref, o_ref, hx0_ref, hx1_ref = refs
    r, j = pl.program_id(0), pl.program_id(1)

    def cast_block():
        if cast_next:
            wdst_ref[...] = wsrc_ref[...].astype(BF16)
    xs = x_ref.shape[0]
    gain = nw_ref[...] * (1.0 + scale_ref[0])
    add = shift_ref[0]

    def normalise_slice(hx_ref):
        base = jnp.minimum(j, _NORM_SLICES - 1) * xs
        for k in range(xs // _NORM_CHUNK):
            x = x_ref[k * _NORM_CHUNK:(k + 1) * _NORM_CHUNK, :]
            inv = lax.rsqrt(jnp.mean(x * x, axis=-1, keepdims=True) + NORM_EPS)
            rows = pl.ds(pl.multiple_of(base + k * _NORM_CHUNK, _NORM_CHUNK), _NORM_CHUNK)
            hx_ref[rows, :] = (x * inv * gain + add).astype(BF16)

    def step(hx_cur, hx_next):
        normalise_slice(hx_next)
        cast_block()
        o_ref[...] = jnp.dot(hx_cur[...], w_ref[...], preferred_element_type=F32).astype(o_ref.dtype)

    @pl.when(r == 0)
    def _():
        normalise_slice(hx0_ref)
        cast_block()

    pl.when(r % 2 == 1)(functools.partial(step, hx0_ref, hx1_ref))
    pl.when(jnp.logical_and(r % 2 == 0, r > 0))(functools.partial(step, hx1_ref, hx0_ref))


class ProjCols:
    def __init__(self, groups):
        self.start, pos = {}, 0
        for name, _, width in groups:
            self.start[name] = pos
            pos += width
        self.width = pos * HB
        self.src_blocks = [src + k for _, src, width in groups for k in range(width)]

    def __getitem__(self, name):
        return self.start[name]


_ALL_GROUPS = (("qa", COL_QA, A_HEADS), ("ka", COL_KA, A_HEADS), ("va", COL_VA, A_HEADS),
               ("ga", COL_GA, A_HEADS), ("qb", COL_QB, B_HEADS), ("kb", COL_KB, B_HEADS),
               ("vb", COL_VB, B_HEADS), ("gb", COL_GB, B_HEADS), ("xc", COL_XC, C_WIDTH // HB),
               ("gc", COL_GC, C_WIDTH // HB))
FULL_COLS = ProjCols(_ALL_GROUPS)
KV_COLS = ProjCols(tuple(g for g in _ALL_GROUPS if g[0] in ("ka", "va", "kb", "vb", "xc")))


def input_projection(x2d, norm_w3, mod3, w_bf16, layer, rows_per_mod, mod_row0, cols=FULL_COLS,
                     cast_next=None):
    m = x2d.shape[0]
    tm = 512
    tn = 2048 if cols is FULL_COLS else 512
    per_tile = tn // HB
    n_tiles = cols.width // tn
    src_tiles = [cols.src_blocks[t * per_tile] // per_tile for t in range(n_tiles)]
    assert all(cols.src_blocks[t * per_tile + k] == src_tiles[t] * per_tile + k
               for t in range(n_tiles) for k in range(per_tile))

    def src_tile(j):
        t = j + src_tiles[0]
        for k in range(1, n_tiles):
            jump = (src_tiles[k] - k) - (src_tiles[k - 1] - (k - 1))
            if jump:
                t = t + jnp.where(j >= k, jump, 0)
        return t

    n_rows = m // tm
    xs = tm // _NORM_SLICES
    assert n_tiles >= _NORM_SLICES and xs % _NORM_CHUNK == 0

    def norm_tile(r):
        return jnp.minimum(r, n_rows - 1)

    def mod_row(r):
        return mod_row0 + (norm_tile(r) * tm) // rows_per_mod

    def parked(r, j):
        return jnp.where(r == 0, 0, j)

    in_specs = [
        pl.BlockSpec((xs, D_MODEL),
                     lambda r, j: (norm_tile(r) * _NORM_SLICES + jnp.minimum(j, _NORM_SLICES - 1), 0)),
        pl.BlockSpec((None, 1, D_MODEL), lambda r, j: (layer, 0, 0)),
        pl.BlockSpec((1, 1, D_MODEL), lambda r, j: (mod_row(r), 0, 0)),
        pl.BlockSpec((1, 1, D_MODEL), lambda r, j: (mod_row(r), 0, 1)),
        pl.BlockSpec((None, D_MODEL, tn), lambda r, j: (0, 0, src_tile(parked(r, j)))),
    ]
    args = [x2d, norm_w3, mod3, mod3, w_bf16]
    out_shape = [jax.ShapeDtypeStruct((m, cols.width), BF16)]
    out_specs = [pl.BlockSpec((tm, tn), lambda r, j: (jnp.maximum(r - 1, 0), parked(r, j)))]
    if cast_next is not None:
        w_f32, next_layer = cast_next
        cast_rows, cast_cols = D_MODEL // n_rows, IN_WIDTH // n_tiles
        assert D_MODEL % n_rows == 0 and cast_rows % 16 == 0 and IN_WIDTH % n_tiles == 0
        cast_map = lambda r, j: (jnp.minimum(r, n_rows - 1), jnp.where(r == n_rows, n_tiles - 1, j))
        in_specs.append(pl.BlockSpec((None, cast_rows, cast_cols), lambda r, j: (next_layer, *cast_map(r, j))))
        args.append(w_f32)
        out_shape.append(jax.ShapeDtypeStruct((1, D_MODEL, IN_WIDTH), BF16))
        out_specs.append(pl.BlockSpec((None, cast_rows, cast_cols), lambda r, j: (0, *cast_map(r, j))))
    outs = pl.pallas_call(
        functools.partial(_inproj_kernel, cast_next=cast_next is not None),
        out_shape=out_shape,
        grid=(n_rows + 1, n_tiles),
        in_specs=in_specs,
        out_specs=out_specs,
        scratch_shapes=[pltpu.VMEM((tm, D_MODEL), BF16), pltpu.VMEM((tm, D_MODEL), BF16)],
        compiler_params=_params(("arbitrary", "arbitrary")),
        name="input_projection",
    )(*args)
    return outs if cast_next is not None else outs[0]


def _rope(x, cos, sin_lo, sin_hi):
    return (x * cos + pltpu.roll(x, HEAD_DIM - 16, axis=1) * sin_lo
            + pltpu.roll(x, 16, axis=1) * sin_hi)


def _diff_lambda(lq_ref, lam_init):
    lq = lq_ref[...]
    a = jnp.sum(lq[0:1] * lq[1:2], axis=-1, keepdims=True)
    b = jnp.sum(lq[2:3] * lq[3:4], axis=-1, keepdims=True)
    return jnp.exp(a) - jnp.exp(b) + lam_init


LOG2E = 1.4426950408889634
NKEYS = CTX_LEN + SEQ
DA_TQ = 256
DA_W = 2 * DA_TQ
DA_ROWS = 16
DA_KCHUNK = 256


def _diffattn_lat_kernel(one_ref, q_ref, kc_ref, vc_ref, kl_ref, vl_ref, g_ref, cos_ref, slo_ref, shi_ref,
                         lq_ref, sw_ref, o_ref, k_scr, qt_scr, s0_scr, s1_scr, m_scr, p_scr,
                         *, lam_init):
    n_tiles = SEQ // DA_TQ
    n_chunks = NKEYS // DA_KCHUNK
    groups_per_chunk = DA_KCHUNK // DA_ROWS

    def score_chunk(s_scr, qt2, c, m_run):
        rows = slice(c * DA_KCHUNK, (c + 1) * DA_KCHUNK)
        s = jnp.dot(k_scr[rows, :], qt2, preferred_element_type=F32)
        s_scr[rows, :] = s
        parts = [s[i * DA_ROWS:(i + 1) * DA_ROWS, :] for i in range(groups_per_chunk)]
        if m_run is not None:
            parts.append(m_run)
        while len(parts) > 1:
            parts = [jnp.maximum(parts[i], parts[i + 1]) for i in range(0, len(parts) - 1, 2)] + (
                [parts[-1]] if len(parts) % 2 else [])
        return parts[0]

    def colmax(m_run):
        return jnp.broadcast_to(jnp.max(m_run, axis=0, keepdims=True), (DA_ROWS, DA_W))

    def always(k):
        return one_ref[0] > -k

    @pl.when(always(0))
    def _():
        assert DA_KCHUNK == DA_TQ == CTX_LEN
        dim = lax.broadcasted_iota(jnp.int32, (HB, DA_TQ), 0)

        def stage_queries(c):
            rows = slice(c * DA_TQ, (c + 1) * DA_TQ)
            q = _rope(q_ref[rows, :].astype(F32), cos_ref[rows, :], slo_ref[rows, :], shi_ref[rows, :])
            qt = (q * (A_QK_DIM ** -0.5 * LOG2E)).T
            qt_scr[:, c * DA_W:c * DA_W + DA_TQ] = jnp.where(dim < A_QK_DIM, qt, 0.0).astype(BF16)
            qt_scr[:, c * DA_W + DA_TQ:(c + 1) * DA_W] = jnp.where(dim >= A_QK_DIM, qt, 0.0).astype(BF16)

        stage_queries(0)
        qt0 = qt_scr[:, 0:DA_W]
        k_scr[0:CTX_LEN, :] = kc_ref[...]
        m_run = score_chunk(s0_scr, qt0, 0, None)
        for c in range(SEQ // DA_TQ):
            rows = slice(c * DA_TQ, (c + 1) * DA_TQ)
            dst = slice(CTX_LEN + c * DA_TQ, CTX_LEN + (c + 1) * DA_TQ)
            k_scr[dst, :] = _rope(kl_ref[rows, :].astype(F32), cos_ref[rows, :], slo_ref[rows, :],
                                  shi_ref[rows, :]).astype(BF16)
            m_run = score_chunk(s0_scr, qt0, c + 1, m_run)
            if c > 0:
                stage_queries(c)
        m_scr[...] = colmax(m_run)

    def tile(t, s_scr, s_next_scr):
        rows = slice(t * DA_TQ, (t + 1) * DA_TQ)
        mb = m_scr[...]
        if s_next_scr is not None:
            qt_next = qt_scr[:, (t + 1) * DA_W:(t + 2) * DA_W]
        m_run = None
        l_acc = [None, None]
        acc = None

        def value_chunk(c, acc):
            rows = slice(c * DA_KCHUNK, (c + 1) * DA_KCHUNK)
            v_rows = vc_ref[...] if c == 0 else vl_ref[(c - 1) * DA_KCHUNK:c * DA_KCHUNK, :]
            part = lax.dot_general(v_rows, p_scr[rows, :], _TN, preferred_element_type=F32)
            return part if acc is None else acc + part

        for c in range(n_chunks):
            if s_next_scr is not None:
                m_run = score_chunk(s_next_scr, qt_next, c, m_run)
            for r in range(c * groups_per_chunk, (c + 1) * groups_per_chunk):
                grp = slice(r * DA_ROWS, (r + 1) * DA_ROWS)
                p = jnp.exp2(s_scr[grp, :] - mb)
                p_scr[grp, :] = p.astype(BF16)
                l_acc[r % 2] = p if l_acc[r % 2] is None else l_acc[r % 2] + p
            acc = value_chunk(c, acc)
        l = jnp.sum(l_acc[0] + l_acc[1], axis=0, keepdims=True)

        lam = _diff_lambda(lq_ref, lam_init)
        ot = acc[:, :DA_TQ] * (1.0 / l[:, :DA_TQ]) - acc[:, DA_TQ:] * (lam / l[:, DA_TQ:])
        ot = ot * lax.rsqrt(jnp.mean(ot * ot, axis=0, keepdims=True) + SUBLN_EPS)
        o = ot.T * sw_ref[...] * (1.0 - lam_init)
        o_ref[rows, :] = (o * _silu(g_ref[rows, :].astype(F32))).astype(o_ref.dtype)
        if s_next_scr is not None:
            m_scr[...] = colmax(m_run)

    bufs = (s0_scr, s1_scr)
    for t in range(n_tiles):
        nxt = bufs[(t + 1) % 2] if t + 1 < n_tiles else None
        pl.when(always(t + 1))(functools.partial(tile, t, bufs[t % 2], nxt))


_NT = (((1,), (1,)), ((), ()))
_TN = (((0,), (0,)), ((), ()))


def _diffattn_ctx_kernel(q_ref, k_ref, v_ref, g_ref, lq_ref, sw_ref, o_ref, *, lam_init):
    q = q_ref[...].astype(F32) * (A_QK_DIM ** -0.5)
    lane = lax.broadcasted_iota(jnp.int32, q.shape, 1)
    outs = []
    for qn in (jnp.where(lane < A_QK_DIM, q, 0.0), jnp.where(lane >= A_QK_DIM, q, 0.0)):
        s = lax.dot_general(qn.astype(BF16), k_ref[...], _NT, preferred_element_type=F32)
        p = jnp.exp(s - jnp.max(s, axis=-1, keepdims=True))
        acc = jnp.dot(p.astype(BF16), v_ref[...], preferred_element_type=F32)
        outs.append((acc, jnp.sum(p, axis=-1, keepdims=True)))
    lam = _diff_lambda(lq_ref, lam_init)
    o = outs[0][0] * (1.0 / outs[0][1]) - (lam / outs[1][1]) * outs[1][0]
    o = o * lax.rsqrt(jnp.mean(o * o, axis=-1, keepdims=True) + SUBLN_EPS) * sw_ref[...]
    o = o * (1.0 - lam_init)
    o_ref[...] = (o * _silu(g_ref[...].astype(F32))).astype(o_ref.dtype)


def diff_attention_latent(proj_l, proj_c, ccols, rope_tabs, lambda_qk, subln_w, lam_init):
    head = lambda col: (lambda b, h, one: (b, col + h))
    const = lambda b, h, one: (0, 0)
    grid_spec = pltpu.PrefetchScalarGridSpec(
        num_scalar_prefetch=1,
        grid=(BATCH, A_HEADS),
        in_specs=[
            pl.BlockSpec((SEQ, HB), head(COL_QA)),
            pl.BlockSpec((CTX_LEN, HB), head(ccols["ka"])),
            pl.BlockSpec((CTX_LEN, HB), head(ccols["va"])),
            pl.BlockSpec((SEQ, HB), head(COL_KA)),
            pl.BlockSpec((SEQ, HB), head(COL_VA)),
            pl.BlockSpec((SEQ, HB), head(COL_GA)),
            pl.BlockSpec((SEQ, HB), const),
            pl.BlockSpec((SEQ, HB), const),
            pl.BlockSpec((SEQ, HB), const),
            pl.BlockSpec((4, A_QK_DIM), const),
            pl.BlockSpec((1, HB), const),
        ],
        out_specs=pl.BlockSpec((SEQ, HB), lambda b, h, one: (b, h)),
        scratch_shapes=[pltpu.VMEM((NKEYS, HB), BF16),
                        pltpu.VMEM((HB, 2 * SEQ), BF16),
                        pltpu.VMEM((NKEYS, DA_W), F32), pltpu.VMEM((NKEYS, DA_W), F32),
                        pltpu.VMEM((DA_ROWS, DA_W), F32), pltpu.VMEM((NKEYS, DA_W), BF16)],
    )
    return pl.pallas_call(
        functools.partial(_diffattn_lat_kernel, lam_init=lam_init),
        out_shape=jax.ShapeDtypeStruct((BATCH * SEQ, A_WIDTH), BF16),
        grid_spec=grid_spec,
        compiler_params=_params(("arbitrary", "arbitrary")),
        name="diff_attention_latent",
    )(jnp.ones((1,), jnp.int32), proj_l, proj_c, proj_c, proj_l, proj_l, proj_l, *rope_tabs,
      lambda_qk, subln_w.reshape(1, HB))


def diff_attention_context(proj_c, lambda_qk, subln_w, lam_init):
    head = lambda col: (lambda b, h: (b, col + h))
    const = lambda b, h: (0, 0)
    return pl.pallas_call(
        functools.partial(_diffattn_ctx_kernel, lam_init=lam_init),
        out_shape=jax.ShapeDtypeStruct((BATCH * CTX_LEN, A_WIDTH), BF16),
        grid=(BATCH, A_HEADS),
        in_specs=[pl.BlockSpec((CTX_LEN, HB), head(COL_QA)),
                  pl.BlockSpec((CTX_LEN, HB), head(COL_KA)),
                  pl.BlockSpec((CTX_LEN, HB), head(COL_VA)),
                  pl.BlockSpec((CTX_LEN, HB), head(COL_GA)),
                  pl.BlockSpec((4, A_QK_DIM), const),
                  pl.BlockSpec((1, HB), const)],
        out_specs=pl.BlockSpec((CTX_LEN, HB), lambda b, h: (b, h)),
        compiler_params=_params(("arbitrary", "arbitrary")),
        name="diff_attention_context",
    )(proj_c, proj_c, proj_c, proj_c, lambda_qk, subln_w.reshape(1, HB))


def _nbr_slab_start(g):
    return min(max(g * NB_QROWS - NA_KH // 2, 0), GRID_H - NB_KROWS)


def _nbr_pattern(g):
    return 0 if g == 0 else (2 if g == NB_GROUPS - 1 else 1)


def _nbr_kernel(q_ref, kc_ref, vc_ref, kl_ref, vl_ref, g_ref, bias_ref, o_ref):
    def scores(g):
        rows = slice(g * NB_Q, (g + 1) * NB_Q)
        k0 = _nbr_slab_start(g) * GRID_W
        qt = (q_ref[rows, :].astype(F32) * (HEAD_DIM ** -0.5 * LOG2E)).T.astype(BF16)
        s_c = jnp.dot(kc_ref[...], qt, preferred_element_type=F32)
        s_n = (jnp.dot(kl_ref[k0:k0 + NB_K, :], qt, preferred_element_type=F32)
               + bias_ref[0, _nbr_pattern(g)])
        return s_c, s_n

    def softmax(s_c, s_n):
        m = jnp.maximum(jnp.max(s_c, axis=0, keepdims=True), jnp.max(s_n, axis=0, keepdims=True))
        p_c = jnp.exp2(s_c - m)
        p_n = jnp.exp2(s_n - m)
        l = jnp.sum(p_c, axis=0, keepdims=True) + jnp.sum(p_n, axis=0, keepdims=True)
        return p_c.astype(BF16), p_n.astype(BF16), l

    def finish(g, p_c, p_n, l):
        rows = slice(g * NB_Q, (g + 1) * NB_Q)
        k0 = _nbr_slab_start(g) * GRID_W
        acc = (lax.dot_general(vc_ref[...], p_c, _TN, preferred_element_type=F32)
               + lax.dot_general(vl_ref[k0:k0 + NB_K, :], p_n, _TN,
                                 preferred_element_type=F32))
        o = (acc * (1.0 / l)).T
        o_ref[rows, :] = (o * _silu(g_ref[rows, :].astype(F32))).astype(o_ref.dtype)

    ahead = scores(0)
    probs = None
    for g in range(NB_GROUPS):
        s_c, s_n = ahead
        if g + 1 < NB_GROUPS:
            ahead = scores(g + 1)
        new_probs = softmax(s_c, s_n)
        if probs is not None:
            finish(g - 1, *probs)
        probs = new_probs
    finish(NB_GROUPS - 1, *probs)


def neighbourhood_bias_table(rpb):
    w = GRID_W
    cq = np.arange(w)
    c0 = np.clip(cq - NA_KW // 2, 0, w - NA_KW)
    ck = np.arange(w)
    col_valid = (ck[None, :] >= c0[:, None]) & (ck[None, :] < c0[:, None] + NA_KW)
    pad = w - NA_KW
    padded = jnp.pad(rpb.astype(F32), ((0, 0), (0, 0), (pad, pad)))
    toep = jnp.stack([padded[:, :, w - 1 - q:2 * w - 1 - q] for q in range(w)], axis=2)
    toep = jnp.where(col_valid[None, None], toep, NEG_INF)
    toep_t = jnp.swapaxes(toep, 2, 3) * LOG2E
    masked = jnp.full((B_HEADS, w, w), NEG_INF, F32)
    tabs = []
    for g in (0, 1, NB_GROUPS - 1):
        slab0 = _nbr_slab_start(g)
        qr = g * NB_QROWS + np.arange(NB_QROWS)
        kr = slab0 + np.arange(NB_KROWS)
        win0 = np.clip(qr - NA_KH // 2, 0, GRID_H - NA_KH)
        row_valid = (kr[None, :] >= win0[:, None]) & (kr[None, :] < win0[:, None] + NA_KH)
        dr_idx = kr[None, :] - qr[:, None] + NA_KH - 1
        rows = [jnp.concatenate([toep_t[:, int(dr_idx[a, k])] if row_valid[a, k] else masked
                                 for a in range(NB_QROWS)], axis=-1) for k in range(NB_KROWS)]
        tabs.append(jnp.concatenate(rows, axis=1))
    return jnp.stack(tabs, axis=1)


def neighbourhood_attention(proj_l, proj_c, ccols, bias_tab):
    head = lambda col: (lambda h, b: (b, col + h))
    return pl.pallas_call(
        _nbr_kernel,
        out_shape=jax.ShapeDtypeStruct((BATCH * SEQ, B_WIDTH), BF16),
        grid=(B_HEADS, BATCH),
        in_specs=[
            pl.BlockSpec((SEQ, HB), head(COL_QB)),
            pl.BlockSpec((CTX_LEN, HB), head(ccols["kb"])),
            pl.BlockSpec((CTX_LEN, HB), head(ccols["vb"])),
            pl.BlockSpec((SEQ, HB), head(COL_KB)),
            pl.BlockSpec((SEQ, HB), head(COL_VB)),
            pl.BlockSpec((SEQ, HB), head(COL_GB)),
            pl.BlockSpec((1, 3, NB_K, NB_Q), lambda h, b: (h, 0, 0, 0)),
        ],
        out_specs=pl.BlockSpec((SEQ, HB), lambda h, b: (b, h)),
        compiler_params=_params(("arbitrary", "arbitrary")),
        name="neighbourhood_attention",
    )(proj_l, proj_c, proj_c, proj_l, proj_l, proj_l, bias_tab)


def _dense_ctx_kernel(q_ref, k_ref, v_ref, g_ref, o_ref):
    scale = HEAD_DIM ** -0.5
    s = lax.dot_general(q_ref[...], k_ref[...], _NT, preferred_element_type=F32) * scale
    p = jnp.exp(s - jnp.max(s, axis=-1, keepdims=True))
    p = p * (1.0 / jnp.sum(p, axis=-1, keepdims=True))
    o = jnp.dot(p.astype(BF16), v_ref[...], preferred_element_type=F32)
    o_ref[...] = (o * _silu(g_ref[...].astype(F32))).astype(o_ref.dtype)


def dense_context_attention(proj_c):
    head = lambda col: (lambda b, h: (b, col + h))
    return pl.pallas_call(
        _dense_ctx_kernel,
        out_shape=jax.ShapeDtypeStruct((BATCH * CTX_LEN, B_WIDTH), BF16),
        grid=(BATCH, B_HEADS),
        in_specs=[pl.BlockSpec((CTX_LEN, HB), head(COL_QB)),
                  pl.BlockSpec((CTX_LEN, HB), head(COL_KB)),
                  pl.BlockSpec((CTX_LEN, HB), head(COL_VB)),
                  pl.BlockSpec((CTX_LEN, HB), head(COL_GB))],
        out_specs=pl.BlockSpec((CTX_LEN, HB), lambda b, h: (b, h)),
        compiler_params=_params(("arbitrary", "arbitrary")),
        name="dense_context_attention",
    )(proj_c, proj_c, proj_c, proj_c)


def _shift_rows(x, d, fill):
    n = x.shape[0]
    if d % SUBLANES == 0:
        pad = jnp.full((abs(d), x.shape[1]), fill, x.dtype)
        return jnp.concatenate([pad, x[:n - d]] if d > 0 else [x[-d:], pad], axis=0)
    rolled = pltpu.roll(x, d % n, axis=0)
    row = lax.broadcasted_iota(jnp.int32, x.shape, 0)
    keep = (row >= d) if d > 0 else (row < n + d)
    return jnp.where(keep, rolled, fill)


def _dwconv(x, w_ref, b_ref):
    acc = b_ref[...] + w_ref[CONV_LEFT:CONV_LEFT + 1, :] * x
    for j in range(CONV_W):
        if j != CONV_LEFT:
            acc = acc + w_ref[j:j + 1, :] * _shift_rows(x, CONV_LEFT - j, 0.0)
    return acc


def _linear_scan(a, b, h0, reverse):
    n = a.shape[0]
    row = lax.broadcasted_iota(jnp.int32, a.shape, 0)
    first = (row == n - 1) if reverse else (row == 0)
    b = b + jnp.where(first, a * h0, 0.0)
    d = 1
    while d < n:
        sd = -d if reverse else d
        b = b + a * _shift_rows(b, sd, 0.0)
        if 2 * d < n:
            a = a * _shift_rows(a, sd, 1.0)
        d *= 2
    return b


def _sigmoid_tanh(x):
    return 0.5 * jnp.tanh(0.5 * x) + 0.5


def _rglru_gates(u, wa_ref, ba_ref, wx_ref, bx_ref, lam_ref, d):
    ub = u.astype(BF16)
    r = _sigmoid_tanh(jnp.dot(ub, wa_ref[0, d], preferred_element_type=F32) + ba_ref[0, d])
    i = _sigmoid_tanh(jnp.dot(ub, wx_ref[0, d], preferred_element_type=F32) + bx_ref[0, d])
    neg_lam = -lam_ref[0, d]
    softplus = jnp.maximum(neg_lam, 0.0) + jnp.log1p(jnp.exp(-jnp.abs(neg_lam)))
    log_a = (-RGLRU_C) * r * softplus
    a = jnp.exp(log_a)
    t = jnp.tanh(log_a)
    b = jnp.sqrt(-2.0 * t / (1.0 - t)) * (i * u)
    return a, b


SCAN_CHUNKS = SUBLANES


def _scan_chunk_len(n):
    return n // SCAN_CHUNKS + 1


def _blocked_scan(a_scr, b_scr, n, h0, reverse, need_states):
    length = _scan_chunk_len(n)
    order = range(length - 1, -1, -1) if reverse else range(length)
    h = jnp.zeros((SCAN_CHUNKS, a_scr.shape[1]), F32)
    p = jnp.ones_like(h)
    hs, ps = [None] * length, [None] * length
    for i in order:
        va = a_scr[pl.ds(i, SCAN_CHUNKS, stride=length), :]
        vb = b_scr[pl.ds(i, SCAN_CHUNKS, stride=length), :]
        h = va * h + vb
        p = va * p
        hs[i], ps[i] = h, p
    rows = [None] * SCAN_CHUNKS
    carry = h0
    for j in (range(SCAN_CHUNKS - 1, -1, -1) if reverse else range(SCAN_CHUNKS)):
        rows[j] = carry
        carry = h[j:j + 1, :] + p[j:j + 1, :] * carry
    if not need_states:
        return None, carry
    start = jnp.concatenate(rows, axis=0)
    return [hs[i] + ps[i] * start for i in range(length)], carry


def _rglru_kernel(*refs, need_ctx):
    if need_ctx:
        (xl_ref, xc_ref, gl_ref, gc_ref, cw_ref, cb_ref, wa_ref, ba_ref, wx_ref, bx_ref, lam_ref,
         ol_ref, oc_ref, al_scr, bl_scr, ac_scr, bc_scr, yl_scr, yc_scr) = refs
    else:
        (xl_ref, xc_ref, gl_ref, cw_ref, cb_ref, wa_ref, ba_ref, wx_ref, bx_ref, lam_ref,
         ol_ref, al_scr, bl_scr, ac_scr, bc_scr, yl_scr, yc_scr) = refs
    ct = al_scr.shape[1]
    u_l = _dwconv(xl_ref[...].astype(F32), cw_ref, cb_ref)
    u_c = _dwconv(xc_ref[...].astype(F32), cw_ref, cb_ref)
    for a_scr, b_scr, n in ((al_scr, bl_scr, SEQ), (ac_scr, bc_scr, CTX_LEN)):
        a_scr[n:n + SUBLANES, :] = jnp.ones((SUBLANES, ct), F32)
        b_scr[n:n + SUBLANES, :] = jnp.zeros((SUBLANES, ct), F32)
    y_l = None
    y_c = None
    for d in range(2):
        rev = d == 1
        a_c, b_c = _rglru_gates(u_c, wa_ref, ba_ref, wx_ref, bx_ref, lam_ref, d)
        ac_scr[0:CTX_LEN, :] = a_c
        bc_scr[0:CTX_LEN, :] = b_c
        h_c, fin = _blocked_scan(ac_scr, bc_scr, CTX_LEN, jnp.zeros((1, ct), F32), rev, need_ctx)
        a_l, b_l = _rglru_gates(u_l, wa_ref, ba_ref, wx_ref, bx_ref, lam_ref, d)
        al_scr[0:SEQ, :] = a_l
        bl_scr[0:SEQ, :] = b_l
        h_l, _ = _blocked_scan(al_scr, bl_scr, SEQ, fin, rev, True)
        y_l = h_l if y_l is None else [f + r for f, r in zip(y_l, h_l)]
        if need_ctx:
            y_c = h_c if y_c is None else [f + r for f, r in zip(y_c, h_c)]
    length = _scan_chunk_len(SEQ)
    for i in range(length):
        yl_scr[pl.ds(i, SCAN_CHUNKS, stride=length), :] = y_l[i]
    ol_ref[...] = (yl_scr[0:SEQ, :] * _silu(gl_ref[...].astype(F32))).astype(ol_ref.dtype)
    if need_ctx:
        length = _scan_chunk_len(CTX_LEN)
        for i in range(length):
            yc_scr[pl.ds(i, SCAN_CHUNKS, stride=length), :] = y_c[i]
        oc_ref[...] = (yc_scr[0:CTX_LEN, :] * _silu(gc_ref[...].astype(F32))).astype(oc_ref.dtype)


def _block_diag(w, per_tile):
    n = w.shape[-1]
    tiles = C_BLOCKS // per_tile
    w = w.reshape(2, tiles, per_tile, n, n)
    eye = jnp.eye(per_tile, dtype=w.dtype)
    bd = jnp.einsum('dtpij,pq->dtpiqj', w, eye).reshape(2, tiles, per_tile * n, per_tile * n)
    return jnp.transpose(bd, (1, 0, 2, 3)).astype(BF16)


def rglru_mixer(proj_l, proj_c, ccols, conv_w, conv_b, rg_wa, rg_ba, rg_wx, rg_bx, rg_lambda, need_ctx):
    ct = 128
    nt = C_WIDTH // ct
    per_tile = ct // C_BLOCK_DIM
    vec = lambda p: jnp.transpose(p.reshape(2, nt, 1, ct), (1, 0, 2, 3))
    wa = _block_diag(rg_wa, per_tile)
    wx = _block_diag(rg_wx, per_tile)
    xcol = lambda col: (lambda b, t: (b, col + t))
    par = lambda b, t: (t, 0, 0, 0)
    out_shape = [jax.ShapeDtypeStruct((BATCH * SEQ, C_WIDTH), BF16)]
    out_specs = [pl.BlockSpec((SEQ, ct), lambda b, t: (b, t))]
    if need_ctx:
        out_shape.append(jax.ShapeDtypeStruct((BATCH * CTX_LEN, C_WIDTH), BF16))
        out_specs.append(pl.BlockSpec((CTX_LEN, ct), lambda b, t: (b, t)))
    in_specs = [pl.BlockSpec((SEQ, ct), xcol(COL_XC)),
                pl.BlockSpec((CTX_LEN, ct), xcol(ccols["xc"])),
                pl.BlockSpec((SEQ, ct), xcol(COL_GC))]
    args = [proj_l, proj_c, proj_l]
    if need_ctx:
        in_specs.append(pl.BlockSpec((CTX_LEN, ct), xcol(ccols["gc"])))
        args.append(proj_c)
    in_specs += [pl.BlockSpec((CONV_W, ct), lambda b, t: (0, t)),
                 pl.BlockSpec((1, ct), lambda b, t: (0, t)),
                 pl.BlockSpec((1, 2, ct, ct), par),
                 pl.BlockSpec((1, 2, 1, ct), par),
                 pl.BlockSpec((1, 2, ct, ct), par),
                 pl.BlockSpec((1, 2, 1, ct), par),
                 pl.BlockSpec((1, 2, 1, ct), par)]
    args += [conv_w, conv_b.reshape(1, C_WIDTH), wa, vec(rg_ba), wx, vec(rg_bx), vec(rg_lambda)]
    outs = pl.pallas_call(
        functools.partial(_rglru_kernel, need_ctx=need_ctx),
        out_shape=out_shape,
        grid=(BATCH, nt),
        in_specs=in_specs,
        out_specs=out_specs,
        scratch_shapes=([pltpu.VMEM((SEQ + SUBLANES, ct), F32)] * 2
                        + [pltpu.VMEM((CTX_LEN + SUBLANES, ct), F32)] * 2
                        + [pltpu.VMEM((SEQ + SUBLANES, ct), F32), pltpu.VMEM((CTX_LEN + SUBLANES, ct), F32)]),
        compiler_params=_params(("arbitrary", "arbitrary")),
        name="rglru_mixer",
    )(*args)
    return outs if need_ctx else (outs[0], None)


def _outproj_kernel(x_ref, gate_ref, ma_ref, mb_ref, mc_ref, wa_ref, wb_ref, wc_ref, o_ref):
    y = jnp.dot(ma_ref[...], wa_ref[...], preferred_element_type=F32)
    y = y + jnp.dot(mb_ref[...], wb_ref[...], preferred_element_type=F32)
    y = y + jnp.dot(mc_ref[...], wc_ref[...], preferred_element_type=F32)
    o_ref[...] = x_ref[...] + gate_ref[0] * y


def output_projection(x2d, mod3, ma, mb, mc, w_bf16, layer, rows_per_mod, mod_row0):
    m = x2d.shape[0]
    tm, tn = 1024, 1024

    def mod_row(i):
        return mod_row0 + (i * tm) // rows_per_mod

    return pl.pallas_call(
        _outproj_kernel,
        out_shape=jax.ShapeDtypeStruct((m, D_MODEL), F32),
        grid=(m // tm, D_MODEL // tn),
        in_specs=[
            pl.BlockSpec((tm, tn), lambda i, j: (i, j)),
            pl.BlockSpec((1, 1, tn), lambda i, j: (mod_row(i), 0, 2 * (D_MODEL // tn) + j)),
            pl.BlockSpec((tm, A_WIDTH), lambda i, j: (i, 0)),
            pl.BlockSpec((tm, B_WIDTH), lambda i, j: (i, 0)),
            pl.BlockSpec((tm, C_WIDTH), lambda i, j: (i, 0)),
            pl.BlockSpec((None, A_WIDTH, tn), lambda i, j: (layer, 0, j)),
            pl.BlockSpec((None, B_WIDTH, tn), lambda i, j: (layer, 1, j)),
            pl.BlockSpec((None, C_WIDTH, tn), lambda i, j: (layer, (A_WIDTH + B_WIDTH) // C_WIDTH, j)),
        ],
        out_specs=pl.BlockSpec((tm, tn), lambda i, j: (i, j)),
        compiler_params=_params(("arbitrary", "arbitrary")),
        name="output_projection",
    )(x2d, mod3, ma, mb, mc, w_bf16, w_bf16, w_bf16)


def _final_norm_kernel(x_ref, w_ref, o_ref):
    x = x_ref[...]
    o_ref[...] = x * lax.rsqrt(jnp.mean(x * x, axis=-1, keepdims=True) + NORM_EPS) * w_ref[...]


def final_norm(x2d, w):
    tm = 256
    return pl.pallas_call(
        _final_norm_kernel,
        out_shape=jax.ShapeDtypeStruct(x2d.shape, F32),
        grid=(x2d.shape[0] // tm,),
        in_specs=[pl.BlockSpec((tm, D_MODEL), lambda i: (i, 0)),
                  pl.BlockSpec((1, D_MODEL), lambda i: (0, 0))],
        out_specs=pl.BlockSpec((tm, D_MODEL), lambda i: (i, 0)),
        compiler_params=_params(("arbitrary",)),
        name="final_norm",
    )(x2d, w.reshape(1, D_MODEL))


def _rope_tables():
    t = np.arange(SEQ)
    pos = np.stack([t // GRID_W, t % GRID_W], axis=0).astype(np.float32)
    lane = np.arange(HEAD_DIM)
    within = lane % A_QK_DIM
    axis = within // (A_QK_DIM // 2)
    k = within % (A_QK_DIM // 2)
    nfreq = A_QK_DIM // 4
    upper = k >= nfreq
    freqs = jnp.asarray(ROPE_THETA, F32) ** (-jnp.arange(nfreq, dtype=F32) / nfreq)
    ang = jnp.asarray(pos)[axis, :].T * freqs[k % nfreq][None, :]
    cos, sin = jnp.cos(ang), jnp.sin(ang)
    sin_lo = jnp.where(upper[None, :], 0.0, -sin)
    sin_hi = jnp.where(upper[None, :], sin, 0.0)
    return cos, sin_lo, sin_hi


def kernel(x, c, ctx, c_ctx, ada_w, ada_b, norm_w, w_in, w_out, lambda_qk, subln_w, rpb,
           conv_w, conv_b, rg_wa, rg_ba, rg_wx, rg_bx, rg_lambda, final_norm_w):
    cvec = jnp.zeros((MOD_ROWS, D_MODEL), F32).at[:BATCH].set(c).at[CTX_MOD_ROW].set(c_ctx)
    mod = adaln_modulation(cvec, ada_w, ada_b)
    mod3 = mod.reshape(DEPTH * MOD_ROWS, 1, 3 * D_MODEL)
    norm_w3 = norm_w.reshape(DEPTH, 1, D_MODEL)
    w_in_bf = w_in[0:1].astype(BF16)
    w_out_bf = w_out.astype(BF16)
    rope_tabs = _rope_tables()
    xl = x.reshape(BATCH * SEQ, D_MODEL)
    xc = ctx.reshape(BATCH * CTX_LEN, D_MODEL)
    for l in range(DEPTH):
        need_ctx = l < DEPTH - 1
        lam_init = 0.8 - 0.6 * math.exp(-0.3 * l)
        row_l, row_c = l * MOD_ROWS, l * MOD_ROWS + CTX_MOD_ROW
        ccols = FULL_COLS if need_ctx else KV_COLS
        proj_c = input_projection(xc, norm_w3, mod3, w_in_bf, l, BATCH * CTX_LEN, row_c, ccols)
        if l + 1 < DEPTH:
            proj_l, w_in_bf = input_projection(xl, norm_w3, mod3, w_in_bf, l, SEQ, row_l,
                                               cast_next=(w_in, l + 1))
        else:
            proj_l = input_projection(xl, norm_w3, mod3, w_in_bf, l, SEQ, row_l)

        ma = diff_attention_latent(proj_l, proj_c, ccols, rope_tabs, lambda_qk[l], subln_w[l], lam_init)
        mb = neighbourhood_attention(proj_l, proj_c, ccols, neighbourhood_bias_table(rpb[l]))
        mc, mc_c = rglru_mixer(proj_l, proj_c, ccols, conv_w[l], conv_b[l], rg_wa[l], rg_ba[l],
                               rg_wx[l], rg_bx[l], rg_lambda[l], need_ctx)
        xl = output_projection(xl, mod3, ma, mb, mc, w_out_bf, l, SEQ, row_l)
        if need_ctx:
            ma_c = diff_attention_context(proj_c, lambda_qk[l], subln_w[l], lam_init)
            mb_c = dense_context_attention(proj_c)
            xc = output_projection(xc, mod3, ma_c, mb_c, mc_c, w_out_bf, l, BATCH * CTX_LEN, row_c)
    return final_norm(xl, final_norm_w).reshape(BATCH, SEQ, D_MODEL)
```

```python
import functools
import math

import numpy as np
import jax
import jax.numpy as jnp
from jax import lax
from jax.experimental import pallas as pl
from jax.experimental.pallas import tpu as pltpu

F32 = jnp.float32
BF16 = jnp.bfloat16

D_MODEL = 4096
BATCH = 8
SEQ = 2048
DEPTH = 2
GRID_W = 64
GRID_H = SEQ // GRID_W
CTX_LEN = 256
HEAD_DIM = 128
A_WIDTH = (3 * D_MODEL) // 8
A_HEADS = A_WIDTH // HEAD_DIM
A_QK_DIM = HEAD_DIM // 2
B_WIDTH = (3 * D_MODEL) // 8
B_HEADS = B_WIDTH // HEAD_DIM
C_WIDTH = D_MODEL - A_WIDTH - B_WIDTH
C_BLOCKS = 16
C_BLOCK_DIM = C_WIDTH // C_BLOCKS
MIX_WIDTH = A_WIDTH + B_WIDTH + C_WIDTH
IN_WIDTH = 4 * A_WIDTH + 4 * B_WIDTH + 2 * C_WIDTH
NA_KH = 8
NA_KW = 16
ROPE_THETA = 10000.0
RGLRU_C = 8.0
CONV_W = 4
CONV_LEFT = 2
NORM_EPS = 1e-6
SUBLN_EPS = 1e-5
NEG_INF = -1e30

HB = HEAD_DIM
COL_QA, COL_KA, COL_VA, COL_GA = 0, A_HEADS, 2 * A_HEADS, 3 * A_HEADS
COL_QB = 4 * A_HEADS
COL_KB, COL_VB, COL_GB = COL_QB + B_HEADS, COL_QB + 2 * B_HEADS, COL_QB + 3 * B_HEADS
COL_XC = COL_QB + 4 * B_HEADS
COL_GC = COL_XC + C_WIDTH // HB

MOD_ROWS = 16
CTX_MOD_ROW = BATCH

VMEM_LIMIT = 56 * 1024 * 1024
SUBLANES = 8

NB_QROWS = 4
NB_KROWS = 12
NB_GROUPS = GRID_H // NB_QROWS
NB_Q = NB_QROWS * GRID_W
NB_K = NB_KROWS * GRID_W


def _params(sem):
    return pltpu.CompilerParams(dimension_semantics=sem, vmem_limit_bytes=VMEM_LIMIT)


def _sigmoid(x):
    return 1.0 / (1.0 + jnp.exp(-x))


def _silu(x):
    return x * _sigmoid(x)


def _adaln_kernel(c_ref, w_ref, b_ref, o_ref):
    s = _silu(c_ref[...]).astype(BF16)
    w = w_ref[0].astype(BF16)
    o_ref[0] = jnp.dot(s, w, preferred_element_type=F32) + b_ref[0]


def adaln_modulation(cvec, ada_w, ada_b):
    tn = 512
    n3 = 3 * D_MODEL
    return pl.pallas_call(
        _adaln_kernel,
        out_shape=jax.ShapeDtypeStruct((DEPTH, MOD_ROWS, n3), F32),
        grid=(DEPTH, n3 // tn),
        in_specs=[
            pl.BlockSpec((MOD_ROWS, D_MODEL), lambda l, j: (0, 0)),
            pl.BlockSpec((1, D_MODEL, tn), lambda l, j: (l, 0, j)),
            pl.BlockSpec((1, 1, tn), lambda l, j: (l, 0, j)),
        ],
        out_specs=pl.BlockSpec((1, MOD_ROWS, tn), lambda l, j: (l, 0, j)),
        compiler_params=_params(("arbitrary", "arbitrary")),
        name="adaln_modulation",
    )(cvec, ada_w, ada_b.reshape(DEPTH, 1, n3))


_NORM_CHUNK = 32
_NORM_SLICES = 4


def _inproj_kernel(*refs, cast_next):
    if cast_next:
        x_ref, nw_ref, shift_ref, scale_ref, w_ref, wsrc_ref, o_ref, wdst_ref, hx0_ref, hx1_ref = refs
    else:
        x_ref, nw_ref, shift_ref, scale_ref, w_ref, o_ref, hx0_ref, hx1_ref = refs
    r, j = pl.program_id(0), pl.program_id(1)

    def cast_block():
        if cast_next:
            wdst_ref[...] = wsrc_ref[...].astype(BF16)
    xs = x_ref.shape[0]
    gain = nw_ref[...] * (1.0 + scale_ref[0])
    add = shift_ref[0]

    def normalise_slice(hx_ref):
        base = jnp.minimum(j, _NORM_SLICES - 1) * xs
        for k in range(xs // _NORM_CHUNK):
            x = x_ref[k * _NORM_CHUNK:(k + 1) * _NORM_CHUNK, :]
            inv = lax.rsqrt(jnp.mean(x * x, axis=-1, keepdims=True) + NORM_EPS)
            rows = pl.ds(pl.multiple_of(base + k * _NORM_CHUNK, _NORM_CHUNK), _NORM_CHUNK)
            hx_ref[rows, :] = (x * inv * gain + add).astype(BF16)

    def step(hx_cur, hx_next):
        normalise_slice(hx_next)
        cast_block()
        o_ref[...] = jnp.dot(hx_cur[...], w_ref[...], preferred_element_type=F32).astype(o_ref.dtype)

    @pl.when(r == 0)
    def _():
        normalise_slice(hx0_ref)
        cast_block()

    pl.when(r % 2 == 1)(functools.partial(step, hx0_ref, hx1_ref))
    pl.when(jnp.logical_and(r % 2 == 0, r > 0))(functools.partial(step, hx1_ref, hx0_ref))


class ProjCols:
    def __init__(self, groups):
        self.start, pos = {}, 0
        for name, _, width in groups:
            self.start[name] = pos
            pos += width
        self.width = pos * HB
        self.src_blocks = [src + k for _, src, width in groups for k in range(width)]

    def __getitem__(self, name):
        return self.start[name]


_ALL_GROUPS = (("qa", COL_QA, A_HEADS), ("ka", COL_KA, A_HEADS), ("va", COL_VA, A_HEADS),
               ("ga", COL_GA, A_HEADS), ("qb", COL_QB, B_HEADS), ("kb", COL_KB, B_HEADS),
               ("vb", COL_VB, B_HEADS), ("gb", COL_GB, B_HEADS), ("xc", COL_XC, C_WIDTH // HB),
               ("gc", COL_GC, C_WIDTH // HB))
FULL_COLS = ProjCols(_ALL_GROUPS)
KV_COLS = ProjCols(tuple(g for g in _ALL_GROUPS if g[0] in ("ka", "va", "kb", "vb", "xc")))


def input_projection(x2d, norm_w3, mod3, w_bf16, layer, rows_per_mod, mod_row0, cols=FULL_COLS,
                     cast_next=None):
    m = x2d.shape[0]
    tm = 512
    tn = 2048 if cols is FULL_COLS else 512
    per_tile = tn // HB
    n_tiles = cols.width // tn
    src_tiles = [cols.src_blocks[t * per_tile] // per_tile for t in range(n_tiles)]
    assert all(cols.src_blocks[t * per_tile + k] == src_tiles[t] * per_tile + k
               for t in range(n_tiles) for k in range(per_tile))

    def src_tile(j):
        t = j + src_tiles[0]
        for k in range(1, n_tiles):
            jump = (src_tiles[k] - k) - (src_tiles[k - 1] - (k - 1))
            if jump:
                t = t + jnp.where(j >= k, jump, 0)
        return t

    n_rows = m // tm
    xs = tm // _NORM_SLICES
    assert n_tiles >= _NORM_SLICES and xs % _NORM_CHUNK == 0

    def norm_tile(r):
        return jnp.minimum(r, n_rows - 1)

    def mod_row(r):
        return mod_row0 + (norm_tile(r) * tm) // rows_per_mod

    def parked(r, j):
        return jnp.where(r == 0, 0, j)

    in_specs = [
        pl.BlockSpec((xs, D_MODEL),
                     lambda r, j: (norm_tile(r) * _NORM_SLICES + jnp.minimum(j, _NORM_SLICES - 1), 0)),
        pl.BlockSpec((None, 1, D_MODEL), lambda r, j: (layer, 0, 0)),
        pl.BlockSpec((1, 1, D_MODEL), lambda r, j: (mod_row(r), 0, 0)),
        pl.BlockSpec((1, 1, D_MODEL), lambda r, j: (mod_row(r), 0, 1)),
        pl.BlockSpec((None, D_MODEL, tn), lambda r, j: (0, 0, src_tile(parked(r, j)))),
    ]
    args = [x2d, norm_w3, mod3, mod3, w_bf16]
    out_shape = [jax.ShapeDtypeStruct((m, cols.width), BF16)]
    out_specs = [pl.BlockSpec((tm, tn), lambda r, j: (jnp.maximum(r - 1, 0), parked(r, j)))]
    if cast_next is not None:
        w_f32, next_layer = cast_next
        cast_rows, cast_cols = D_MODEL // n_rows, IN_WIDTH // n_tiles
        assert D_MODEL % n_rows == 0 and cast_rows % 16 == 0 and IN_WIDTH % n_tiles == 0
        cast_map = lambda r, j: (jnp.minimum(r, n_rows - 1), jnp.where(r == n_rows, n_tiles - 1, j))
        in_specs.append(pl.BlockSpec((None, cast_rows, cast_cols), lambda r, j: (next_layer, *cast_map(r, j))))
        args.append(w_f32)
        out_shape.append(jax.ShapeDtypeStruct((1, D_MODEL, IN_WIDTH), BF16))
        out_specs.append(pl.BlockSpec((None, cast_rows, cast_cols), lambda r, j: (0, *cast_map(r, j))))
    outs = pl.pallas_call(
        functools.partial(_inproj_kernel, cast_next=cast_next is not None),
        out_shape=out_shape,
        grid=(n_rows + 1, n_tiles),
        in_specs=in_specs,
        out_specs=out_specs,
        scratch_shapes=[pltpu.VMEM((tm, D_MODEL), BF16), pltpu.VMEM((tm, D_MODEL), BF16)],
        compiler_params=_params(("arbitrary", "arbitrary")),
        name="input_projection",
    )(*args)
    return outs if cast_next is not None else outs[0]


def _rope(x, cos, sin_lo, sin_hi):
    return (x * cos + pltpu.roll(x, HEAD_DIM - 16, axis=1) * sin_lo
            + pltpu.roll(x, 16, axis=1) * sin_hi)


def _diff_lambda(lq_ref, lam_init):
    lq = lq_ref[...]
    a = jnp.sum(lq[0:1] * lq[1:2], axis=-1, keepdims=True)
    b = jnp.sum(lq[2:3] * lq[3:4], axis=-1, keepdims=True)
    return jnp.exp(a) - jnp.exp(b) + lam_init


LOG2E = 1.4426950408889634
NKEYS = CTX_LEN + SEQ
DA_TQ = 256
DA_W = 2 * DA_TQ
DA_ROWS = 16
DA_KCHUNK = 256


def _diffattn_lat_kernel(one_ref, q_ref, kc_ref, vc_ref, kl_ref, vl_ref, g_ref, cos_ref, slo_ref, shi_ref,
                         lq_ref, sw_ref, o_ref, k_scr, qt_scr, s0_scr, s1_scr, m_scr, p_scr,
                         *, lam_init):
    n_tiles = SEQ // DA_TQ
    n_chunks = NKEYS // DA_KCHUNK
    groups_per_chunk = DA_KCHUNK // DA_ROWS

    def score_chunk(s_scr, qt2, c, m_run):
        rows = slice(c * DA_KCHUNK, (c + 1) * DA_KCHUNK)
        s = jnp.dot(k_scr[rows, :], qt2, preferred_element_type=F32)
        s_scr[rows, :] = s
        parts = [s[i * DA_ROWS:(i + 1) * DA_ROWS, :] for i in range(groups_per_chunk)]
        if m_run is not None:
            parts.append(m_run)
        while len(parts) > 1:
            parts = [jnp.maximum(parts[i], parts[i + 1]) for i in range(0, len(parts) - 1, 2)] + (
                [parts[-1]] if len(parts) % 2 else [])
        return parts[0]

    def colmax(m_run):
        return jnp.broadcast_to(jnp.max(m_run, axis=0, keepdims=True), (DA_ROWS, DA_W))

    def always(k):
        return one_ref[0] > -k

    @pl.when(always(0))
    def _():
        assert DA_KCHUNK == DA_TQ == CTX_LEN
        dim = lax.broadcasted_iota(jnp.int32, (HB, DA_TQ), 0)

        def stage_queries(c):
            rows = slice(c * DA_TQ, (c + 1) * DA_TQ)
            q = _rope(q_ref[rows, :].astype(F32), cos_ref[rows, :], slo_ref[rows, :], shi_ref[rows, :])
            qt = (q * (A_QK_DIM ** -0.5 * LOG2E)).T
            qt_scr[:, c * DA_W:c * DA_W + DA_TQ] = jnp.where(dim < A_QK_DIM, qt, 0.0).astype(BF16)
            qt_scr[:, c * DA_W + DA_TQ:(c + 1) * DA_W] = jnp.where(dim >= A_QK_DIM, qt, 0.0).astype(BF16)

        stage_queries(0)
        qt0 = qt_scr[:, 0:DA_W]
        k_scr[0:CTX_LEN, :] = kc_ref[...]
        m_run = score_chunk(s0_scr, qt0, 0, None)
        for c in range(SEQ // DA_TQ):
            rows = slice(c * DA_TQ, (c + 1) * DA_TQ)
            dst = slice(CTX_LEN + c * DA_TQ, CTX_LEN + (c + 1) * DA_TQ)
            k_scr[dst, :] = _rope(kl_ref[rows, :].astype(F32), cos_ref[rows, :], slo_ref[rows, :],
                                  shi_ref[rows, :]).astype(BF16)
            m_run = score_chunk(s0_scr, qt0, c + 1, m_run)
            if c > 0:
                stage_queries(c)
        m_scr[...] = colmax(m_run)

    def tile(t, s_scr, s_next_scr):
        rows = slice(t * DA_TQ, (t + 1) * DA_TQ)
        mb = m_scr[...]
        if s_next_scr is not None:
            qt_next = qt_scr[:, (t + 1) * DA_W:(t + 2) * DA_W]
        m_run = None
        l_acc = [None, None]
        acc = None

        def value_chunk(c, acc):
            rows = slice(c * DA_KCHUNK, (c + 1) * DA_KCHUNK)
            v_rows = vc_ref[...] if c == 0 else vl_ref[(c - 1) * DA_KCHUNK:c * DA_KCHUNK, :]
            part = lax.dot_general(v_rows, p_scr[rows, :], _TN, preferred_element_type=F32)
            return part if acc is None else acc + part

        for c in range(n_chunks):
            if s_next_scr is not None:
                m_run = score_chunk(s_next_scr, qt_next, c, m_run)
            for r in range(c * groups_per_chunk, (c + 1) * groups_per_chunk):
                grp = slice(r * DA_ROWS, (r + 1) * DA_ROWS)
                p = jnp.exp2(s_scr[grp, :] - mb)
                p_scr[grp, :] = p.astype(BF16)
                l_acc[r % 2] = p if l_acc[r % 2] is None else l_acc[r % 2] + p
            acc = value_chunk(c, acc)
        l = jnp.sum(l_acc[0] + l_acc[1], axis=0, keepdims=True)

        lam = _diff_lambda(lq_ref, lam_init)
        ot = acc[:, :DA_TQ] * (1.0 / l[:, :DA_TQ]) - acc[:, DA_TQ:] * (lam / l[:, DA_TQ:])
        ot = ot * lax.rsqrt(jnp.mean(ot * ot, axis=0, keepdims=True) + SUBLN_EPS)
        o = ot.T * sw_ref[...] * (1.0 - lam_init)
        o_ref[rows, :] = (o * _silu(g_ref[rows, :].astype(F32))).astype(o_ref.dtype)
        if s_next_scr is not None:
            m_scr[...] = colmax(m_run)

    bufs = (s0_scr, s1_scr)
    for t in range(n_tiles):
        nxt = bufs[(t + 1) % 2] if t + 1 < n_tiles else None
        pl.when(always(t + 1))(functools.partial(tile, t, bufs[t % 2], nxt))


_NT = (((1,), (1,)), ((), ()))
_TN = (((0,), (0,)), ((), ()))


def _diffattn_ctx_kernel(q_ref, k_ref, v_ref, g_ref, lq_ref, sw_ref, o_ref, *, lam_init):
    q = q_ref[...].astype(F32) * (A_QK_DIM ** -0.5)
    lane = lax.broadcasted_iota(jnp.int32, q.shape, 1)
    outs = []
    for qn in (jnp.where(lane < A_QK_DIM, q, 0.0), jnp.where(lane >= A_QK_DIM, q, 0.0)):
        s = lax.dot_general(qn.astype(BF16), k_ref[...], _NT, preferred_element_type=F32)
        p = jnp.exp(s - jnp.max(s, axis=-1, keepdims=True))
        acc = jnp.dot(p.astype(BF16), v_ref[...], preferred_element_type=F32)
        outs.append((acc, jnp.sum(p, axis=-1, keepdims=True)))
    lam = _diff_lambda(lq_ref, lam_init)
    o = outs[0][0] * (1.0 / outs[0][1]) - (lam / outs[1][1]) * outs[1][0]
    o = o * lax.rsqrt(jnp.mean(o * o, axis=-1, keepdims=True) + SUBLN_EPS) * sw_ref[...]
    o = o * (1.0 - lam_init)
    o_ref[...] = (o * _silu(g_ref[...].astype(F32))).astype(o_ref.dtype)


def diff_attention_latent(proj_l, proj_c, ccols, rope_tabs, lambda_qk, subln_w, lam_init):
    head = lambda col: (lambda b, h, one: (b, col + h))
    const = lambda b, h, one: (0, 0)
    grid_spec = pltpu.PrefetchScalarGridSpec(
        num_scalar_prefetch=1,
        grid=(BATCH, A_HEADS),
        in_specs=[
            pl.BlockSpec((SEQ, HB), head(COL_QA)),
            pl.BlockSpec((CTX_LEN, HB), head(ccols["ka"])),
            pl.BlockSpec((CTX_LEN, HB), head(ccols["va"])),
            pl.BlockSpec((SEQ, HB), head(COL_KA)),
            pl.BlockSpec((SEQ, HB), head(COL_VA)),
            pl.BlockSpec((SEQ, HB), head(COL_GA)),
            pl.BlockSpec((SEQ, HB), const),
            pl.BlockSpec((SEQ, HB), const),
            pl.BlockSpec((SEQ, HB), const),
            pl.BlockSpec((4, A_QK_DIM), const),
            pl.BlockSpec((1, HB), const),
        ],
        out_specs=pl.BlockSpec((SEQ, HB), lambda b, h, one: (b, h)),
        scratch_shapes=[pltpu.VMEM((NKEYS, HB), BF16),
                        pltpu.VMEM((HB, 2 * SEQ), BF16),
                        pltpu.VMEM((NKEYS, DA_W), F32), pltpu.VMEM((NKEYS, DA_W), F32),
                        pltpu.VMEM((DA_ROWS, DA_W), F32), pltpu.VMEM((NKEYS, DA_W), BF16)],
    )
    return pl.pallas_call(
        functools.partial(_diffattn_lat_kernel, lam_init=lam_init),
        out_shape=jax.ShapeDtypeStruct((BATCH * SEQ, A_WIDTH), BF16),
        grid_spec=grid_spec,
        compiler_params=_params(("arbitrary", "arbitrary")),
        name="diff_attention_latent",
    )(jnp.ones((1,), jnp.int32), proj_l, proj_c, proj_c, proj_l, proj_l, proj_l, *rope_tabs,
      lambda_qk, subln_w.reshape(1, HB))


def diff_attention_context(proj_c, lambda_qk, subln_w, lam_init):
    head = lambda col: (lambda b, h: (b, col + h))
    const = lambda b, h: (0, 0)
    return pl.pallas_call(
        functools.partial(_diffattn_ctx_kernel, lam_init=lam_init),
        out_shape=jax.ShapeDtypeStruct((BATCH * CTX_LEN, A_WIDTH), BF16),
        grid=(BATCH, A_HEADS),
        in_specs=[pl.BlockSpec((CTX_LEN, HB), head(COL_QA)),
                  pl.BlockSpec((CTX_LEN, HB), head(COL_KA)),
                  pl.BlockSpec((CTX_LEN, HB), head(COL_VA)),
                  pl.BlockSpec((CTX_LEN, HB), head(COL_GA)),
                  pl.BlockSpec((4, A_QK_DIM), const),
                  pl.BlockSpec((1, HB), const)],
        out_specs=pl.BlockSpec((CTX_LEN, HB), lambda b, h: (b, h)),
        compiler_params=_params(("arbitrary", "arbitrary")),
        name="diff_attention_context",
    )(proj_c, proj_c, proj_c, proj_c, lambda_qk, subln_w.reshape(1, HB))


def _nbr_slab_start(g):
    return min(max(g * NB_QROWS - NA_KH // 2, 0), GRID_H - NB_KROWS)


def _nbr_pattern(g):
    return 0 if g == 0 else (2 if g == NB_GROUPS - 1 else 1)


def _nbr_kernel(q_ref, kc_ref, vc_ref, kl_ref, vl_ref, g_ref, bias_ref, o_ref):
    def scores(g):
        rows = slice(g * NB_Q, (g + 1) * NB_Q)
        k0 = _nbr_slab_start(g) * GRID_W
        qs = (q_ref[rows, :].astype(F32) * (HEAD_DIM ** -0.5 * LOG2E)).astype(BF16)
        s_c = lax.dot_general(kc_ref[...], qs, _NT, preferred_element_type=F32)
        s_n = (lax.dot_general(kl_ref[k0:k0 + NB_K, :], qs, _NT, preferred_element_type=F32)
               + bias_ref[0, _nbr_pattern(g)])
        return s_c, s_n

    def softmax(s_c, s_n):
        m = jnp.maximum(jnp.max(s_c, axis=0, keepdims=True), jnp.max(s_n, axis=0, keepdims=True))
        p_c = jnp.exp2(s_c - m)
        p_n = jnp.exp2(s_n - m)
        l = jnp.sum(p_c, axis=0, keepdims=True) + jnp.sum(p_n, axis=0, keepdims=True)
        return p_c.astype(BF16), p_n.astype(BF16), l

    def finish(g, p_c, p_n, l):
        rows = slice(g * NB_Q, (g + 1) * NB_Q)
        k0 = _nbr_slab_start(g) * GRID_W
        acc = (lax.dot_general(vc_ref[...], p_c, _TN, preferred_element_type=F32)
               + lax.dot_general(vl_ref[k0:k0 + NB_K, :], p_n, _TN,
                                 preferred_element_type=F32))
        o = (acc * (1.0 / l)).T
        o_ref[rows, :] = (o * _silu(g_ref[rows, :].astype(F32))).astype(o_ref.dtype)

    ahead = scores(0)
    probs = None
    for g in range(NB_GROUPS):
        s_c, s_n = ahead
        if g + 1 < NB_GROUPS:
            ahead = scores(g + 1)
        new_probs = softmax(s_c, s_n)
        if probs is not None:
            finish(g - 1, *probs)
        probs = new_probs
    finish(NB_GROUPS - 1, *probs)


def neighbourhood_bias_table(rpb):
    w = GRID_W
    cq = np.arange(w)
    c0 = np.clip(cq - NA_KW // 2, 0, w - NA_KW)
    ck = np.arange(w)
    col_valid = (ck[None, :] >= c0[:, None]) & (ck[None, :] < c0[:, None] + NA_KW)
    pad = w - NA_KW
    padded = jnp.pad(rpb.astype(F32), ((0, 0), (0, 0), (pad, pad)))
    toep = jnp.stack([padded[:, :, w - 1 - q:2 * w - 1 - q] for q in range(w)], axis=2)
    toep = jnp.where(col_valid[None, None], toep, NEG_INF)
    toep_t = jnp.swapaxes(toep, 2, 3) * LOG2E
    masked = jnp.full((B_HEADS, w, w), NEG_INF, F32)
    tabs = []
    for g in (0, 1, NB_GROUPS - 1):
        slab0 = _nbr_slab_start(g)
        qr = g * NB_QROWS + np.arange(NB_QROWS)
        kr = slab0 + np.arange(NB_KROWS)
        win0 = np.clip(qr - NA_KH // 2, 0, GRID_H - NA_KH)
        row_valid = (kr[None, :] >= win0[:, None]) & (kr[None, :] < win0[:, None] + NA_KH)
        dr_idx = kr[None, :] - qr[:, None] + NA_KH - 1
        rows = [jnp.concatenate([toep_t[:, int(dr_idx[a, k])] if row_valid[a, k] else masked
                                 for a in range(NB_QROWS)], axis=-1) for k in range(NB_KROWS)]
        tabs.append(jnp.concatenate(rows, axis=1))
    return jnp.stack(tabs, axis=1)


def neighbourhood_attention(proj_l, proj_c, ccols, bias_tab):
    head = lambda col: (lambda h, b: (b, col + h))
    return pl.pallas_call(
        _nbr_kernel,
        out_shape=jax.ShapeDtypeStruct((BATCH * SEQ, B_WIDTH), BF16),
        grid=(B_HEADS, BATCH),
        in_specs=[
            pl.BlockSpec((SEQ, HB), head(COL_QB)),
            pl.BlockSpec((CTX_LEN, HB), head(ccols["kb"])),
            pl.BlockSpec((CTX_LEN, HB), head(ccols["vb"])),
            pl.BlockSpec((SEQ, HB), head(COL_KB)),
            pl.BlockSpec((SEQ, HB), head(COL_VB)),
            pl.BlockSpec((SEQ, HB), head(COL_GB)),
            pl.BlockSpec((1, 3, NB_K, NB_Q), lambda h, b: (h, 0, 0, 0)),
        ],
        out_specs=pl.BlockSpec((SEQ, HB), lambda h, b: (b, h)),
        compiler_params=_params(("arbitrary", "arbitrary")),
        name="neighbourhood_attention",
    )(proj_l, proj_c, proj_c, proj_l, proj_l, proj_l, bias_tab)


def _dense_ctx_kernel(q_ref, k_ref, v_ref, g_ref, o_ref):
    scale = HEAD_DIM ** -0.5
    s = lax.dot_general(q_ref[...], k_ref[...], _NT, preferred_element_type=F32) * scale
    p = jnp.exp(s - jnp.max(s, axis=-1, keepdims=True))
    p = p * (1.0 / jnp.sum(p, axis=-1, keepdims=True))
    o = jnp.dot(p.astype(BF16), v_ref[...], preferred_element_type=F32)
    o_ref[...] = (o * _silu(g_ref[...].astype(F32))).astype(o_ref.dtype)


def dense_context_attention(proj_c):
    head = lambda col: (lambda b, h: (b, col + h))
    return pl.pallas_call(
        _dense_ctx_kernel,
        out_shape=jax.ShapeDtypeStruct((BATCH * CTX_LEN, B_WIDTH), BF16),
        grid=(BATCH, B_HEADS),
        in_specs=[pl.BlockSpec((CTX_LEN, HB), head(COL_QB)),
                  pl.BlockSpec((CTX_LEN, HB), head(COL_KB)),
                  pl.BlockSpec((CTX_LEN, HB), head(COL_VB)),
                  pl.BlockSpec((CTX_LEN, HB), head(COL_GB))],
        out_specs=pl.BlockSpec((CTX_LEN, HB), lambda b, h: (b, h)),
        compiler_params=_params(("arbitrary", "arbitrary")),
        name="dense_context_attention",
    )(proj_c, proj_c, proj_c, proj_c)


def _shift_rows(x, d, fill):
    n = x.shape[0]
    if d % SUBLANES == 0:
        pad = jnp.full((abs(d), x.shape[1]), fill, x.dtype)
        return jnp.concatenate([pad, x[:n - d]] if d > 0 else [x[-d:], pad], axis=0)
    rolled = pltpu.roll(x, d % n, axis=0)
    row = lax.broadcasted_iota(jnp.int32, x.shape, 0)
    keep = (row >= d) if d > 0 else (row < n + d)
    return jnp.where(keep, rolled, fill)


def _dwconv(x, w_ref, b_ref):
    acc = b_ref[...] + w_ref[CONV_LEFT:CONV_LEFT + 1, :] * x
    for j in range(CONV_W):
        if j != CONV_LEFT:
            acc = acc + w_ref[j:j + 1, :] * _shift_rows(x, CONV_LEFT - j, 0.0)
    return acc


def _linear_scan(a, b, h0, reverse):
    n = a.shape[0]
    row = lax.broadcasted_iota(jnp.int32, a.shape, 0)
    first = (row == n - 1) if reverse else (row == 0)
    b = b + jnp.where(first, a * h0, 0.0)
    d = 1
    while d < n:
        sd = -d if reverse else d
        b = b + a * _shift_rows(b, sd, 0.0)
        if 2 * d < n:
            a = a * _shift_rows(a, sd, 1.0)
        d *= 2
    return b


def _sigmoid_tanh(x):
    return 0.5 * jnp.tanh(0.5 * x) + 0.5


def _rglru_gates(u, wa_ref, ba_ref, wx_ref, bx_ref, lam_ref, d):
    ub = u.astype(BF16)
    r = _sigmoid_tanh(jnp.dot(ub, wa_ref[0, d], preferred_element_type=F32) + ba_ref[0, d])
    i = _sigmoid_tanh(jnp.dot(ub, wx_ref[0, d], preferred_element_type=F32) + bx_ref[0, d])
    neg_lam = -lam_ref[0, d]
    softplus = jnp.maximum(neg_lam, 0.0) + jnp.log1p(jnp.exp(-jnp.abs(neg_lam)))
    log_a = (-RGLRU_C) * r * softplus
    a = jnp.exp(log_a)
    t = jnp.tanh(log_a)
    b = jnp.sqrt(-2.0 * t / (1.0 - t)) * (i * u)
    return a, b


SCAN_CHUNKS = SUBLANES


def _scan_chunk_len(n):
    return n // SCAN_CHUNKS + 1


def _blocked_scan(a_scr, b_scr, n, h0, reverse, need_states):
    length = _scan_chunk_len(n)
    order = range(length - 1, -1, -1) if reverse else range(length)
    h = jnp.zeros((SCAN_CHUNKS, a_scr.shape[1]), F32)
    p = jnp.ones_like(h)
    hs, ps = [None] * length, [None] * length
    for i in order:
        va = a_scr[pl.ds(i, SCAN_CHUNKS, stride=length), :]
        vb = b_scr[pl.ds(i, SCAN_CHUNKS, stride=length), :]
        h = va * h + vb
        p = va * p
        hs[i], ps[i] = h, p
    rows = [None] * SCAN_CHUNKS
    carry = h0
    for j in (range(SCAN_CHUNKS - 1, -1, -1) if reverse else range(SCAN_CHUNKS)):
        rows[j] = carry
        carry = h[j:j + 1, :] + p[j:j + 1, :] * carry
    if not need_states:
        return None, carry
    start = jnp.concatenate(rows, axis=0)
    return [hs[i] + ps[i] * start for i in range(length)], carry


def _rglru_kernel(*refs, need_ctx):
    if need_ctx:
        (xl_ref, xc_ref, gl_ref, gc_ref, cw_ref, cb_ref, wa_ref, ba_ref, wx_ref, bx_ref, lam_ref,
         ol_ref, oc_ref, al_scr, bl_scr, ac_scr, bc_scr, yl_scr, yc_scr) = refs
    else:
        (xl_ref, xc_ref, gl_ref, cw_ref, cb_ref, wa_ref, ba_ref, wx_ref, bx_ref, lam_ref,
         ol_ref, al_scr, bl_scr, ac_scr, bc_scr, yl_scr, yc_scr) = refs
    ct = al_scr.shape[1]
    u_l = _dwconv(xl_ref[...].astype(F32), cw_ref, cb_ref)
    u_c = _dwconv(xc_ref[...].astype(F32), cw_ref, cb_ref)
    for a_scr, b_scr, n in ((al_scr, bl_scr, SEQ), (ac_scr, bc_scr, CTX_LEN)):
        a_scr[n:n + SUBLANES, :] = jnp.ones((SUBLANES, ct), F32)
        b_scr[n:n + SUBLANES, :] = jnp.zeros((SUBLANES, ct), F32)
    y_l = None
    y_c = None
    for d in range(2):
        rev = d == 1
        a_c, b_c = _rglru_gates(u_c, wa_ref, ba_ref, wx_ref, bx_ref, lam_ref, d)
        ac_scr[0:CTX_LEN, :] = a_c
        bc_scr[0:CTX_LEN, :] = b_c
        h_c, fin = _blocked_scan(ac_scr, bc_scr, CTX_LEN, jnp.zeros((1, ct), F32), rev, need_ctx)
        a_l, b_l = _rglru_gates(u_l, wa_ref, ba_ref, wx_ref, bx_ref, lam_ref, d)
        al_scr[0:SEQ, :] = a_l
        bl_scr[0:SEQ, :] = b_l
        h_l, _ = _blocked_scan(al_scr, bl_scr, SEQ, fin, rev, True)
        y_l = h_l if y_l is None else [f + r for f, r in zip(y_l, h_l)]
        if need_ctx:
            y_c = h_c if y_c is None else [f + r for f, r in zip(y_c, h_c)]
    length = _scan_chunk_len(SEQ)
    for i in range(length):
        yl_scr[pl.ds(i, SCAN_CHUNKS, stride=length), :] = y_l[i]
    ol_ref[...] = (yl_scr[0:SEQ, :] * _silu(gl_ref[...].astype(F32))).astype(ol_ref.dtype)
    if need_ctx:
        length = _scan_chunk_len(CTX_LEN)
        for i in range(length):
            yc_scr[pl.ds(i, SCAN_CHUNKS, stride=length), :] = y_c[i]
        oc_ref[...] = (yc_scr[0:CTX_LEN, :] * _silu(gc_ref[...].astype(F32))).astype(oc_ref.dtype)


def _block_diag(w, per_tile):
    n = w.shape[-1]
    tiles = C_BLOCKS // per_tile
    w = w.reshape(2, tiles, per_tile, n, n)
    eye = jnp.eye(per_tile, dtype=w.dtype)
    bd = jnp.einsum('dtpij,pq->dtpiqj', w, eye).reshape(2, tiles, per_tile * n, per_tile * n)
    return jnp.transpose(bd, (1, 0, 2, 3)).astype(BF16)


def rglru_mixer(proj_l, proj_c, ccols, conv_w, conv_b, rg_wa, rg_ba, rg_wx, rg_bx, rg_lambda, need_ctx):
    ct = 128
    nt = C_WIDTH // ct
    per_tile = ct // C_BLOCK_DIM
    vec = lambda p: jnp.transpose(p.reshape(2, nt, 1, ct), (1, 0, 2, 3))
    wa = _block_diag(rg_wa, per_tile)
    wx = _block_diag(rg_wx, per_tile)
    xcol = lambda col: (lambda b, t: (b, col + t))
    par = lambda b, t: (t, 0, 0, 0)
    out_shape = [jax.ShapeDtypeStruct((BATCH * SEQ, C_WIDTH), BF16)]
    out_specs = [pl.BlockSpec((SEQ, ct), lambda b, t: (b, t))]
    if need_ctx:
        out_shape.append(jax.ShapeDtypeStruct((BATCH * CTX_LEN, C_WIDTH), BF16))
        out_specs.append(pl.BlockSpec((CTX_LEN, ct), lambda b, t: (b, t)))
    in_specs = [pl.BlockSpec((SEQ, ct), xcol(COL_XC)),
                pl.BlockSpec((CTX_LEN, ct), xcol(ccols["xc"])),
                pl.BlockSpec((SEQ, ct), xcol(COL_GC))]
    args = [proj_l, proj_c, proj_l]
    if need_ctx:
        in_specs.append(pl.BlockSpec((CTX_LEN, ct), xcol(ccols["gc"])))
        args.append(proj_c)
    in_specs += [pl.BlockSpec((CONV_W, ct), lambda b, t: (0, t)),
                 pl.BlockSpec((1, ct), lambda b, t: (0, t)),
                 pl.BlockSpec((1, 2, ct, ct), par),
                 pl.BlockSpec((1, 2, 1, ct), par),
                 pl.BlockSpec((1, 2, ct, ct), par),
                 pl.BlockSpec((1, 2, 1, ct), par),
                 pl.BlockSpec((1, 2, 1, ct), par)]
    args += [conv_w, conv_b.reshape(1, C_WIDTH), wa, vec(rg_ba), wx, vec(rg_bx), vec(rg_lambda)]
    outs = pl.pallas_call(
        functools.partial(_rglru_kernel, need_ctx=need_ctx),
        out_shape=out_shape,
        grid=(BATCH, nt),
        in_specs=in_specs,
        out_specs=out_specs,
        scratch_shapes=([pltpu.VMEM((SEQ + SUBLANES, ct), F32)] * 2
                        + [pltpu.VMEM((CTX_LEN + SUBLANES, ct), F32)] * 2
                        + [pltpu.VMEM((SEQ + SUBLANES, ct), F32), pltpu.VMEM((CTX_LEN + SUBLANES, ct), F32)]),
        compiler_params=_params(("arbitrary", "arbitrary")),
        name="rglru_mixer",
    )(*args)
    return outs if need_ctx else (outs[0], None)


def _outproj_kernel(x_ref, gate_ref, ma_ref, mb_ref, mc_ref, wa_ref, wb_ref, wc_ref, o_ref):
    y = jnp.dot(ma_ref[...], wa_ref[...], preferred_element_type=F32)
    y = y + jnp.dot(mb_ref[...], wb_ref[...], preferred_element_type=F32)
    y = y + jnp.dot(mc_ref[...], wc_ref[...], preferred_element_type=F32)
    o_ref[...] = x_ref[...] + gate_ref[0] * y


def output_projection(x2d, mod3, ma, mb, mc, w_bf16, layer, rows_per_mod, mod_row0):
    m = x2d.shape[0]
    tm, tn = 1024, 1024

    def mod_row(i):
        return mod_row0 + (i * tm) // rows_per_mod

    return pl.pallas_call(
        _outproj_kernel,
        out_shape=jax.ShapeDtypeStruct((m, D_MODEL), F32),
        grid=(m // tm, D_MODEL // tn),
        in_specs=[
            pl.BlockSpec((tm, tn), lambda i, j: (i, j)),
            pl.BlockSpec((1, 1, tn), lambda i, j: (mod_row(i), 0, 2 * (D_MODEL // tn) + j)),
            pl.BlockSpec((tm, A_WIDTH), lambda i, j: (i, 0)),
            pl.BlockSpec((tm, B_WIDTH), lambda i, j: (i, 0)),
            pl.BlockSpec((tm, C_WIDTH), lambda i, j: (i, 0)),
            pl.BlockSpec((None, A_WIDTH, tn), lambda i, j: (layer, 0, j)),
            pl.BlockSpec((None, B_WIDTH, tn), lambda i, j: (layer, 1, j)),
            pl.BlockSpec((None, C_WIDTH, tn), lambda i, j: (layer, (A_WIDTH + B_WIDTH) // C_WIDTH, j)),
        ],
        out_specs=pl.BlockSpec((tm, tn), lambda i, j: (i, j)),
        compiler_params=_params(("arbitrary", "arbitrary")),
        name="output_projection",
    )(x2d, mod3, ma, mb, mc, w_bf16, w_bf16, w_bf16)


def _final_norm_kernel(x_ref, w_ref, o_ref):
    x = x_ref[...]
    o_ref[...] = x * lax.rsqrt(jnp.mean(x * x, axis=-1, keepdims=True) + NORM_EPS) * w_ref[...]


def final_norm(x2d, w):
    tm = 256
    return pl.pallas_call(
        _final_norm_kernel,
        out_shape=jax.ShapeDtypeStruct(x2d.shape, F32),
        grid=(x2d.shape[0] // tm,),
        in_specs=[pl.BlockSpec((tm, D_MODEL), lambda i: (i, 0)),
                  pl.BlockSpec((1, D_MODEL), lambda i: (0, 0))],
        out_specs=pl.BlockSpec((tm, D_MODEL), lambda i: (i, 0)),
        compiler_params=_params(("arbitrary",)),
        name="final_norm",
    )(x2d, w.reshape(1, D_MODEL))


def _rope_tables():
    t = np.arange(SEQ)
    pos = np.stack([t // GRID_W, t % GRID_W], axis=0).astype(np.float32)
    lane = np.arange(HEAD_DIM)
    within = lane % A_QK_DIM
    axis = within // (A_QK_DIM // 2)
    k = within % (A_QK_DIM // 2)
    nfreq = A_QK_DIM // 4
    upper = k >= nfreq
    freqs = jnp.asarray(ROPE_THETA, F32) ** (-jnp.arange(nfreq, dtype=F32) / nfreq)
    ang = jnp.asarray(pos)[axis, :].T * freqs[k % nfreq][None, :]
    cos, sin = jnp.cos(ang), jnp.sin(ang)
    sin_lo = jnp.where(upper[None, :], 0.0, -sin)
    sin_hi = jnp.where(upper[None, :], sin, 0.0)
    return cos, sin_lo, sin_hi


def kernel(x, c, ctx, c_ctx, ada_w, ada_b, norm_w, w_in, w_out, lambda_qk, subln_w, rpb,
           conv_w, conv_b, rg_wa, rg_ba, rg_wx, rg_bx, rg_lambda, final_norm_w):
    cvec = jnp.zeros((MOD_ROWS, D_MODEL), F32).at[:BATCH].set(c).at[CTX_MOD_ROW].set(c_ctx)
    mod = adaln_modulation(cvec, ada_w, ada_b)
    mod3 = mod.reshape(DEPTH * MOD_ROWS, 1, 3 * D_MODEL)
    norm_w3 = norm_w.reshape(DEPTH, 1, D_MODEL)
    w_in_bf = w_in[0:1].astype(BF16)
    w_out_bf = w_out.astype(BF16)
    rope_tabs = _rope_tables()
    xl = x.reshape(BATCH * SEQ, D_MODEL)
    xc = ctx.reshape(BATCH * CTX_LEN, D_MODEL)
    for l in range(DEPTH):
        need_ctx = l < DEPTH - 1
        lam_init = 0.8 - 0.6 * math.exp(-0.3 * l)
        row_l, row_c = l * MOD_ROWS, l * MOD_ROWS + CTX_MOD_ROW
        ccols = FULL_COLS if need_ctx else KV_COLS
        proj_c = input_projection(xc, norm_w3, mod3, w_in_bf, l, BATCH * CTX_LEN, row_c, ccols)
        if l + 1 < DEPTH:
            proj_l, w_in_bf = input_projection(xl, norm_w3, mod3, w_in_bf, l, SEQ, row_l,
                                               cast_next=(w_in, l + 1))
        else:
            proj_l = input_projection(xl, norm_w3, mod3, w_in_bf, l, SEQ, row_l)

        ma = diff_attention_latent(proj_l, proj_c, ccols, rope_tabs, lambda_qk[l], subln_w[l], lam_init)
        mb = neighbourhood_attention(proj_l, proj_c, ccols, neighbourhood_bias_table(rpb[l]))
        mc, mc_c = rglru_mixer(proj_l, proj_c, ccols, conv_w[l], conv_b[l], rg_wa[l], rg_ba[l],
                               rg_wx[l], rg_bx[l], rg_lambda[l], need_ctx)
        xl = output_projection(xl, mod3, ma, mb, mc, w_out_bf, l, SEQ, row_l)
        if need_ctx:
            ma_c = diff_attention_context(proj_c, lambda_qk[l], subln_w[l], lam_init)
            mb_c = dense_context_attention(proj_c)
            xc = output_projection(xc, mod3, ma_c, mb_c, mc_c, w_out_bf, l, BATCH * CTX_LEN, row_c)
    return final_norm(xl, final_norm_w).reshape(BATCH, SEQ, D_MODEL)
```
